```python
import math
import jax, jax.numpy as jnp
from jax import lax
import numpy as np

D_MODEL = 1024
BATCH = 8
SEQ = 8192
DEPTH = 2

MEM_LEN = 256
BLK = 128
N_BRANCH = 4
BRANCH_W = D_MODEL // 2
HEAD_DIM = 64
SGU_GROUPS = 4
SGU_CHUNK = 128
SGU_GW = BRANCH_W // SGU_GROUPS
LRU_W = BRANCH_W
LRU_HEADS = 4
LRU_HW = LRU_W // LRU_HEADS
CONV_W = 4
LRU_C = 8.0
SWA_HEADS = BRANCH_W // HEAD_DIM
SWA_KV = SWA_HEADS // 4
SWA_WINDOW = 128
FOX_HEADS = BRANCH_W // HEAD_DIM
X_HEADS = 4
X_HEAD_DIM = D_MODEL // 8
D_FF = ((8 * D_MODEL // 3 + 127) // 128) * 128
N_EXPERTS = 8
TOP_K = 2
EPS = 1e-6
IN_SIZES = (BRANCH_W, BRANCH_W, LRU_W, LRU_W, SWA_HEADS * HEAD_DIM, SWA_KV * HEAD_DIM, SWA_KV * HEAD_DIM, FOX_HEADS * HEAD_DIM, FOX_HEADS * HEAD_DIM, FOX_HEADS * HEAD_DIM, FOX_HEADS)
D_IN = sum(IN_SIZES)
N_DENSE = (DEPTH + 1) // 2
N_MOE = DEPTH // 2

kernel_name = 'hybrid_gated_four_mixer_moe_block'


def rms_norm(x, g):
    x32 = x.astype(jnp.float32)
    y = x32 * lax.rsqrt(jnp.mean(x32 * x32, axis=-1, keepdims=True) + EPS)
    return (y * g.astype(jnp.float32)).astype(x.dtype)


def alibi_slopes(n):
    return 2.0 ** (-(8.0 / n) * jnp.arange(1, n + 1, dtype=jnp.float32))


def sgu_mixer(u, v, g_norm, w_s, b_s):
    bsz, seq, _ = v.shape
    v = rms_norm(v, g_norm)
    causal = jnp.tril(jnp.ones((SGU_CHUNK, SGU_CHUNK), dtype=bool))
    w = jnp.where(causal[None], w_s, 0).astype(v.dtype)
    vc = v.reshape(bsz, seq // SGU_CHUNK, SGU_CHUNK, SGU_GROUPS, SGU_GW)
    mixed = jnp.einsum('gts,bcsgd->bctgd', w, vc) + b_s.T[None, None, :, :, None]
    return u * mixed.reshape(bsz, seq, BRANCH_W)


def causal_depthwise_conv(x, w, b):
    ch = x.shape[-1]
    y = lax.conv_general_dilated(x, w[:, None, :].astype(x.dtype), window_strides=(1,), padding=[(CONV_W - 1, 0)], dimension_numbers=('NWC', 'WIO', 'NWC'), feature_group_count=ch)
    return y + b


def rg_lru(x, wa, ba, wx, bx, lam):
    bsz, seq, width = x.shape
    xh = x.reshape(bsz, seq, LRU_HEADS, LRU_HW)
    r = jax.nn.sigmoid(jnp.einsum('bshi,hio->bsho', xh, wa).reshape(bsz, seq, width) + ba)
    i = jax.nn.sigmoid(jnp.einsum('bshi,hio->bsho', xh, wx).reshape(bsz, seq, width) + bx)
    log_a = (-LRU_C * r.astype(jnp.float32)) * jax.nn.softplus(-lam.astype(jnp.float32))
    a = jnp.exp(log_a)
    mult = jnp.sqrt(-jnp.expm1(2.0 * log_a))
    inp = (x * i).astype(jnp.float32) * mult

    def combine(left, right):
        a1, b1 = left
        a2, b2 = right
        return a1 * a2, a2 * b1 + b2

    _, h = lax.associative_scan(combine, (a, inp), axis=1)
    return h.astype(x.dtype)


def swa_sink_attention(q, k, v, sinks):
    bsz, seq, nh, dh = q.shape
    grp = nh // SWA_KV
    nb = seq // BLK
    kp = jnp.pad(k, ((0, 0), (BLK, 0), (0, 0), (0, 0)))
    vp = jnp.pad(v, ((0, 0), (BLK, 0), (0, 0), (0, 0)))
    slopes = alibi_slopes(nh).reshape(SWA_KV, grp)
    sink = sinks.astype(jnp.float32).reshape(SWA_KV, grp)
    s_idx = jnp.arange(2 * BLK)[None, :]
    dist = (jnp.arange(BLK)[:, None] + BLK) - s_idx
    in_win = (dist >= 0) & (dist < SWA_WINDOW)
    alibi = slopes[:, :, None, None] * dist.astype(jnp.float32)
    scale = dh ** -0.5

    def block(i):
        qb = lax.dynamic_slice_in_dim(q, i * BLK, BLK, axis=1).reshape(bsz, BLK, SWA_KV, grp, dh)
        kb = lax.dynamic_slice_in_dim(kp, i * BLK, 2 * BLK, axis=1)
        vb = lax.dynamic_slice_in_dim(vp, i * BLK, 2 * BLK, axis=1)
        s = jnp.einsum('bqkgd,bskd->bkgqs', qb, kb).astype(jnp.float32) * scale - alibi
        valid = in_win & (s_idx + (i - 1) * BLK >= 0)
        s = jnp.where(valid, s, -jnp.inf)
        sink_col = jnp.broadcast_to(sink[None, :, :, None, None], s.shape[:-1] + (1,))
        p = jax.nn.softmax(jnp.concatenate([s, sink_col], axis=-1), axis=-1)[..., :-1]
        o = jnp.einsum('bkgqs,bskd->bqkgd', p.astype(vb.dtype), vb)
        return o.reshape(bsz, BLK, nh * dh)

    out = lax.map(block, jnp.arange(nb))
    return out.transpose(1, 0, 2, 3).reshape(bsz, seq, nh * dh)


def forgetting_attention(q, k, v, f_logit):
    bsz, seq, nh, dh = q.shape
    nb = seq // BLK
    cum = jnp.cumsum(jax.nn.log_sigmoid(f_logit.astype(jnp.float32)), axis=1).transpose(0, 2, 1)
    k_pos = jnp.arange(seq)
    scale = dh ** -0.5

    def block(i):
        qb = lax.dynamic_slice_in_dim(q, i * BLK, BLK, axis=1)
        cq = lax.dynamic_slice_in_dim(cum, i * BLK, BLK, axis=2)
        s = jnp.einsum('bqhd,bkhd->bhqk', qb, k).astype(jnp.float32) * scale + cq[..., None] - cum[:, :, None, :]
        q_pos = i * BLK + jnp.arange(BLK)
        s = jnp.where(k_pos[None, :] <= q_pos[:, None], s, -jnp.inf)
        p = jax.nn.softmax(s, axis=-1)
        o = jnp.einsum('bhqk,bkhd->bqhd', p.astype(v.dtype), v)
        return o.reshape(bsz, BLK, nh * dh)

    out = lax.map(block, jnp.arange(nb))
    return out.transpose(1, 0, 2, 3).reshape(bsz, seq, nh * dh)


def hybrid_mixer(xn, w_in, sgu_g, sgu_w, sgu_b, conv_w, conv_b, rg_wa, rg_ba, rg_wx, rg_bx, rg_lambda, swa_sinks, fox_bf, w_branch, w_gate, b_gate, w_out):
    bsz, seq, _ = xn.shape
    proj = xn @ w_in
    a_u, a_v, b_x, b_y, c_q, c_k, c_v, d_q, d_k, d_v, d_f = jnp.split(proj, np.cumsum(IN_SIZES)[:-1].tolist(), axis=-1)
    o_a = sgu_mixer(jax.nn.gelu(a_u), jax.nn.gelu(a_v), sgu_g, sgu_w, sgu_b)
    o_b = rg_lru(causal_depthwise_conv(b_x, conv_w, conv_b), rg_wa, rg_ba, rg_wx, rg_bx, rg_lambda) * jax.nn.gelu(b_y)
    o_c = swa_sink_attention(c_q.reshape(bsz, seq, SWA_HEADS, HEAD_DIM), c_k.reshape(bsz, seq, SWA_KV, HEAD_DIM), c_v.reshape(bsz, seq, SWA_KV, HEAD_DIM), swa_sinks)
    o_d = forgetting_attention(d_q.reshape(bsz, seq, FOX_HEADS, HEAD_DIM), d_k.reshape(bsz, seq, FOX_HEADS, HEAD_DIM), d_v.reshape(bsz, seq, FOX_HEADS, HEAD_DIM), d_f + fox_bf)
    merged = jnp.zeros_like(xn)
    for br, o in enumerate((o_a, o_b, o_c, o_d)):
        gate = jax.nn.sigmoid(xn @ w_gate[br] + b_gate[br])
        merged = merged + gate * (o @ w_branch[br])
    return merged @ w_out


def memory_cross_attention(hn, mem, g_mem, wq, wkv, wo):
    bsz, seq, _ = hn.shape
    m_len = mem.shape[1]
    q = (hn @ wq).reshape(bsz, seq, X_HEADS, X_HEAD_DIM)
    k, v = jnp.split(rms_norm(mem, g_mem) @ wkv, 2, axis=-1)
    k = k.reshape(bsz, m_len, X_HEADS, X_HEAD_DIM)
    v = v.reshape(bsz, m_len, X_HEADS, X_HEAD_DIM)
    s = jnp.einsum('bshd,bmhd->bhsm', q, k).astype(jnp.float32) * (X_HEAD_DIM ** -0.5)
    p = jax.nn.softmax(s, axis=-1)
    o = jnp.einsum('bhsm,bmhd->bshd', p.astype(v.dtype), v).reshape(bsz, seq, X_HEADS * X_HEAD_DIM)
    return o @ wo


def swiglu(x, w13, w2):
    g, u = jnp.split(x @ w13, 2, axis=-1)
    return (jax.nn.silu(g) * u) @ w2


def moe_ffn(x, w_r, b_r, w13, w2):
    logits = (x @ w_r).astype(jnp.float32) + b_r.astype(jnp.float32)
    top_v, top_i = lax.top_k(logits, TOP_K)
    top_w = jax.nn.softmax(top_v, axis=-1)
    combine = jnp.einsum('bsk,bske->bse', top_w, jax.nn.one_hot(top_i, N_EXPERTS, dtype=jnp.float32)).astype(x.dtype)
    out = jnp.zeros_like(x)
    for e in range(N_EXPERTS):
        out = out + combine[..., e:e + 1] * swiglu(x, w13[e], w2[e])
    return out


def setup_inputs(seed: int = 0) -> dict:
    key = jax.random.key(seed)
    ks = iter(jax.random.split(key, 48))
    f32 = jnp.float32
    L = DEPTH

    def nrm(shape, scale):
        return jax.random.normal(next(ks), shape, f32) * scale

    def gain(shape):
        return 1.0 + 0.05 * jax.random.normal(next(ks), shape, f32)

    x = nrm((BATCH, SEQ, D_MODEL), 1.0)
    mem = nrm((BATCH, MEM_LEN, D_MODEL), 1.0)
    norm_mix = gain((L, D_MODEL))
    w_in = nrm((L, D_MODEL, D_IN), D_MODEL ** -0.5)
    sgu_g = gain((L, BRANCH_W))
    sgu_w = nrm((L, SGU_GROUPS, SGU_CHUNK, SGU_CHUNK), SGU_CHUNK ** -0.5)
    sgu_b = 1.0 + nrm((L, SGU_GROUPS, SGU_CHUNK), 0.1)
    conv_w = nrm((L, CONV_W, LRU_W), CONV_W ** -0.5)
    conv_b = nrm((L, LRU_W), 0.01)
    rg_wa = nrm((L, LRU_HEADS, LRU_HW, LRU_HW), LRU_HW ** -0.5)
    rg_ba = nrm((L, LRU_W), 0.1)
    rg_wx = nrm((L, LRU_HEADS, LRU_HW, LRU_HW), LRU_HW ** -0.5)
    rg_bx = nrm((L, LRU_W), 0.1)
    a_c = jax.random.uniform(next(ks), (L, LRU_W), f32, 0.9, 0.999)
    p_a = a_c ** (1.0 / LRU_C)
    rg_lambda = jnp.log(p_a) - jnp.log1p(-p_a)
    swa_sinks = nrm((L, SWA_HEADS), 0.5)
    fox_bf = jax.random.uniform(next(ks), (L, FOX_HEADS), f32, 2.0, 4.0)
    w_branch = nrm((L, N_BRANCH, BRANCH_W, D_MODEL), BRANCH_W ** -0.5)
    w_gate = nrm((L, N_BRANCH, D_MODEL, D_MODEL), D_MODEL ** -0.5)
    b_gate = nrm((L, N_BRANCH, D_MODEL), 0.1)
    w_out = nrm((L, D_MODEL, D_MODEL), D_MODEL ** -0.5)
    norm_cross = gain((L, D_MODEL))
    norm_mem = gain((L, D_MODEL))
    wq_c = nrm((L, D_MODEL, X_HEADS * X_HEAD_DIM), D_MODEL ** -0.5)
    wkv_c = nrm((L, D_MODEL, 2 * X_HEADS * X_HEAD_DIM), D_MODEL ** -0.5)
    wo_c = nrm((L, X_HEADS * X_HEAD_DIM, D_MODEL), (X_HEADS * X_HEAD_DIM) ** -0.5)
    norm_ffn = gain((L, D_MODEL))
    dense_w13 = nrm((N_DENSE, D_MODEL, 2 * D_FF), D_MODEL ** -0.5)
    dense_w2 = nrm((N_DENSE, D_FF, D_MODEL), D_FF ** -0.5)
    router_w = nrm((N_MOE, D_MODEL, N_EXPERTS), D_MODEL ** -0.5)
    router_b = nrm((N_MOE, N_EXPERTS), 0.01)
    moe_w13 = nrm((N_MOE, N_EXPERTS, D_MODEL, 2 * D_FF), D_MODEL ** -0.5)
    moe_w2 = nrm((N_MOE, N_EXPERTS, D_FF, D_MODEL), D_FF ** -0.5)
    norm_final = gain((D_MODEL,))
    return dict(x=x, mem=mem, norm_mix=norm_mix, w_in=w_in, sgu_g=sgu_g, sgu_w=sgu_w, sgu_b=sgu_b, conv_w=conv_w, conv_b=conv_b, rg_wa=rg_wa, rg_ba=rg_ba, rg_wx=rg_wx, rg_bx=rg_bx, rg_lambda=rg_lambda, swa_sinks=swa_sinks, fox_bf=fox_bf, w_branch=w_branch, w_gate=w_gate, b_gate=b_gate, w_out=w_out, norm_cross=norm_cross, norm_mem=norm_mem, wq_c=wq_c, wkv_c=wkv_c, wo_c=wo_c, norm_ffn=norm_ffn, dense_w13=dense_w13, dense_w2=dense_w2, router_w=router_w, router_b=router_b, moe_w13=moe_w13, moe_w2=moe_w2, norm_final=norm_final)


def reference(x, mem, norm_mix, w_in, sgu_g, sgu_w, sgu_b, conv_w, conv_b, rg_wa, rg_ba, rg_wx, rg_bx, rg_lambda, swa_sinks, fox_bf, w_branch, w_gate, b_gate, w_out, norm_cross, norm_mem, wq_c, wkv_c, wo_c, norm_ffn, dense_w13, dense_w2, router_w, router_b, moe_w13, moe_w2, norm_final):
    h = x
    for l in range(DEPTH):
        h = h + hybrid_mixer(rms_norm(h, norm_mix[l]), w_in[l], sgu_g[l], sgu_w[l], sgu_b[l], conv_w[l], conv_b[l], rg_wa[l], rg_ba[l], rg_wx[l], rg_bx[l], rg_lambda[l], swa_sinks[l], fox_bf[l], w_branch[l], w_gate[l], b_gate[l], w_out[l])
        h = h + memory_cross_attention(rms_norm(h, norm_cross[l]), mem, norm_mem[l], wq_c[l], wkv_c[l], wo_c[l])
        hn = rms_norm(h, norm_ffn[l])
        if l % 2 == 0:
            h = h + swiglu(hn, dense_w13[l // 2], dense_w2[l // 2])
        else:
            h = h + moe_ffn(hn, router_w[l // 2], router_b[l // 2], moe_w13[l // 2], moe_w2[l // 2])
    return rms_norm(h, norm_final)
```

```python
import functools
import math

import numpy as np
import jax
import jax.numpy as jnp
from jax import lax
from jax.experimental import pallas as pl
from jax.experimental.pallas import tpu as pltpu

F32 = jnp.float32
BF = jnp.bfloat16

D_MODEL = 1024
BRANCH_W = 512
HEAD_DIM = 64
CHUNK = 128
SGU_GROUPS = 4
LRU_HEADS = 4
LRU_C = 8.0
CONV_W = 4
SWA_HEADS = 8
SWA_KV = 2
FOX_HEADS = 8
FOX_PAIRS = FOX_HEADS // 2
X_HEADS = 4
X_HEAD_DIM = 128
D_FF = 2816
N_EXPERTS = 8
EPS = 1e-6
NEG = -1e30
LANES = 128
VMEM_LIMIT = 56 * 1024 * 1024

COL_AU, COL_AV, COL_BX, COL_BY, COL_CQ, COL_DQ, COL_DK, COL_DV = range(8)
N_PROJ = 8 * BRANCH_W + 2 * SWA_KV * HEAD_DIM
FF_TILE = 1408


def _rms(x, g):
    return x * lax.rsqrt(jnp.mean(x * x, axis=-1, keepdims=True) + EPS) * g


def _sigmoid(x):
    return 1.0 / (1.0 + jnp.exp(-x))


def _gelu(x):
    return 0.5 * x * (1.0 + jnp.tanh(math.sqrt(2.0 / math.pi) * (x + 0.044715 * (x * x * x))))


def _dot(a, b):
    return jnp.dot(a, b, preferred_element_type=F32)


def _dot_nt(a, b):
    return lax.dot_general(a, b, (((1,), (1,)), ((), ())), preferred_element_type=F32)


def _shift_rows(x, d, fill):
    row = lax.broadcasted_iota(jnp.int32, x.shape, 0)
    return jnp.where(row >= d, pltpu.roll(x, d, 0), fill)


def _params(*sem):
    return pltpu.CompilerParams(dimension_semantics=sem, vmem_limit_bytes=VMEM_LIMIT)


def _const_spec(shape):
    nd = len(shape)
    return pl.BlockSpec(shape, lambda *_: (0,) * nd, pipeline_mode=pl.Buffered(1))


def _inproj_kernel(h_ref, g_ref, w_ref, wf_ref, proj_ref, f_ref):
    xn = _rms(h_ref[...], g_ref[...]).astype(BF)
    n_col = w_ref.shape[1]
    for c in range(0, n_col, BRANCH_W):
        w = min(BRANCH_W, n_col - c)
        proj_ref[:, c:c + w] = _dot(xn, w_ref[:, c:c + w]).astype(BF)
    f_ref[...] = _dot(xn, wf_ref[...])


def _inproj(h, g, w, wf, tm=512):
    t = h.shape[0]
    return pl.pallas_call(
        _inproj_kernel,
        grid=(t // tm,),
        in_specs=[pl.BlockSpec((tm, D_MODEL), lambda i: (i, 0)),
                  _const_spec((1, D_MODEL)),
                  _const_spec((D_MODEL, N_PROJ)),
                  _const_spec((D_MODEL, LANES))],
        out_specs=[pl.BlockSpec((tm, N_PROJ), lambda i: (i, 0)),
                   pl.BlockSpec((tm, LANES), lambda i: (i, 0))],
        out_shape=[jax.ShapeDtypeStruct((t, N_PROJ), BF), jax.ShapeDtypeStruct((t, LANES), F32)],
        compiler_params=_params("parallel"),
        name="inproj",
    )(h, g, w, wf)


def _sgu_kernel(u_ref, v_ref, g_ref, w_ref, bt_ref, o_ref):
    u = _gelu(u_ref[...].astype(F32))
    v = _gelu(v_ref[...].astype(F32))
    vn = _rms(v, g_ref[...]).astype(BF)
    row = lax.broadcasted_iota(jnp.int32, (CHUNK, CHUNK), 0)
    col = lax.broadcasted_iota(jnp.int32, (CHUNK, CHUNK), 1)
    tm = u.shape[0]
    gw = BRANCH_W // SGU_GROUPS
    for g in range(SGU_GROUPS):
        wg = jnp.where(col <= row, w_ref[g], 0.0).astype(BF)
        bg = bt_ref[:, g:g + 1]
        for c in range(tm // CHUNK):
            rs = slice(c * CHUNK, (c + 1) * CHUNK)
            cs = slice(g * gw, (g + 1) * gw)
            mixed = _dot(wg, vn[rs, cs]) + bg
            o_ref[rs, cs] = (u[rs, cs] * mixed).astype(BF)


def _sgu(proj, g, w, bt, tm=512):
    t = proj.shape[0]
    return pl.pallas_call(
        _sgu_kernel,
        grid=(t // tm,),
        in_specs=[pl.BlockSpec((tm, BRANCH_W), lambda i: (i, COL_AU)),
                  pl.BlockSpec((tm, BRANCH_W), lambda i: (i, COL_AV)),
                  _const_spec((1, BRANCH_W)),
                  _const_spec((SGU_GROUPS, CHUNK, CHUNK)),
                  _const_spec((CHUNK, SGU_GROUPS))],
        out_specs=pl.BlockSpec((tm, BRANCH_W), lambda i: (i, 0)),
        out_shape=jax.ShapeDtypeStruct((t, BRANCH_W), BF),
        compiler_params=_params("parallel"),
        name="sgu",
    )(proj, proj, g, w, bt)


def _rglru_kernel(x_ref, y_ref, cw_ref, cb_ref, wax_ref, ba_ref, bx_ref, lam_ref, o_ref, xs_ref, hc_ref):
    ts = x_ref.shape[0]
    hw = BRANCH_W // LRU_HEADS

    @pl.when(pl.program_id(1) == 0)
    def _():
        xs_ref[0:8, :] = jnp.zeros((8, BRANCH_W), F32)
        hc_ref[...] = jnp.zeros_like(hc_ref)

    x = x_ref[...].astype(F32)
    xs_ref[8:8 + ts, :] = x
    cw = cw_ref[...]
    xc = cb_ref[...] + cw[CONV_W - 1:CONV_W] * x
    for k in range(1, CONV_W):
        xc = xc + cw[CONV_W - 1 - k:CONV_W - k] * xs_ref[8 - k:8 - k + ts, :]
    xs_ref[0:8, :] = x[ts - 8:ts, :]

    xcb = xc.astype(BF)
    r_parts, i_parts = [], []
    for hd in range(LRU_HEADS):
        z = _dot(xcb[:, hd * hw:(hd + 1) * hw], wax_ref[hd])
        r_parts.append(z[:, :hw])
        i_parts.append(z[:, hw:])
    r = _sigmoid(jnp.concatenate(r_parts, axis=1) + ba_ref[...])
    gi = _sigmoid(jnp.concatenate(i_parts, axis=1) + bx_ref[...])
    nl = -lam_ref[...]
    softplus = jnp.maximum(nl, 0.0) + jnp.log1p(jnp.exp(-jnp.abs(nl)))
    log_a = (-LRU_C * r) * softplus
    a = jnp.exp(log_a)
    th = jnp.tanh(log_a)
    b = (xc * gi) * jnp.sqrt(-2.0 * th / (1.0 - th))

    d = 1
    while d < ts:
        a_sh = _shift_rows(a, d, 1.0)
        b_sh = _shift_rows(b, d, 0.0)
        b = b + a * b_sh
        a = a * a_sh
        d *= 2
    h = a * hc_ref[...] + b
    hc_ref[...] = h[ts - 1:ts, :]
    o_ref[...] = (h * _gelu(y_ref[...].astype(F32))).astype(BF)


def _rglru(proj, bsz, cw, cb, wax, ba, bx, lam, ts=256):
    t = proj.shape[0]
    ns = t // bsz // ts
    return pl.pallas_call(
        _rglru_kernel,
        grid=(bsz, ns),
        in_specs=[pl.BlockSpec((ts, BRANCH_W), lambda b, j: (b * ns + j, COL_BX)),
                  pl.BlockSpec((ts, BRANCH_W), lambda b, j: (b * ns + j, COL_BY)),
                  _const_spec((CONV_W, BRANCH_W)),
                  _const_spec((1, BRANCH_W)),
                  _const_spec((LRU_HEADS, BRANCH_W // LRU_HEADS, 2 * BRANCH_W // LRU_HEADS)),
                  _const_spec((1, BRANCH_W)),
                  _const_spec((1, BRANCH_W)),
                  _const_spec((1, BRANCH_W))],
        out_specs=pl.BlockSpec((ts, BRANCH_W), lambda b, j: (b * ns + j, 0)),
        out_shape=jax.ShapeDtypeStruct((t, BRANCH_W), BF),
        scratch_shapes=[pltpu.VMEM((ts + 8, BRANCH_W), F32), pltpu.VMEM((1, BRANCH_W), F32)],
        compiler_params=_params("parallel", "arbitrary"),
        name="rglru",
    )(proj, proj, cw, cb, wax, ba, bx, lam)


def _swa_kernel(q_ref, kv_ref, kvp_ref, sink_ref, o_ref):
    tq = q_ref.shape[0]
    first_key = jnp.where(pl.program_id(1) == 0, CHUNK, 0)
    kv_all = jnp.concatenate([kvp_ref[...], kv_ref[...]], axis=0)
    q = q_ref[...]
    qi = lax.broadcasted_iota(jnp.int32, (CHUNK, 2 * CHUNK), 0)
    sj = lax.broadcasted_iota(jnp.int32, (CHUNK, 2 * CHUNK), 1)
    dist = qi + CHUNK - sj
    in_win = (dist >= 0) & (dist < CHUNK)
    distf = dist.astype(F32)
    grp = SWA_HEADS // SWA_KV
    kvw = SWA_KV * HEAD_DIM
    for qb in range(tq // CHUNK):
        kvb = kv_all[qb * CHUNK:(qb + 2) * CHUNK]
        if qb == 0:
            valid = in_win & (sj >= first_key)
        else:
            valid = in_win
        outs = []
        for kh in range(SWA_KV):
            k = kvb[:, kh * HEAD_DIM:(kh + 1) * HEAD_DIM]
            v = kvb[:, kvw + kh * HEAD_DIM:kvw + (kh + 1) * HEAD_DIM]
            for g in range(grp):
                hh = kh * grp + g
                slope = 2.0 ** (-(8.0 / SWA_HEADS) * (hh + 1))
                qh = q[qb * CHUNK:(qb + 1) * CHUNK, hh * HEAD_DIM:(hh + 1) * HEAD_DIM]
                s = _dot_nt(qh, k) * (HEAD_DIM ** -0.5) - slope * distf
                s = jnp.where(valid, s, NEG)
                sink = sink_ref[hh]
                m = jnp.maximum(jnp.max(s, axis=-1, keepdims=True), sink)
                p = jnp.exp(s - m)
                denom = jnp.sum(p, axis=-1, keepdims=True) + jnp.exp(sink - m)
                outs.append(_dot(p.astype(BF), v) / denom)
        o_ref[qb * CHUNK:(qb + 1) * CHUNK, :] = jnp.concatenate(outs, axis=1).astype(BF)


def _swa(proj, bsz, sinks, tq=512):
    t = proj.shape[0]
    seq = t // bsz
    nq = seq // tq
    per = tq // CHUNK
    kv_blk = 8 * BRANCH_W // (2 * SWA_KV * HEAD_DIM)
    return pl.pallas_call(
        _swa_kernel,
        grid=(bsz, nq),
        in_specs=[pl.BlockSpec((tq, BRANCH_W), lambda b, i: (b * nq + i, COL_CQ)),
                  pl.BlockSpec((tq, 2 * SWA_KV * HEAD_DIM), lambda b, i: (b * nq + i, kv_blk)),
                  pl.BlockSpec((CHUNK, 2 * SWA_KV * HEAD_DIM),
                               lambda b, i: (b * (seq // CHUNK) + jnp.maximum(i * per - 1, 0), kv_blk)),
                  pl.BlockSpec(memory_space=pltpu.SMEM)],
        out_specs=pl.BlockSpec((tq, BRANCH_W), lambda b, i: (b * nq + i, 0)),
        out_shape=jax.ShapeDtypeStruct((t, BRANCH_W), BF),
        compiler_params=_params("parallel", "parallel"),
        name="swa",
    )(proj, proj, proj, sinks)


def _aug_tables():
    eq = np.zeros((3, LANES, FOX_PAIRS * LANES), np.float32)
    ek = np.zeros((3, LANES, FOX_PAIRS * LANES), np.float32)
    oq = np.zeros((1, FOX_PAIRS * LANES), np.float32)
    ok = np.zeros((1, FOX_PAIRS * LANES), np.float32)
    for h in range(FOX_HEADS):
        base = (h // 2) * LANES + 6 * (h % 2)
        for s in range(3):
            eq[s, h, base + s] = 1.0
            ek[s, h, base + 3 + s] = -1.0
            ok[0, base + s] = 1.0
            oq[0, base + 3 + s] = 1.0
    return eq, ek, oq, ok


def _fox_prep_kernel(q_ref, k_ref, f_ref, bf_ref, eq_ref, ek_ref, oq_ref, ok_ref, qp_ref, kp_ref, cum_ref):
    ts = q_ref.shape[0]

    @pl.when(pl.program_id(1) == 0)
    def _():
        cum_ref[...] = jnp.zeros_like(cum_ref)

    z = f_ref[...] + bf_ref[...]
    c = jnp.minimum(z, 0.0) - jnp.log1p(jnp.exp(-jnp.abs(z)))
    d = 1
    while d < ts:
        c = c + _shift_rows(c, d, 0.0)
        d *= 2
    c = c + cum_ref[...]
    cum_ref[...] = c[ts - 1:ts, :]
    c1 = c.astype(BF)
    r1 = c - c1.astype(F32)
    c2 = r1.astype(BF)
    c3 = (r1 - c2.astype(F32)).astype(BF)
    augq = _dot(c1, eq_ref[0]) + _dot(c2, eq_ref[1]) + _dot(c3, eq_ref[2]) + oq_ref[...]
    augk = _dot(c1, ek_ref[0]) + _dot(c2, ek_ref[1]) + _dot(c3, ek_ref[2]) + ok_ref[...]
    q = (q_ref[...].astype(F32) * (HEAD_DIM ** -0.5)).astype(BF)
    k = k_ref[...]
    for p in range(FOX_PAIRS):
        qp_ref[:, 2 * p * LANES:(2 * p + 1) * LANES] = q[:, p * LANES:(p + 1) * LANES]
        qp_ref[:, (2 * p + 1) * LANES:(2 * p + 2) * LANES] = augq[:, p * LANES:(p + 1) * LANES].astype(BF)
        kp_ref[:, 2 * p * LANES:(2 * p + 1) * LANES] = k[:, p * LANES:(p + 1) * LANES]
        kp_ref[:, (2 * p + 1) * LANES:(2 * p + 2) * LANES] = augk[:, p * LANES:(p + 1) * LANES].astype(BF)


def _fox_prep(proj, f, bsz, bf_pad, ts=512):
    t = proj.shape[0]
    ns = t // bsz // ts
    eq, ek, oq, ok = _aug_tables()
    wide = 2 * FOX_PAIRS * LANES
    return pl.pallas_call(
        _fox_prep_kernel,
        grid=(bsz, ns),
        in_specs=[pl.BlockSpec((ts, BRANCH_W), lambda b, j: (b * ns + j, COL_DQ)),
                  pl.BlockSpec((ts, BRANCH_W), lambda b, j: (b * ns + j, COL_DK)),
                  pl.BlockSpec((ts, LANES), lambda b, j: (b * ns + j, 0)),
                  _const_spec((1, LANES)),
                  _const_spec(eq.shape), _const_spec(ek.shape), _const_spec(oq.shape), _const_spec(ok.shape)],
        out_specs=[pl.BlockSpec((ts, wide), lambda b, j: (b * ns + j, 0)),
                   pl.BlockSpec((ts, wide), lambda b, j: (b * ns + j, 0))],
        out_shape=[jax.ShapeDtypeStruct((t, wide), BF), jax.ShapeDtypeStruct((t, wide), BF)],
        scratch_shapes=[pltpu.VMEM((1, LANES), F32)],
        compiler_params=_params("parallel", "arbitrary"),
        name="fox_prep",
    )(proj, proj, f, bf_pad, jnp.asarray(eq, BF), jnp.asarray(ek, BF), jnp.asarray(oq), jnp.asarray(ok))


def _fox_kernel(q_ref, k_ref, v_ref, o_ref, m_ref, l_ref, acc_ref):
    tq = q_ref.shape[0]
    tk = tq
    qi = pl.program_id(2)
    lane = lax.broadcasted_iota(jnp.int32, (tq, 2 * LANES), 1)
    q = q_ref[...]
    zero = jnp.zeros_like(q)
    in0 = (lane < HEAD_DIM) | ((lane >= LANES) & (lane < LANES + 6))
    in1 = ((lane >= HEAD_DIM) & (lane < LANES)) | ((lane >= LANES + 6) & (lane < LANES + 12))
    qh = (jnp.where(in0, q, zero), jnp.where(in1, q, zero))
    m_ref[...] = jnp.full(m_ref.shape, NEG, F32)
    l_ref[...] = jnp.zeros(l_ref.shape, F32)
    acc_ref[...] = jnp.zeros(acc_ref.shape, F32)

    def step(kt, diagonal):
        start = pl.multiple_of(kt * tk, tk)
        k = k_ref[pl.ds(start, tk), :]
        v = v_ref[pl.ds(start, tk), :]
        for h in range(2):
            s = _dot_nt(qh[h], k)
            if diagonal:
                row = lax.broadcasted_iota(jnp.int32, (tq, tk), 0)
                col = lax.broadcasted_iota(jnp.int32, (tq, tk), 1)
                s = jnp.where(col <= row, s, NEG)
            m_old = m_ref[h]
            m_new = jnp.maximum(m_old, jnp.max(s, axis=-1, keepdims=True))
            alpha = jnp.exp(m_old - m_new)
            p = jnp.exp(s - m_new[:, :1])
            l_ref[h] = alpha * l_ref[h] + jnp.sum(p, axis=-1, keepdims=True)
            acc_ref[h] = alpha * acc_ref[h] + _dot(p.astype(BF), v)
            m_ref[h] = m_new

    def body(kt, carry):
        step(kt, False)
        return carry

    lax.fori_loop(0, qi, body, 0)
    step(qi, True)
    lane_o = lax.broadcasted_iota(jnp.int32, (tq, LANES), 1)
    o_ref[...] = jnp.where(lane_o < HEAD_DIM, acc_ref[0] / l_ref[0], acc_ref[1] / l_ref[1]).astype(BF)


def _fox(qp, kp, proj, bsz, tq=512):
    t = qp.shape[0]
    seq = t // bsz
    nq = seq // tq
    v_blk0 = COL_DV * BRANCH_W // LANES
    return pl.pallas_call(
        _fox_kernel,
        grid=(bsz, FOX_PAIRS, nq),
        in_specs=[pl.BlockSpec((tq, 2 * LANES), lambda b, p, i: (b * nq + i, p)),
                  pl.BlockSpec((seq, 2 * LANES), lambda b, p, i: (b, p)),
                  pl.BlockSpec((seq, LANES), lambda b, p, i: (b, v_blk0 + p))],
        out_specs=pl.BlockSpec((tq, LANES), lambda b, p, i: (b * nq + i, p)),
        out_shape=jax.ShapeDtypeStruct((t, BRANCH_W), BF),
        scratch_shapes=[pltpu.VMEM((2, tq, LANES), F32), pltpu.VMEM((2, tq, LANES), F32),
                        pltpu.VMEM((2, tq, LANES), F32)],
        compiler_params=_params("parallel", "parallel", "arbitrary"),
        name="fox",
    )(qp, kp, proj)


def _merge_kernel(h_ref, g_ref, oa_ref, ob_ref, oc_ref, od_ref, wg_ref, bg_ref, wb_ref, wo_ref, out_ref):
    h = h_ref[...]
    xn = _rms(h, g_ref[...]).astype(BF)
    merged = None
    for br, o_ref in enumerate((oa_ref, ob_ref, oc_ref, od_ref)):
        gate = _sigmoid(_dot(xn, wg_ref[br]) + bg_ref[br])
        term = gate * _dot(o_ref[...], wb_ref[br])
        merged = term if merged is None else merged + term
    out_ref[...] = h + _dot(merged.astype(BF), wo_ref[...])


def _merge(h, g, oa, ob, oc, od, wg, bg, wb, wo, tm=512):
    t = h.shape[0]
    row = lambda i: (i, 0)
    return pl.pallas_call(
        _merge_kernel,
        grid=(t // tm,),
        in_specs=[pl.BlockSpec((tm, D_MODEL), row),
                  _const_spec((1, D_MODEL)),
                  pl.BlockSpec((tm, BRANCH_W), row), pl.BlockSpec((tm, BRANCH_W), row),
                  pl.BlockSpec((tm, BRANCH_W), row), pl.BlockSpec((tm, BRANCH_W), row),
                  _const_spec((4, D_MODEL, D_MODEL)),
                  _const_spec((4, 1, D_MODEL)),
                  _const_spec((4, BRANCH_W, D_MODEL)),
                  _const_spec((D_MODEL, D_MODEL))],
        out_specs=pl.BlockSpec((tm, D_MODEL), row),
        out_shape=jax.ShapeDtypeStruct((t, D_MODEL), F32),
        compiler_params=_params("parallel"),
        name="merge",
    )(h, g, oa, ob, oc, od, wg, bg, wb, wo)


def _memkv_kernel(mem_ref, g_ref, w_ref, kv_ref):
    mn = _rms(mem_ref[...], g_ref[...]).astype(BF)
    kv_ref[...] = _dot(mn, w_ref[...]).astype(BF)


def _memkv(mem2, g, w, m_len):
    n = mem2.shape[0]
    width = 2 * X_HEADS * X_HEAD_DIM
    return pl.pallas_call(
        _memkv_kernel,
        grid=(n // m_len,),
        in_specs=[pl.BlockSpec((m_len, D_MODEL), lambda b: (b, 0)),
                  _const_spec((1, D_MODEL)),
                  _const_spec((D_MODEL, width))],
        out_specs=pl.BlockSpec((m_len, width), lambda b: (b, 0)),
        out_shape=jax.ShapeDtypeStruct((n, width), BF),
        compiler_params=_params("parallel"),
        name="memkv",
    )(mem2, g, w)


def _cross_kernel(h_ref, g_ref, wq_ref, kv_ref, wo_ref, out_ref):
    h = h_ref[...]
    hn = _rms(h, g_ref[...]).astype(BF)
    q = _dot(hn, wq_ref[...]).astype(BF)
    kv = kv_ref[...]
    width = X_HEADS * X_HEAD_DIM
    outs = []
    for hd in range(X_HEADS):
        cs = slice(hd * X_HEAD_DIM, (hd + 1) * X_HEAD_DIM)
        s = _dot_nt(q[:, cs], kv[:, cs]) * (X_HEAD_DIM ** -0.5)
        m = jnp.max(s, axis=-1, keepdims=True)
        p = jnp.exp(s - m)
        denom = jnp.sum(p, axis=-1, keepdims=True)
        v = kv[:, width + hd * X_HEAD_DIM:width + (hd + 1) * X_HEAD_DIM]
        outs.append((_dot(p.astype(BF), v) / denom).astype(BF))
    o = jnp.concatenate(outs, axis=1)
    out_ref[...] = h + _dot(o, wo_ref[...])


def _cross(h, g, wq, kv, wo, bsz, m_len, tm=512):
    t = h.shape[0]
    per = t // bsz // tm
    width = X_HEADS * X_HEAD_DIM
    return pl.pallas_call(
        _cross_kernel,
        grid=(bsz, per),
        in_specs=[pl.BlockSpec((tm, D_MODEL), lambda b, i: (b * per + i, 0)),
                  _const_spec((1, D_MODEL)),
                  _const_spec((D_MODEL, width)),
                  pl.BlockSpec((m_len, 2 * width), lambda b, i: (b, 0)),
                  _const_spec((width, D_MODEL))],
        out_specs=pl.BlockSpec((tm, D_MODEL), lambda b, i: (b * per + i, 0)),
        out_shape=jax.ShapeDtypeStruct((t, D_MODEL), F32),
        compiler_params=_params("parallel", "parallel"),
        name="cross",
    )(h, g, wq, kv, wo)


def _ffn_kernel(h_ref, g_ref, w1_ref, w3_ref, w2_ref, out_ref, hn_ref, acc_ref):
    j = pl.program_id(1)

    @pl.when(j == 0)
    def _():
        hn_ref[...] = _rms(h_ref[...], g_ref[...]).astype(BF)
        acc_ref[...] = jnp.zeros_like(acc_ref)

    hn = hn_ref[...]
    gate = _dot(hn, w1_ref[...])
    up = _dot(hn, w3_ref[...])
    act = (gate * _sigmoid(gate) * up).astype(BF)
    acc_ref[...] += _dot(act, w2_ref[...])

    @pl.when(j == pl.num_programs(1) - 1)
    def _():
        out_ref[...] = h_ref[...] + acc_ref[...]


def _ffn(h, g, w13, w2, tm=512, tf=FF_TILE):
    t = h.shape[0]
    nf = D_FF // tf
    return pl.pallas_call(
        _ffn_kernel,
        grid=(t // tm, nf),
        in_specs=[pl.BlockSpec((tm, D_MODEL), lambda i, j: (i, 0)),
                  _const_spec((1, D_MODEL)),
                  pl.BlockSpec((D_MODEL, tf), lambda i, j: (0, j)),
                  pl.BlockSpec((D_MODEL, tf), lambda i, j: (0, nf + j)),
                  pl.BlockSpec((tf, D_MODEL), lambda i, j: (j, 0))],
        out_specs=pl.BlockSpec((tm, D_MODEL), lambda i, j: (i, 0)),
        out_shape=jax.ShapeDtypeStruct((t, D_MODEL), F32),
        scratch_shapes=[pltpu.VMEM((tm, D_MODEL), BF), pltpu.VMEM((tm, D_MODEL), F32)],
        compiler_params=_params("parallel", "arbitrary"),
        name="ffn",
    )(h, g, w13, w13, w2)


def _router_kernel(h_ref, g_ref, wr_ref, br_ref, comb_ref):
    hn = _rms(h_ref[...], g_ref[...])
    logits = jnp.dot(hn, wr_ref[...], preferred_element_type=F32, precision=lax.Precision.HIGHEST) + br_ref[...]
    lane = lax.broadcasted_iota(jnp.int32, logits.shape, 1)
    logits = jnp.where(lane < N_EXPERTS, logits, NEG)
    v1 = jnp.max(logits, axis=-1, keepdims=True)
    i1 = jnp.min(jnp.where(logits == v1, lane, LANES), axis=-1, keepdims=True)
    rest = jnp.where(lane == i1, NEG, logits)
    v2 = jnp.max(rest, axis=-1, keepdims=True)
    i2 = jnp.min(jnp.where(rest == v2, lane, LANES), axis=-1, keepdims=True)
    e2 = jnp.exp(v2 - v1)
    w1 = 1.0 / (1.0 + e2)
    w2 = e2 / (1.0 + e2)
    comb_ref[...] = jnp.where(lane == i1, w1, 0.0) + jnp.where(lane == i2, w2, 0.0)


def _router(h, g, wr, br, tm=512):
    t = h.shape[0]
    return pl.pallas_call(
        _router_kernel,
        grid=(t // tm,),
        in_specs=[pl.BlockSpec((tm, D_MODEL), lambda i: (i, 0)),
                  _const_spec((1, D_MODEL)),
                  _const_spec((D_MODEL, LANES)),
                  _const_spec((1, LANES))],
        out_specs=pl.BlockSpec((tm, LANES), lambda i: (i, 0)),
        out_shape=jax.ShapeDtypeStruct((t, LANES), F32),
        compiler_params=_params("parallel"),
        name="router",
    )(h, g, wr, br)


def _moe_kernel(h_ref, g_ref, comb_ref, w1_ref, w3_ref, w2_ref, gf_ref, out_ref, hn_ref, acc_ref, tot_ref):
    e = pl.program_id(1)
    j = pl.program_id(2)
    last_j = pl.num_programs(2) - 1

    @pl.when((e == 0) & (j == 0))
    def _():
        hn_ref[...] = _rms(h_ref[...], g_ref[...]).astype(BF)
        tot_ref[...] = jnp.zeros_like(tot_ref)

    @pl.when(j == 0)
    def _():
        acc_ref[...] = jnp.zeros_like(acc_ref)

    hn = hn_ref[...]
    gate = _dot(hn, w1_ref[0])
    up = _dot(hn, w3_ref[0])
    act = (gate * _sigmoid(gate) * up).astype(BF)
    acc_ref[...] += _dot(act, w2_ref[0])

    @pl.when(j == last_j)
    def _():
        comb = comb_ref[...]
        lane = lax.broadcasted_iota(jnp.int32, comb.shape, 1)
        ce = jnp.sum(jnp.where(lane == e, comb, 0.0), axis=-1, keepdims=True)
        tot_ref[...] += ce * acc_ref[...]

    @pl.when((e == pl.num_programs(1) - 1) & (j == last_j))
    def _():
        out_ref[...] = _rms(h_ref[...] + tot_ref[...], gf_ref[...])


def _moe(h, g, comb, w13, w2, g_final, tm=512, tf=FF_TILE):
    t = h.shape[0]
    nf = D_FF // tf
    return pl.pallas_call(
        _moe_kernel,
        grid=(t // tm, N_EXPERTS, nf),
        in_specs=[pl.BlockSpec((tm, D_MODEL), lambda i, e, j: (i, 0)),
                  _const_spec((1, D_MODEL)),
                  pl.BlockSpec((tm, LANES), lambda i, e, j: (i, 0)),
                  pl.BlockSpec((1, D_MODEL, tf), lambda i, e, j: (e, 0, j)),
                  pl.BlockSpec((1, D_MODEL, tf), lambda i, e, j: (e, 0, nf + j)),
                  pl.BlockSpec((1, tf, D_MODEL), lambda i, e, j: (e, j, 0)),
                  _const_spec((1, D_MODEL))],
        out_specs=pl.BlockSpec((tm, D_MODEL), lambda i, e, j: (i, 0)),
        out_shape=jax.ShapeDtypeStruct((t, D_MODEL), F32),
        scratch_shapes=[pltpu.VMEM((tm, D_MODEL), BF), pltpu.VMEM((tm, D_MODEL), F32),
                        pltpu.VMEM((tm, D_MODEL), F32)],
        compiler_params=_params("parallel", "arbitrary", "arbitrary"),
        name="moe",
    )(h, g, comb, w13, w13, w2, g_final)


def _pack_w_in(w_in):
    cuts = np.cumsum((512, 512, 512, 512, 512, 128, 128, 512, 512, 512, 8))[:-1].tolist()
    a_u, a_v, b_x, b_y, c_q, c_k, c_v, d_q, d_k, d_v, d_f = jnp.split(w_in, cuts, axis=-1)
    w = jnp.concatenate([a_u, a_v, b_x, b_y, c_q, d_q, d_k, d_v, c_k, c_v], axis=-1).astype(BF)
    wf = jnp.pad(d_f, ((0, 0), (0, LANES - FOX_HEADS))).astype(BF)
    return w, wf


def _row(v):
    return v.reshape(1, -1)


def _hybrid_mixer(h, bsz, norm_mix, w_in, sgu_g, sgu_w, sgu_b, conv_w, conv_b, rg_wa, rg_ba, rg_wx, rg_bx,
                  rg_lambda, swa_sinks, fox_bf, w_branch, w_gate, b_gate, w_out):
    w, wf = _pack_w_in(w_in)
    proj, f = _inproj(h, _row(norm_mix), w, wf)
    o_a = _sgu(proj, _row(sgu_g), sgu_w, sgu_b.T)
    wax = jnp.concatenate([rg_wa, rg_wx], axis=-1).astype(BF)
    o_b = _rglru(proj, bsz, conv_w, _row(conv_b), wax, _row(rg_ba), _row(rg_bx), _row(rg_lambda))
    o_c = _swa(proj, bsz, swa_sinks)
    bf_pad = jnp.pad(fox_bf, (0, LANES - FOX_HEADS)).reshape(1, LANES)
    qp, kp = _fox_prep(proj, f, bsz, bf_pad)
    o_d = _fox(qp, kp, proj, bsz)
    return _merge(h, _row(norm_mix), o_a, o_b, o_c, o_d, w_gate.astype(BF), b_gate[:, None, :],
                  w_branch.astype(BF), w_out.astype(BF))


def kernel(x, mem, norm_mix, w_in, sgu_g, sgu_w, sgu_b, conv_w, conv_b, rg_wa, rg_ba, rg_wx, rg_bx, rg_lambda, swa_sinks, fox_bf, w_branch, w_gate, b_gate, w_out, norm_cross, norm_mem, wq_c, wkv_c, wo_c, norm_ffn, dense_w13, dense_w2, router_w, router_b, moe_w13, moe_w2, norm_final):
    bsz, seq, d = x.shape
    m_len = mem.shape[1]
    depth = norm_mix.shape[0]
    h = x.reshape(bsz * seq, d)
    mem2 = mem.reshape(bsz * m_len, d)
    for l in range(depth):
        h = _hybrid_mixer(h, bsz, norm_mix[l], w_in[l], sgu_g[l], sgu_w[l], sgu_b[l], conv_w[l], conv_b[l],
                          rg_wa[l], rg_ba[l], rg_wx[l], rg_bx[l], rg_lambda[l], swa_sinks[l], fox_bf[l],
                          w_branch[l], w_gate[l], b_gate[l], w_out[l])
        kv = _memkv(mem2, _row(norm_mem[l]), wkv_c[l].astype(BF), m_len)
        h = _cross(h, _row(norm_cross[l]), wq_c[l].astype(BF), kv, wo_c[l].astype(BF), bsz, m_len)
        if l % 2 == 0:
            h = _ffn(h, _row(norm_ffn[l]), dense_w13[l // 2].astype(BF), dense_w2[l // 2].astype(BF))
        else:
            wr = jnp.pad(router_w[l // 2], ((0, 0), (0, LANES - N_EXPERTS)))
            br = jnp.pad(router_b[l // 2], (0, LANES - N_EXPERTS)).reshape(1, LANES)
            comb = _router(h, _row(norm_ffn[l]), wr, br)
            h = _moe(h, _row(norm_ffn[l]), comb, moe_w13[l // 2].astype(BF), moe_w2[l // 2].astype(BF),
                     _row(norm_final))
    return h.reshape(bsz, seq, d)
```

```python
import functools
import math

import numpy as np
import jax
import jax.numpy as jnp
from jax import lax
from jax.experimental import pallas as pl
from jax.experimental.pallas import tpu as pltpu

F32 = jnp.float32
BF = jnp.bfloat16

D_MODEL = 1024
BRANCH_W = 512
HEAD_DIM = 64
CHUNK = 128
SGU_GROUPS = 4
LRU_HEADS = 4
LRU_C = 8.0
CONV_W = 4
SWA_HEADS = 8
SWA_KV = 2
FOX_HEADS = 8
FOX_PAIRS = FOX_HEADS // 2
X_HEADS = 4
X_HEAD_DIM = 128
D_FF = 2816
N_EXPERTS = 8
EPS = 1e-6
NEG = -1e30
LANES = 128
VMEM_LIMIT = 56 * 1024 * 1024

COL_AU, COL_AV, COL_BX, COL_BY, COL_CQ, COL_DQ, COL_DK = range(7)
N_WIDE = 7
N_PROJ = N_WIDE * BRANCH_W + 2 * SWA_KV * HEAD_DIM
VT_ROWS = 80
VT_ALL = FOX_HEADS * VT_ROWS
FF_TILE = 1408


def _rms(x, g):
    return x * lax.rsqrt(jnp.mean(x * x, axis=-1, keepdims=True) + EPS) * g


def _sigmoid(x):
    return 1.0 / (1.0 + jnp.exp(-x))


def _gelu(x):
    return 0.5 * x * (1.0 + jnp.tanh(math.sqrt(2.0 / math.pi) * (x + 0.044715 * (x * x * x))))


def _dot(a, b):
    return jnp.dot(a, b, preferred_element_type=F32)


def _dot_nt(a, b):
    return lax.dot_general(a, b, (((1,), (1,)), ((), ())), preferred_element_type=F32)


def _shift_rows(x, d, fill):
    row = lax.broadcasted_iota(jnp.int32, x.shape, 0)
    return jnp.where(row >= d, pltpu.roll(x, d, 0), fill)


def _params(*sem):
    return pltpu.CompilerParams(dimension_semantics=sem, vmem_limit_bytes=VMEM_LIMIT)


def _const_spec(shape):
    nd = len(shape)
    return pl.BlockSpec(shape, lambda *_: (0,) * nd, pipeline_mode=pl.Buffered(1))


def _inproj_kernel(h_ref, g_ref, w_ref, wf_ref, wvt_ref, ones_ref, proj_ref, f_ref, vt_ref):
    xn = _rms(h_ref[...], g_ref[...]).astype(BF)
    n_col = w_ref.shape[1]
    for c in range(0, n_col, BRANCH_W):
        w = min(BRANCH_W, n_col - c)
        proj_ref[:, c:c + w] = _dot(xn, w_ref[:, c:c + w]).astype(BF)
    f_ref[...] = _dot(xn, wf_ref[...])
    ones = jnp.concatenate([ones_ref[...]] * (xn.shape[0] // LANES), axis=1)
    vt_ref[...] = (_dot_nt(wvt_ref[...], xn) + ones).astype(BF)


def _inproj(h, g, w, wf, wvt, ones, tm=512):
    t = h.shape[0]
    return pl.pallas_call(
        _inproj_kernel,
        grid=(t // tm,),
        in_specs=[pl.BlockSpec((tm, D_MODEL), lambda i: (i, 0)),
                  _const_spec((1, D_MODEL)),
                  _const_spec((D_MODEL, N_PROJ)),
                  _const_spec((D_MODEL, LANES)),
                  _const_spec((VT_ALL, D_MODEL)),
                  _const_spec((VT_ALL, LANES))],
        out_specs=[pl.BlockSpec((tm, N_PROJ), lambda i: (i, 0)),
                   pl.BlockSpec((tm, LANES), lambda i: (i, 0)),
                   pl.BlockSpec((VT_ALL, tm), lambda i: (0, i))],
        out_shape=[jax.ShapeDtypeStruct((t, N_PROJ), BF), jax.ShapeDtypeStruct((t, LANES), F32),
                   jax.ShapeDtypeStruct((VT_ALL, t), BF)],
        compiler_params=_params("parallel"),
        name="inproj",
    )(h, g, w, wf, wvt, ones)


def _sgu_kernel(u_ref, v_ref, g_ref, w_ref, bt_ref, o_ref):
    u = _gelu(u_ref[...].astype(F32))
    v = _gelu(v_ref[...].astype(F32))
    vn = _rms(v, g_ref[...]).astype(BF)
    row = lax.broadcasted_iota(jnp.int32, (CHUNK, CHUNK), 0)
    col = lax.broadcasted_iota(jnp.int32, (CHUNK, CHUNK), 1)
    tm = u.shape[0]
    gw = BRANCH_W // SGU_GROUPS
    for g in range(SGU_GROUPS):
        wg = jnp.where(col <= row, w_ref[g], 0.0).astype(BF)
        bg = bt_ref[:, g:g + 1]
        for c in range(tm // CHUNK):
            rs = slice(c * CHUNK, (c + 1) * CHUNK)
            cs = slice(g * gw, (g + 1) * gw)
            mixed = _dot(wg, vn[rs, cs]) + bg
            o_ref[rs, cs] = (u[rs, cs] * mixed).astype(BF)


def _sgu(proj, g, w, bt, tm=512):
    t = proj.shape[0]
    return pl.pallas_call(
        _sgu_kernel,
        grid=(t // tm,),
        in_specs=[pl.BlockSpec((tm, BRANCH_W), lambda i: (i, COL_AU)),
                  pl.BlockSpec((tm, BRANCH_W), lambda i: (i, COL_AV)),
                  _const_spec((1, BRANCH_W)),
                  _const_spec((SGU_GROUPS, CHUNK, CHUNK)),
                  _const_spec((CHUNK, SGU_GROUPS))],
        out_specs=pl.BlockSpec((tm, BRANCH_W), lambda i: (i, 0)),
        out_shape=jax.ShapeDtypeStruct((t, BRANCH_W), BF),
        compiler_params=_params("parallel"),
        name="sgu",
    )(proj, proj, g, w, bt)


def _rglru_kernel(x_ref, y_ref, cw_ref, cb_ref, wax_ref, ba_ref, bx_ref, lam_ref, o_ref, xs_ref, hc_ref):
    ts = x_ref.shape[0]
    hw = BRANCH_W // LRU_HEADS

    @pl.when(pl.program_id(1) == 0)
    def _():
        xs_ref[0:8, :] = jnp.zeros((8, BRANCH_W), F32)
        hc_ref[...] = jnp.zeros_like(hc_ref)

    x = x_ref[...].astype(F32)
    xs_ref[8:8 + ts, :] = x
    cw = cw_ref[...]
    xc = cb_ref[...] + cw[CONV_W - 1:CONV_W] * x
    for k in range(1, CONV_W):
        xc = xc + cw[CONV_W - 1 - k:CONV_W - k] * xs_ref[8 - k:8 - k + ts, :]
    xs_ref[0:8, :] = x[ts - 8:ts, :]

    xcb = xc.astype(BF)
    r_parts, i_parts = [], []
    for hd in range(LRU_HEADS):
        z = _dot(xcb[:, hd * hw:(hd + 1) * hw], wax_ref[hd])
        r_parts.append(z[:, :hw])
        i_parts.append(z[:, hw:])
    r = _sigmoid(jnp.concatenate(r_parts, axis=1) + ba_ref[...])
    gi = _sigmoid(jnp.concatenate(i_parts, axis=1) + bx_ref[...])
    nl = -lam_ref[...]
    softplus = jnp.maximum(nl, 0.0) + jnp.log1p(jnp.exp(-jnp.abs(nl)))
    log_a = (-LRU_C * r) * softplus
    a = jnp.exp(log_a)
    th = jnp.tanh(log_a)
    b = (xc * gi) * jnp.sqrt(-2.0 * th / (1.0 - th))

    d = 1
    while d < ts:
        a_sh = _shift_rows(a, d, 1.0)
        b_sh = _shift_rows(b, d, 0.0)
        b = b + a * b_sh
        a = a * a_sh
        d *= 2
    h = a * hc_ref[...] + b
    hc_ref[...] = h[ts - 1:ts, :]
    o_ref[...] = (h * _gelu(y_ref[...].astype(F32))).astype(BF)


def _rglru(proj, bsz, cw, cb, wax, ba, bx, lam, ts=256):
    t = proj.shape[0]
    ns = t // bsz // ts
    return pl.pallas_call(
        _rglru_kernel,
        grid=(bsz, ns),
        in_specs=[pl.BlockSpec((ts, BRANCH_W), lambda b, j: (b * ns + j, COL_BX)),
                  pl.BlockSpec((ts, BRANCH_W), lambda b, j: (b * ns + j, COL_BY)),
                  _const_spec((CONV_W, BRANCH_W)),
                  _const_spec((1, BRANCH_W)),
                  _const_spec((LRU_HEADS, BRANCH_W // LRU_HEADS, 2 * BRANCH_W // LRU_HEADS)),
                  _const_spec((1, BRANCH_W)),
                  _const_spec((1, BRANCH_W)),
                  _const_spec((1, BRANCH_W))],
        out_specs=pl.BlockSpec((ts, BRANCH_W), lambda b, j: (b * ns + j, 0)),
        out_shape=jax.ShapeDtypeStruct((t, BRANCH_W), BF),
        scratch_shapes=[pltpu.VMEM((ts + 8, BRANCH_W), F32), pltpu.VMEM((1, BRANCH_W), F32)],
        compiler_params=_params("parallel", "arbitrary"),
        name="rglru",
    )(proj, proj, cw, cb, wax, ba, bx, lam)


def _swa_kernel(q_ref, kv_ref, kvp_ref, sink_ref, o_ref):
    tq = q_ref.shape[0]
    first_key = jnp.where(pl.program_id(1) == 0, CHUNK, 0)
    kv_all = jnp.concatenate([kvp_ref[...], kv_ref[...]], axis=0)
    q = q_ref[...]
    qi = lax.broadcasted_iota(jnp.int32, (CHUNK, 2 * CHUNK), 0)
    sj = lax.broadcasted_iota(jnp.int32, (CHUNK, 2 * CHUNK), 1)
    dist = qi + CHUNK - sj
    in_win = (dist >= 0) & (dist < CHUNK)
    distf = dist.astype(F32)
    grp = SWA_HEADS // SWA_KV
    kvw = SWA_KV * HEAD_DIM
    for qb in range(tq // CHUNK):
        kvb = kv_all[qb * CHUNK:(qb + 2) * CHUNK]
        if qb == 0:
            valid = in_win & (sj >= first_key)
        else:
            valid = in_win
        outs = []
        for kh in range(SWA_KV):
            k = kvb[:, kh * HEAD_DIM:(kh + 1) * HEAD_DIM]
            v = kvb[:, kvw + kh * HEAD_DIM:kvw + (kh + 1) * HEAD_DIM]
            for g in range(grp):
                hh = kh * grp + g
                slope = 2.0 ** (-(8.0 / SWA_HEADS) * (hh + 1))
                qh = q[qb * CHUNK:(qb + 1) * CHUNK, hh * HEAD_DIM:(hh + 1) * HEAD_DIM]
                s = _dot_nt(qh, k) * (HEAD_DIM ** -0.5) - slope * distf
                s = jnp.where(valid, s, NEG)
                sink = sink_ref[hh]
                m = jnp.maximum(jnp.max(s, axis=-1, keepdims=True), sink)
                p = jnp.exp(s - m)
                denom = jnp.sum(p, axis=-1, keepdims=True) + jnp.exp(sink - m)
                outs.append(_dot(p.astype(BF), v) / denom)
        o_ref[qb * CHUNK:(qb + 1) * CHUNK, :] = jnp.concatenate(outs, axis=1).astype(BF)


def _swa(proj, bsz, sinks, tq=512):
    t = proj.shape[0]
    seq = t // bsz
    nq = seq // tq
    per = tq // CHUNK
    kv_blk = N_WIDE * BRANCH_W // (2 * SWA_KV * HEAD_DIM)
    return pl.pallas_call(
        _swa_kernel,
        grid=(bsz, nq),
        in_specs=[pl.BlockSpec((tq, BRANCH_W), lambda b, i: (b * nq + i, COL_CQ)),
                  pl.BlockSpec((tq, 2 * SWA_KV * HEAD_DIM), lambda b, i: (b * nq + i, kv_blk)),
                  pl.BlockSpec((CHUNK, 2 * SWA_KV * HEAD_DIM),
                               lambda b, i: (b * (seq // CHUNK) + jnp.maximum(i * per - 1, 0), kv_blk)),
                  pl.BlockSpec(memory_space=pltpu.SMEM)],
        out_specs=pl.BlockSpec((tq, BRANCH_W), lambda b, i: (b * nq + i, 0)),
        out_shape=jax.ShapeDtypeStruct((t, BRANCH_W), BF),
        compiler_params=_params("parallel", "parallel"),
        name="swa",
    )(proj, proj, proj, sinks)


def _aug_tables():
    eq = np.zeros((3, LANES, FOX_PAIRS * LANES), np.float32)
    ek = np.zeros((3, LANES, FOX_PAIRS * LANES), np.float32)
    oq = np.zeros((1, FOX_PAIRS * LANES), np.float32)
    ok = np.zeros((1, FOX_PAIRS * LANES), np.float32)
    for h in range(FOX_HEADS):
        base = (h // 2) * LANES + 6 * (h % 2)
        for s in range(3):
            eq[s, h, base + s] = 1.0
            ek[s, h, base + 3 + s] = -1.0
            ok[0, base + s] = 1.0
            oq[0, base + 3 + s] = 1.0
    return eq, ek, oq, ok


def _fox_prep_kernel(q_ref, k_ref, f_ref, bf_ref, eq_ref, ek_ref, oq_ref, ok_ref, qp_ref, kp_ref, cum_ref):
    ts = q_ref.shape[0]

    @pl.when(pl.program_id(1) == 0)
    def _():
        cum_ref[...] = jnp.zeros_like(cum_ref)

    z = f_ref[...] + bf_ref[...]
    c = jnp.minimum(z, 0.0) - jnp.log1p(jnp.exp(-jnp.abs(z)))
    d = 1
    while d < ts:
        c = c + _shift_rows(c, d, 0.0)
        d *= 2
    c = c + cum_ref[...]
    cum_ref[...] = c[ts - 1:ts, :]
    c1 = c.astype(BF)
    r1 = c - c1.astype(F32)
    c2 = r1.astype(BF)
    c3 = (r1 - c2.astype(F32)).astype(BF)
    augq = _dot(c1, eq_ref[0]) + _dot(c2, eq_ref[1]) + _dot(c3, eq_ref[2]) + oq_ref[...]
    augk = _dot(c1, ek_ref[0]) + _dot(c2, ek_ref[1]) + _dot(c3, ek_ref[2]) + ok_ref[...]
    q = (q_ref[...].astype(F32) * (HEAD_DIM ** -0.5)).astype(BF)
    k = k_ref[...]
    for p in range(FOX_PAIRS):
        qp_ref[:, 2 * p * LANES:(2 * p + 1) * LANES] = q[:, p * LANES:(p + 1) * LANES]
        qp_ref[:, (2 * p + 1) * LANES:(2 * p + 2) * LANES] = augq[:, p * LANES:(p + 1) * LANES].astype(BF)
        kp_ref[:, 2 * p * LANES:(2 * p + 1) * LANES] = k[:, p * LANES:(p + 1) * LANES]
        kp_ref[:, (2 * p + 1) * LANES:(2 * p + 2) * LANES] = augk[:, p * LANES:(p + 1) * LANES].astype(BF)


def _fox_prep(proj, f, bsz, bf_pad, ts=512):
    t = proj.shape[0]
    ns = t // bsz // ts
    eq, ek, oq, ok = _aug_tables()
    wide = 2 * FOX_PAIRS * LANES
    return pl.pallas_call(
        _fox_prep_kernel,
        grid=(bsz, ns),
        in_specs=[pl.BlockSpec((ts, BRANCH_W), lambda b, j: (b * ns + j, COL_DQ)),
                  pl.BlockSpec((ts, BRANCH_W), lambda b, j: (b * ns + j, COL_DK)),
                  pl.BlockSpec((ts, LANES), lambda b, j: (b * ns + j, 0)),
                  _const_spec((1, LANES)),
                  _const_spec(eq.shape), _const_spec(ek.shape), _const_spec(oq.shape), _const_spec(ok.shape)],
        out_specs=[pl.BlockSpec((ts, wide), lambda b, j: (b * ns + j, 0)),
                   pl.BlockSpec((ts, wide), lambda b, j: (b * ns + j, 0))],
        out_shape=[jax.ShapeDtypeStruct((t, wide), BF), jax.ShapeDtypeStruct((t, wide), BF)],
        scratch_shapes=[pltpu.VMEM((1, LANES), F32)],
        compiler_params=_params("parallel", "arbitrary"),
        name="fox_prep",
    )(proj, proj, f, bf_pad, jnp.asarray(eq, BF), jnp.asarray(ek, BF), jnp.asarray(oq), jnp.asarray(ok))


def _fox_kernel(q_ref, k_ref, vt_ref, o_ref, st_ref, pt_ref, al_ref, m_ref, acc_ref):
    tq = q_ref.shape[0]
    tk = tq // 2
    qi = pl.program_id(2)
    lane = lax.broadcasted_iota(jnp.int32, (tq, 2 * LANES), 1)
    q = q_ref[...]
    zero = jnp.zeros_like(q)
    in0 = (lane < HEAD_DIM) | ((lane >= LANES) & (lane < LANES + 6))
    in1 = ((lane >= HEAD_DIM) & (lane < LANES)) | ((lane >= LANES + 6) & (lane < LANES + 12))
    qh = (jnp.where(in0, q, zero), jnp.where(in1, q, zero))
    for h in range(2):
        m_ref[h] = jnp.full(m_ref.shape[1:], NEG, F32)
        acc_ref[h] = jnp.zeros(acc_ref.shape[1:], F32)
        pt_ref[1, h] = jnp.zeros(pt_ref.shape[2:], BF)
        al_ref[1, h] = jnp.ones(al_ref.shape[2:], F32)

    def scores(t, slot):
        k = k_ref[pl.ds(pl.multiple_of(t * tk, tk), tk), :]
        for h in range(2):
            st_ref[slot, h] = _dot_nt(k, qh[h])

    def numerators(slot, key_offset):
        for h in range(2):
            st = st_ref[slot, h]
            if key_offset is not None:
                key = lax.broadcasted_iota(jnp.int32, (tk, tq), 0) + key_offset
                qry = lax.broadcasted_iota(jnp.int32, (tk, tq), 1)
                st = jnp.where(key <= qry, st, NEG)
            m_old = m_ref[h]
            m_new = jnp.maximum(m_old, jnp.max(st, axis=0, keepdims=True))
            al_ref[slot, h] = jnp.exp(m_old - m_new)
            pt_ref[slot, h] = jnp.exp(st - m_new).astype(BF)
            m_ref[h] = m_new

    def accumulate(t, slot):
        vt = vt_ref[:, pl.ds(pl.multiple_of(t * tk, tk), tk)]
        for h in range(2):
            acc_ref[h] = al_ref[slot, h] * acc_ref[h] + _dot(vt[h * VT_ROWS:(h + 1) * VT_ROWS], pt_ref[slot, h])

    def tile_pair(j, diagonal):
        scores(2 * j + 1, 1)
        accumulate(jnp.maximum(2 * j - 1, 0), 1)
        numerators(0, 0 if diagonal else None)
        if not diagonal:
            scores(2 * j + 2, 0)
        accumulate(2 * j, 0)
        numerators(1, tk if diagonal else None)

    def body(j, carry):
        tile_pair(j, False)
        return carry

    scores(0, 0)
    lax.fori_loop(0, qi, body, 0)
    tile_pair(qi, True)
    accumulate(2 * qi + 1, 1)
    outs = []
    for h in range(2):
        acc = acc_ref[h]
        outs.append(acc[:HEAD_DIM] / acc[HEAD_DIM:HEAD_DIM + 1])
    o_ref[...] = jnp.concatenate(outs, axis=0).T.astype(BF)


def _fox(qp, kp, vt, bsz, tq=512):
    t = qp.shape[0]
    seq = t // bsz
    nq = seq // tq
    return pl.pallas_call(
        _fox_kernel,
        grid=(bsz, FOX_PAIRS, nq),
        in_specs=[pl.BlockSpec((tq, 2 * LANES), lambda b, p, i: (b * nq + i, p)),
                  pl.BlockSpec((seq, 2 * LANES), lambda b, p, i: (b, p)),
                  pl.BlockSpec((2 * VT_ROWS, seq), lambda b, p, i: (p, b))],
        out_specs=pl.BlockSpec((tq, LANES), lambda b, p, i: (b * nq + i, p)),
        out_shape=jax.ShapeDtypeStruct((t, BRANCH_W), BF),
        scratch_shapes=[pltpu.VMEM((2, 2, tq // 2, tq), F32),
                        pltpu.VMEM((2, 2, tq // 2, tq), BF),
                        pltpu.VMEM((2, 2, 1, tq), F32),
                        pltpu.VMEM((2, 1, tq), F32),
                        pltpu.VMEM((2, VT_ROWS, tq), F32)],
        compiler_params=_params("parallel", "parallel", "arbitrary"),
        name="fox",
    )(qp, kp, vt)


def _merge_kernel(h_ref, g_ref, oa_ref, ob_ref, oc_ref, od_ref, wg_ref, bg_ref, wb_ref, wo_ref, out_ref):
    h = h_ref[...]
    xn = _rms(h, g_ref[...]).astype(BF)
    merged = None
    for br, o_ref in enumerate((oa_ref, ob_ref, oc_ref, od_ref)):
        gate = _sigmoid(_dot(xn, wg_ref[br]) + bg_ref[br])
        term = gate * _dot(o_ref[...], wb_ref[br])
        merged = term if merged is None else merged + term
    out_ref[...] = h + _dot(merged.astype(BF), wo_ref[...])


def _merge(h, g, oa, ob, oc, od, wg, bg, wb, wo, tm=512):
    t = h.shape[0]
    row = lambda i: (i, 0)
    return pl.pallas_call(
        _merge_kernel,
        grid=(t // tm,),
        in_specs=[pl.BlockSpec((tm, D_MODEL), row),
                  _const_spec((1, D_MODEL)),
                  pl.BlockSpec((tm, BRANCH_W), row), pl.BlockSpec((tm, BRANCH_W), row),
                  pl.BlockSpec((tm, BRANCH_W), row), pl.BlockSpec((tm, BRANCH_W), row),
                  _const_spec((4, D_MODEL, D_MODEL)),
                  _const_spec((4, 1, D_MODEL)),
                  _const_spec((4, BRANCH_W, D_MODEL)),
                  _const_spec((D_MODEL, D_MODEL))],
        out_specs=pl.BlockSpec((tm, D_MODEL), row),
        out_shape=jax.ShapeDtypeStruct((t, D_MODEL), F32),
        compiler_params=_params("parallel"),
        name="merge",
    )(h, g, oa, ob, oc, od, wg, bg, wb, wo)


def _memkv_kernel(mem_ref, g_ref, w_ref, kv_ref):
    mn = _rms(mem_ref[...], g_ref[...]).astype(BF)
    kv_ref[...] = _dot(mn, w_ref[...]).astype(BF)


def _memkv(mem2, g, w, m_len):
    n = mem2.shape[0]
    width = 2 * X_HEADS * X_HEAD_DIM
    return pl.pallas_call(
        _memkv_kernel,
        grid=(n // m_len,),
        in_specs=[pl.BlockSpec((m_len, D_MODEL), lambda b: (b, 0)),
                  _const_spec((1, D_MODEL)),
                  _const_spec((D_MODEL, width))],
        out_specs=pl.BlockSpec((m_len, width), lambda b: (b, 0)),
        out_shape=jax.ShapeDtypeStruct((n, width), BF),
        compiler_params=_params("parallel"),
        name="memkv",
    )(mem2, g, w)


def _cross_kernel(h_ref, g_ref, wq_ref, kv_ref, wo_ref, out_ref):
    h = h_ref[...]
    hn = _rms(h, g_ref[...]).astype(BF)
    q = _dot(hn, wq_ref[...]).astype(BF)
    kv = kv_ref[...]
    width = X_HEADS * X_HEAD_DIM
    outs = []
    for hd in range(X_HEADS):
        cs = slice(hd * X_HEAD_DIM, (hd + 1) * X_HEAD_DIM)
        s = _dot_nt(q[:, cs], kv[:, cs]) * (X_HEAD_DIM ** -0.5)
        m = jnp.max(s, axis=-1, keepdims=True)
        p = jnp.exp(s - m)
        denom = jnp.sum(p, axis=-1, keepdims=True)
        v = kv[:, width + hd * X_HEAD_DIM:width + (hd + 1) * X_HEAD_DIM]
        outs.append((_dot(p.astype(BF), v) / denom).astype(BF))
    o = jnp.concatenate(outs, axis=1)
    out_ref[...] = h + _dot(o, wo_ref[...])


def _cross(h, g, wq, kv, wo, bsz, m_len, tm=512):
    t = h.shape[0]
    per = t // bsz // tm
    width = X_HEADS * X_HEAD_DIM
    return pl.pallas_call(
        _cross_kernel,
        grid=(bsz, per),
        in_specs=[pl.BlockSpec((tm, D_MODEL), lambda b, i: (b * per + i, 0)),
                  _const_spec((1, D_MODEL)),
                  _const_spec((D_MODEL, width)),
                  pl.BlockSpec((m_len, 2 * width), lambda b, i: (b, 0)),
                  _const_spec((width, D_MODEL))],
        out_specs=pl.BlockSpec((tm, D_MODEL), lambda b, i: (b * per + i, 0)),
        out_shape=jax.ShapeDtypeStruct((t, D_MODEL), F32),
        compiler_params=_params("parallel", "parallel"),
        name="cross",
    )(h, g, wq, kv, wo)


def _ffn_kernel(h_ref, g_ref, w1_ref, w3_ref, w2_ref, out_ref, hn_ref, acc_ref):
    j = pl.program_id(1)

    @pl.when(j == 0)
    def _():
        hn_ref[...] = _rms(h_ref[...], g_ref[...]).astype(BF)
        acc_ref[...] = jnp.zeros_like(acc_ref)

    hn = hn_ref[...]
    gate = _dot(hn, w1_ref[...])
    up = _dot(hn, w3_ref[...])
    act = (gate * _sigmoid(gate) * up).astype(BF)
    acc_ref[...] += _dot(act, w2_ref[...])

    @pl.when(j == pl.num_programs(1) - 1)
    def _():
        out_ref[...] = h_ref[...] + acc_ref[...]


def _ffn(h, g, w13, w2, tm=512, tf=FF_TILE):
    t = h.shape[0]
    nf = D_FF // tf
    return pl.pallas_call(
        _ffn_kernel,
        grid=(t // tm, nf),
        in_specs=[pl.BlockSpec((tm, D_MODEL), lambda i, j: (i, 0)),
                  _const_spec((1, D_MODEL)),
                  pl.BlockSpec((D_MODEL, tf), lambda i, j: (0, j)),
                  pl.BlockSpec((D_MODEL, tf), lambda i, j: (0, nf + j)),
                  pl.BlockSpec((tf, D_MODEL), lambda i, j: (j, 0))],
        out_specs=pl.BlockSpec((tm, D_MODEL), lambda i, j: (i, 0)),
        out_shape=jax.ShapeDtypeStruct((t, D_MODEL), F32),
        scratch_shapes=[pltpu.VMEM((tm, D_MODEL), BF), pltpu.VMEM((tm, D_MODEL), F32)],
        compiler_params=_params("parallel", "arbitrary"),
        name="ffn",
    )(h, g, w13, w13, w2)


def _router_kernel(h_ref, g_ref, wr_ref, br_ref, comb_ref):
    hn = _rms(h_ref[...], g_ref[...])
    logits = jnp.dot(hn, wr_ref[...], preferred_element_type=F32, precision=lax.Precision.HIGHEST) + br_ref[...]
    lane = lax.broadcasted_iota(jnp.int32, logits.shape, 1)
    logits = jnp.where(lane < N_EXPERTS, logits, NEG)
    v1 = jnp.max(logits, axis=-1, keepdims=True)
    i1 = jnp.min(jnp.where(logits == v1, lane, LANES), axis=-1, keepdims=True)
    rest = jnp.where(lane == i1, NEG, logits)
    v2 = jnp.max(rest, axis=-1, keepdims=True)
    i2 = jnp.min(jnp.where(rest == v2, lane, LANES), axis=-1, keepdims=True)
    e2 = jnp.exp(v2 - v1)
    w1 = 1.0 / (1.0 + e2)
    w2 = e2 / (1.0 + e2)
    comb_ref[...] = jnp.where(lane == i1, w1, 0.0) + jnp.where(lane == i2, w2, 0.0)


def _router(h, g, wr, br, tm=512):
    t = h.shape[0]
    return pl.pallas_call(
        _router_kernel,
        grid=(t // tm,),
        in_specs=[pl.BlockSpec((tm, D_MODEL), lambda i: (i, 0)),
                  _const_spec((1, D_MODEL)),
                  _const_spec((D_MODEL, LANES)),
                  _const_spec((1, LANES))],
        out_specs=pl.BlockSpec((tm, LANES), lambda i: (i, 0)),
        out_shape=jax.ShapeDtypeStruct((t, LANES), F32),
        compiler_params=_params("parallel"),
        name="router",
    )(h, g, wr, br)


def _moe_kernel(h_ref, g_ref, comb_ref, w1_ref, w3_ref, w2_ref, gf_ref, out_ref, hn_ref, acc_ref, tot_ref):
    e = pl.program_id(1)
    j = pl.program_id(2)
    last_j = pl.num_programs(2) - 1

    @pl.when((e == 0) & (j == 0))
    def _():
        hn_ref[...] = _rms(h_ref[...], g_ref[...]).astype(BF)
        tot_ref[...] = jnp.zeros_like(tot_ref)

    @pl.when(j == 0)
    def _():
        acc_ref[...] = jnp.zeros_like(acc_ref)

    hn = hn_ref[...]
    gate = _dot(hn, w1_ref[0])
    up = _dot(hn, w3_ref[0])
    act = (gate * _sigmoid(gate) * up).astype(BF)
    acc_ref[...] += _dot(act, w2_ref[0])

    @pl.when(j == last_j)
    def _():
        comb = comb_ref[...]
        lane = lax.broadcasted_iota(jnp.int32, comb.shape, 1)
        ce = jnp.sum(jnp.where(lane == e, comb, 0.0), axis=-1, keepdims=True)
        tot_ref[...] += ce * acc_ref[...]

    @pl.when((e == pl.num_programs(1) - 1) & (j == last_j))
    def _():
        out_ref[...] = _rms(h_ref[...] + tot_ref[...], gf_ref[...])


def _moe(h, g, comb, w13, w2, g_final, tm=512, tf=FF_TILE):
    t = h.shape[0]
    nf = D_FF // tf
    return pl.pallas_call(
        _moe_kernel,
        grid=(t // tm, N_EXPERTS, nf),
        in_specs=[pl.BlockSpec((tm, D_MODEL), lambda i, e, j: (i, 0)),
                  _const_spec((1, D_MODEL)),
                  pl.BlockSpec((tm, LANES), lambda i, e, j: (i, 0)),
                  pl.BlockSpec((1, D_MODEL, tf), lambda i, e, j: (e, 0, j)),
                  pl.BlockSpec((1, D_MODEL, tf), lambda i, e, j: (e, 0, nf + j)),
                  pl.BlockSpec((1, tf, D_MODEL), lambda i, e, j: (e, j, 0)),
                  _const_spec((1, D_MODEL))],
        out_specs=pl.BlockSpec((tm, D_MODEL), lambda i, e, j: (i, 0)),
        out_shape=jax.ShapeDtypeStruct((t, D_MODEL), F32),
        scratch_shapes=[pltpu.VMEM((tm, D_MODEL), BF), pltpu.VMEM((tm, D_MODEL), F32),
                        pltpu.VMEM((tm, D_MODEL), F32)],
        compiler_params=_params("parallel", "arbitrary", "arbitrary"),
        name="moe",
    )(h, g, comb, w13, w13, w2, g_final)


def _pack_w_in(w_in):
    cuts = np.cumsum((512, 512, 512, 512, 512, 128, 128, 512, 512, 512, 8))[:-1].tolist()
    a_u, a_v, b_x, b_y, c_q, c_k, c_v, d_q, d_k, d_v, d_f = jnp.split(w_in, cuts, axis=-1)
    w = jnp.concatenate([a_u, a_v, b_x, b_y, c_q, d_q, d_k, c_k, c_v], axis=-1).astype(BF)
    wf = jnp.pad(d_f, ((0, 0), (0, LANES - FOX_HEADS))).astype(BF)
    wvt = d_v.T.reshape(FOX_HEADS, HEAD_DIM, D_MODEL)
    wvt = jnp.pad(wvt, ((0, 0), (0, VT_ROWS - HEAD_DIM), (0, 0))).reshape(VT_ALL, D_MODEL).astype(BF)
    return w, wf, wvt


def _vt_ones():
    ones = np.zeros((FOX_HEADS, VT_ROWS, LANES), np.float32)
    ones[:, HEAD_DIM, :] = 1.0
    return jnp.asarray(ones.reshape(VT_ALL, LANES))


def _row(v):
    return v.reshape(1, -1)


def _hybrid_mixer(h, bsz, norm_mix, w_in, sgu_g, sgu_w, sgu_b, conv_w, conv_b, rg_wa, rg_ba, rg_wx, rg_bx,
                  rg_lambda, swa_sinks, fox_bf, w_branch, w_gate, b_gate, w_out):
    w, wf, wvt = _pack_w_in(w_in)
    proj, f, vt = _inproj(h, _row(norm_mix), w, wf, wvt, _vt_ones())
    o_a = _sgu(proj, _row(sgu_g), sgu_w, sgu_b.T)
    wax = jnp.concatenate([rg_wa, rg_wx], axis=-1).astype(BF)
    o_b = _rglru(proj, bsz, conv_w, _row(conv_b), wax, _row(rg_ba), _row(rg_bx), _row(rg_lambda))
    o_c = _swa(proj, bsz, swa_sinks)
    bf_pad = jnp.pad(fox_bf, (0, LANES - FOX_HEADS)).reshape(1, LANES)
    qp, kp = _fox_prep(proj, f, bsz, bf_pad)
    o_d = _fox(qp, kp, vt, bsz)
    return _merge(h, _row(norm_mix), o_a, o_b, o_c, o_d, w_gate.astype(BF), b_gate[:, None, :],
                  w_branch.astype(BF), w_out.astype(BF))


def kernel(x, mem, norm_mix, w_in, sgu_g, sgu_w, sgu_b, conv_w, conv_b, rg_wa, rg_ba, rg_wx, rg_bx, rg_lambda, swa_sinks, fox_bf, w_branch, w_gate, b_gate, w_out, norm_cross, norm_mem, wq_c, wkv_c, wo_c, norm_ffn, dense_w13, dense_w2, router_w, router_b, moe_w13, moe_w2, norm_final):
    bsz, seq, d = x.shape
    m_len = mem.shape[1]
    depth = norm_mix.shape[0]
    h = x.reshape(bsz * seq, d)
    mem2 = mem.reshape(bsz * m_len, d)
    for l in range(depth):
        h = _hybrid_mixer(h, bsz, norm_mix[l], w_in[l], sgu_g[l], sgu_w[l], sgu_b[l], conv_w[l], conv_b[l],
                          rg_wa[l], rg_ba[l], rg_wx[l], rg_bx[l], rg_lambda[l], swa_sinks[l], fox_bf[l],
                          w_branch[l], w_gate[l], b_gate[l], w_out[l])
        kv = _memkv(mem2, _row(norm_mem[l]), wkv_c[l].astype(BF), m_len)
        h = _cross(h, _row(norm_cross[l]), wq_c[l].astype(BF), kv, wo_c[l].astype(BF), bsz, m_len)
        if l % 2 == 0:
            h = _ffn(h, _row(norm_ffn[l]), dense_w13[l // 2].astype(BF), dense_w2[l // 2].astype(BF))
        else:
            wr = jnp.pad(router_w[l // 2], ((0, 0), (0, LANES - N_EXPERTS)))
            br = jnp.pad(router_b[l // 2], (0, LANES - N_EXPERTS)).reshape(1, LANES)
            comb = _router(h, _row(norm_ffn[l]), wr, br)
            h = _moe(h, _row(norm_ffn[l]), comb, moe_w13[l // 2].astype(BF), moe_w2[l // 2].astype(BF),
                     _row(norm_final))
    return h.reshape(bsz, seq, d)
```

```python
import functools
import math

import numpy as np
import jax
import jax.numpy as jnp
from jax import lax
from jax.experimental import pallas as pl
from jax.experimental.pallas import tpu as pltpu
from jax.experimental.pallas import tpu_sc as plsc

F32 = jnp.float32
BF = jnp.bfloat16

D_MODEL = 1024
BRANCH_W = 512
HEAD_DIM = 64
CHUNK = 128
SGU_GROUPS = 4
LRU_HEADS = 4
LRU_C = 8.0
CONV_W = 4
SWA_HEADS = 8
SWA_KV = 2
FOX_HEADS = 8
FOX_PAIRS = FOX_HEADS // 2
X_HEADS = 4
X_HEAD_DIM = 128
D_FF = 2816
N_EXPERTS = 8
EPS = 1e-6
NEG = -1e30
LANES = 128
ROW_SLABS = D_MODEL // LANES
SC_CHUNK = 64
VMEM_LIMIT = 56 * 1024 * 1024

COL_AU, COL_AV, COL_BX, COL_BY, COL_CQ, COL_DQ, COL_DK = range(7)
N_WIDE = 7
N_PROJ = N_WIDE * BRANCH_W + 2 * SWA_KV * HEAD_DIM
VT_ROWS = 80
VT_ALL = FOX_HEADS * VT_ROWS
FF_TILE = 1408


def _rms(x, g):
    return x * lax.rsqrt(jnp.mean(x * x, axis=-1, keepdims=True) + EPS) * g


def _sigmoid(x):
    return 1.0 / (1.0 + jnp.exp(-x))


def _gelu(x):
    return 0.5 * x * (1.0 + jnp.tanh(math.sqrt(2.0 / math.pi) * (x + 0.044715 * (x * x * x))))


def _dot(a, b):
    return jnp.dot(a, b, preferred_element_type=F32)


def _dot_nt(a, b):
    return lax.dot_general(a, b, (((1,), (1,)), ((), ())), preferred_element_type=F32)


def _shift_rows(x, d, fill):
    row = lax.broadcasted_iota(jnp.int32, x.shape, 0)
    return jnp.where(row >= d, pltpu.roll(x, d, 0), fill)


def _params(*sem):
    return pltpu.CompilerParams(dimension_semantics=sem, vmem_limit_bytes=VMEM_LIMIT)


def _const_spec(shape):
    nd = len(shape)
    return pl.BlockSpec(shape, lambda *_: (0,) * nd, pipeline_mode=pl.Buffered(1))


def _inproj_kernel(h_ref, g_ref, w_ref, wf_ref, wvt_ref, ones_ref, proj_ref, f_ref, vt_ref):
    xn = _rms(h_ref[...], g_ref[...]).astype(BF)
    n_col = w_ref.shape[1]
    for c in range(0, n_col, BRANCH_W):
        w = min(BRANCH_W, n_col - c)
        proj_ref[:, c:c + w] = _dot(xn, w_ref[:, c:c + w]).astype(BF)
    f_ref[...] = _dot(xn, wf_ref[...])
    ones = jnp.concatenate([ones_ref[...]] * (xn.shape[0] // LANES), axis=1)
    vt_ref[...] = (_dot_nt(wvt_ref[...], xn) + ones).astype(BF)


def _inproj(h, g, w, wf, wvt, ones, tm=512):
    t = h.shape[0]
    return pl.pallas_call(
        _inproj_kernel,
        grid=(t // tm,),
        in_specs=[pl.BlockSpec((tm, D_MODEL), lambda i: (i, 0)),
                  _const_spec((1, D_MODEL)),
                  _const_spec((D_MODEL, N_PROJ)),
                  _const_spec((D_MODEL, LANES)),
                  _const_spec((VT_ALL, D_MODEL)),
                  _const_spec((VT_ALL, LANES))],
        out_specs=[pl.BlockSpec((tm, N_PROJ), lambda i: (i, 0)),
                   pl.BlockSpec((tm, LANES), lambda i: (i, 0)),
                   pl.BlockSpec((VT_ALL, tm), lambda i: (0, i))],
        out_shape=[jax.ShapeDtypeStruct((t, N_PROJ), BF), jax.ShapeDtypeStruct((t, LANES), F32),
                   jax.ShapeDtypeStruct((VT_ALL, t), BF)],
        compiler_params=_params("parallel"),
        name="inproj",
    )(h, g, w, wf, wvt, ones)


def _sgu_kernel(u_ref, v_ref, g_ref, w_ref, bt_ref, o_ref):
    u = _gelu(u_ref[...].astype(F32))
    v = _gelu(v_ref[...].astype(F32))
    vn = _rms(v, g_ref[...]).astype(BF)
    row = lax.broadcasted_iota(jnp.int32, (CHUNK, CHUNK), 0)
    col = lax.broadcasted_iota(jnp.int32, (CHUNK, CHUNK), 1)
    tm = u.shape[0]
    gw = BRANCH_W // SGU_GROUPS
    for g in range(SGU_GROUPS):
        wg = jnp.where(col <= row, w_ref[g], 0.0).astype(BF)
        bg = bt_ref[:, g:g + 1]
        for c in range(tm // CHUNK):
            rs = slice(c * CHUNK, (c + 1) * CHUNK)
            cs = slice(g * gw, (g + 1) * gw)
            mixed = _dot(wg, vn[rs, cs]) + bg
            o_ref[rs, cs] = (u[rs, cs] * mixed).astype(BF)


def _sgu(proj, g, w, bt, tm=512):
    t = proj.shape[0]
    return pl.pallas_call(
        _sgu_kernel,
        grid=(t // tm,),
        in_specs=[pl.BlockSpec((tm, BRANCH_W), lambda i: (i, COL_AU)),
                  pl.BlockSpec((tm, BRANCH_W), lambda i: (i, COL_AV)),
                  _const_spec((1, BRANCH_W)),
                  _const_spec((SGU_GROUPS, CHUNK, CHUNK)),
                  _const_spec((CHUNK, SGU_GROUPS))],
        out_specs=pl.BlockSpec((tm, BRANCH_W), lambda i: (i, 0)),
        out_shape=jax.ShapeDtypeStruct((t, BRANCH_W), BF),
        compiler_params=_params("parallel"),
        name="sgu",
    )(proj, proj, g, w, bt)


def _rglru_kernel(x_ref, y_ref, cw_ref, cb_ref, wax_ref, ba_ref, bx_ref, lam_ref, o_ref, xs_ref, hc_ref):
    ts = x_ref.shape[0]
    hw = BRANCH_W // LRU_HEADS

    @pl.when(pl.program_id(1) == 0)
    def _():
        xs_ref[0:8, :] = jnp.zeros((8, BRANCH_W), F32)
        hc_ref[...] = jnp.zeros_like(hc_ref)

    x = x_ref[...].astype(F32)
    xs_ref[8:8 + ts, :] = x
    cw = cw_ref[...]
    xc = cb_ref[...] + cw[CONV_W - 1:CONV_W] * x
    for k in range(1, CONV_W):
        xc = xc + cw[CONV_W - 1 - k:CONV_W - k] * xs_ref[8 - k:8 - k + ts, :]
    xs_ref[0:8, :] = x[ts - 8:ts, :]

    xcb = xc.astype(BF)
    r_parts, i_parts = [], []
    for hd in range(LRU_HEADS):
        z = _dot(xcb[:, hd * hw:(hd + 1) * hw], wax_ref[hd])
        r_parts.append(z[:, :hw])
        i_parts.append(z[:, hw:])
    r = _sigmoid(jnp.concatenate(r_parts, axis=1) + ba_ref[...])
    gi = _sigmoid(jnp.concatenate(i_parts, axis=1) + bx_ref[...])
    nl = -lam_ref[...]
    softplus = jnp.maximum(nl, 0.0) + jnp.log1p(jnp.exp(-jnp.abs(nl)))
    log_a = (-LRU_C * r) * softplus
    a = jnp.exp(log_a)
    th = jnp.tanh(log_a)
    b = (xc * gi) * jnp.sqrt(-2.0 * th / (1.0 - th))

    d = 1
    while d < ts:
        a_sh = _shift_rows(a, d, 1.0)
        b_sh = _shift_rows(b, d, 0.0)
        b = b + a * b_sh
        a = a * a_sh
        d *= 2
    h = a * hc_ref[...] + b
    hc_ref[...] = h[ts - 1:ts, :]
    o_ref[...] = (h * _gelu(y_ref[...].astype(F32))).astype(BF)


def _rglru(proj, bsz, cw, cb, wax, ba, bx, lam, ts=256):
    t = proj.shape[0]
    ns = t // bsz // ts
    return pl.pallas_call(
        _rglru_kernel,
        grid=(bsz, ns),
        in_specs=[pl.BlockSpec((ts, BRANCH_W), lambda b, j: (b * ns + j, COL_BX)),
                  pl.BlockSpec((ts, BRANCH_W), lambda b, j: (b * ns + j, COL_BY)),
                  _const_spec((CONV_W, BRANCH_W)),
                  _const_spec((1, BRANCH_W)),
                  _const_spec((LRU_HEADS, BRANCH_W // LRU_HEADS, 2 * BRANCH_W // LRU_HEADS)),
                  _const_spec((1, BRANCH_W)),
                  _const_spec((1, BRANCH_W)),
                  _const_spec((1, BRANCH_W))],
        out_specs=pl.BlockSpec((ts, BRANCH_W), lambda b, j: (b * ns + j, 0)),
        out_shape=jax.ShapeDtypeStruct((t, BRANCH_W), BF),
        scratch_shapes=[pltpu.VMEM((ts + 8, BRANCH_W), F32), pltpu.VMEM((1, BRANCH_W), F32)],
        compiler_params=_params("parallel", "arbitrary"),
        name="rglru",
    )(proj, proj, cw, cb, wax, ba, bx, lam)


def _swa_kernel(q_ref, kv_ref, kvp_ref, sink_ref, o_ref):
    tq = q_ref.shape[0]
    first_key = jnp.where(pl.program_id(1) == 0, CHUNK, 0)
    kv_all = jnp.concatenate([kvp_ref[...], kv_ref[...]], axis=0)
    q = q_ref[...]
    qi = lax.broadcasted_iota(jnp.int32, (CHUNK, 2 * CHUNK), 0)
    sj = lax.broadcasted_iota(jnp.int32, (CHUNK, 2 * CHUNK), 1)
    dist = qi + CHUNK - sj
    in_win = (dist >= 0) & (dist < CHUNK)
    distf = dist.astype(F32)
    grp = SWA_HEADS // SWA_KV
    kvw = SWA_KV * HEAD_DIM
    for qb in range(tq // CHUNK):
        kvb = kv_all[qb * CHUNK:(qb + 2) * CHUNK]
        if qb == 0:
            valid = in_win & (sj >= first_key)
        else:
            valid = in_win
        outs = []
        for kh in range(SWA_KV):
            k = kvb[:, kh * HEAD_DIM:(kh + 1) * HEAD_DIM]
            v = kvb[:, kvw + kh * HEAD_DIM:kvw + (kh + 1) * HEAD_DIM]
            for g in range(grp):
                hh = kh * grp + g
                slope = 2.0 ** (-(8.0 / SWA_HEADS) * (hh + 1))
                qh = q[qb * CHUNK:(qb + 1) * CHUNK, hh * HEAD_DIM:(hh + 1) * HEAD_DIM]
                s = _dot_nt(qh, k) * (HEAD_DIM ** -0.5) - slope * distf
                s = jnp.where(valid, s, NEG)
                sink = sink_ref[hh]
                m = jnp.maximum(jnp.max(s, axis=-1, keepdims=True), sink)
                p = jnp.exp(s - m)
                denom = jnp.sum(p, axis=-1, keepdims=True) + jnp.exp(sink - m)
                outs.append(_dot(p.astype(BF), v) / denom)
        o_ref[qb * CHUNK:(qb + 1) * CHUNK, :] = jnp.concatenate(outs, axis=1).astype(BF)


def _swa(proj, bsz, sinks, tq=512):
    t = proj.shape[0]
    seq = t // bsz
    nq = seq // tq
    per = tq // CHUNK
    kv_blk = N_WIDE * BRANCH_W // (2 * SWA_KV * HEAD_DIM)
    return pl.pallas_call(
        _swa_kernel,
        grid=(bsz, nq),
        in_specs=[pl.BlockSpec((tq, BRANCH_W), lambda b, i: (b * nq + i, COL_CQ)),
                  pl.BlockSpec((tq, 2 * SWA_KV * HEAD_DIM), lambda b, i: (b * nq + i, kv_blk)),
                  pl.BlockSpec((CHUNK, 2 * SWA_KV * HEAD_DIM),
                               lambda b, i: (b * (seq // CHUNK) + jnp.maximum(i * per - 1, 0), kv_blk)),
                  pl.BlockSpec(memory_space=pltpu.SMEM)],
        out_specs=pl.BlockSpec((tq, BRANCH_W), lambda b, i: (b * nq + i, 0)),
        out_shape=jax.ShapeDtypeStruct((t, BRANCH_W), BF),
        compiler_params=_params("parallel", "parallel"),
        name="swa",
    )(proj, proj, proj, sinks)


def _aug_tables():
    eq = np.zeros((3, LANES, FOX_PAIRS * LANES), np.float32)
    ek = np.zeros((3, LANES, FOX_PAIRS * LANES), np.float32)
    oq = np.zeros((1, FOX_PAIRS * LANES), np.float32)
    ok = np.zeros((1, FOX_PAIRS * LANES), np.float32)
    for h in range(FOX_HEADS):
        base = (h // 2) * LANES + 6 * (h % 2)
        for s in range(3):
            eq[s, h, base + s] = 1.0
            ek[s, h, base + 3 + s] = -1.0
            ok[0, base + s] = 1.0
            oq[0, base + 3 + s] = 1.0
    return eq, ek, oq, ok


def _fox_prep_kernel(q_ref, k_ref, f_ref, bf_ref, eq_ref, ek_ref, oq_ref, ok_ref, qp_ref, kp_ref, cum_ref):
    ts = q_ref.shape[0]

    @pl.when(pl.program_id(1) == 0)
    def _():
        cum_ref[...] = jnp.zeros_like(cum_ref)

    z = f_ref[...] + bf_ref[...]
    c = jnp.minimum(z, 0.0) - jnp.log1p(jnp.exp(-jnp.abs(z)))
    d = 1
    while d < ts:
        c = c + _shift_rows(c, d, 0.0)
        d *= 2
    c = c + cum_ref[...]
    cum_ref[...] = c[ts - 1:ts, :]
    c1 = c.astype(BF)
    r1 = c - c1.astype(F32)
    c2 = r1.astype(BF)
    c3 = (r1 - c2.astype(F32)).astype(BF)
    augq = _dot(c1, eq_ref[0]) + _dot(c2, eq_ref[1]) + _dot(c3, eq_ref[2]) + oq_ref[...]
    augk = _dot(c1, ek_ref[0]) + _dot(c2, ek_ref[1]) + _dot(c3, ek_ref[2]) + ok_ref[...]
    q = (q_ref[...].astype(F32) * (HEAD_DIM ** -0.5)).astype(BF)
    k = k_ref[...]
    for p in range(FOX_PAIRS):
        qp_ref[:, 2 * p * LANES:(2 * p + 1) * LANES] = q[:, p * LANES:(p + 1) * LANES]
        qp_ref[:, (2 * p + 1) * LANES:(2 * p + 2) * LANES] = augq[:, p * LANES:(p + 1) * LANES].astype(BF)
        kp_ref[:, 2 * p * LANES:(2 * p + 1) * LANES] = k[:, p * LANES:(p + 1) * LANES]
        kp_ref[:, (2 * p + 1) * LANES:(2 * p + 2) * LANES] = augk[:, p * LANES:(p + 1) * LANES].astype(BF)


def _fox_prep(proj, f, bsz, bf_pad, ts=512):
    t = proj.shape[0]
    ns = t // bsz // ts
    eq, ek, oq, ok = _aug_tables()
    wide = 2 * FOX_PAIRS * LANES
    return pl.pallas_call(
        _fox_prep_kernel,
        grid=(bsz, ns),
        in_specs=[pl.BlockSpec((ts, BRANCH_W), lambda b, j: (b * ns + j, COL_DQ)),
                  pl.BlockSpec((ts, BRANCH_W), lambda b, j: (b * ns + j, COL_DK)),
                  pl.BlockSpec((ts, LANES), lambda b, j: (b * ns + j, 0)),
                  _const_spec((1, LANES)),
                  _const_spec(eq.shape), _const_spec(ek.shape), _const_spec(oq.shape), _const_spec(ok.shape)],
        out_specs=[pl.BlockSpec((ts, wide), lambda b, j: (b * ns + j, 0)),
                   pl.BlockSpec((ts, wide), lambda b, j: (b * ns + j, 0))],
        out_shape=[jax.ShapeDtypeStruct((t, wide), BF), jax.ShapeDtypeStruct((t, wide), BF)],
        scratch_shapes=[pltpu.VMEM((1, LANES), F32)],
        compiler_params=_params("parallel", "arbitrary"),
        name="fox_prep",
    )(proj, proj, f, bf_pad, jnp.asarray(eq, BF), jnp.asarray(ek, BF), jnp.asarray(oq), jnp.asarray(ok))


def _fox_kernel(q_ref, k_ref, vt_ref, o_ref, st_ref, pt_ref, al_ref, m_ref, acc_ref):
    tq = q_ref.shape[0]
    tk = tq // 2
    qi = pl.program_id(2)
    lane = lax.broadcasted_iota(jnp.int32, (tq, 2 * LANES), 1)
    q = q_ref[...]
    zero = jnp.zeros_like(q)
    in0 = (lane < HEAD_DIM) | ((lane >= LANES) & (lane < LANES + 6))
    in1 = ((lane >= HEAD_DIM) & (lane < LANES)) | ((lane >= LANES + 6) & (lane < LANES + 12))
    qh = (jnp.where(in0, q, zero), jnp.where(in1, q, zero))
    for h in range(2):
        m_ref[h] = jnp.full(m_ref.shape[1:], NEG, F32)
        acc_ref[h] = jnp.zeros(acc_ref.shape[1:], F32)
        pt_ref[1, h] = jnp.zeros(pt_ref.shape[2:], BF)
        al_ref[1, h] = jnp.ones(al_ref.shape[2:], F32)

    def scores(t, slot):
        k = k_ref[pl.ds(pl.multiple_of(t * tk, tk), tk), :]
        for h in range(2):
            st_ref[slot, h] = _dot_nt(k, qh[h])

    def numerators(slot, key_offset):
        for h in range(2):
            st = st_ref[slot, h]
            if key_offset is not None:
                key = lax.broadcasted_iota(jnp.int32, (tk, tq), 0) + key_offset
                qry = lax.broadcasted_iota(jnp.int32, (tk, tq), 1)
                st = jnp.where(key <= qry, st, NEG)
            m_old = m_ref[h]
            m_new = jnp.maximum(m_old, jnp.max(st, axis=0, keepdims=True))
            al_ref[slot, h] = jnp.exp(m_old - m_new)
            pt_ref[slot, h] = jnp.exp(st - m_new).astype(BF)
            m_ref[h] = m_new

    def accumulate(t, slot):
        vt = vt_ref[:, pl.ds(pl.multiple_of(t * tk, tk), tk)]
        for h in range(2):
            acc_ref[h] = al_ref[slot, h] * acc_ref[h] + _dot(vt[h * VT_ROWS:(h + 1) * VT_ROWS], pt_ref[slot, h])

    def tile_pair(j, diagonal):
        scores(2 * j + 1, 1)
        accumulate(jnp.maximum(2 * j - 1, 0), 1)
        numerators(0, 0 if diagonal else None)
        if not diagonal:
            scores(2 * j + 2, 0)
        accumulate(2 * j, 0)
        numerators(1, tk if diagonal else None)

    def body(j, carry):
        tile_pair(j, False)
        return carry

    scores(0, 0)
    lax.fori_loop(0, qi, body, 0)
    tile_pair(qi, True)
    accumulate(2 * qi + 1, 1)
    outs = []
    for h in range(2):
        acc = acc_ref[h]
        outs.append(acc[:HEAD_DIM] / acc[HEAD_DIM:HEAD_DIM + 1])
    o_ref[...] = jnp.concatenate(outs, axis=0).T.astype(BF)


def _fox(qp, kp, vt, bsz, tq=512):
    t = qp.shape[0]
    seq = t // bsz
    nq = seq // tq
    return pl.pallas_call(
        _fox_kernel,
        grid=(bsz, FOX_PAIRS, nq),
        in_specs=[pl.BlockSpec((tq, 2 * LANES), lambda b, p, i: (b * nq + i, p)),
                  pl.BlockSpec((seq, 2 * LANES), lambda b, p, i: (b, p)),
                  pl.BlockSpec((2 * VT_ROWS, seq), lambda b, p, i: (p, b))],
        out_specs=pl.BlockSpec((tq, LANES), lambda b, p, i: (b * nq + i, p)),
        out_shape=jax.ShapeDtypeStruct((t, BRANCH_W), BF),
        scratch_shapes=[pltpu.VMEM((2, 2, tq // 2, tq), F32),
                        pltpu.VMEM((2, 2, tq // 2, tq), BF),
                        pltpu.VMEM((2, 2, 1, tq), F32),
                        pltpu.VMEM((2, 1, tq), F32),
                        pltpu.VMEM((2, VT_ROWS, tq), F32)],
        compiler_params=_params("parallel", "parallel", "arbitrary"),
        name="fox",
    )(qp, kp, vt)


def _merge_kernel(h_ref, g_ref, oa_ref, ob_ref, oc_ref, od_ref, wg_ref, bg_ref, wb_ref, wo_ref, out_ref):
    h = h_ref[...]
    xn = _rms(h, g_ref[...]).astype(BF)
    merged = None
    for br, o_ref in enumerate((oa_ref, ob_ref, oc_ref, od_ref)):
        gate = _sigmoid(_dot(xn, wg_ref[br]) + bg_ref[br])
        term = gate * _dot(o_ref[...], wb_ref[br])
        merged = term if merged is None else merged + term
    out_ref[...] = h + _dot(merged.astype(BF), wo_ref[...])


def _merge(h, g, oa, ob, oc, od, wg, bg, wb, wo, tm=512):
    t = h.shape[0]
    row = lambda i: (i, 0)
    return pl.pallas_call(
        _merge_kernel,
        grid=(t // tm,),
        in_specs=[pl.BlockSpec((tm, D_MODEL), row),
                  _const_spec((1, D_MODEL)),
                  pl.BlockSpec((tm, BRANCH_W), row), pl.BlockSpec((tm, BRANCH_W), row),
                  pl.BlockSpec((tm, BRANCH_W), row), pl.BlockSpec((tm, BRANCH_W), row),
                  _const_spec((4, D_MODEL, D_MODEL)),
                  _const_spec((4, 1, D_MODEL)),
                  _const_spec((4, BRANCH_W, D_MODEL)),
                  _const_spec((D_MODEL, D_MODEL))],
        out_specs=pl.BlockSpec((tm, D_MODEL), row),
        out_shape=jax.ShapeDtypeStruct((t, D_MODEL), F32),
        compiler_params=_params("parallel"),
        name="merge",
    )(h, g, oa, ob, oc, od, wg, bg, wb, wo)


def _memkv_kernel(mem_ref, g_ref, w_ref, kv_ref):
    mn = _rms(mem_ref[...], g_ref[...]).astype(BF)
    kv_ref[...] = _dot(mn, w_ref[...]).astype(BF)


def _memkv(mem2, g, w, m_len):
    n = mem2.shape[0]
    width = 2 * X_HEADS * X_HEAD_DIM
    return pl.pallas_call(
        _memkv_kernel,
        grid=(n // m_len,),
        in_specs=[pl.BlockSpec((m_len, D_MODEL), lambda b: (b, 0)),
                  _const_spec((1, D_MODEL)),
                  _const_spec((D_MODEL, width))],
        out_specs=pl.BlockSpec((m_len, width), lambda b: (b, 0)),
        out_shape=jax.ShapeDtypeStruct((n, width), BF),
        compiler_params=_params("parallel"),
        name="memkv",
    )(mem2, g, w)


def _cross_kernel(h_ref, g_ref, wq_ref, kv_ref, wo_ref, out_ref):
    h = h_ref[...]
    hn = _rms(h, g_ref[...]).astype(BF)
    q = _dot(hn, wq_ref[...]).astype(BF)
    kv = kv_ref[...]
    width = X_HEADS * X_HEAD_DIM
    outs = []
    for hd in range(X_HEADS):
        cs = slice(hd * X_HEAD_DIM, (hd + 1) * X_HEAD_DIM)
        s = _dot_nt(q[:, cs], kv[:, cs]) * (X_HEAD_DIM ** -0.5)
        m = jnp.max(s, axis=-1, keepdims=True)
        p = jnp.exp(s - m)
        denom = jnp.sum(p, axis=-1, keepdims=True)
        v = kv[:, width + hd * X_HEAD_DIM:width + (hd + 1) * X_HEAD_DIM]
        outs.append((_dot(p.astype(BF), v) / denom).astype(BF))
    o = jnp.concatenate(outs, axis=1)
    out_ref[...] = h + _dot(o, wo_ref[...])


def _cross(h, g, wq, kv, wo, bsz, m_len, tm=512):
    t = h.shape[0]
    per = t // bsz // tm
    width = X_HEADS * X_HEAD_DIM
    return pl.pallas_call(
        _cross_kernel,
        grid=(bsz, per),
        in_specs=[pl.BlockSpec((tm, D_MODEL), lambda b, i: (b * per + i, 0)),
                  _const_spec((1, D_MODEL)),
                  _const_spec((D_MODEL, width)),
                  pl.BlockSpec((m_len, 2 * width), lambda b, i: (b, 0)),
                  _const_spec((width, D_MODEL))],
        out_specs=pl.BlockSpec((tm, D_MODEL), lambda b, i: (b * per + i, 0)),
        out_shape=jax.ShapeDtypeStruct((t, D_MODEL), F32),
        compiler_params=_params("parallel", "parallel"),
        name="cross",
    )(h, g, wq, kv, wo)


def _ffn_kernel(h_ref, g_ref, w1_ref, w3_ref, w2_ref, out_ref, hn_ref, acc_ref):
    j = pl.program_id(1)

    @pl.when(j == 0)
    def _():
        hn_ref[...] = _rms(h_ref[...], g_ref[...]).astype(BF)
        acc_ref[...] = jnp.zeros_like(acc_ref)

    hn = hn_ref[...]
    gate = _dot(hn, w1_ref[...])
    up = _dot(hn, w3_ref[...])
    act = (gate * _sigmoid(gate) * up).astype(BF)
    acc_ref[...] += _dot(act, w2_ref[...])

    @pl.when(j == pl.num_programs(1) - 1)
    def _():
        out_ref[...] = h_ref[...] + acc_ref[...]


def _ffn(h, g, w13, w2, tm=512, tf=FF_TILE):
    t = h.shape[0]
    nf = D_FF // tf
    return pl.pallas_call(
        _ffn_kernel,
        grid=(t // tm, nf),
        in_specs=[pl.BlockSpec((tm, D_MODEL), lambda i, j: (i, 0)),
                  _const_spec((1, D_MODEL)),
                  pl.BlockSpec((D_MODEL, tf), lambda i, j: (0, j)),
                  pl.BlockSpec((D_MODEL, tf), lambda i, j: (0, nf + j)),
                  pl.BlockSpec((tf, D_MODEL), lambda i, j: (j, 0))],
        out_specs=pl.BlockSpec((tm, D_MODEL), lambda i, j: (i, 0)),
        out_shape=jax.ShapeDtypeStruct((t, D_MODEL), F32),
        scratch_shapes=[pltpu.VMEM((tm, D_MODEL), BF), pltpu.VMEM((tm, D_MODEL), F32)],
        compiler_params=_params("parallel", "arbitrary"),
        name="ffn",
    )(h, g, w13, w13, w2)


def _split_slabs(ref):
    rows = ref.shape[0] // ROW_SLABS
    return jnp.concatenate([ref[pl.ds(c, rows, stride=ROW_SLABS), :] for c in range(ROW_SLABS)], axis=1)


def _store_slabs(ref, x):
    rows = ref.shape[0] // ROW_SLABS
    for c in range(ROW_SLABS):
        ref[pl.ds(c, rows, stride=ROW_SLABS), :] = x[:, c * LANES:(c + 1) * LANES]


def _slab_spec(tm, index_map):
    return pl.BlockSpec((tm * ROW_SLABS, LANES), index_map)


def _router_kernel(h_ref, g_ref, whi_ref, wlo_ref, br_ref, hn_ref, idx_ref, wts_ref):
    hn = _rms(h_ref[...], g_ref[...])
    _store_slabs(hn_ref, hn)
    hi = hn.astype(BF)
    lo = (hn - hi.astype(F32)).astype(BF)
    logits = _dot(hi, whi_ref[...]) + (_dot(lo, whi_ref[...]) + _dot(hi, wlo_ref[...])) + br_ref[...]
    lane = lax.broadcasted_iota(jnp.int32, logits.shape, 1)
    logits = jnp.where(lane < N_EXPERTS, logits, NEG)
    v1 = jnp.max(logits, axis=-1, keepdims=True)
    i1 = jnp.min(jnp.where(logits == v1, lane, LANES), axis=-1, keepdims=True)
    rest = jnp.where(lane == i1, NEG, logits)
    v2 = jnp.max(rest, axis=-1, keepdims=True)
    i2 = jnp.min(jnp.where(rest == v2, lane, LANES), axis=-1, keepdims=True)
    e2 = jnp.exp(v2 - v1)
    w1 = 1.0 / (1.0 + e2)
    w2 = e2 / (1.0 + e2)
    idx_ref[...] = jnp.where(lane == 0, i1, jnp.where(lane == 1, i2, 0))
    wts_ref[...] = jnp.where(lane == 0, w1, jnp.where(lane == 1, w2, 0.0))


def _router(h, g, whi, wlo, br, tm=512):
    t = h.shape[0]
    return pl.pallas_call(
        _router_kernel,
        grid=(t // tm,),
        in_specs=[pl.BlockSpec((tm, D_MODEL), lambda i: (i, 0)),
                  _const_spec((1, D_MODEL)),
                  _const_spec((D_MODEL, LANES)),
                  _const_spec((D_MODEL, LANES)),
                  _const_spec((1, LANES))],
        out_specs=[_slab_spec(tm, lambda i: (i, 0)),
                   pl.BlockSpec((tm, LANES), lambda i: (i, 0)),
                   pl.BlockSpec((tm, LANES), lambda i: (i, 0))],
        out_shape=[jax.ShapeDtypeStruct((t * ROW_SLABS, LANES), F32),
                   jax.ShapeDtypeStruct((t, LANES), jnp.int32),
                   jax.ShapeDtypeStruct((t, LANES), F32)],
        compiler_params=_params("parallel"),
        name="router",
    )(h, g, whi, wlo, br)


def _route_plan(idx, tm):
    t = idx.shape[0]
    n_pairs = 2 * t
    n_rows = n_pairs + N_EXPERTS * tm
    e_flat = jnp.concatenate([idx[:, 0], idx[:, 1]])
    onehot = (e_flat[:, None] == jnp.arange(N_EXPERTS, dtype=jnp.int32)[None, :]).astype(jnp.int32)
    csum = jnp.cumsum(onehot, axis=0)
    rank = jnp.sum(onehot * csum, axis=1) - 1
    counts = csum[-1]
    padded = ((counts + tm - 1) // tm) * tm
    ends = jnp.cumsum(padded)
    starts = ends - padded
    pos = starts[e_flat] + rank
    order = jnp.argsort(e_flat, stable=True).astype(jnp.int32)
    first = jnp.cumsum(counts) - counts
    r = jnp.minimum(jnp.arange(n_rows, dtype=jnp.int32), ends[-1] - 1)
    e_r = jnp.minimum(jnp.searchsorted(ends, r, side="right").astype(jnp.int32), N_EXPERTS - 1)
    local = r - starts[e_r]
    src_pair = order[jnp.clip(first[e_r] + local, 0, n_pairs - 1)]
    src_tok = jnp.where(local < counts[e_r], src_pair % t, 0).astype(jnp.int32)
    tile_expert = e_r[::tm]
    n_valid = (ends[-1] // tm).astype(jnp.int32).reshape(1)
    return pos.astype(jnp.int32), src_tok, tile_expert, n_valid


def _gather_rows(table, idx):
    n = idx.shape[0]
    info = plsc.get_sparse_core_info()
    n_workers = info.num_cores * info.num_subcores
    per_worker = n // n_workers
    assert per_worker * n_workers == n and per_worker % SC_CHUNK == 0, (n, n_workers)
    mesh = plsc.VectorSubcoreMesh(core_axis_name="c", subcore_axis_name="s")

    @functools.partial(
        pl.kernel, mesh=mesh,
        out_type=jax.ShapeDtypeStruct((n,) + table.shape[1:], table.dtype),
        scratch_types=[pltpu.VMEM((SC_CHUNK,), jnp.int32),
                       pltpu.VMEM((SC_CHUNK,) + table.shape[1:], table.dtype),
                       pltpu.SemaphoreType.DMA],
    )
    def gather(table_hbm, idx_hbm, out_hbm, idx_v, rows_v, sem):
        worker = lax.axis_index("s") * info.num_cores + lax.axis_index("c")
        base = worker * per_worker

        @pl.loop(0, per_worker // SC_CHUNK)
        def _(i):
            off = pl.multiple_of(base + i * SC_CHUNK, SC_CHUNK)
            pltpu.sync_copy(idx_hbm.at[pl.ds(off, SC_CHUNK)], idx_v)
            pltpu.async_copy(table_hbm.at[idx_v], rows_v, sem).wait()
            pltpu.sync_copy(rows_v, out_hbm.at[pl.ds(off, SC_CHUNK)])

    return gather(table, idx)


def _expert_ffn_kernel(te_ref, nv_ref, x_ref, w1_ref, w3_ref, w2_ref, y_ref, xb_ref, acc_ref):
    i = pl.program_id(0)
    j = pl.program_id(1)
    valid = i < nv_ref[0]

    @pl.when(valid & (j == 0))
    def _():
        xb_ref[...] = _split_slabs(x_ref).astype(BF)
        acc_ref[...] = jnp.zeros_like(acc_ref)

    @pl.when(valid)
    def _():
        xb = xb_ref[...]
        gate = _dot(xb, w1_ref[0])
        up = _dot(xb, w3_ref[0])
        act = (gate * _sigmoid(gate) * up).astype(BF)
        acc_ref[...] += _dot(act, w2_ref[0])

    @pl.when(j == pl.num_programs(1) - 1)
    def _():
        _store_slabs(y_ref, jnp.where(valid, acc_ref[...], 0.0))


def _expert_ffn(xs, tile_expert, n_valid, w13, w2, tm, tf=FF_TILE):
    n_rows = xs.shape[0] // ROW_SLABS
    nf = D_FF // tf
    last = lambda nv: jnp.maximum(nv[0] - 1, 0)
    grid_spec = pltpu.PrefetchScalarGridSpec(
        num_scalar_prefetch=2,
        grid=(n_rows // tm, nf),
        in_specs=[_slab_spec(tm, lambda i, j, te, nv: (jnp.minimum(i, last(nv)), 0)),
                  pl.BlockSpec((1, D_MODEL, tf), lambda i, j, te, nv: (te[i], 0, j)),
                  pl.BlockSpec((1, D_MODEL, tf), lambda i, j, te, nv: (te[i], 0, nf + j)),
                  pl.BlockSpec((1, tf, D_MODEL), lambda i, j, te, nv: (te[i], j, 0))],
        out_specs=_slab_spec(tm, lambda i, j, te, nv: (i, 0)),
        scratch_shapes=[pltpu.VMEM((tm, D_MODEL), BF), pltpu.VMEM((tm, D_MODEL), F32)],
    )
    return pl.pallas_call(
        _expert_ffn_kernel,
        grid_spec=grid_spec,
        out_shape=jax.ShapeDtypeStruct((n_rows * ROW_SLABS, LANES), F32),
        compiler_params=_params("arbitrary", "arbitrary"),
        name="expert_ffn",
    )(tile_expert, n_valid, xs, w13, w13, w2)


def _combine_kernel(h_ref, y0_ref, y1_ref, wts_ref, gf_ref, out_ref):
    wts = wts_ref[...]
    tot = h_ref[...] + wts[:, 0:1] * _split_slabs(y0_ref) + wts[:, 1:2] * _split_slabs(y1_ref)
    out_ref[...] = _rms(tot, gf_ref[...])


def _combine(h, yg, wts, g_final, tm=512):
    t = h.shape[0]
    nt = t // tm
    return pl.pallas_call(
        _combine_kernel,
        grid=(nt,),
        in_specs=[pl.BlockSpec((tm, D_MODEL), lambda i: (i, 0)),
                  _slab_spec(tm, lambda i: (i, 0)),
                  _slab_spec(tm, lambda i: (nt + i, 0)),
                  pl.BlockSpec((tm, LANES), lambda i: (i, 0)),
                  _const_spec((1, D_MODEL))],
        out_specs=pl.BlockSpec((tm, D_MODEL), lambda i: (i, 0)),
        out_shape=jax.ShapeDtypeStruct((t, D_MODEL), F32),
        compiler_params=_params("parallel"),
        name="combine",
    )(h, yg, yg, wts, g_final)


def _moe(h, g, router_w, router_b, w13, w2, g_final, tm=512):
    wr = jnp.pad(router_w, ((0, 0), (0, LANES - N_EXPERTS)))
    whi = wr.astype(BF)
    wlo = (wr - whi.astype(F32)).astype(BF)
    br = jnp.pad(router_b, (0, LANES - N_EXPERTS)).reshape(1, LANES)
    hn, idx, wts = _router(h, g, whi, wlo, br)
    pos, src_tok, tile_expert, n_valid = _route_plan(idx, tm)
    as_rows = lambda a: a.reshape(-1, ROW_SLABS, LANES)
    as_slabs = lambda a: a.reshape(-1, LANES)
    xs = as_slabs(_gather_rows(as_rows(hn), src_tok))
    ys = _expert_ffn(xs, tile_expert, n_valid, w13, w2, tm)
    yg = as_slabs(_gather_rows(as_rows(ys), pos))
    return _combine(h, yg, wts, g_final)


def _pack_w_in(w_in):
    cuts = np.cumsum((512, 512, 512, 512, 512, 128, 128, 512, 512, 512, 8))[:-1].tolist()
    a_u, a_v, b_x, b_y, c_q, c_k, c_v, d_q, d_k, d_v, d_f = jnp.split(w_in, cuts, axis=-1)
    w = jnp.concatenate([a_u, a_v, b_x, b_y, c_q, d_q, d_k, c_k, c_v], axis=-1).astype(BF)
    wf = jnp.pad(d_f, ((0, 0), (0, LANES - FOX_HEADS))).astype(BF)
    wvt = d_v.T.reshape(FOX_HEADS, HEAD_DIM, D_MODEL)
    wvt = jnp.pad(wvt, ((0, 0), (0, VT_ROWS - HEAD_DIM), (0, 0))).reshape(VT_ALL, D_MODEL).astype(BF)
    return w, wf, wvt


def _vt_ones():
    ones = np.zeros((FOX_HEADS, VT_ROWS, LANES), np.float32)
    ones[:, HEAD_DIM, :] = 1.0
    return jnp.asarray(ones.reshape(VT_ALL, LANES))


def _row(v):
    return v.reshape(1, -1)


def _hybrid_mixer(h, bsz, norm_mix, w_in, sgu_g, sgu_w, sgu_b, conv_w, conv_b, rg_wa, rg_ba, rg_wx, rg_bx,
                  rg_lambda, swa_sinks, fox_bf, w_branch, w_gate, b_gate, w_out):
    w, wf, wvt = _pack_w_in(w_in)
    proj, f, vt = _inproj(h, _row(norm_mix), w, wf, wvt, _vt_ones())
    o_a = _sgu(proj, _row(sgu_g), sgu_w, sgu_b.T)
    wax = jnp.concatenate([rg_wa, rg_wx], axis=-1).astype(BF)
    o_b = _rglru(proj, bsz, conv_w, _row(conv_b), wax, _row(rg_ba), _row(rg_bx), _row(rg_lambda))
    o_c = _swa(proj, bsz, swa_sinks)
    bf_pad = jnp.pad(fox_bf, (0, LANES - FOX_HEADS)).reshape(1, LANES)
    qp, kp = _fox_prep(proj, f, bsz, bf_pad)
    o_d = _fox(qp, kp, vt, bsz)
    return _merge(h, _row(norm_mix), o_a, o_b, o_c, o_d, w_gate.astype(BF), b_gate[:, None, :],
                  w_branch.astype(BF), w_out.astype(BF))


def kernel(x, mem, norm_mix, w_in, sgu_g, sgu_w, sgu_b, conv_w, conv_b, rg_wa, rg_ba, rg_wx, rg_bx, rg_lambda, swa_sinks, fox_bf, w_branch, w_gate, b_gate, w_out, norm_cross, norm_mem, wq_c, wkv_c, wo_c, norm_ffn, dense_w13, dense_w2, router_w, router_b, moe_w13, moe_w2, norm_final):
    bsz, seq, d = x.shape
    m_len = mem.shape[1]
    depth = norm_mix.shape[0]
    assert depth == 2, "the final RMSNorm is fused into the routed layer, which must be the last one"
    h = x.reshape(bsz * seq, d)
    mem2 = mem.reshape(bsz * m_len, d)
    for l in range(depth):
        h = _hybrid_mixer(h, bsz, norm_mix[l], w_in[l], sgu_g[l], sgu_w[l], sgu_b[l], conv_w[l], conv_b[l],
                          rg_wa[l], rg_ba[l], rg_wx[l], rg_bx[l], rg_lambda[l], swa_sinks[l], fox_bf[l],
                          w_branch[l], w_gate[l], b_gate[l], w_out[l])
        kv = _memkv(mem2, _row(norm_mem[l]), wkv_c[l].astype(BF), m_len)
        h = _cross(h, _row(norm_cross[l]), wq_c[l].astype(BF), kv, wo_c[l].astype(BF), bsz, m_len)
        if l % 2 == 0:
            h = _ffn(h, _row(norm_ffn[l]), dense_w13[l // 2].astype(BF), dense_w2[l // 2].astype(BF))
        else:
            h = _moe(h, _row(norm_ffn[l]), router_w[l // 2], router_b[l // 2], moe_w13[l // 2].astype(BF),
                     moe_w2[l // 2].astype(BF), _row(norm_final))
    return h.reshape(bsz, seq, d)
```

```python
import functools
import math

import numpy as np
import jax
import jax.numpy as jnp
from jax import lax
from jax.experimental import pallas as pl
from jax.experimental.pallas import tpu as pltpu
from jax.experimental.pallas import tpu_sc as plsc

F32 = jnp.float32
BF = jnp.bfloat16

D_MODEL = 1024
BRANCH_W = 512
HEAD_DIM = 64
CHUNK = 128
SGU_GROUPS = 4
LRU_HEADS = 4
LRU_C = 8.0
CONV_W = 4
SWA_HEADS = 8
SWA_KV = 2
FOX_HEADS = 8
FOX_PAIRS = FOX_HEADS // 2
X_HEADS = 4
X_HEAD_DIM = 128
D_FF = 2816
N_EXPERTS = 8
EPS = 1e-6
LOG2E = math.log2(math.e)
NEG = -1e30
LANES = 128
ROW_SLABS = D_MODEL // LANES
SC_CHUNK = 64
VMEM_LIMIT = 56 * 1024 * 1024

COL_AU, COL_AV, COL_BX, COL_BY, COL_CQ, COL_DQ, COL_DK = range(7)
N_WIDE = 7
N_PROJ = N_WIDE * BRANCH_W + 2 * SWA_KV * HEAD_DIM
VT_ROWS = 80
VT_ALL = FOX_HEADS * VT_ROWS
FF_TILE = 1408


def _rms(x, g):
    return x * lax.rsqrt(jnp.mean(x * x, axis=-1, keepdims=True) + EPS) * g


def _sigmoid(x):
    return 1.0 / (1.0 + jnp.exp(-x))


def _gelu(x):
    return 0.5 * x * (1.0 + jnp.tanh(math.sqrt(2.0 / math.pi) * (x + 0.044715 * (x * x * x))))


def _dot(a, b):
    return jnp.dot(a, b, preferred_element_type=F32)


def _dot_nt(a, b):
    return lax.dot_general(a, b, (((1,), (1,)), ((), ())), preferred_element_type=F32)


def _shift_rows(x, d, fill):
    row = lax.broadcasted_iota(jnp.int32, x.shape, 0)
    return jnp.where(row >= d, pltpu.roll(x, d, 0), fill)


def _params(*sem):
    return pltpu.CompilerParams(dimension_semantics=sem, vmem_limit_bytes=VMEM_LIMIT)


def _const_spec(shape):
    nd = len(shape)
    return pl.BlockSpec(shape, lambda *_: (0,) * nd, pipeline_mode=pl.Buffered(1))


def _inproj_kernel(h_ref, g_ref, w_ref, wf_ref, wvt_ref, ones_ref, proj_ref, f_ref, vt_ref):
    xn = _rms(h_ref[...], g_ref[...]).astype(BF)
    n_col = w_ref.shape[1]
    for c in range(0, n_col, BRANCH_W):
        w = min(BRANCH_W, n_col - c)
        proj_ref[:, c:c + w] = _dot(xn, w_ref[:, c:c + w]).astype(BF)
    f_ref[...] = _dot(xn, wf_ref[...])
    ones = jnp.concatenate([ones_ref[...]] * (xn.shape[0] // LANES), axis=1)
    vt_ref[...] = (_dot_nt(wvt_ref[...], xn) + ones).astype(BF)


def _inproj(h, g, w, wf, wvt, ones, tm=512):
    t = h.shape[0]
    return pl.pallas_call(
        _inproj_kernel,
        grid=(t // tm,),
        in_specs=[pl.BlockSpec((tm, D_MODEL), lambda i: (i, 0)),
                  _const_spec((1, D_MODEL)),
                  _const_spec((D_MODEL, N_PROJ)),
                  _const_spec((D_MODEL, LANES)),
                  _const_spec((VT_ALL, D_MODEL)),
                  _const_spec((VT_ALL, LANES))],
        out_specs=[pl.BlockSpec((tm, N_PROJ), lambda i: (i, 0)),
                   pl.BlockSpec((tm, LANES), lambda i: (i, 0)),
                   pl.BlockSpec((VT_ALL, tm), lambda i: (0, i))],
        out_shape=[jax.ShapeDtypeStruct((t, N_PROJ), BF), jax.ShapeDtypeStruct((t, LANES), F32),
                   jax.ShapeDtypeStruct((VT_ALL, t), BF)],
        compiler_params=_params("parallel"),
        name="inproj",
    )(h, g, w, wf, wvt, ones)


def _sgu_kernel(u_ref, v_ref, g_ref, w_ref, bt_ref, o_ref):
    u = _gelu(u_ref[...].astype(F32))
    v = _gelu(v_ref[...].astype(F32))
    vn = _rms(v, g_ref[...]).astype(BF)
    row = lax.broadcasted_iota(jnp.int32, (CHUNK, CHUNK), 0)
    col = lax.broadcasted_iota(jnp.int32, (CHUNK, CHUNK), 1)
    tm = u.shape[0]
    gw = BRANCH_W // SGU_GROUPS
    for g in range(SGU_GROUPS):
        wg = jnp.where(col <= row, w_ref[g], 0.0).astype(BF)
        bg = bt_ref[:, g:g + 1]
        for c in range(tm // CHUNK):
            rs = slice(c * CHUNK, (c + 1) * CHUNK)
            cs = slice(g * gw, (g + 1) * gw)
            mixed = _dot(wg, vn[rs, cs]) + bg
            o_ref[rs, cs] = (u[rs, cs] * mixed).astype(BF)


def _sgu(proj, g, w, bt, tm=512):
    t = proj.shape[0]
    return pl.pallas_call(
        _sgu_kernel,
        grid=(t // tm,),
        in_specs=[pl.BlockSpec((tm, BRANCH_W), lambda i: (i, COL_AU)),
                  pl.BlockSpec((tm, BRANCH_W), lambda i: (i, COL_AV)),
                  _const_spec((1, BRANCH_W)),
                  _const_spec((SGU_GROUPS, CHUNK, CHUNK)),
                  _const_spec((CHUNK, SGU_GROUPS))],
        out_specs=pl.BlockSpec((tm, BRANCH_W), lambda i: (i, 0)),
        out_shape=jax.ShapeDtypeStruct((t, BRANCH_W), BF),
        compiler_params=_params("parallel"),
        name="sgu",
    )(proj, proj, g, w, bt)


def _rglru_kernel(x_ref, y_ref, cw_ref, cb_ref, wax_ref, ba_ref, bx_ref, lam_ref, o_ref, xs_ref, hc_ref):
    ts = x_ref.shape[0]
    hw = BRANCH_W // LRU_HEADS

    @pl.when(pl.program_id(1) == 0)
    def _():
        xs_ref[0:8, :] = jnp.zeros((8, BRANCH_W), F32)
        hc_ref[...] = jnp.zeros_like(hc_ref)

    x = x_ref[...].astype(F32)
    xs_ref[8:8 + ts, :] = x
    cw = cw_ref[...]
    xc = cb_ref[...] + cw[CONV_W - 1:CONV_W] * x
    for k in range(1, CONV_W):
        xc = xc + cw[CONV_W - 1 - k:CONV_W - k] * xs_ref[8 - k:8 - k + ts, :]
    xs_ref[0:8, :] = x[ts - 8:ts, :]

    xcb = xc.astype(BF)
    r_parts, i_parts = [], []
    for hd in range(LRU_HEADS):
        z = _dot(xcb[:, hd * hw:(hd + 1) * hw], wax_ref[hd])
        r_parts.append(z[:, :hw])
        i_parts.append(z[:, hw:])
    r = _sigmoid(jnp.concatenate(r_parts, axis=1) + ba_ref[...])
    gi = _sigmoid(jnp.concatenate(i_parts, axis=1) + bx_ref[...])
    nl = -lam_ref[...]
    softplus = jnp.maximum(nl, 0.0) + jnp.log1p(jnp.exp(-jnp.abs(nl)))
    log_a = (-LRU_C * r) * softplus
    a = jnp.exp(log_a)
    th = jnp.tanh(log_a)
    b = (xc * gi) * jnp.sqrt(-2.0 * th / (1.0 - th))

    d = 1
    while d < ts:
        a_sh = _shift_rows(a, d, 1.0)
        b_sh = _shift_rows(b, d, 0.0)
        b = b + a * b_sh
        a = a * a_sh
        d *= 2
    h = a * hc_ref[...] + b
    hc_ref[...] = h[ts - 1:ts, :]
    o_ref[...] = (h * _gelu(y_ref[...].astype(F32))).astype(BF)


def _rglru(proj, bsz, cw, cb, wax, ba, bx, lam, ts=256):
    t = proj.shape[0]
    ns = t // bsz // ts
    return pl.pallas_call(
        _rglru_kernel,
        grid=(bsz, ns),
        in_specs=[pl.BlockSpec((ts, BRANCH_W), lambda b, j: (b * ns + j, COL_BX)),
                  pl.BlockSpec((ts, BRANCH_W), lambda b, j: (b * ns + j, COL_BY)),
                  _const_spec((CONV_W, BRANCH_W)),
                  _const_spec((1, BRANCH_W)),
                  _const_spec((LRU_HEADS, BRANCH_W // LRU_HEADS, 2 * BRANCH_W // LRU_HEADS)),
                  _const_spec((1, BRANCH_W)),
                  _const_spec((1, BRANCH_W)),
                  _const_spec((1, BRANCH_W))],
        out_specs=pl.BlockSpec((ts, BRANCH_W), lambda b, j: (b * ns + j, 0)),
        out_shape=jax.ShapeDtypeStruct((t, BRANCH_W), BF),
        scratch_shapes=[pltpu.VMEM((ts + 8, BRANCH_W), F32), pltpu.VMEM((1, BRANCH_W), F32)],
        compiler_params=_params("parallel", "arbitrary"),
        name="rglru",
    )(proj, proj, cw, cb, wax, ba, bx, lam)


def _swa_kernel(q_ref, kv_ref, kvp_ref, sink_ref, o_ref):
    tq = q_ref.shape[0]
    first_key = jnp.where(pl.program_id(1) == 0, CHUNK, 0)
    kv_all = jnp.concatenate([kvp_ref[...], kv_ref[...]], axis=0)
    q = q_ref[...]
    qi = lax.broadcasted_iota(jnp.int32, (CHUNK, 2 * CHUNK), 0)
    sj = lax.broadcasted_iota(jnp.int32, (CHUNK, 2 * CHUNK), 1)
    dist = qi + CHUNK - sj
    in_win = (dist >= 0) & (dist < CHUNK)
    distf = dist.astype(F32)
    grp = SWA_HEADS // SWA_KV
    kvw = SWA_KV * HEAD_DIM
    for qb in range(tq // CHUNK):
        kvb = kv_all[qb * CHUNK:(qb + 2) * CHUNK]
        if qb == 0:
            valid = in_win & (sj >= first_key)
        else:
            valid = in_win
        outs = []
        for kh in range(SWA_KV):
            k = kvb[:, kh * HEAD_DIM:(kh + 1) * HEAD_DIM]
            v = kvb[:, kvw + kh * HEAD_DIM:kvw + (kh + 1) * HEAD_DIM]
            for g in range(grp):
                hh = kh * grp + g
                slope = 2.0 ** (-(8.0 / SWA_HEADS) * (hh + 1))
                qh = q[qb * CHUNK:(qb + 1) * CHUNK, hh * HEAD_DIM:(hh + 1) * HEAD_DIM]
                s = _dot_nt(qh, k) * (HEAD_DIM ** -0.5) - slope * distf
                s = jnp.where(valid, s, NEG)
                sink = sink_ref[hh]
                m = jnp.maximum(jnp.max(s, axis=-1, keepdims=True), sink)
                p = jnp.exp(s - m)
                denom = jnp.sum(p, axis=-1, keepdims=True) + jnp.exp(sink - m)
                outs.append(_dot(p.astype(BF), v) / denom)
        o_ref[qb * CHUNK:(qb + 1) * CHUNK, :] = jnp.concatenate(outs, axis=1).astype(BF)


def _swa(proj, bsz, sinks, tq=512):
    t = proj.shape[0]
    seq = t // bsz
    nq = seq // tq
    per = tq // CHUNK
    kv_blk = N_WIDE * BRANCH_W // (2 * SWA_KV * HEAD_DIM)
    return pl.pallas_call(
        _swa_kernel,
        grid=(bsz, nq),
        in_specs=[pl.BlockSpec((tq, BRANCH_W), lambda b, i: (b * nq + i, COL_CQ)),
                  pl.BlockSpec((tq, 2 * SWA_KV * HEAD_DIM), lambda b, i: (b * nq + i, kv_blk)),
                  pl.BlockSpec((CHUNK, 2 * SWA_KV * HEAD_DIM),
                               lambda b, i: (b * (seq // CHUNK) + jnp.maximum(i * per - 1, 0), kv_blk)),
                  pl.BlockSpec(memory_space=pltpu.SMEM)],
        out_specs=pl.BlockSpec((tq, BRANCH_W), lambda b, i: (b * nq + i, 0)),
        out_shape=jax.ShapeDtypeStruct((t, BRANCH_W), BF),
        compiler_params=_params("parallel", "parallel"),
        name="swa",
    )(proj, proj, proj, sinks)


def _aug_tables():
    eq = np.zeros((3, LANES, FOX_PAIRS * LANES), np.float32)
    ek = np.zeros((3, LANES, FOX_PAIRS * LANES), np.float32)
    oq = np.zeros((1, FOX_PAIRS * LANES), np.float32)
    ok = np.zeros((1, FOX_PAIRS * LANES), np.float32)
    for h in range(FOX_HEADS):
        base = (h // 2) * LANES + 6 * (h % 2)
        for s in range(3):
            eq[s, h, base + s] = 1.0
            ek[s, h, base + 3 + s] = -1.0
            ok[0, base + s] = 1.0
            oq[0, base + 3 + s] = 1.0
    return eq, ek, oq, ok


def _fox_prep_kernel(q_ref, k_ref, f_ref, bf_ref, eq_ref, ek_ref, oq_ref, ok_ref, qp_ref, kp_ref, cum_ref):
    ts = q_ref.shape[0]

    @pl.when(pl.program_id(1) == 0)
    def _():
        cum_ref[...] = jnp.zeros_like(cum_ref)

    z = f_ref[...] + bf_ref[...]
    c = jnp.minimum(z, 0.0) - jnp.log1p(jnp.exp(-jnp.abs(z)))
    d = 1
    while d < ts:
        c = c + _shift_rows(c, d, 0.0)
        d *= 2
    c = c + cum_ref[...]
    cum_ref[...] = c[ts - 1:ts, :]
    c = c * LOG2E
    c1 = c.astype(BF)
    r1 = c - c1.astype(F32)
    c2 = r1.astype(BF)
    c3 = (r1 - c2.astype(F32)).astype(BF)
    augq = _dot(c1, eq_ref[0]) + _dot(c2, eq_ref[1]) + _dot(c3, eq_ref[2]) + oq_ref[...]
    augk = _dot(c1, ek_ref[0]) + _dot(c2, ek_ref[1]) + _dot(c3, ek_ref[2]) + ok_ref[...]
    qt = (q_ref[...].astype(F32) * (HEAD_DIM ** -0.5 * LOG2E)).T.astype(BF)
    augqt = augq.T.astype(BF)
    k = k_ref[...]
    for p in range(FOX_PAIRS):
        qp_ref[2 * p * LANES:(2 * p + 1) * LANES, :] = qt[p * LANES:(p + 1) * LANES]
        qp_ref[(2 * p + 1) * LANES:(2 * p + 2) * LANES, :] = augqt[p * LANES:(p + 1) * LANES]
        kp_ref[:, 2 * p * LANES:(2 * p + 1) * LANES] = k[:, p * LANES:(p + 1) * LANES]
        kp_ref[:, (2 * p + 1) * LANES:(2 * p + 2) * LANES] = augk[:, p * LANES:(p + 1) * LANES].astype(BF)


def _fox_prep(proj, f, bsz, bf_pad, ts=512):
    t = proj.shape[0]
    ns = t // bsz // ts
    eq, ek, oq, ok = _aug_tables()
    wide = 2 * FOX_PAIRS * LANES
    return pl.pallas_call(
        _fox_prep_kernel,
        grid=(bsz, ns),
        in_specs=[pl.BlockSpec((ts, BRANCH_W), lambda b, j: (b * ns + j, COL_DQ)),
                  pl.BlockSpec((ts, BRANCH_W), lambda b, j: (b * ns + j, COL_DK)),
                  pl.BlockSpec((ts, LANES), lambda b, j: (b * ns + j, 0)),
                  _const_spec((1, LANES)),
                  _const_spec(eq.shape), _const_spec(ek.shape), _const_spec(oq.shape), _const_spec(ok.shape)],
        out_specs=[pl.BlockSpec((wide, ts), lambda b, j: (0, b * ns + j)),
                   pl.BlockSpec((ts, wide), lambda b, j: (b * ns + j, 0))],
        out_shape=[jax.ShapeDtypeStruct((wide, t), BF), jax.ShapeDtypeStruct((t, wide), BF)],
        scratch_shapes=[pltpu.VMEM((1, LANES), F32)],
        compiler_params=_params("parallel", "arbitrary"),
        name="fox_prep",
    )(proj, proj, f, bf_pad, jnp.asarray(eq, BF), jnp.asarray(ek, BF), jnp.asarray(oq), jnp.asarray(ok))


def _fox_kernel(q_ref, k_ref, vt_ref, o_ref, st_ref, pt_ref, al_ref, m_ref, acc_ref):
    tq = q_ref.shape[1]
    tk = tq // 2
    qi = pl.program_id(2)
    row = lax.broadcasted_iota(jnp.int32, (2 * LANES, tq), 0)
    q = q_ref[...]
    zero = jnp.zeros_like(q)
    in0 = (row < HEAD_DIM) | ((row >= LANES) & (row < LANES + 6))
    in1 = ((row >= HEAD_DIM) & (row < LANES)) | ((row >= LANES + 6) & (row < LANES + 12))
    qh = (jnp.where(in0, q, zero), jnp.where(in1, q, zero))
    for h in range(2):
        m_ref[h] = jnp.full(m_ref.shape[1:], NEG, F32)
        acc_ref[h] = jnp.zeros(acc_ref.shape[1:], F32)
        pt_ref[1, h] = jnp.zeros(pt_ref.shape[2:], BF)
        al_ref[1, h] = jnp.ones(al_ref.shape[2:], F32)

    def scores(t, slot):
        k = k_ref[pl.ds(pl.multiple_of(t * tk, tk), tk), :]
        for h in range(2):
            st_ref[slot, h] = _dot(k, qh[h])

    def numerators(slot, key_offset):
        for h in range(2):
            st = st_ref[slot, h]
            if key_offset is not None:
                key = lax.broadcasted_iota(jnp.int32, (tk, tq), 0) + key_offset
                qry = lax.broadcasted_iota(jnp.int32, (tk, tq), 1)
                st = jnp.where(key <= qry, st, NEG)
            m_old = m_ref[h]
            m_new = jnp.maximum(m_old, jnp.max(st, axis=0, keepdims=True))
            al_ref[slot, h] = jnp.exp2(m_old - m_new)
            pt_ref[slot, h] = jnp.exp2(st - m_new).astype(BF)
            m_ref[h] = m_new

    def accumulate(t, slot):
        vt = vt_ref[:, pl.ds(pl.multiple_of(t * tk, tk), tk)]
        for h in range(2):
            acc_ref[h] = al_ref[slot, h] * acc_ref[h] + _dot(vt[h * VT_ROWS:(h + 1) * VT_ROWS], pt_ref[slot, h])

    def tile_pair(j, diagonal):
        scores(2 * j + 1, 1)
        accumulate(jnp.maximum(2 * j - 1, 0), 1)
        numerators(0, 0 if diagonal else None)
        if not diagonal:
            scores(2 * j + 2, 0)
        accumulate(2 * j, 0)
        numerators(1, tk if diagonal else None)

    def body(jj, carry):
        tile_pair(2 * jj, False)
        tile_pair(2 * jj + 1, False)
        return carry

    scores(0, 0)
    lax.fori_loop(0, qi // 2, body, 0)

    @pl.when(qi % 2 == 1)
    def _():
        tile_pair(qi - 1, False)

    tile_pair(qi, True)
    accumulate(2 * qi + 1, 1)
    outs = []
    for h in range(2):
        acc = acc_ref[h]
        outs.append(acc[:HEAD_DIM] / acc[HEAD_DIM:HEAD_DIM + 1])
    o_ref[...] = jnp.concatenate(outs, axis=0).T.astype(BF)


def _fox(qp, kp, vt, bsz, tq=512):
    t = kp.shape[0]
    seq = t // bsz
    nq = seq // tq
    return pl.pallas_call(
        _fox_kernel,
        grid=(bsz, FOX_PAIRS, nq),
        in_specs=[pl.BlockSpec((2 * LANES, tq), lambda b, p, i: (p, b * nq + i)),
                  pl.BlockSpec((seq, 2 * LANES), lambda b, p, i: (b, p)),
                  pl.BlockSpec((2 * VT_ROWS, seq), lambda b, p, i: (p, b))],
        out_specs=pl.BlockSpec((tq, LANES), lambda b, p, i: (b * nq + i, p)),
        out_shape=jax.ShapeDtypeStruct((t, BRANCH_W), BF),
        scratch_shapes=[pltpu.VMEM((2, 2, tq // 2, tq), F32),
                        pltpu.VMEM((2, 2, tq // 2, tq), BF),
                        pltpu.VMEM((2, 2, 1, tq), F32),
                        pltpu.VMEM((2, 1, tq), F32),
                        pltpu.VMEM((2, VT_ROWS, tq), F32)],
        compiler_params=_params("parallel", "parallel", "arbitrary"),
        name="fox",
    )(qp, kp, vt)


def _merge_kernel(h_ref, g_ref, oa_ref, ob_ref, oc_ref, od_ref, wg_ref, bg_ref, wb_ref, wo_ref, out_ref):
    h = h_ref[...]
    xn = _rms(h, g_ref[...]).astype(BF)
    merged = None
    for br, o_ref in enumerate((oa_ref, ob_ref, oc_ref, od_ref)):
        gate = _sigmoid(_dot(xn, wg_ref[br]) + bg_ref[br])
        term = gate * _dot(o_ref[...], wb_ref[br])
        merged = term if merged is None else merged + term
    out_ref[...] = h + _dot(merged.astype(BF), wo_ref[...])


def _merge(h, g, oa, ob, oc, od, wg, bg, wb, wo, tm=512):
    t = h.shape[0]
    row = lambda i: (i, 0)
    return pl.pallas_call(
        _merge_kernel,
        grid=(t // tm,),
        in_specs=[pl.BlockSpec((tm, D_MODEL), row),
                  _const_spec((1, D_MODEL)),
                  pl.BlockSpec((tm, BRANCH_W), row), pl.BlockSpec((tm, BRANCH_W), row),
                  pl.BlockSpec((tm, BRANCH_W), row), pl.BlockSpec((tm, BRANCH_W), row),
                  _const_spec((4, D_MODEL, D_MODEL)),
                  _const_spec((4, 1, D_MODEL)),
                  _const_spec((4, BRANCH_W, D_MODEL)),
                  _const_spec((D_MODEL, D_MODEL))],
        out_specs=pl.BlockSpec((tm, D_MODEL), row),
        out_shape=jax.ShapeDtypeStruct((t, D_MODEL), F32),
        compiler_params=_params("parallel"),
        name="merge",
    )(h, g, oa, ob, oc, od, wg, bg, wb, wo)


def _memkv_kernel(mem_ref, g_ref, w_ref, kv_ref):
    mn = _rms(mem_ref[...], g_ref[...]).astype(BF)
    kv_ref[...] = _dot(mn, w_ref[...]).astype(BF)


def _memkv(mem2, g, w, m_len):
    n = mem2.shape[0]
    width = 2 * X_HEADS * X_HEAD_DIM
    return pl.pallas_call(
        _memkv_kernel,
        grid=(n // m_len,),
        in_specs=[pl.BlockSpec((m_len, D_MODEL), lambda b: (b, 0)),
                  _const_spec((1, D_MODEL)),
                  _const_spec((D_MODEL, width))],
        out_specs=pl.BlockSpec((m_len, width), lambda b: (b, 0)),
        out_shape=jax.ShapeDtypeStruct((n, width), BF),
        compiler_params=_params("parallel"),
        name="memkv",
    )(mem2, g, w)


def _cross_kernel(h_ref, g_ref, wq_ref, kv_ref, wo_ref, out_ref):
    h = h_ref[...]
    hn = _rms(h, g_ref[...]).astype(BF)
    q = _dot(hn, wq_ref[...]).astype(BF)
    kv = kv_ref[...]
    width = X_HEADS * X_HEAD_DIM
    outs = []
    for hd in range(X_HEADS):
        cs = slice(hd * X_HEAD_DIM, (hd + 1) * X_HEAD_DIM)
        s = _dot_nt(q[:, cs], kv[:, cs]) * (X_HEAD_DIM ** -0.5)
        m = jnp.max(s, axis=-1, keepdims=True)
        p = jnp.exp(s - m)
        denom = jnp.sum(p, axis=-1, keepdims=True)
        v = kv[:, width + hd * X_HEAD_DIM:width + (hd + 1) * X_HEAD_DIM]
        outs.append((_dot(p.astype(BF), v) / denom).astype(BF))
    o = jnp.concatenate(outs, axis=1)
    out_ref[...] = h + _dot(o, wo_ref[...])


def _cross(h, g, wq, kv, wo, bsz, m_len, tm=512):
    t = h.shape[0]
    per = t // bsz // tm
    width = X_HEADS * X_HEAD_DIM
    return pl.pallas_call(
        _cross_kernel,
        grid=(bsz, per),
        in_specs=[pl.BlockSpec((tm, D_MODEL), lambda b, i: (b * per + i, 0)),
                  _const_spec((1, D_MODEL)),
                  _const_spec((D_MODEL, width)),
                  pl.BlockSpec((m_len, 2 * width), lambda b, i: (b, 0)),
                  _const_spec((width, D_MODEL))],
        out_specs=pl.BlockSpec((tm, D_MODEL), lambda b, i: (b * per + i, 0)),
        out_shape=jax.ShapeDtypeStruct((t, D_MODEL), F32),
        compiler_params=_params("parallel", "parallel"),
        name="cross",
    )(h, g, wq, kv, wo)


def _ffn_kernel(h_ref, g_ref, w1_ref, w3_ref, w2_ref, out_ref, hn_ref, acc_ref):
    j = pl.program_id(1)

    @pl.when(j == 0)
    def _():
        hn_ref[...] = _rms(h_ref[...], g_ref[...]).astype(BF)
        acc_ref[...] = jnp.zeros_like(acc_ref)

    hn = hn_ref[...]
    gate = _dot(hn, w1_ref[...])
    up = _dot(hn, w3_ref[...])
    act = (gate * _sigmoid(gate) * up).astype(BF)
    acc_ref[...] += _dot(act, w2_ref[...])

    @pl.when(j == pl.num_programs(1) - 1)
    def _():
        out_ref[...] = h_ref[...] + acc_ref[...]


def _ffn(h, g, w13, w2, tm=512, tf=FF_TILE):
    t = h.shape[0]
    nf = D_FF // tf
    return pl.pallas_call(
        _ffn_kernel,
        grid=(t // tm, nf),
        in_specs=[pl.BlockSpec((tm, D_MODEL), lambda i, j: (i, 0)),
                  _const_spec((1, D_MODEL)),
                  pl.BlockSpec((D_MODEL, tf), lambda i, j: (0, j)),
                  pl.BlockSpec((D_MODEL, tf), lambda i, j: (0, nf + j)),
                  pl.BlockSpec((tf, D_MODEL), lambda i, j: (j, 0))],
        out_specs=pl.BlockSpec((tm, D_MODEL), lambda i, j: (i, 0)),
        out_shape=jax.ShapeDtypeStruct((t, D_MODEL), F32),
        scratch_shapes=[pltpu.VMEM((tm, D_MODEL), BF), pltpu.VMEM((tm, D_MODEL), F32)],
        compiler_params=_params("parallel", "arbitrary"),
        name="ffn",
    )(h, g, w13, w13, w2)


def _split_slabs(ref):
    rows = ref.shape[0] // ROW_SLABS
    return jnp.concatenate([ref[pl.ds(c, rows, stride=ROW_SLABS), :] for c in range(ROW_SLABS)], axis=1)


def _store_slabs(ref, x):
    rows = ref.shape[0] // ROW_SLABS
    for c in range(ROW_SLABS):
        ref[pl.ds(c, rows, stride=ROW_SLABS), :] = x[:, c * LANES:(c + 1) * LANES]


def _slab_spec(tm, index_map):
    return pl.BlockSpec((tm * ROW_SLABS, LANES), index_map)


def _router_kernel(h_ref, g_ref, whi_ref, wlo_ref, br_ref, hn_ref, idx_ref, wts_ref):
    hn = _rms(h_ref[...], g_ref[...])
    _store_slabs(hn_ref, hn)
    hi = hn.astype(BF)
    lo = (hn - hi.astype(F32)).astype(BF)
    logits = _dot(hi, whi_ref[...]) + (_dot(lo, whi_ref[...]) + _dot(hi, wlo_ref[...])) + br_ref[...]
    lane = lax.broadcasted_iota(jnp.int32, logits.shape, 1)
    logits = jnp.where(lane < N_EXPERTS, logits, NEG)
    v1 = jnp.max(logits, axis=-1, keepdims=True)
    i1 = jnp.min(jnp.where(logits == v1, lane, LANES), axis=-1, keepdims=True)
    rest = jnp.where(lane == i1, NEG, logits)
    v2 = jnp.max(rest, axis=-1, keepdims=True)
    i2 = jnp.min(jnp.where(rest == v2, lane, LANES), axis=-1, keepdims=True)
    e2 = jnp.exp(v2 - v1)
    w1 = 1.0 / (1.0 + e2)
    w2 = e2 / (1.0 + e2)
    idx_ref[...] = jnp.where(lane == 0, i1, jnp.where(lane == 1, i2, 0))
    wts_ref[...] = jnp.where(lane == 0, w1, jnp.where(lane == 1, w2, 0.0))


def _router(h, g, whi, wlo, br, tm=512):
    t = h.shape[0]
    return pl.pallas_call(
        _router_kernel,
        grid=(t // tm,),
        in_specs=[pl.BlockSpec((tm, D_MODEL), lambda i: (i, 0)),
                  _const_spec((1, D_MODEL)),
                  _const_spec((D_MODEL, LANES)),
                  _const_spec((D_MODEL, LANES)),
                  _const_spec((1, LANES))],
        out_specs=[_slab_spec(tm, lambda i: (i, 0)),
                   pl.BlockSpec((tm, LANES), lambda i: (i, 0)),
                   pl.BlockSpec((tm, LANES), lambda i: (i, 0))],
        out_shape=[jax.ShapeDtypeStruct((t * ROW_SLABS, LANES), F32),
                   jax.ShapeDtypeStruct((t, LANES), jnp.int32),
                   jax.ShapeDtypeStruct((t, LANES), F32)],
        compiler_params=_params("parallel"),
        name="router",
    )(h, g, whi, wlo, br)


def _route_plan(idx, tm):
    t = idx.shape[0]
    n_pairs = 2 * t
    n_rows = n_pairs + N_EXPERTS * tm
    e_flat = jnp.concatenate([idx[:, 0], idx[:, 1]])
    onehot = (e_flat[:, None] == jnp.arange(N_EXPERTS, dtype=jnp.int32)[None, :]).astype(jnp.int32)
    csum = jnp.cumsum(onehot, axis=0)
    rank = jnp.sum(onehot * csum, axis=1) - 1
    counts = csum[-1]
    padded = ((counts + tm - 1) // tm) * tm
    ends = jnp.cumsum(padded)
    starts = ends - padded
    pos = starts[e_flat] + rank
    order = jnp.argsort(e_flat, stable=True).astype(jnp.int32)
    first = jnp.cumsum(counts) - counts
    r = jnp.minimum(jnp.arange(n_rows, dtype=jnp.int32), ends[-1] - 1)
    e_r = jnp.minimum(jnp.searchsorted(ends, r, side="right").astype(jnp.int32), N_EXPERTS - 1)
    local = r - starts[e_r]
    src_pair = order[jnp.clip(first[e_r] + local, 0, n_pairs - 1)]
    src_tok = jnp.where(local < counts[e_r], src_pair % t, 0).astype(jnp.int32)
    tile_expert = e_r[::tm]
    n_valid = (ends[-1] // tm).astype(jnp.int32).reshape(1)
    return pos.astype(jnp.int32), src_tok, tile_expert, n_valid


def _gather_rows(table, idx):
    n = idx.shape[0]
    info = plsc.get_sparse_core_info()
    n_workers = info.num_cores * info.num_subcores
    per_worker = n // n_workers
    assert per_worker * n_workers == n and per_worker % SC_CHUNK == 0, (n, n_workers)
    mesh = plsc.VectorSubcoreMesh(core_axis_name="c", subcore_axis_name="s")

    @functools.partial(
        pl.kernel, mesh=mesh,
        out_type=jax.ShapeDtypeStruct((n,) + table.shape[1:], table.dtype),
        scratch_types=[pltpu.VMEM((SC_CHUNK,), jnp.int32),
                       pltpu.VMEM((SC_CHUNK,) + table.shape[1:], table.dtype),
                       pltpu.SemaphoreType.DMA],
    )
    def gather(table_hbm, idx_hbm, out_hbm, idx_v, rows_v, sem):
        worker = lax.axis_index("s") * info.num_cores + lax.axis_index("c")
        base = worker * per_worker

        @pl.loop(0, per_worker // SC_CHUNK)
        def _(i):
            off = pl.multiple_of(base + i * SC_CHUNK, SC_CHUNK)
            pltpu.sync_copy(idx_hbm.at[pl.ds(off, SC_CHUNK)], idx_v)
            pltpu.async_copy(table_hbm.at[idx_v], rows_v, sem).wait()
            pltpu.sync_copy(rows_v, out_hbm.at[pl.ds(off, SC_CHUNK)])

    return gather(table, idx)


def _expert_ffn_kernel(te_ref, nv_ref, x_ref, w1_ref, w3_ref, w2_ref, y_ref, xb_ref, acc_ref):
    i = pl.program_id(0)
    j = pl.program_id(1)
    valid = i < nv_ref[0]

    @pl.when(valid & (j == 0))
    def _():
        xb_ref[...] = _split_slabs(x_ref).astype(BF)
        acc_ref[...] = jnp.zeros_like(acc_ref)

    @pl.when(valid)
    def _():
        xb = xb_ref[...]
        gate = _dot(xb, w1_ref[0])
        up = _dot(xb, w3_ref[0])
        act = (gate * _sigmoid(gate) * up).astype(BF)
        acc_ref[...] += _dot(act, w2_ref[0])

    @pl.when(j == pl.num_programs(1) - 1)
    def _():
        _store_slabs(y_ref, jnp.where(valid, acc_ref[...], 0.0))


def _expert_ffn(xs, tile_expert, n_valid, w13, w2, tm, tf=FF_TILE):
    n_rows = xs.shape[0] // ROW_SLABS
    nf = D_FF // tf
    last = lambda nv: jnp.maximum(nv[0] - 1, 0)
    grid_spec = pltpu.PrefetchScalarGridSpec(
        num_scalar_prefetch=2,
        grid=(n_rows // tm, nf),
        in_specs=[_slab_spec(tm, lambda i, j, te, nv: (jnp.minimum(i, last(nv)), 0)),
                  pl.BlockSpec((1, D_MODEL, tf), lambda i, j, te, nv: (te[i], 0, j)),
                  pl.BlockSpec((1, D_MODEL, tf), lambda i, j, te, nv: (te[i], 0, nf + j)),
                  pl.BlockSpec((1, tf, D_MODEL), lambda i, j, te, nv: (te[i], j, 0))],
        out_specs=_slab_spec(tm, lambda i, j, te, nv: (i, 0)),
        scratch_shapes=[pltpu.VMEM((tm, D_MODEL), BF), pltpu.VMEM((tm, D_MODEL), F32)],
    )
    return pl.pallas_call(
        _expert_ffn_kernel,
        grid_spec=grid_spec,
        out_shape=jax.ShapeDtypeStruct((n_rows * ROW_SLABS, LANES), F32),
        compiler_params=_params("arbitrary", "arbitrary"),
        name="expert_ffn",
    )(tile_expert, n_valid, xs, w13, w13, w2)


def _combine_kernel(h_ref, y0_ref, y1_ref, wts_ref, gf_ref, out_ref):
    wts = wts_ref[...]
    tot = h_ref[...] + wts[:, 0:1] * _split_slabs(y0_ref) + wts[:, 1:2] * _split_slabs(y1_ref)
    out_ref[...] = _rms(tot, gf_ref[...])


def _combine(h, yg, wts, g_final, tm=512):
    t = h.shape[0]
    nt = t // tm
    return pl.pallas_call(
        _combine_kernel,
        grid=(nt,),
        in_specs=[pl.BlockSpec((tm, D_MODEL), lambda i: (i, 0)),
                  _slab_spec(tm, lambda i: (i, 0)),
                  _slab_spec(tm, lambda i: (nt + i, 0)),
                  pl.BlockSpec((tm, LANES), lambda i: (i, 0)),
                  _const_spec((1, D_MODEL))],
        out_specs=pl.BlockSpec((tm, D_MODEL), lambda i: (i, 0)),
        out_shape=jax.ShapeDtypeStruct((t, D_MODEL), F32),
        compiler_params=_params("parallel"),
        name="combine",
    )(h, yg, yg, wts, g_final)


def _moe(h, g, router_w, router_b, w13, w2, g_final, tm=512):
    wr = jnp.pad(router_w, ((0, 0), (0, LANES - N_EXPERTS)))
    whi = wr.astype(BF)
    wlo = (wr - whi.astype(F32)).astype(BF)
    br = jnp.pad(router_b, (0, LANES - N_EXPERTS)).reshape(1, LANES)
    hn, idx, wts = _router(h, g, whi, wlo, br)
    pos, src_tok, tile_expert, n_valid = _route_plan(idx, tm)
    as_rows = lambda a: a.reshape(-1, ROW_SLABS, LANES)
    as_slabs = lambda a: a.reshape(-1, LANES)
    xs = as_slabs(_gather_rows(as_rows(hn), src_tok))
    ys = _expert_ffn(xs, tile_expert, n_valid, w13, w2, tm)
    yg = as_slabs(_gather_rows(as_rows(ys), pos))
    return _combine(h, yg, wts, g_final)


def _pack_w_in(w_in):
    cuts = np.cumsum((512, 512, 512, 512, 512, 128, 128, 512, 512, 512, 8))[:-1].tolist()
    a_u, a_v, b_x, b_y, c_q, c_k, c_v, d_q, d_k, d_v, d_f = jnp.split(w_in, cuts, axis=-1)
    w = jnp.concatenate([a_u, a_v, b_x, b_y, c_q, d_q, d_k, c_k, c_v], axis=-1).astype(BF)
    wf = jnp.pad(d_f, ((0, 0), (0, LANES - FOX_HEADS))).astype(BF)
    wvt = d_v.T.reshape(FOX_HEADS, HEAD_DIM, D_MODEL)
    wvt = jnp.pad(wvt, ((0, 0), (0, VT_ROWS - HEAD_DIM), (0, 0))).reshape(VT_ALL, D_MODEL).astype(BF)
    return w, wf, wvt


def _vt_ones():
    ones = np.zeros((FOX_HEADS, VT_ROWS, LANES), np.float32)
    ones[:, HEAD_DIM, :] = 1.0
    return jnp.asarray(ones.reshape(VT_ALL, LANES))


def _row(v):
    return v.reshape(1, -1)


def _hybrid_mixer(h, bsz, norm_mix, w_in, sgu_g, sgu_w, sgu_b, conv_w, conv_b, rg_wa, rg_ba, rg_wx, rg_bx,
                  rg_lambda, swa_sinks, fox_bf, w_branch, w_gate, b_gate, w_out):
    w, wf, wvt = _pack_w_in(w_in)
    proj, f, vt = _inproj(h, _row(norm_mix), w, wf, wvt, _vt_ones())
    o_a = _sgu(proj, _row(sgu_g), sgu_w, sgu_b.T)
    wax = jnp.concatenate([rg_wa, rg_wx], axis=-1).astype(BF)
    o_b = _rglru(proj, bsz, conv_w, _row(conv_b), wax, _row(rg_ba), _row(rg_bx), _row(rg_lambda))
    o_c = _swa(proj, bsz, swa_sinks)
    bf_pad = jnp.pad(fox_bf, (0, LANES - FOX_HEADS)).reshape(1, LANES)
    qp, kp = _fox_prep(proj, f, bsz, bf_pad)
    o_d = _fox(qp, kp, vt, bsz)
    return _merge(h, _row(norm_mix), o_a, o_b, o_c, o_d, w_gate.astype(BF), b_gate[:, None, :],
                  w_branch.astype(BF), w_out.astype(BF))


def kernel(x, mem, norm_mix, w_in, sgu_g, sgu_w, sgu_b, conv_w, conv_b, rg_wa, rg_ba, rg_wx, rg_bx, rg_lambda, swa_sinks, fox_bf, w_branch, w_gate, b_gate, w_out, norm_cross, norm_mem, wq_c, wkv_c, wo_c, norm_ffn, dense_w13, dense_w2, router_w, router_b, moe_w13, moe_w2, norm_final):
    bsz, seq, d = x.shape
    m_len = mem.shape[1]
    depth = norm_mix.shape[0]
    assert depth == 2, "the final RMSNorm is fused into the routed layer, which must be the last one"
    h = x.reshape(bsz * seq, d)
    mem2 = mem.reshape(bsz * m_len, d)
    for l in range(depth):
        h = _hybrid_mixer(h, bsz, norm_mix[l], w_in[l], sgu_g[l], sgu_w[l], sgu_b[l], conv_w[l], conv_b[l],
                          rg_wa[l], rg_ba[l], rg_wx[l], rg_bx[l], rg_lambda[l], swa_sinks[l], fox_bf[l],
                          w_branch[l], w_gate[l], b_gate[l], w_out[l])
        kv = _memkv(mem2, _row(norm_mem[l]), wkv_c[l].astype(BF), m_len)
        h = _cross(h, _row(norm_cross[l]), wq_c[l].astype(BF), kv, wo_c[l].astype(BF), bsz, m_len)
        if l % 2 == 0:
            h = _ffn(h, _row(norm_ffn[l]), dense_w13[l // 2].astype(BF), dense_w2[l // 2].astype(BF))
        else:
            h = _moe(h, _row(norm_ffn[l]), router_w[l // 2], router_b[l // 2], moe_w13[l // 2].astype(BF),
                     moe_w2[l // 2].astype(BF), _row(norm_final))
    return h.reshape(bsz, seq, d)
```

```python
import functools
import math

import numpy as np
import jax
import jax.numpy as jnp
from jax import lax
from jax.experimental import pallas as pl
from jax.experimental.pallas import tpu as pltpu
from jax.experimental.pallas import tpu_sc as plsc

F32 = jnp.float32
BF = jnp.bfloat16

D_MODEL = 1024
BRANCH_W = 512
HEAD_DIM = 64
CHUNK = 128
SGU_GROUPS = 4
LRU_HEADS = 4
LRU_C = 8.0
CONV_W = 4
SWA_HEADS = 8
SWA_KV = 2
FOX_HEADS = 8
FOX_PAIRS = FOX_HEADS // 2
X_HEADS = 4
X_HEAD_DIM = 128
D_FF = 2816
N_EXPERTS = 8
EPS = 1e-6
LOG2E = math.log2(math.e)
NEG = -1e30
LANES = 128
ROW_SLABS = D_MODEL // LANES
SC_CHUNK = 64
VMEM_LIMIT = 56 * 1024 * 1024

COL_AU, COL_AV, COL_BX, COL_BY, COL_CQ, COL_DQ, COL_DK = range(7)
N_WIDE = 7
N_PROJ = N_WIDE * BRANCH_W + 2 * SWA_KV * HEAD_DIM
VT_ROWS = 80
VT_ALL = FOX_HEADS * VT_ROWS
FOX_UNROLL = 4
FF_SPLITS = ((0, 1536), (1536, D_FF))


def _rms(x, g):
    return x * lax.rsqrt(jnp.mean(x * x, axis=-1, keepdims=True) + EPS) * g


def _sigmoid(x):
    return 1.0 / (1.0 + jnp.exp(-x))


def _gelu(x):
    return 0.5 * x * (1.0 + jnp.tanh(math.sqrt(2.0 / math.pi) * (x + 0.044715 * (x * x * x))))


def _dot(a, b):
    return jnp.dot(a, b, preferred_element_type=F32)


def _dot_nt(a, b):
    return lax.dot_general(a, b, (((1,), (1,)), ((), ())), preferred_element_type=F32)


def _shift_rows(x, d, fill):
    row = lax.broadcasted_iota(jnp.int32, x.shape, 0)
    return jnp.where(row >= d, pltpu.roll(x, d, 0), fill)


def _params(*sem):
    return pltpu.CompilerParams(dimension_semantics=sem, vmem_limit_bytes=VMEM_LIMIT)


def _const_spec(shape):
    nd = len(shape)
    return pl.BlockSpec(shape, lambda *_: (0,) * nd, pipeline_mode=pl.Buffered(1))


def _inproj_kernel(h_ref, g_ref, w_ref, wf_ref, wvt_ref, ones_ref, proj_ref, f_ref, vt_ref):
    xn = _rms(h_ref[...], g_ref[...]).astype(BF)
    n_col = w_ref.shape[1]
    for c in range(0, n_col, BRANCH_W):
        w = min(BRANCH_W, n_col - c)
        proj_ref[:, c:c + w] = _dot(xn, w_ref[:, c:c + w]).astype(BF)
    f_ref[...] = _dot(xn, wf_ref[...])
    ones = jnp.concatenate([ones_ref[...]] * (xn.shape[0] // LANES), axis=1)
    vt_ref[...] = (_dot_nt(wvt_ref[...], xn) + ones).astype(BF)


def _inproj(h, g, w, wf, wvt, ones, tm=512):
    t = h.shape[0]
    return pl.pallas_call(
        _inproj_kernel,
        grid=(t // tm,),
        in_specs=[pl.BlockSpec((tm, D_MODEL), lambda i: (i, 0)),
                  _const_spec((1, D_MODEL)),
                  _const_spec((D_MODEL, N_PROJ)),
                  _const_spec((D_MODEL, LANES)),
                  _const_spec((VT_ALL, D_MODEL)),
                  _const_spec((VT_ALL, LANES))],
        out_specs=[pl.BlockSpec((tm, N_PROJ), lambda i: (i, 0)),
                   pl.BlockSpec((tm, LANES), lambda i: (i, 0)),
                   pl.BlockSpec((VT_ALL, tm), lambda i: (0, i))],
        out_shape=[jax.ShapeDtypeStruct((t, N_PROJ), BF), jax.ShapeDtypeStruct((t, LANES), F32),
                   jax.ShapeDtypeStruct((VT_ALL, t), BF)],
        compiler_params=_params("parallel"),
        name="inproj",
    )(h, g, w, wf, wvt, ones)


def _sgu_kernel(u_ref, v_ref, g_ref, w_ref, bt_ref, o_ref):
    u = _gelu(u_ref[...].astype(F32))
    v = _gelu(v_ref[...].astype(F32))
    vn = _rms(v, g_ref[...]).astype(BF)
    row = lax.broadcasted_iota(jnp.int32, (CHUNK, CHUNK), 0)
    col = lax.broadcasted_iota(jnp.int32, (CHUNK, CHUNK), 1)
    tm = u.shape[0]
    gw = BRANCH_W // SGU_GROUPS
    for g in range(SGU_GROUPS):
        wg = jnp.where(col <= row, w_ref[g], 0.0).astype(BF)
        bg = bt_ref[:, g:g + 1]
        for c in range(tm // CHUNK):
            rs = slice(c * CHUNK, (c + 1) * CHUNK)
            cs = slice(g * gw, (g + 1) * gw)
            mixed = _dot(wg, vn[rs, cs]) + bg
            o_ref[rs, cs] = (u[rs, cs] * mixed).astype(BF)


def _sgu(proj, g, w, bt, tm=512):
    t = proj.shape[0]
    return pl.pallas_call(
        _sgu_kernel,
        grid=(t // tm,),
        in_specs=[pl.BlockSpec((tm, BRANCH_W), lambda i: (i, COL_AU)),
                  pl.BlockSpec((tm, BRANCH_W), lambda i: (i, COL_AV)),
                  _const_spec((1, BRANCH_W)),
                  _const_spec((SGU_GROUPS, CHUNK, CHUNK)),
                  _const_spec((CHUNK, SGU_GROUPS))],
        out_specs=pl.BlockSpec((tm, BRANCH_W), lambda i: (i, 0)),
        out_shape=jax.ShapeDtypeStruct((t, BRANCH_W), BF),
        compiler_params=_params("parallel"),
        name="sgu",
    )(proj, proj, g, w, bt)


def _rglru_kernel(x_ref, y_ref, cw_ref, cb_ref, wax_ref, ba_ref, bx_ref, lam_ref, o_ref, xs_ref, hc_ref):
    ts = x_ref.shape[0]
    hw = BRANCH_W // LRU_HEADS

    @pl.when(pl.program_id(1) == 0)
    def _():
        xs_ref[0:8, :] = jnp.zeros((8, BRANCH_W), F32)
        hc_ref[...] = jnp.zeros_like(hc_ref)

    x = x_ref[...].astype(F32)
    xs_ref[8:8 + ts, :] = x
    cw = cw_ref[...]
    xc = cb_ref[...] + cw[CONV_W - 1:CONV_W] * x
    for k in range(1, CONV_W):
        xc = xc + cw[CONV_W - 1 - k:CONV_W - k] * xs_ref[8 - k:8 - k + ts, :]
    xs_ref[0:8, :] = x[ts - 8:ts, :]

    xcb = xc.astype(BF)
    r_parts, i_parts = [], []
    for hd in range(LRU_HEADS):
        z = _dot(xcb[:, hd * hw:(hd + 1) * hw], wax_ref[hd])
        r_parts.append(z[:, :hw])
        i_parts.append(z[:, hw:])
    r = _sigmoid(jnp.concatenate(r_parts, axis=1) + ba_ref[...])
    gi = _sigmoid(jnp.concatenate(i_parts, axis=1) + bx_ref[...])
    nl = -lam_ref[...]
    softplus = jnp.maximum(nl, 0.0) + jnp.log1p(jnp.exp(-jnp.abs(nl)))
    log_a = (-LRU_C * r) * softplus
    a = jnp.exp(log_a)
    th = jnp.tanh(log_a)
    b = (xc * gi) * jnp.sqrt(-2.0 * th / (1.0 - th))

    d = 1
    while d < ts:
        a_sh = _shift_rows(a, d, 1.0)
        b_sh = _shift_rows(b, d, 0.0)
        b = b + a * b_sh
        a = a * a_sh
        d *= 2
    h = a * hc_ref[...] + b
    hc_ref[...] = h[ts - 1:ts, :]
    o_ref[...] = (h * _gelu(y_ref[...].astype(F32))).astype(BF)


def _rglru(proj, bsz, cw, cb, wax, ba, bx, lam, ts=256):
    t = proj.shape[0]
    ns = t // bsz // ts
    return pl.pallas_call(
        _rglru_kernel,
        grid=(bsz, ns),
        in_specs=[pl.BlockSpec((ts, BRANCH_W), lambda b, j: (b * ns + j, COL_BX)),
                  pl.BlockSpec((ts, BRANCH_W), lambda b, j: (b * ns + j, COL_BY)),
                  _const_spec((CONV_W, BRANCH_W)),
                  _const_spec((1, BRANCH_W)),
                  _const_spec((LRU_HEADS, BRANCH_W // LRU_HEADS, 2 * BRANCH_W // LRU_HEADS)),
                  _const_spec((1, BRANCH_W)),
                  _const_spec((1, BRANCH_W)),
                  _const_spec((1, BRANCH_W))],
        out_specs=pl.BlockSpec((ts, BRANCH_W), lambda b, j: (b * ns + j, 0)),
        out_shape=jax.ShapeDtypeStruct((t, BRANCH_W), BF),
        scratch_shapes=[pltpu.VMEM((ts + 8, BRANCH_W), F32), pltpu.VMEM((1, BRANCH_W), F32)],
        compiler_params=_params("parallel", "arbitrary"),
        name="rglru",
    )(proj, proj, cw, cb, wax, ba, bx, lam)


def _swa_kernel(q_ref, kv_ref, kvp_ref, sink_ref, o_ref):
    tq = q_ref.shape[0]
    first_key = jnp.where(pl.program_id(1) == 0, CHUNK, 0)
    kv_all = jnp.concatenate([kvp_ref[...], kv_ref[...]], axis=0)
    q = q_ref[...]
    qi = lax.broadcasted_iota(jnp.int32, (CHUNK, 2 * CHUNK), 0)
    sj = lax.broadcasted_iota(jnp.int32, (CHUNK, 2 * CHUNK), 1)
    dist = qi + CHUNK - sj
    in_win = (dist >= 0) & (dist < CHUNK)
    distf = dist.astype(F32)
    grp = SWA_HEADS // SWA_KV
    kvw = SWA_KV * HEAD_DIM
    for qb in range(tq // CHUNK):
        kvb = kv_all[qb * CHUNK:(qb + 2) * CHUNK]
        if qb == 0:
            valid = in_win & (sj >= first_key)
        else:
            valid = in_win
        outs = []
        for kh in range(SWA_KV):
            k = kvb[:, kh * HEAD_DIM:(kh + 1) * HEAD_DIM]
            v = kvb[:, kvw + kh * HEAD_DIM:kvw + (kh + 1) * HEAD_DIM]
            for g in range(grp):
                hh = kh * grp + g
                slope = 2.0 ** (-(8.0 / SWA_HEADS) * (hh + 1))
                qh = q[qb * CHUNK:(qb + 1) * CHUNK, hh * HEAD_DIM:(hh + 1) * HEAD_DIM]
                s = _dot_nt(qh, k) * (HEAD_DIM ** -0.5) - slope * distf
                s = jnp.where(valid, s, NEG)
                sink = sink_ref[hh]
                m = jnp.maximum(jnp.max(s, axis=-1, keepdims=True), sink)
                p = jnp.exp(s - m)
                denom = jnp.sum(p, axis=-1, keepdims=True) + jnp.exp(sink - m)
                outs.append(_dot(p.astype(BF), v) / denom)
        o_ref[qb * CHUNK:(qb + 1) * CHUNK, :] = jnp.concatenate(outs, axis=1).astype(BF)


def _swa(proj, bsz, sinks, tq=512):
    t = proj.shape[0]
    seq = t // bsz
    nq = seq // tq
    per = tq // CHUNK
    kv_blk = N_WIDE * BRANCH_W // (2 * SWA_KV * HEAD_DIM)
    return pl.pallas_call(
        _swa_kernel,
        grid=(bsz, nq),
        in_specs=[pl.BlockSpec((tq, BRANCH_W), lambda b, i: (b * nq + i, COL_CQ)),
                  pl.BlockSpec((tq, 2 * SWA_KV * HEAD_DIM), lambda b, i: (b * nq + i, kv_blk)),
                  pl.BlockSpec((CHUNK, 2 * SWA_KV * HEAD_DIM),
                               lambda b, i: (b * (seq // CHUNK) + jnp.maximum(i * per - 1, 0), kv_blk)),
                  pl.BlockSpec(memory_space=pltpu.SMEM)],
        out_specs=pl.BlockSpec((tq, BRANCH_W), lambda b, i: (b * nq + i, 0)),
        out_shape=jax.ShapeDtypeStruct((t, BRANCH_W), BF),
        compiler_params=_params("parallel", "parallel"),
        name="swa",
    )(proj, proj, proj, sinks)


def _aug_tables():
    eq = np.zeros((3, LANES, FOX_PAIRS * LANES), np.float32)
    ek = np.zeros((3, LANES, FOX_PAIRS * LANES), np.float32)
    oq = np.zeros((1, FOX_PAIRS * LANES), np.float32)
    ok = np.zeros((1, FOX_PAIRS * LANES), np.float32)
    for h in range(FOX_HEADS):
        base = (h // 2) * LANES + 6 * (h % 2)
        for s in range(3):
            eq[s, h, base + s] = 1.0
            ek[s, h, base + 3 + s] = -1.0
            ok[0, base + s] = 1.0
            oq[0, base + 3 + s] = 1.0
    return eq, ek, oq, ok


def _fox_prep_kernel(q_ref, k_ref, f_ref, bf_ref, eq_ref, ek_ref, oq_ref, ok_ref, qp_ref, kp_ref, cum_ref):
    ts = q_ref.shape[0]

    @pl.when(pl.program_id(1) == 0)
    def _():
        cum_ref[...] = jnp.zeros_like(cum_ref)

    z = f_ref[...] + bf_ref[...]
    c = jnp.minimum(z, 0.0) - jnp.log1p(jnp.exp(-jnp.abs(z)))
    d = 1
    while d < ts:
        c = c + _shift_rows(c, d, 0.0)
        d *= 2
    c = c + cum_ref[...]
    cum_ref[...] = c[ts - 1:ts, :]
    c = c * LOG2E
    c1 = c.astype(BF)
    r1 = c - c1.astype(F32)
    c2 = r1.astype(BF)
    c3 = (r1 - c2.astype(F32)).astype(BF)
    augq = _dot(c1, eq_ref[0]) + _dot(c2, eq_ref[1]) + _dot(c3, eq_ref[2]) + oq_ref[...]
    augk = _dot(c1, ek_ref[0]) + _dot(c2, ek_ref[1]) + _dot(c3, ek_ref[2]) + ok_ref[...]
    qt = (q_ref[...].astype(F32) * (HEAD_DIM ** -0.5 * LOG2E)).T.astype(BF)
    augqt = augq.T.astype(BF)
    k = k_ref[...]
    for p in range(FOX_PAIRS):
        qp_ref[2 * p * LANES:(2 * p + 1) * LANES, :] = qt[p * LANES:(p + 1) * LANES]
        qp_ref[(2 * p + 1) * LANES:(2 * p + 2) * LANES, :] = augqt[p * LANES:(p + 1) * LANES]
        kp_ref[:, 2 * p * LANES:(2 * p + 1) * LANES] = k[:, p * LANES:(p + 1) * LANES]
        kp_ref[:, (2 * p + 1) * LANES:(2 * p + 2) * LANES] = augk[:, p * LANES:(p + 1) * LANES].astype(BF)


def _fox_prep(proj, f, bsz, bf_pad, ts=512):
    t = proj.shape[0]
    ns = t // bsz // ts
    eq, ek, oq, ok = _aug_tables()
    wide = 2 * FOX_PAIRS * LANES
    return pl.pallas_call(
        _fox_prep_kernel,
        grid=(bsz, ns),
        in_specs=[pl.BlockSpec((ts, BRANCH_W), lambda b, j: (b * ns + j, COL_DQ)),
                  pl.BlockSpec((ts, BRANCH_W), lambda b, j: (b * ns + j, COL_DK)),
                  pl.BlockSpec((ts, LANES), lambda b, j: (b * ns + j, 0)),
                  _const_spec((1, LANES)),
                  _const_spec(eq.shape), _const_spec(ek.shape), _const_spec(oq.shape), _const_spec(ok.shape)],
        out_specs=[pl.BlockSpec((wide, ts), lambda b, j: (0, b * ns + j)),
                   pl.BlockSpec((ts, wide), lambda b, j: (b * ns + j, 0))],
        out_shape=[jax.ShapeDtypeStruct((wide, t), BF), jax.ShapeDtypeStruct((t, wide), BF)],
        scratch_shapes=[pltpu.VMEM((1, LANES), F32)],
        compiler_params=_params("parallel", "arbitrary"),
        name="fox_prep",
    )(proj, proj, f, bf_pad, jnp.asarray(eq, BF), jnp.asarray(ek, BF), jnp.asarray(oq), jnp.asarray(ok))


def _fox_kernel(q_ref, k_ref, vt_ref, o_ref, st_ref, pt_ref, al_ref, m_ref, acc_ref):
    tq = q_ref.shape[1]
    tk = tq // 2
    qi = pl.program_id(2)
    row = lax.broadcasted_iota(jnp.int32, (2 * LANES, tq), 0)
    q = q_ref[...]
    zero = jnp.zeros_like(q)
    in0 = (row < HEAD_DIM) | ((row >= LANES) & (row < LANES + 6))
    in1 = ((row >= HEAD_DIM) & (row < LANES)) | ((row >= LANES + 6) & (row < LANES + 12))
    qh = (jnp.where(in0, q, zero), jnp.where(in1, q, zero))
    for h in range(2):
        m_ref[h] = jnp.full(m_ref.shape[1:], NEG, F32)
        acc_ref[h] = jnp.zeros(acc_ref.shape[1:], F32)
        pt_ref[1, h] = jnp.zeros(pt_ref.shape[2:], BF)
        al_ref[1, h] = jnp.ones(al_ref.shape[2:], F32)

    def scores(t, slot):
        k = k_ref[pl.ds(pl.multiple_of(t * tk, tk), tk), :]
        for h in range(2):
            st_ref[slot, h] = _dot(k, qh[h])

    def numerators(slot, key_offset):
        for h in range(2):
            st = st_ref[slot, h]
            if key_offset is not None:
                key = lax.broadcasted_iota(jnp.int32, (tk, tq), 0) + key_offset
                qry = lax.broadcasted_iota(jnp.int32, (tk, tq), 1)
                st = jnp.where(key <= qry, st, NEG)
            m_old = m_ref[h]
            m_new = jnp.maximum(m_old, jnp.max(st, axis=0, keepdims=True))
            al_ref[slot, h] = jnp.exp2(m_old - m_new)
            pt_ref[slot, h] = jnp.exp2(st - m_new).astype(BF)
            m_ref[h] = m_new

    def accumulate(t, slot):
        vt = vt_ref[:, pl.ds(pl.multiple_of(t * tk, tk), tk)]
        for h in range(2):
            acc_ref[h] = al_ref[slot, h] * acc_ref[h] + _dot(vt[h * VT_ROWS:(h + 1) * VT_ROWS], pt_ref[slot, h])

    def tile_pair(j, diagonal):
        scores(2 * j + 1, 1)
        accumulate(jnp.maximum(2 * j - 1, 0), 1)
        numerators(0, 0 if diagonal else None)
        if not diagonal:
            scores(2 * j + 2, 0)
        accumulate(2 * j, 0)
        numerators(1, tk if diagonal else None)

    def body(jj, carry):
        for u in range(FOX_UNROLL):
            tile_pair(FOX_UNROLL * jj + u, False)
        return carry

    def remainder(j, carry):
        tile_pair(j, False)
        return carry

    scores(0, 0)
    lax.fori_loop(0, qi // FOX_UNROLL, body, 0)
    lax.fori_loop((qi // FOX_UNROLL) * FOX_UNROLL, qi, remainder, 0)
    tile_pair(qi, True)
    accumulate(2 * qi + 1, 1)
    outs = []
    for h in range(2):
        acc = acc_ref[h]
        outs.append(acc[:HEAD_DIM] / acc[HEAD_DIM:HEAD_DIM + 1])
    o_ref[...] = jnp.concatenate(outs, axis=0).T.astype(BF)


def _fox(qp, kp, vt, bsz, tq=512):
    t = kp.shape[0]
    seq = t // bsz
    nq = seq // tq
    return pl.pallas_call(
        _fox_kernel,
        grid=(bsz, FOX_PAIRS, nq),
        in_specs=[pl.BlockSpec((2 * LANES, tq), lambda b, p, i: (p, b * nq + i)),
                  pl.BlockSpec((seq, 2 * LANES), lambda b, p, i: (b, p)),
                  pl.BlockSpec((2 * VT_ROWS, seq), lambda b, p, i: (p, b))],
        out_specs=pl.BlockSpec((tq, LANES), lambda b, p, i: (b * nq + i, p)),
        out_shape=jax.ShapeDtypeStruct((t, BRANCH_W), BF),
        scratch_shapes=[pltpu.VMEM((2, 2, tq // 2, tq), F32),
                        pltpu.VMEM((2, 2, tq // 2, tq), BF),
                        pltpu.VMEM((2, 2, 1, tq), F32),
                        pltpu.VMEM((2, 1, tq), F32),
                        pltpu.VMEM((2, VT_ROWS, tq), F32)],
        compiler_params=_params("parallel", "parallel", "arbitrary"),
        name="fox",
    )(qp, kp, vt)


def _merge_kernel(h_ref, g_ref, oa_ref, ob_ref, oc_ref, od_ref, wg_ref, bg_ref, wb_ref, wo_ref, out_ref):
    h = h_ref[...]
    xn = _rms(h, g_ref[...]).astype(BF)
    merged = None
    for br, o_ref in enumerate((oa_ref, ob_ref, oc_ref, od_ref)):
        gate = _sigmoid(_dot(xn, wg_ref[br]) + bg_ref[br])
        term = gate * _dot(o_ref[...], wb_ref[br])
        merged = term if merged is None else merged + term
    out_ref[...] = h + _dot(merged.astype(BF), wo_ref[...])


def _merge(h, g, oa, ob, oc, od, wg, bg, wb, wo, tm=512):
    t = h.shape[0]
    row = lambda i: (i, 0)
    return pl.pallas_call(
        _merge_kernel,
        grid=(t // tm,),
        in_specs=[pl.BlockSpec((tm, D_MODEL), row),
                  _const_spec((1, D_MODEL)),
                  pl.BlockSpec((tm, BRANCH_W), row), pl.BlockSpec((tm, BRANCH_W), row),
                  pl.BlockSpec((tm, BRANCH_W), row), pl.BlockSpec((tm, BRANCH_W), row),
                  _const_spec((4, D_MODEL, D_MODEL)),
                  _const_spec((4, 1, D_MODEL)),
                  _const_spec((4, BRANCH_W, D_MODEL)),
                  _const_spec((D_MODEL, D_MODEL))],
        out_specs=pl.BlockSpec((tm, D_MODEL), row),
        out_shape=jax.ShapeDtypeStruct((t, D_MODEL), F32),
        compiler_params=_params("parallel"),
        name="merge",
    )(h, g, oa, ob, oc, od, wg, bg, wb, wo)


def _memkv_kernel(mem_ref, g_ref, w_ref, kv_ref):
    mn = _rms(mem_ref[...], g_ref[...]).astype(BF)
    kv_ref[...] = _dot(mn, w_ref[...]).astype(BF)


def _memkv(mem2, g, w, m_len):
    n = mem2.shape[0]
    width = 2 * X_HEADS * X_HEAD_DIM
    return pl.pallas_call(
        _memkv_kernel,
        grid=(n // m_len,),
        in_specs=[pl.BlockSpec((m_len, D_MODEL), lambda b: (b, 0)),
                  _const_spec((1, D_MODEL)),
                  _const_spec((D_MODEL, width))],
        out_specs=pl.BlockSpec((m_len, width), lambda b: (b, 0)),
        out_shape=jax.ShapeDtypeStruct((n, width), BF),
        compiler_params=_params("parallel"),
        name="memkv",
    )(mem2, g, w)


def _cross_kernel(h_ref, g_ref, wq_ref, kv_ref, wo_ref, out_ref):
    h = h_ref[...]
    hn = _rms(h, g_ref[...]).astype(BF)
    q = _dot(hn, wq_ref[...]).astype(BF)
    kv = kv_ref[...]
    width = X_HEADS * X_HEAD_DIM
    outs = []
    for hd in range(X_HEADS):
        cs = slice(hd * X_HEAD_DIM, (hd + 1) * X_HEAD_DIM)
        s = _dot_nt(q[:, cs], kv[:, cs]) * (X_HEAD_DIM ** -0.5)
        m = jnp.max(s, axis=-1, keepdims=True)
        p = jnp.exp(s - m)
        denom = jnp.sum(p, axis=-1, keepdims=True)
        v = kv[:, width + hd * X_HEAD_DIM:width + (hd + 1) * X_HEAD_DIM]
        outs.append((_dot(p.astype(BF), v) / denom).astype(BF))
    o = jnp.concatenate(outs, axis=1)
    out_ref[...] = h + _dot(o, wo_ref[...])


def _cross(h, g, wq, kv, wo, bsz, m_len, tm=512):
    t = h.shape[0]
    per = t // bsz // tm
    width = X_HEADS * X_HEAD_DIM
    return pl.pallas_call(
        _cross_kernel,
        grid=(bsz, per),
        in_specs=[pl.BlockSpec((tm, D_MODEL), lambda b, i: (b * per + i, 0)),
                  _const_spec((1, D_MODEL)),
                  _const_spec((D_MODEL, width)),
                  pl.BlockSpec((m_len, 2 * width), lambda b, i: (b, 0)),
                  _const_spec((width, D_MODEL))],
        out_specs=pl.BlockSpec((tm, D_MODEL), lambda b, i: (b * per + i, 0)),
        out_shape=jax.ShapeDtypeStruct((t, D_MODEL), F32),
        compiler_params=_params("parallel", "parallel"),
        name="cross",
    )(h, g, wq, kv, wo)


def _swiglu(xb, w13_ref, w2_ref):
    out = None
    for lo, hi in FF_SPLITS:
        gate = _dot(xb, w13_ref[:, lo:hi])
        up = _dot(xb, w13_ref[:, D_FF + lo:D_FF + hi])
        act = (gate * _sigmoid(gate) * up).astype(BF)
        part = _dot(act, w2_ref[lo:hi, :])
        out = part if out is None else out + part
    return out


def _ffn_kernel(h_ref, g_ref, w13_ref, w2_ref, out_ref):
    h = h_ref[...]
    out_ref[...] = h + _swiglu(_rms(h, g_ref[...]).astype(BF), w13_ref, w2_ref)


def _ffn(h, g, w13, w2, tm=512):
    t = h.shape[0]
    return pl.pallas_call(
        _ffn_kernel,
        grid=(t // tm,),
        in_specs=[pl.BlockSpec((tm, D_MODEL), lambda i: (i, 0)),
                  _const_spec((1, D_MODEL)),
                  _const_spec((D_MODEL, 2 * D_FF)),
                  _const_spec((D_FF, D_MODEL))],
        out_specs=pl.BlockSpec((tm, D_MODEL), lambda i: (i, 0)),
        out_shape=jax.ShapeDtypeStruct((t, D_MODEL), F32),
        compiler_params=_params("parallel"),
        name="ffn",
    )(h, g, w13, w2)


def _split_slabs(ref):
    rows = ref.shape[0] // ROW_SLABS
    return jnp.concatenate([ref[pl.ds(c, rows, stride=ROW_SLABS), :] for c in range(ROW_SLABS)], axis=1)


def _store_slabs(ref, x):
    rows = ref.shape[0] // ROW_SLABS
    for c in range(ROW_SLABS):
        ref[pl.ds(c, rows, stride=ROW_SLABS), :] = x[:, c * LANES:(c + 1) * LANES]


def _slab_spec(tm, index_map):
    return pl.BlockSpec((tm * ROW_SLABS, LANES), index_map)


def _router_kernel(h_ref, g_ref, whi_ref, wlo_ref, br_ref, hn_ref, idx_ref, wts_ref):
    hn = _rms(h_ref[...], g_ref[...])
    _store_slabs(hn_ref, hn)
    hi = hn.astype(BF)
    lo = (hn - hi.astype(F32)).astype(BF)
    logits = _dot(hi, whi_ref[...]) + (_dot(lo, whi_ref[...]) + _dot(hi, wlo_ref[...])) + br_ref[...]
    lane = lax.broadcasted_iota(jnp.int32, logits.shape, 1)
    logits = jnp.where(lane < N_EXPERTS, logits, NEG)
    v1 = jnp.max(logits, axis=-1, keepdims=True)
    i1 = jnp.min(jnp.where(logits == v1, lane, LANES), axis=-1, keepdims=True)
    rest = jnp.where(lane == i1, NEG, logits)
    v2 = jnp.max(rest, axis=-1, keepdims=True)
    i2 = jnp.min(jnp.where(rest == v2, lane, LANES), axis=-1, keepdims=True)
    e2 = jnp.exp(v2 - v1)
    w1 = 1.0 / (1.0 + e2)
    w2 = e2 / (1.0 + e2)
    idx_ref[...] = jnp.where(lane == 0, i1, jnp.where(lane == 1, i2, 0))
    wts_ref[...] = jnp.where(lane == 0, w1, jnp.where(lane == 1, w2, 0.0))


def _router(h, g, whi, wlo, br, tm=512):
    t = h.shape[0]
    return pl.pallas_call(
        _router_kernel,
        grid=(t // tm,),
        in_specs=[pl.BlockSpec((tm, D_MODEL), lambda i: (i, 0)),
                  _const_spec((1, D_MODEL)),
                  _const_spec((D_MODEL, LANES)),
                  _const_spec((D_MODEL, LANES)),
                  _const_spec((1, LANES))],
        out_specs=[_slab_spec(tm, lambda i: (i, 0)),
                   pl.BlockSpec((tm, LANES), lambda i: (i, 0)),
                   pl.BlockSpec((tm, LANES), lambda i: (i, 0))],
        out_shape=[jax.ShapeDtypeStruct((t * ROW_SLABS, LANES), F32),
                   jax.ShapeDtypeStruct((t, LANES), jnp.int32),
                   jax.ShapeDtypeStruct((t, LANES), F32)],
        compiler_params=_params("parallel"),
        name="router",
    )(h, g, whi, wlo, br)


def _route_plan(idx, tm):
    t = idx.shape[0]
    n_pairs = 2 * t
    n_rows = n_pairs + N_EXPERTS * tm
    e_flat = jnp.concatenate([idx[:, 0], idx[:, 1]])
    onehot = (e_flat[:, None] == jnp.arange(N_EXPERTS, dtype=jnp.int32)[None, :]).astype(jnp.int32)
    csum = jnp.cumsum(onehot, axis=0)
    rank = jnp.sum(onehot * csum, axis=1) - 1
    counts = csum[-1]
    padded = ((counts + tm - 1) // tm) * tm
    ends = jnp.cumsum(padded)
    starts = ends - padded
    pos = starts[e_flat] + rank
    order = jnp.argsort(e_flat, stable=True).astype(jnp.int32)
    first = jnp.cumsum(counts) - counts
    r = jnp.minimum(jnp.arange(n_rows, dtype=jnp.int32), ends[-1] - 1)
    e_r = jnp.minimum(jnp.searchsorted(ends, r, side="right").astype(jnp.int32), N_EXPERTS - 1)
    local = r - starts[e_r]
    src_pair = order[jnp.clip(first[e_r] + local, 0, n_pairs - 1)]
    src_tok = jnp.where(local < counts[e_r], src_pair % t, 0).astype(jnp.int32)
    tile_expert = e_r[::tm]
    n_valid = (ends[-1] // tm).astype(jnp.int32).reshape(1)
    return pos.astype(jnp.int32), src_tok, tile_expert, n_valid


def _gather_rows(table, idx):
    n = idx.shape[0]
    info = plsc.get_sparse_core_info()
    n_workers = info.num_cores * info.num_subcores
    per_worker = n // n_workers
    assert per_worker * n_workers == n and per_worker % SC_CHUNK == 0, (n, n_workers)
    mesh = plsc.VectorSubcoreMesh(core_axis_name="c", subcore_axis_name="s")

    @functools.partial(
        pl.kernel, mesh=mesh,
        out_type=jax.ShapeDtypeStruct((n,) + table.shape[1:], table.dtype),
        scratch_types=[pltpu.VMEM((SC_CHUNK,), jnp.int32),
                       pltpu.VMEM((SC_CHUNK,) + table.shape[1:], table.dtype),
                       pltpu.SemaphoreType.DMA],
    )
    def gather(table_hbm, idx_hbm, out_hbm, idx_v, rows_v, sem):
        worker = lax.axis_index("s") * info.num_cores + lax.axis_index("c")
        base = worker * per_worker

        @pl.loop(0, per_worker // SC_CHUNK)
        def _(i):
            off = pl.multiple_of(base + i * SC_CHUNK, SC_CHUNK)
            pltpu.sync_copy(idx_hbm.at[pl.ds(off, SC_CHUNK)], idx_v)
            pltpu.async_copy(table_hbm.at[idx_v], rows_v, sem).wait()
            pltpu.sync_copy(rows_v, out_hbm.at[pl.ds(off, SC_CHUNK)])

    return gather(table, idx)


def _expert_ffn_kernel(te_ref, nv_ref, x_ref, w13_ref, w2_ref, y_ref):
    valid = pl.program_id(0) < nv_ref[0]

    @pl.when(valid)
    def _():
        _store_slabs(y_ref, _swiglu(_split_slabs(x_ref).astype(BF), w13_ref.at[0], w2_ref.at[0]))

    @pl.when(jnp.logical_not(valid))
    def _():
        y_ref[...] = jnp.zeros_like(y_ref)


def _expert_ffn(xs, tile_expert, n_valid, w13, w2, tm):
    n_rows = xs.shape[0] // ROW_SLABS
    last = lambda nv: jnp.maximum(nv[0] - 1, 0)
    grid_spec = pltpu.PrefetchScalarGridSpec(
        num_scalar_prefetch=2,
        grid=(n_rows // tm,),
        in_specs=[_slab_spec(tm, lambda i, te, nv: (jnp.minimum(i, last(nv)), 0)),
                  pl.BlockSpec((1, D_MODEL, 2 * D_FF), lambda i, te, nv: (te[i], 0, 0), pipeline_mode=pl.Buffered(1)),
                  pl.BlockSpec((1, D_FF, D_MODEL), lambda i, te, nv: (te[i], 0, 0), pipeline_mode=pl.Buffered(1))],
        out_specs=_slab_spec(tm, lambda i, te, nv: (i, 0)),
    )
    return pl.pallas_call(
        _expert_ffn_kernel,
        grid_spec=grid_spec,
        out_shape=jax.ShapeDtypeStruct((n_rows * ROW_SLABS, LANES), F32),
        compiler_params=_params("arbitrary"),
        name="expert_ffn",
    )(tile_expert, n_valid, xs, w13, w2)


def _combine_kernel(h_ref, y0_ref, y1_ref, wts_ref, gf_ref, out_ref):
    wts = wts_ref[...]
    tot = h_ref[...] + wts[:, 0:1] * _split_slabs(y0_ref) + wts[:, 1:2] * _split_slabs(y1_ref)
    out_ref[...] = _rms(tot, gf_ref[...])


def _combine(h, yg, wts, g_final, tm=512):
    t = h.shape[0]
    nt = t // tm
    return pl.pallas_call(
        _combine_kernel,
        grid=(nt,),
        in_specs=[pl.BlockSpec((tm, D_MODEL), lambda i: (i, 0)),
                  _slab_spec(tm, lambda i: (i, 0)),
                  _slab_spec(tm, lambda i: (nt + i, 0)),
                  pl.BlockSpec((tm, LANES), lambda i: (i, 0)),
                  _const_spec((1, D_MODEL))],
        out_specs=pl.BlockSpec((tm, D_MODEL), lambda i: (i, 0)),
        out_shape=jax.ShapeDtypeStruct((t, D_MODEL), F32),
        compiler_params=_params("parallel"),
        name="combine",
    )(h, yg, yg, wts, g_final)


def _moe(h, g, router_w, router_b, w13, w2, g_final, tm=512):
    wr = jnp.pad(router_w, ((0, 0), (0, LANES - N_EXPERTS)))
    whi = wr.astype(BF)
    wlo = (wr - whi.astype(F32)).astype(BF)
    br = jnp.pad(router_b, (0, LANES - N_EXPERTS)).reshape(1, LANES)
    hn, idx, wts = _router(h, g, whi, wlo, br)
    pos, src_tok, tile_expert, n_valid = _route_plan(idx, tm)
    as_rows = lambda a: a.reshape(-1, ROW_SLABS, LANES)
    as_slabs = lambda a: a.reshape(-1, LANES)
    xs = as_slabs(_gather_rows(as_rows(hn), src_tok))
    ys = _expert_ffn(xs, tile_expert, n_valid, w13, w2, tm)
    yg = as_slabs(_gather_rows(as_rows(ys), pos))
    return _combine(h, yg, wts, g_final)


def _pack_w_in(w_in):
    cuts = np.cumsum((512, 512, 512, 512, 512, 128, 128, 512, 512, 512, 8))[:-1].tolist()
    a_u, a_v, b_x, b_y, c_q, c_k, c_v, d_q, d_k, d_v, d_f = jnp.split(w_in, cuts, axis=-1)
    w = jnp.concatenate([a_u, a_v, b_x, b_y, c_q, d_q, d_k, c_k, c_v], axis=-1).astype(BF)
    wf = jnp.pad(d_f, ((0, 0), (0, LANES - FOX_HEADS))).astype(BF)
    wvt = d_v.T.reshape(FOX_HEADS, HEAD_DIM, D_MODEL)
    wvt = jnp.pad(wvt, ((0, 0), (0, VT_ROWS - HEAD_DIM), (0, 0))).reshape(VT_ALL, D_MODEL).astype(BF)
    return w, wf, wvt


def _vt_ones():
    ones = np.zeros((FOX_HEADS, VT_ROWS, LANES), np.float32)
    ones[:, HEAD_DIM, :] = 1.0
    return jnp.asarray(ones.reshape(VT_ALL, LANES))


def _row(v):
    return v.reshape(1, -1)


def _hybrid_mixer(h, bsz, norm_mix, w_in, sgu_g, sgu_w, sgu_b, conv_w, conv_b, rg_wa, rg_ba, rg_wx, rg_bx,
                  rg_lambda, swa_sinks, fox_bf, w_branch, w_gate, b_gate, w_out):
    w, wf, wvt = _pack_w_in(w_in)
    proj, f, vt = _inproj(h, _row(norm_mix), w, wf, wvt, _vt_ones())
    o_a = _sgu(proj, _row(sgu_g), sgu_w, sgu_b.T)
    wax = jnp.concatenate([rg_wa, rg_wx], axis=-1).astype(BF)
    o_b = _rglru(proj, bsz, conv_w, _row(conv_b), wax, _row(rg_ba), _row(rg_bx), _row(rg_lambda))
    o_c = _swa(proj, bsz, swa_sinks)
    bf_pad = jnp.pad(fox_bf, (0, LANES - FOX_HEADS)).reshape(1, LANES)
    qp, kp = _fox_prep(proj, f, bsz, bf_pad)
    o_d = _fox(qp, kp, vt, bsz)
    return _merge(h, _row(norm_mix), o_a, o_b, o_c, o_d, w_gate.astype(BF), b_gate[:, None, :],
                  w_branch.astype(BF), w_out.astype(BF))


def kernel(x, mem, norm_mix, w_in, sgu_g, sgu_w, sgu_b, conv_w, conv_b, rg_wa, rg_ba, rg_wx, rg_bx, rg_lambda, swa_sinks, fox_bf, w_branch, w_gate, b_gate, w_out, norm_cross, norm_mem, wq_c, wkv_c, wo_c, norm_ffn, dense_w13, dense_w2, router_w, router_b, moe_w13, moe_w2, norm_final):
    bsz, seq, d = x.shape
    m_len = mem.shape[1]
    depth = norm_mix.shape[0]
    assert depth == 2, "the final RMSNorm is fused into the routed layer, which must be the last one"
    h = x.reshape(bsz * seq, d)
    mem2 = mem.reshape(bsz * m_len, d)
    for l in range(depth):
        h = _hybrid_mixer(h, bsz, norm_mix[l], w_in[l], sgu_g[l], sgu_w[l], sgu_b[l], conv_w[l], conv_b[l],
                          rg_wa[l], rg_ba[l], rg_wx[l], rg_bx[l], rg_lambda[l], swa_sinks[l], fox_bf[l],
                          w_branch[l], w_gate[l], b_gate[l], w_out[l])
        kv = _memkv(mem2, _row(norm_mem[l]), wkv_c[l].astype(BF), m_len)
        h = _cross(h, _row(norm_cross[l]), wq_c[l].astype(BF), kv, wo_c[l].astype(BF), bsz, m_len)
        if l % 2 == 0:
            h = _ffn(h, _row(norm_ffn[l]), dense_w13[l // 2].astype(BF), dense_w2[l // 2].astype(BF))
        else:
            h = _moe(h, _row(norm_ffn[l]), router_w[l // 2], router_b[l // 2], moe_w13[l // 2].astype(BF),
                     moe_w2[l // 2].astype(BF), _row(norm_final))
    return h.reshape(bsz, seq, d)
```

```python
import functools
import math

import numpy as np
import jax
import jax.numpy as jnp
from jax import lax
from jax.experimental import pallas as pl
from jax.experimental.pallas import tpu as pltpu
from jax.experimental.pallas import tpu_sc as plsc

F32 = jnp.float32
BF = jnp.bfloat16

D_MODEL = 1024
BRANCH_W = 512
HEAD_DIM = 64
CHUNK = 128
SGU_GROUPS = 4
LRU_HEADS = 4
LRU_C = 8.0
CONV_W = 4
SWA_HEADS = 8
SWA_KV = 2
FOX_HEADS = 8
FOX_PAIRS = FOX_HEADS // 2
X_HEADS = 4
X_HEAD_DIM = 128
D_FF = 2816
N_EXPERTS = 8
EPS = 1e-6
LOG2E = math.log2(math.e)
NEG = -1e30
LANES = 128
ROW_SLABS = D_MODEL // LANES
SC_CHUNK = 64
VMEM_LIMIT = 56 * 1024 * 1024

COL_BX, COL_BY, COL_DQ, COL_DK = range(4)
N_WIDE = 4
N_PROJ = N_WIDE * BRANCH_W + 2 * SWA_KV * HEAD_DIM
VT_ROWS = 80
VT_ALL = FOX_HEADS * VT_ROWS
SWA_VT = SWA_KV * VT_ROWS
FM_ROWS = VT_ALL + BRANCH_W + SWA_VT
FOX_UNROLL = 4
FF_SPLITS = ((0, 1536), (1536, D_FF))


def _rms(x, g):
    return x * lax.rsqrt(jnp.mean(x * x, axis=-1, keepdims=True) + EPS) * g


def _sigmoid(x):
    return 1.0 / (1.0 + jnp.exp(-x))


def _gelu(x):
    return 0.5 * x * (1.0 + jnp.tanh(math.sqrt(2.0 / math.pi) * (x + 0.044715 * (x * x * x))))


def _dot(a, b):
    return jnp.dot(a, b, preferred_element_type=F32)


def _dot_nt(a, b):
    return lax.dot_general(a, b, (((1,), (1,)), ((), ())), preferred_element_type=F32)


def _shift_rows(x, d, fill):
    row = lax.broadcasted_iota(jnp.int32, x.shape, 0)
    return jnp.where(row >= d, pltpu.roll(x, d, 0), fill)


def _params(*sem):
    return pltpu.CompilerParams(dimension_semantics=sem, vmem_limit_bytes=VMEM_LIMIT)


def _const_spec(shape):
    nd = len(shape)
    return pl.BlockSpec(shape, lambda *_: (0,) * nd, pipeline_mode=pl.Buffered(1))


def _inproj_kernel(h_ref, g_ref, w_ref, wa_ref, wf_ref, wt_ref, ones_ref, sg_ref, sw_ref, sbt_ref,
                   proj_ref, f_ref, oa_ref, vt_ref, sqt_ref, svt_ref):
    xn = _rms(h_ref[...], g_ref[...]).astype(BF)
    tm = xn.shape[0]
    n_col = w_ref.shape[1]
    for c in range(0, n_col, BRANCH_W):
        w = min(BRANCH_W, n_col - c)
        proj_ref[:, c:c + w] = _dot(xn, w_ref[:, c:c + w]).astype(BF)
    f_ref[...] = _dot(xn, wf_ref[...])
    ones = jnp.concatenate([ones_ref[...]] * (tm // LANES), axis=1)
    fm = (_dot_nt(wt_ref[...], xn) + ones).astype(BF)
    vt_ref[...] = fm[:VT_ALL]
    sqt_ref[...] = fm[VT_ALL:VT_ALL + BRANCH_W]
    svt_ref[...] = fm[VT_ALL + BRANCH_W:]

    u = _gelu(_dot(xn, wa_ref[:, :BRANCH_W]))
    v = _gelu(_dot(xn, wa_ref[:, BRANCH_W:]))
    vn = _rms(v, sg_ref[...]).astype(BF)
    row = lax.broadcasted_iota(jnp.int32, (CHUNK, CHUNK), 0)
    col = lax.broadcasted_iota(jnp.int32, (CHUNK, CHUNK), 1)
    gw = BRANCH_W // SGU_GROUPS
    for g in range(SGU_GROUPS):
        wg = jnp.where(col <= row, sw_ref[g], 0.0).astype(BF)
        bg = sbt_ref[:, g:g + 1]
        for c in range(tm // CHUNK):
            rs = slice(c * CHUNK, (c + 1) * CHUNK)
            cs = slice(g * gw, (g + 1) * gw)
            mixed = _dot(wg, vn[rs, cs]) + bg
            oa_ref[rs, cs] = (u[rs, cs] * mixed).astype(BF)


def _inproj(h, g, w, wa, wf, wt, ones, sgu_g, sgu_w, sgu_bt, tm=512):
    t = h.shape[0]
    row = lambda i: (i, 0)
    col = lambda i: (0, i)
    return pl.pallas_call(
        _inproj_kernel,
        grid=(t // tm,),
        in_specs=[pl.BlockSpec((tm, D_MODEL), row),
                  _const_spec((1, D_MODEL)),
                  _const_spec((D_MODEL, N_PROJ)),
                  _const_spec((D_MODEL, 2 * BRANCH_W)),
                  _const_spec((D_MODEL, LANES)),
                  _const_spec((FM_ROWS, D_MODEL)),
                  _const_spec((FM_ROWS, LANES)),
                  _const_spec((1, BRANCH_W)),
                  _const_spec((SGU_GROUPS, CHUNK, CHUNK)),
                  _const_spec((CHUNK, SGU_GROUPS))],
        out_specs=[pl.BlockSpec((tm, N_PROJ), row),
                   pl.BlockSpec((tm, LANES), row),
                   pl.BlockSpec((tm, BRANCH_W), row),
                   pl.BlockSpec((VT_ALL, tm), col),
                   pl.BlockSpec((BRANCH_W, tm), col),
                   pl.BlockSpec((SWA_VT, tm), col)],
        out_shape=[jax.ShapeDtypeStruct((t, N_PROJ), BF), jax.ShapeDtypeStruct((t, LANES), F32),
                   jax.ShapeDtypeStruct((t, BRANCH_W), BF), jax.ShapeDtypeStruct((VT_ALL, t), BF),
                   jax.ShapeDtypeStruct((BRANCH_W, t), BF), jax.ShapeDtypeStruct((SWA_VT, t), BF)],
        compiler_params=_params("parallel"),
        name="inproj",
    )(h, g, w, wa, wf, wt, ones, sgu_g, sgu_w, sgu_bt)


def _rglru_kernel(x_ref, y_ref, cw_ref, cb_ref, wax_ref, ba_ref, bx_ref, lam_ref, o_ref, xs_ref, hc_ref):
    ts = x_ref.shape[0]
    hw = BRANCH_W // LRU_HEADS

    @pl.when(pl.program_id(1) == 0)
    def _():
        xs_ref[0:8, :] = jnp.zeros((8, BRANCH_W), F32)
        hc_ref[...] = jnp.zeros_like(hc_ref)

    x = x_ref[...].astype(F32)
    xs_ref[8:8 + ts, :] = x
    cw = cw_ref[...]
    xc = cb_ref[...] + cw[CONV_W - 1:CONV_W] * x
    for k in range(1, CONV_W):
        xc = xc + cw[CONV_W - 1 - k:CONV_W - k] * xs_ref[8 - k:8 - k + ts, :]
    xs_ref[0:8, :] = x[ts - 8:ts, :]

    xcb = xc.astype(BF)
    r_parts, i_parts = [], []
    for hd in range(LRU_HEADS):
        z = _dot(xcb[:, hd * hw:(hd + 1) * hw], wax_ref[hd])
        r_parts.append(z[:, :hw])
        i_parts.append(z[:, hw:])
    r = _sigmoid(jnp.concatenate(r_parts, axis=1) + ba_ref[...])
    gi = _sigmoid(jnp.concatenate(i_parts, axis=1) + bx_ref[...])
    nl = -lam_ref[...]
    softplus = jnp.maximum(nl, 0.0) + jnp.log1p(jnp.exp(-jnp.abs(nl)))
    log_a = (-LRU_C * r) * softplus
    a = jnp.exp(log_a)
    th = jnp.tanh(log_a)
    b = (xc * gi) * jnp.sqrt(-2.0 * th / (1.0 - th))

    d = 1
    while d < ts:
        a_sh = _shift_rows(a, d, 1.0)
        b_sh = _shift_rows(b, d, 0.0)
        b = b + a * b_sh
        a = a * a_sh
        d *= 2
    h = a * hc_ref[...] + b
    hc_ref[...] = h[ts - 1:ts, :]
    o_ref[...] = (h * _gelu(y_ref[...].astype(F32))).astype(BF)


def _rglru(proj, bsz, cw, cb, wax, ba, bx, lam, ts=256):
    t = proj.shape[0]
    ns = t // bsz // ts
    return pl.pallas_call(
        _rglru_kernel,
        grid=(bsz, ns),
        in_specs=[pl.BlockSpec((ts, BRANCH_W), lambda b, j: (b * ns + j, COL_BX)),
                  pl.BlockSpec((ts, BRANCH_W), lambda b, j: (b * ns + j, COL_BY)),
                  _const_spec((CONV_W, BRANCH_W)),
                  _const_spec((1, BRANCH_W)),
                  _const_spec((LRU_HEADS, BRANCH_W // LRU_HEADS, 2 * BRANCH_W // LRU_HEADS)),
                  _const_spec((1, BRANCH_W)),
                  _const_spec((1, BRANCH_W)),
                  _const_spec((1, BRANCH_W))],
        out_specs=pl.BlockSpec((ts, BRANCH_W), lambda b, j: (b * ns + j, 0)),
        out_shape=jax.ShapeDtypeStruct((t, BRANCH_W), BF),
        scratch_shapes=[pltpu.VMEM((ts + 8, BRANCH_W), F32), pltpu.VMEM((1, BRANCH_W), F32)],
        compiler_params=_params("parallel", "arbitrary"),
        name="rglru",
    )(proj, proj, cw, cb, wax, ba, bx, lam)


def _swa_tables(sinks):
    t_idx = np.arange(CHUNK)[None, :] + CHUNK
    dist = (t_idx - np.arange(2 * CHUNK)[:, None]).astype(np.float32)
    in_win = (dist >= 0) & (dist < CHUNK)
    slopes = 2.0 ** (-(8.0 / SWA_HEADS) * np.arange(1, SWA_HEADS + 1, dtype=np.float32))
    bias = np.where(in_win[None], -slopes[:, None, None] * dist[None] * LOG2E, NEG)
    grp = SWA_HEADS // SWA_KV
    bias = bias.reshape(SWA_KV, grp, 2 * CHUNK, CHUNK).transpose(0, 2, 1, 3).reshape(SWA_KV, 2 * CHUNK, grp * CHUNK)
    sink = jnp.repeat(sinks.astype(F32) * LOG2E, CHUNK).reshape(SWA_KV, 1, grp * CHUNK)
    return jnp.asarray(bias, F32), sink


def _swa_kernel(qt_ref, kv_ref, kvp_ref, vt_ref, vtp_ref, bias_ref, sink_ref, o_ref):
    tq = kv_ref.shape[0]
    grp = SWA_HEADS // SWA_KV
    first_key = jnp.where(pl.program_id(1) == 0, CHUNK, 0)
    k_all = jnp.concatenate([kvp_ref[...], kv_ref[...]], axis=0)
    vt_all = jnp.concatenate([vtp_ref[...], vt_ref[...]], axis=1)
    qt = (qt_ref[...].astype(F32) * (HEAD_DIM ** -0.5 * LOG2E)).astype(BF)
    key_row = lax.broadcasted_iota(jnp.int32, (2 * CHUNK, grp * CHUNK), 0)
    kw = 2 * SWA_KV * HEAD_DIM
    for qb in range(tq // CHUNK):
        cols = slice(qb * CHUNK, (qb + 1) * CHUNK)
        kb = k_all[qb * CHUNK:(qb + 2) * CHUNK]
        vtb = vt_all[:, qb * CHUNK:(qb + 2) * CHUNK]
        outs = []
        for kh in range(SWA_KV):
            qg = jnp.concatenate([qt[(kh * grp + g) * HEAD_DIM:(kh * grp + g + 1) * HEAD_DIM, cols]
                                  for g in range(grp)], axis=1)
            parts = []
            if kh > 0:
                parts.append(jnp.zeros((kh * HEAD_DIM, grp * CHUNK), BF))
            parts.append(qg)
            parts.append(jnp.zeros((kw - (kh + 1) * HEAD_DIM, grp * CHUNK), BF))
            st = _dot(kb, jnp.concatenate(parts, axis=0)) + bias_ref[kh]
            if qb == 0:
                st = jnp.where(key_row >= first_key, st, NEG)
            sink = sink_ref[kh]
            m = jnp.maximum(jnp.max(st, axis=0, keepdims=True), sink)
            pt = jnp.exp2(st - m).astype(BF)
            acc = _dot(vtb[kh * VT_ROWS:(kh + 1) * VT_ROWS], pt)
            ot = acc[:HEAD_DIM] / (acc[HEAD_DIM:HEAD_DIM + 1] + jnp.exp2(sink - m))
            outs.extend(ot[:, g * CHUNK:(g + 1) * CHUNK] for g in range(grp))
        o_ref[cols, :] = jnp.concatenate(outs, axis=0).T.astype(BF)


def _swa(proj, sqt, svt, bsz, sinks, tq=512):
    t = proj.shape[0]
    seq = t // bsz
    nq = seq // tq
    per = tq // CHUNK
    kv_blk = N_WIDE * BRANCH_W // (2 * SWA_KV * HEAD_DIM)
    prev = lambda b, i: b * (seq // CHUNK) + jnp.maximum(i * per - 1, 0)
    bias, sink = _swa_tables(sinks)
    grp = SWA_HEADS // SWA_KV
    return pl.pallas_call(
        _swa_kernel,
        grid=(bsz, nq),
        in_specs=[pl.BlockSpec((BRANCH_W, tq), lambda b, i: (0, b * nq + i)),
                  pl.BlockSpec((tq, 2 * SWA_KV * HEAD_DIM), lambda b, i: (b * nq + i, kv_blk)),
                  pl.BlockSpec((CHUNK, 2 * SWA_KV * HEAD_DIM), lambda b, i: (prev(b, i), kv_blk)),
                  pl.BlockSpec((SWA_VT, tq), lambda b, i: (0, b * nq + i)),
                  pl.BlockSpec((SWA_VT, CHUNK), lambda b, i: (0, prev(b, i))),
                  _const_spec((SWA_KV, 2 * CHUNK, grp * CHUNK)),
                  _const_spec((SWA_KV, 1, grp * CHUNK))],
        out_specs=pl.BlockSpec((tq, BRANCH_W), lambda b, i: (b * nq + i, 0)),
        out_shape=jax.ShapeDtypeStruct((t, BRANCH_W), BF),
        compiler_params=_params("parallel", "parallel"),
        name="swa",
    )(sqt, proj, proj, svt, svt, bias, sink)


def _aug_tables():
    eq = np.zeros((3, LANES, FOX_PAIRS * LANES), np.float32)
    ek = np.zeros((3, LANES, FOX_PAIRS * LANES), np.float32)
    oq = np.zeros((1, FOX_PAIRS * LANES), np.float32)
    ok = np.zeros((1, FOX_PAIRS * LANES), np.float32)
    for h in range(FOX_HEADS):
        base = (h // 2) * LANES + 6 * (h % 2)
        for s in range(3):
            eq[s, h, base + s] = 1.0
            ek[s, h, base + 3 + s] = -1.0
            ok[0, base + s] = 1.0
            oq[0, base + 3 + s] = 1.0
    return eq, ek, oq, ok


def _fox_prep_kernel(q_ref, k_ref, f_ref, bf_ref, eq_ref, ek_ref, oq_ref, ok_ref, qp_ref, kp_ref, cum_ref):
    ts = q_ref.shape[0]

    @pl.when(pl.program_id(1) == 0)
    def _():
        cum_ref[...] = jnp.zeros_like(cum_ref)

    z = f_ref[...] + bf_ref[...]
    c = jnp.minimum(z, 0.0) - jnp.log1p(jnp.exp(-jnp.abs(z)))
    d = 1
    while d < ts:
        c = c + _shift_rows(c, d, 0.0)
        d *= 2
    c = c + cum_ref[...]
    cum_ref[...] = c[ts - 1:ts, :]
    c = c * LOG2E
    c1 = c.astype(BF)
    r1 = c - c1.astype(F32)
    c2 = r1.astype(BF)
    c3 = (r1 - c2.astype(F32)).astype(BF)
    augq = _dot(c1, eq_ref[0]) + _dot(c2, eq_ref[1]) + _dot(c3, eq_ref[2]) + oq_ref[...]
    augk = _dot(c1, ek_ref[0]) + _dot(c2, ek_ref[1]) + _dot(c3, ek_ref[2]) + ok_ref[...]
    qt = (q_ref[...].astype(F32) * (HEAD_DIM ** -0.5 * LOG2E)).T.astype(BF)
    augqt = augq.T.astype(BF)
    k = k_ref[...]
    for p in range(FOX_PAIRS):
        qp_ref[2 * p * LANES:(2 * p + 1) * LANES, :] = qt[p * LANES:(p + 1) * LANES]
        qp_ref[(2 * p + 1) * LANES:(2 * p + 2) * LANES, :] = augqt[p * LANES:(p + 1) * LANES]
        kp_ref[:, 2 * p * LANES:(2 * p + 1) * LANES] = k[:, p * LANES:(p + 1) * LANES]
        kp_ref[:, (2 * p + 1) * LANES:(2 * p + 2) * LANES] = augk[:, p * LANES:(p + 1) * LANES].astype(BF)


def _fox_prep(proj, f, bsz, bf_pad, ts=512):
    t = proj.shape[0]
    ns = t // bsz // ts
    eq, ek, oq, ok = _aug_tables()
    wide = 2 * FOX_PAIRS * LANES
    return pl.pallas_call(
        _fox_prep_kernel,
        grid=(bsz, ns),
        in_specs=[pl.BlockSpec((ts, BRANCH_W), lambda b, j: (b * ns + j, COL_DQ)),
                  pl.BlockSpec((ts, BRANCH_W), lambda b, j: (b * ns + j, COL_DK)),
                  pl.BlockSpec((ts, LANES), lambda b, j: (b * ns + j, 0)),
                  _const_spec((1, LANES)),
                  _const_spec(eq.shape), _const_spec(ek.shape), _const_spec(oq.shape), _const_spec(ok.shape)],
        out_specs=[pl.BlockSpec((wide, ts), lambda b, j: (0, b * ns + j)),
                   pl.BlockSpec((ts, wide), lambda b, j: (b * ns + j, 0))],
        out_shape=[jax.ShapeDtypeStruct((wide, t), BF), jax.ShapeDtypeStruct((t, wide), BF)],
        scratch_shapes=[pltpu.VMEM((1, LANES), F32)],
        compiler_params=_params("parallel", "arbitrary"),
        name="fox_prep",
    )(proj, proj, f, bf_pad, jnp.asarray(eq, BF), jnp.asarray(ek, BF), jnp.asarray(oq), jnp.asarray(ok))


def _fox_kernel(q_ref, k_ref, vt_ref, o_ref, st_ref, pt_ref, al_ref, m_ref, acc_ref):
    tq = q_ref.shape[1]
    tk = tq // 2
    qi = pl.program_id(2)
    row = lax.broadcasted_iota(jnp.int32, (2 * LANES, tq), 0)
    q = q_ref[...]
    zero = jnp.zeros_like(q)
    in0 = (row < HEAD_DIM) | ((row >= LANES) & (row < LANES + 6))
    in1 = ((row >= HEAD_DIM) & (row < LANES)) | ((row >= LANES + 6) & (row < LANES + 12))
    qh = (jnp.where(in0, q, zero), jnp.where(in1, q, zero))
    for h in range(2):
        m_ref[h] = jnp.full(m_ref.shape[1:], NEG, F32)
        acc_ref[h] = jnp.zeros(acc_ref.shape[1:], F32)
        pt_ref[1, h] = jnp.zeros(pt_ref.shape[2:], BF)
        al_ref[1, h] = jnp.ones(al_ref.shape[2:], F32)

    def scores(t, slot):
        k = k_ref[pl.ds(pl.multiple_of(t * tk, tk), tk), :]
        for h in range(2):
            st_ref[slot, h] = _dot(k, qh[h])

    def numerators(slot, key_offset):
        for h in range(2):
            st = st_ref[slot, h]
            if key_offset is not None:
                key = lax.broadcasted_iota(jnp.int32, (tk, tq), 0) + key_offset
                qry = lax.broadcasted_iota(jnp.int32, (tk, tq), 1)
                st = jnp.where(key <= qry, st, NEG)
            m_old = m_ref[h]
            m_new = jnp.maximum(m_old, jnp.max(st, axis=0, keepdims=True))
            al_ref[slot, h] = jnp.exp2(m_old - m_new)
            pt_ref[slot, h] = jnp.exp2(st - m_new).astype(BF)
            m_ref[h] = m_new

    def accumulate(t, slot):
        vt = vt_ref[:, pl.ds(pl.multiple_of(t * tk, tk), tk)]
        for h in range(2):
            acc_ref[h] = al_ref[slot, h] * acc_ref[h] + _dot(vt[h * VT_ROWS:(h + 1) * VT_ROWS], pt_ref[slot, h])

    def tile_pair(j, diagonal):
        scores(2 * j + 1, 1)
        accumulate(jnp.maximum(2 * j - 1, 0), 1)
        numerators(0, 0 if diagonal else None)
        if not diagonal:
            scores(2 * j + 2, 0)
        accumulate(2 * j, 0)
        numerators(1, tk if diagonal else None)

    def body(jj, carry):
        for u in range(FOX_UNROLL):
            tile_pair(FOX_UNROLL * jj + u, False)
        return carry

    def remainder(j, carry):
        tile_pair(j, False)
        return carry

    scores(0, 0)
    lax.fori_loop(0, qi // FOX_UNROLL, body, 0)
    lax.fori_loop((qi // FOX_UNROLL) * FOX_UNROLL, qi, remainder, 0)
    tile_pair(qi, True)
    accumulate(2 * qi + 1, 1)
    outs = []
    for h in range(2):
        acc = acc_ref[h]
        outs.append(acc[:HEAD_DIM] / acc[HEAD_DIM:HEAD_DIM + 1])
    o_ref[...] = jnp.concatenate(outs, axis=0).T.astype(BF)


def _fox(qp, kp, vt, bsz, tq=512):
    t = kp.shape[0]
    seq = t // bsz
    nq = seq // tq
    return pl.pallas_call(
        _fox_kernel,
        grid=(bsz, FOX_PAIRS, nq),
        in_specs=[pl.BlockSpec((2 * LANES, tq), lambda b, p, i: (p, b * nq + i)),
                  pl.BlockSpec((seq, 2 * LANES), lambda b, p, i: (b, p)),
                  pl.BlockSpec((2 * VT_ROWS, seq), lambda b, p, i: (p, b))],
        out_specs=pl.BlockSpec((tq, LANES), lambda b, p, i: (b * nq + i, p)),
        out_shape=jax.ShapeDtypeStruct((t, BRANCH_W), BF),
        scratch_shapes=[pltpu.VMEM((2, 2, tq // 2, tq), F32),
                        pltpu.VMEM((2, 2, tq // 2, tq), BF),
                        pltpu.VMEM((2, 2, 1, tq), F32),
                        pltpu.VMEM((2, 1, tq), F32),
                        pltpu.VMEM((2, VT_ROWS, tq), F32)],
        compiler_params=_params("parallel", "parallel", "arbitrary"),
        name="fox",
    )(qp, kp, vt)


def _merge_kernel(h_ref, g_ref, oa_ref, ob_ref, oc_ref, od_ref, wg_ref, bg_ref, wb_ref, wo_ref, out_ref):
    h = h_ref[...]
    xn = _rms(h, g_ref[...]).astype(BF)
    merged = None
    for br, o_ref in enumerate((oa_ref, ob_ref, oc_ref, od_ref)):
        gate = _sigmoid(_dot(xn, wg_ref[br]) + bg_ref[br])
        term = gate * _dot(o_ref[...], wb_ref[br])
        merged = term if merged is None else merged + term
    out_ref[...] = h + _dot(merged.astype(BF), wo_ref[...])


def _merge(h, g, oa, ob, oc, od, wg, bg, wb, wo, tm=512):
    t = h.shape[0]
    row = lambda i: (i, 0)
    return pl.pallas_call(
        _merge_kernel,
        grid=(t // tm,),
        in_specs=[pl.BlockSpec((tm, D_MODEL), row),
                  _const_spec((1, D_MODEL)),
                  pl.BlockSpec((tm, BRANCH_W), row), pl.BlockSpec((tm, BRANCH_W), row),
                  pl.BlockSpec((tm, BRANCH_W), row), pl.BlockSpec((tm, BRANCH_W), row),
                  _const_spec((4, D_MODEL, D_MODEL)),
                  _const_spec((4, 1, D_MODEL)),
                  _const_spec((4, BRANCH_W, D_MODEL)),
                  _const_spec((D_MODEL, D_MODEL))],
        out_specs=pl.BlockSpec((tm, D_MODEL), row),
        out_shape=jax.ShapeDtypeStruct((t, D_MODEL), F32),
        compiler_params=_params("parallel"),
        name="merge",
    )(h, g, oa, ob, oc, od, wg, bg, wb, wo)


def _memkv_kernel(mem_ref, g_ref, w_ref, kv_ref):
    mn = _rms(mem_ref[...], g_ref[...]).astype(BF)
    kv_ref[...] = _dot(mn, w_ref[...]).astype(BF)


def _memkv(mem2, g, w, m_len):
    n = mem2.shape[0]
    width = 2 * X_HEADS * X_HEAD_DIM
    return pl.pallas_call(
        _memkv_kernel,
        grid=(n // m_len,),
        in_specs=[pl.BlockSpec((m_len, D_MODEL), lambda b: (b, 0)),
                  _const_spec((1, D_MODEL)),
                  _const_spec((D_MODEL, width))],
        out_specs=pl.BlockSpec((m_len, width), lambda b: (b, 0)),
        out_shape=jax.ShapeDtypeStruct((n, width), BF),
        compiler_params=_params("parallel"),
        name="memkv",
    )(mem2, g, w)


def _cross_kernel(h_ref, g_ref, wq_ref, kv_ref, wo_ref, out_ref):
    h = h_ref[...]
    hn = _rms(h, g_ref[...]).astype(BF)
    q = _dot(hn, wq_ref[...]).astype(BF)
    kv = kv_ref[...]
    width = X_HEADS * X_HEAD_DIM
    outs = []
    for hd in range(X_HEADS):
        cs = slice(hd * X_HEAD_DIM, (hd + 1) * X_HEAD_DIM)
        s = _dot_nt(q[:, cs], kv[:, cs]) * (X_HEAD_DIM ** -0.5)
        m = jnp.max(s, axis=-1, keepdims=True)
        p = jnp.exp(s - m)
        denom = jnp.sum(p, axis=-1, keepdims=True)
        v = kv[:, width + hd * X_HEAD_DIM:width + (hd + 1) * X_HEAD_DIM]
        outs.append((_dot(p.astype(BF), v) / denom).astype(BF))
    o = jnp.concatenate(outs, axis=1)
    out_ref[...] = h + _dot(o, wo_ref[...])


def _cross(h, g, wq, kv, wo, bsz, m_len, tm=512):
    t = h.shape[0]
    per = t // bsz // tm
    width = X_HEADS * X_HEAD_DIM
    return pl.pallas_call(
        _cross_kernel,
        grid=(bsz, per),
        in_specs=[pl.BlockSpec((tm, D_MODEL), lambda b, i: (b * per + i, 0)),
                  _const_spec((1, D_MODEL)),
                  _const_spec((D_MODEL, width)),
                  pl.BlockSpec((m_len, 2 * width), lambda b, i: (b, 0)),
                  _const_spec((width, D_MODEL))],
        out_specs=pl.BlockSpec((tm, D_MODEL), lambda b, i: (b * per + i, 0)),
        out_shape=jax.ShapeDtypeStruct((t, D_MODEL), F32),
        compiler_params=_params("parallel", "parallel"),
        name="cross",
    )(h, g, wq, kv, wo)


def _swiglu(xb, w13_ref, w2_ref):
    out = None
    for lo, hi in FF_SPLITS:
        gate = _dot(xb, w13_ref[:, lo:hi])
        up = _dot(xb, w13_ref[:, D_FF + lo:D_FF + hi])
        act = (gate * _sigmoid(gate) * up).astype(BF)
        part = _dot(act, w2_ref[lo:hi, :])
        out = part if out is None else out + part
    return out


def _ffn_kernel(h_ref, g_ref, w13_ref, w2_ref, out_ref):
    h = h_ref[...]
    out_ref[...] = h + _swiglu(_rms(h, g_ref[...]).astype(BF), w13_ref, w2_ref)


def _ffn(h, g, w13, w2, tm=512):
    t = h.shape[0]
    return pl.pallas_call(
        _ffn_kernel,
        grid=(t // tm,),
        in_specs=[pl.BlockSpec((tm, D_MODEL), lambda i: (i, 0)),
                  _const_spec((1, D_MODEL)),
                  _const_spec((D_MODEL, 2 * D_FF)),
                  _const_spec((D_FF, D_MODEL))],
        out_specs=pl.BlockSpec((tm, D_MODEL), lambda i: (i, 0)),
        out_shape=jax.ShapeDtypeStruct((t, D_MODEL), F32),
        compiler_params=_params("parallel"),
        name="ffn",
    )(h, g, w13, w2)


def _split_slabs(ref):
    rows = ref.shape[0] // ROW_SLABS
    return jnp.concatenate([ref[pl.ds(c, rows, stride=ROW_SLABS), :] for c in range(ROW_SLABS)], axis=1)


def _store_slabs(ref, x):
    rows = ref.shape[0] // ROW_SLABS
    for c in range(ROW_SLABS):
        ref[pl.ds(c, rows, stride=ROW_SLABS), :] = x[:, c * LANES:(c + 1) * LANES]


def _slab_spec(tm, index_map):
    return pl.BlockSpec((tm * ROW_SLABS, LANES), index_map)


def _router_kernel(h_ref, g_ref, whi_ref, wlo_ref, br_ref, hn_ref, idx_ref, wts_ref):
    hn = _rms(h_ref[...], g_ref[...])
    _store_slabs(hn_ref, hn)
    hi = hn.astype(BF)
    lo = (hn - hi.astype(F32)).astype(BF)
    logits = _dot(hi, whi_ref[...]) + (_dot(lo, whi_ref[...]) + _dot(hi, wlo_ref[...])) + br_ref[...]
    lane = lax.broadcasted_iota(jnp.int32, logits.shape, 1)
    logits = jnp.where(lane < N_EXPERTS, logits, NEG)
    v1 = jnp.max(logits, axis=-1, keepdims=True)
    i1 = jnp.min(jnp.where(logits == v1, lane, LANES), axis=-1, keepdims=True)
    rest = jnp.where(lane == i1, NEG, logits)
    v2 = jnp.max(rest, axis=-1, keepdims=True)
    i2 = jnp.min(jnp.where(rest == v2, lane, LANES), axis=-1, keepdims=True)
    e2 = jnp.exp(v2 - v1)
    w1 = 1.0 / (1.0 + e2)
    w2 = e2 / (1.0 + e2)
    idx_ref[...] = jnp.where(lane == 0, i1, jnp.where(lane == 1, i2, 0))
    wts_ref[...] = jnp.where(lane == 0, w1, jnp.where(lane == 1, w2, 0.0))


def _router(h, g, whi, wlo, br, tm=512):
    t = h.shape[0]
    return pl.pallas_call(
        _router_kernel,
        grid=(t // tm,),
        in_specs=[pl.BlockSpec((tm, D_MODEL), lambda i: (i, 0)),
                  _const_spec((1, D_MODEL)),
                  _const_spec((D_MODEL, LANES)),
                  _const_spec((D_MODEL, LANES)),
                  _const_spec((1, LANES))],
        out_specs=[_slab_spec(tm, lambda i: (i, 0)),
                   pl.BlockSpec((tm, LANES), lambda i: (i, 0)),
                   pl.BlockSpec((tm, LANES), lambda i: (i, 0))],
        out_shape=[jax.ShapeDtypeStruct((t * ROW_SLABS, LANES), F32),
                   jax.ShapeDtypeStruct((t, LANES), jnp.int32),
                   jax.ShapeDtypeStruct((t, LANES), F32)],
        compiler_params=_params("parallel"),
        name="router",
    )(h, g, whi, wlo, br)


def _route_plan(idx, tm):
    t = idx.shape[0]
    n_pairs = 2 * t
    n_rows = n_pairs + N_EXPERTS * tm
    e_flat = jnp.concatenate([idx[:, 0], idx[:, 1]])
    onehot = (e_flat[:, None] == jnp.arange(N_EXPERTS, dtype=jnp.int32)[None, :]).astype(jnp.int32)
    csum = jnp.cumsum(onehot, axis=0)
    rank = jnp.sum(onehot * csum, axis=1) - 1
    counts = csum[-1]
    padded = ((counts + tm - 1) // tm) * tm
    ends = jnp.cumsum(padded)
    starts = ends - padded
    pos = starts[e_flat] + rank
    order = jnp.argsort(e_flat, stable=True).astype(jnp.int32)
    first = jnp.cumsum(counts) - counts
    r = jnp.minimum(jnp.arange(n_rows, dtype=jnp.int32), ends[-1] - 1)
    e_r = jnp.minimum(jnp.searchsorted(ends, r, side="right").astype(jnp.int32), N_EXPERTS - 1)
    local = r - starts[e_r]
    src_pair = order[jnp.clip(first[e_r] + local, 0, n_pairs - 1)]
    src_tok = jnp.where(local < counts[e_r], src_pair % t, 0).astype(jnp.int32)
    tile_expert = e_r[::tm]
    n_valid = (ends[-1] // tm).astype(jnp.int32).reshape(1)
    return pos.astype(jnp.int32), src_tok, tile_expert, n_valid


def _gather_rows(table, idx):
    n = idx.shape[0]
    info = plsc.get_sparse_core_info()
    n_workers = info.num_cores * info.num_subcores
    per_worker = n // n_workers
    assert per_worker * n_workers == n and per_worker % SC_CHUNK == 0, (n, n_workers)
    mesh = plsc.VectorSubcoreMesh(core_axis_name="c", subcore_axis_name="s")

    @functools.partial(
        pl.kernel, mesh=mesh,
        out_type=jax.ShapeDtypeStruct((n,) + table.shape[1:], table.dtype),
        scratch_types=[pltpu.VMEM((SC_CHUNK,), jnp.int32),
                       pltpu.VMEM((SC_CHUNK,) + table.shape[1:], table.dtype),
                       pltpu.SemaphoreType.DMA],
    )
    def gather(table_hbm, idx_hbm, out_hbm, idx_v, rows_v, sem):
        worker = lax.axis_index("s") * info.num_cores + lax.axis_index("c")
        base = worker * per_worker

        @pl.loop(0, per_worker // SC_CHUNK)
        def _(i):
            off = pl.multiple_of(base + i * SC_CHUNK, SC_CHUNK)
            pltpu.sync_copy(idx_hbm.at[pl.ds(off, SC_CHUNK)], idx_v)
            pltpu.async_copy(table_hbm.at[idx_v], rows_v, sem).wait()
            pltpu.sync_copy(rows_v, out_hbm.at[pl.ds(off, SC_CHUNK)])

    return gather(table, idx)


def _expert_ffn_kernel(te_ref, nv_ref, x_ref, w13_ref, w2_ref, y_ref):
    valid = pl.program_id(0) < nv_ref[0]

    @pl.when(valid)
    def _():
        _store_slabs(y_ref, _swiglu(_split_slabs(x_ref).astype(BF), w13_ref.at[0], w2_ref.at[0]))

    @pl.when(jnp.logical_not(valid))
    def _():
        y_ref[...] = jnp.zeros_like(y_ref)


def _expert_ffn(xs, tile_expert, n_valid, w13, w2, tm):
    n_rows = xs.shape[0] // ROW_SLABS
    last = lambda nv: jnp.maximum(nv[0] - 1, 0)
    grid_spec = pltpu.PrefetchScalarGridSpec(
        num_scalar_prefetch=2,
        grid=(n_rows // tm,),
        in_specs=[_slab_spec(tm, lambda i, te, nv: (jnp.minimum(i, last(nv)), 0)),
                  pl.BlockSpec((1, D_MODEL, 2 * D_FF), lambda i, te, nv: (te[i], 0, 0), pipeline_mode=pl.Buffered(1)),
                  pl.BlockSpec((1, D_FF, D_MODEL), lambda i, te, nv: (te[i], 0, 0), pipeline_mode=pl.Buffered(1))],
        out_specs=_slab_spec(tm, lambda i, te, nv: (i, 0)),
    )
    return pl.pallas_call(
        _expert_ffn_kernel,
        grid_spec=grid_spec,
        out_shape=jax.ShapeDtypeStruct((n_rows * ROW_SLABS, LANES), F32),
        compiler_params=_params("arbitrary"),
        name="expert_ffn",
    )(tile_expert, n_valid, xs, w13, w2)


def _combine_kernel(h_ref, y0_ref, y1_ref, wts_ref, gf_ref, out_ref):
    wts = wts_ref[...]
    tot = h_ref[...] + wts[:, 0:1] * _split_slabs(y0_ref) + wts[:, 1:2] * _split_slabs(y1_ref)
    out_ref[...] = _rms(tot, gf_ref[...])


def _combine(h, yg, wts, g_final, tm=512):
    t = h.shape[0]
    nt = t // tm
    return pl.pallas_call(
        _combine_kernel,
        grid=(nt,),
        in_specs=[pl.BlockSpec((tm, D_MODEL), lambda i: (i, 0)),
                  _slab_spec(tm, lambda i: (i, 0)),
                  _slab_spec(tm, lambda i: (nt + i, 0)),
                  pl.BlockSpec((tm, LANES), lambda i: (i, 0)),
                  _const_spec((1, D_MODEL))],
        out_specs=pl.BlockSpec((tm, D_MODEL), lambda i: (i, 0)),
        out_shape=jax.ShapeDtypeStruct((t, D_MODEL), F32),
        compiler_params=_params("parallel"),
        name="combine",
    )(h, yg, yg, wts, g_final)


def _moe(h, g, router_w, router_b, w13, w2, g_final, tm=512):
    wr = jnp.pad(router_w, ((0, 0), (0, LANES - N_EXPERTS)))
    whi = wr.astype(BF)
    wlo = (wr - whi.astype(F32)).astype(BF)
    br = jnp.pad(router_b, (0, LANES - N_EXPERTS)).reshape(1, LANES)
    hn, idx, wts = _router(h, g, whi, wlo, br)
    pos, src_tok, tile_expert, n_valid = _route_plan(idx, tm)
    as_rows = lambda a: a.reshape(-1, ROW_SLABS, LANES)
    as_slabs = lambda a: a.reshape(-1, LANES)
    xs = as_slabs(_gather_rows(as_rows(hn), src_tok))
    ys = _expert_ffn(xs, tile_expert, n_valid, w13, w2, tm)
    yg = as_slabs(_gather_rows(as_rows(ys), pos))
    return _combine(h, yg, wts, g_final)


def _feature_major(w, heads):
    wt = w.T.reshape(heads, HEAD_DIM, D_MODEL)
    return jnp.pad(wt, ((0, 0), (0, VT_ROWS - HEAD_DIM), (0, 0))).reshape(heads * VT_ROWS, D_MODEL)


def _pack_w_in(w_in):
    cuts = np.cumsum((512, 512, 512, 512, 512, 128, 128, 512, 512, 512, 8))[:-1].tolist()
    a_u, a_v, b_x, b_y, c_q, c_k, c_v, d_q, d_k, d_v, d_f = jnp.split(w_in, cuts, axis=-1)
    w = jnp.concatenate([b_x, b_y, d_q, d_k, c_k, c_v], axis=-1).astype(BF)
    wa = jnp.concatenate([a_u, a_v], axis=-1).astype(BF)
    wf = jnp.pad(d_f, ((0, 0), (0, LANES - FOX_HEADS))).astype(BF)
    wt = jnp.concatenate([_feature_major(d_v, FOX_HEADS), c_q.T, _feature_major(c_v, SWA_KV)], axis=0).astype(BF)
    return w, wa, wf, wt


def _fm_ones():
    ones = np.zeros((FM_ROWS, LANES), np.float32)
    for base, heads in ((0, FOX_HEADS), (VT_ALL + BRANCH_W, SWA_KV)):
        for h in range(heads):
            ones[base + h * VT_ROWS + HEAD_DIM, :] = 1.0
    return jnp.asarray(ones)


def _row(v):
    return v.reshape(1, -1)


def _hybrid_mixer(h, bsz, norm_mix, w_in, sgu_g, sgu_w, sgu_b, conv_w, conv_b, rg_wa, rg_ba, rg_wx, rg_bx,
                  rg_lambda, swa_sinks, fox_bf, w_branch, w_gate, b_gate, w_out):
    w, wa, wf, wt = _pack_w_in(w_in)
    proj, f, o_a, vt, sqt, svt = _inproj(h, _row(norm_mix), w, wa, wf, wt, _fm_ones(), _row(sgu_g), sgu_w, sgu_b.T)
    wax = jnp.concatenate([rg_wa, rg_wx], axis=-1).astype(BF)
    o_b = _rglru(proj, bsz, conv_w, _row(conv_b), wax, _row(rg_ba), _row(rg_bx), _row(rg_lambda))
    o_c = _swa(proj, sqt, svt, bsz, swa_sinks)
    bf_pad = jnp.pad(fox_bf, (0, LANES - FOX_HEADS)).reshape(1, LANES)
    qp, kp = _fox_prep(proj, f, bsz, bf_pad)
    o_d = _fox(qp, kp, vt, bsz)
    return _merge(h, _row(norm_mix), o_a, o_b, o_c, o_d, w_gate.astype(BF), b_gate[:, None, :],
                  w_branch.astype(BF), w_out.astype(BF))


def kernel(x, mem, norm_mix, w_in, sgu_g, sgu_w, sgu_b, conv_w, conv_b, rg_wa, rg_ba, rg_wx, rg_bx, rg_lambda, swa_sinks, fox_bf, w_branch, w_gate, b_gate, w_out, norm_cross, norm_mem, wq_c, wkv_c, wo_c, norm_ffn, dense_w13, dense_w2, router_w, router_b, moe_w13, moe_w2, norm_final):
    bsz, seq, d = x.shape
    m_len = mem.shape[1]
    depth = norm_mix.shape[0]
    assert depth == 2, "the final RMSNorm is fused into the routed layer, which must be the last one"
    h = x.reshape(bsz * seq, d)
    mem2 = mem.reshape(bsz * m_len, d)
    for l in range(depth):
        h = _hybrid_mixer(h, bsz, norm_mix[l], w_in[l], sgu_g[l], sgu_w[l], sgu_b[l], conv_w[l], conv_b[l],
                          rg_wa[l], rg_ba[l], rg_wx[l], rg_bx[l], rg_lambda[l], swa_sinks[l], fox_bf[l],
                          w_branch[l], w_gate[l], b_gate[l], w_out[l])
        kv = _memkv(mem2, _row(norm_mem[l]), wkv_c[l].astype(BF), m_len)
        h = _cross(h, _row(norm_cross[l]), wq_c[l].astype(BF), kv, wo_c[l].astype(BF), bsz, m_len)
        if l % 2 == 0:
            h = _ffn(h, _row(norm_ffn[l]), dense_w13[l // 2].astype(BF), dense_w2[l // 2].astype(BF))
        else:
            h = _moe(h, _row(norm_ffn[l]), router_w[l // 2], router_b[l // 2], moe_w13[l // 2].astype(BF),
                     moe_w2[l // 2].astype(BF), _row(norm_final))
    return h.reshape(bsz, seq, d)
```

```python
import functools
import math

import numpy as np
import jax
import jax.numpy as jnp
from jax import lax
from jax.experimental import pallas as pl
from jax.experimental.pallas import tpu as pltpu
from jax.experimental.pallas import tpu_sc as plsc

F32 = jnp.float32
BF = jnp.bfloat16

D_MODEL = 1024
BRANCH_W = 512
HEAD_DIM = 64
CHUNK = 128
SGU_GROUPS = 4
LRU_HEADS = 4
LRU_C = 8.0
CONV_W = 4
SWA_HEADS = 8
SWA_KV = 2
FOX_HEADS = 8
FOX_PAIRS = FOX_HEADS // 2
X_HEADS = 4
X_HEAD_DIM = 128
D_FF = 2816
N_EXPERTS = 8
EPS = 1e-6
LOG2E = math.log2(math.e)
NEG = -1e30
LANES = 128
ROW_SLABS = D_MODEL // LANES
SC_CHUNK = 64
MOE_CHUNKS = 2
VMEM_LIMIT = 56 * 1024 * 1024

COL_BX, COL_BY, COL_DQ, COL_DK = range(4)
N_WIDE = 4
N_PROJ = N_WIDE * BRANCH_W + 2 * SWA_KV * HEAD_DIM
VT_ROWS = 80
VT_ALL = FOX_HEADS * VT_ROWS
SWA_VT = SWA_KV * VT_ROWS
FM_ROWS = VT_ALL + BRANCH_W + SWA_VT
FOX_UNROLL = 4
FF_SPLITS = ((0, 1536), (1536, D_FF))


def _rms(x, g):
    return x * lax.rsqrt(jnp.mean(x * x, axis=-1, keepdims=True) + EPS) * g


def _sigmoid(x):
    return 1.0 / (1.0 + jnp.exp(-x))


def _gelu(x):
    return 0.5 * x * (1.0 + jnp.tanh(math.sqrt(2.0 / math.pi) * (x + 0.044715 * (x * x * x))))


def _dot(a, b):
    return jnp.dot(a, b, preferred_element_type=F32)


def _dot_nt(a, b):
    return lax.dot_general(a, b, (((1,), (1,)), ((), ())), preferred_element_type=F32)


def _shift_rows(x, d, fill):
    row = lax.broadcasted_iota(jnp.int32, x.shape, 0)
    return jnp.where(row >= d, pltpu.roll(x, d, 0), fill)


def _params(*sem):
    return pltpu.CompilerParams(dimension_semantics=sem, vmem_limit_bytes=VMEM_LIMIT)


def _const_spec(shape):
    nd = len(shape)
    return pl.BlockSpec(shape, lambda *_: (0,) * nd, pipeline_mode=pl.Buffered(1))


def _inproj_kernel(h_ref, g_ref, w_ref, wa_ref, wf_ref, wt_ref, ones_ref, sg_ref, sw_ref, sbt_ref,
                   proj_ref, f_ref, oa_ref, vt_ref, sqt_ref, svt_ref):
    xn = _rms(h_ref[...], g_ref[...]).astype(BF)
    tm = xn.shape[0]
    n_col = w_ref.shape[1]
    for c in range(0, n_col, BRANCH_W):
        w = min(BRANCH_W, n_col - c)
        proj_ref[:, c:c + w] = _dot(xn, w_ref[:, c:c + w]).astype(BF)
    f_ref[...] = _dot(xn, wf_ref[...])
    ones = jnp.concatenate([ones_ref[...]] * (tm // LANES), axis=1)
    fm = (_dot_nt(wt_ref[...], xn) + ones).astype(BF)
    vt_ref[...] = fm[:VT_ALL]
    sqt_ref[...] = fm[VT_ALL:VT_ALL + BRANCH_W]
    svt_ref[...] = fm[VT_ALL + BRANCH_W:]

    u = _gelu(_dot(xn, wa_ref[:, :BRANCH_W]))
    v = _gelu(_dot(xn, wa_ref[:, BRANCH_W:]))
    vn = _rms(v, sg_ref[...]).astype(BF)
    row = lax.broadcasted_iota(jnp.int32, (CHUNK, CHUNK), 0)
    col = lax.broadcasted_iota(jnp.int32, (CHUNK, CHUNK), 1)
    gw = BRANCH_W // SGU_GROUPS
    for g in range(SGU_GROUPS):
        wg = jnp.where(col <= row, sw_ref[g], 0.0).astype(BF)
        bg = sbt_ref[:, g:g + 1]
        for c in range(tm // CHUNK):
            rs = slice(c * CHUNK, (c + 1) * CHUNK)
            cs = slice(g * gw, (g + 1) * gw)
            mixed = _dot(wg, vn[rs, cs]) + bg
            oa_ref[rs, cs] = (u[rs, cs] * mixed).astype(BF)


def _inproj(h, g, w, wa, wf, wt, ones, sgu_g, sgu_w, sgu_bt, tm=512):
    t = h.shape[0]
    row = lambda i: (i, 0)
    col = lambda i: (0, i)
    return pl.pallas_call(
        _inproj_kernel,
        grid=(t // tm,),
        in_specs=[pl.BlockSpec((tm, D_MODEL), row),
                  _const_spec((1, D_MODEL)),
                  _const_spec((D_MODEL, N_PROJ)),
                  _const_spec((D_MODEL, 2 * BRANCH_W)),
                  _const_spec((D_MODEL, LANES)),
                  _const_spec((FM_ROWS, D_MODEL)),
                  _const_spec((FM_ROWS, LANES)),
                  _const_spec((1, BRANCH_W)),
                  _const_spec((SGU_GROUPS, CHUNK, CHUNK)),
                  _const_spec((CHUNK, SGU_GROUPS))],
        out_specs=[pl.BlockSpec((tm, N_PROJ), row),
                   pl.BlockSpec((tm, LANES), row),
                   pl.BlockSpec((tm, BRANCH_W), row),
                   pl.BlockSpec((VT_ALL, tm), col),
                   pl.BlockSpec((BRANCH_W, tm), col),
                   pl.BlockSpec((SWA_VT, tm), col)],
        out_shape=[jax.ShapeDtypeStruct((t, N_PROJ), BF), jax.ShapeDtypeStruct((t, LANES), F32),
                   jax.ShapeDtypeStruct((t, BRANCH_W), BF), jax.ShapeDtypeStruct((VT_ALL, t), BF),
                   jax.ShapeDtypeStruct((BRANCH_W, t), BF), jax.ShapeDtypeStruct((SWA_VT, t), BF)],
        compiler_params=_params("parallel"),
        name="inproj",
    )(h, g, w, wa, wf, wt, ones, sgu_g, sgu_w, sgu_bt)


def _rglru_kernel(x_ref, y_ref, cw_ref, cb_ref, wax_ref, ba_ref, bx_ref, lam_ref, o_ref, xs_ref, hc_ref):
    ts = x_ref.shape[0]
    hw = BRANCH_W // LRU_HEADS

    @pl.when(pl.program_id(1) == 0)
    def _():
        xs_ref[0:8, :] = jnp.zeros((8, BRANCH_W), F32)
        hc_ref[...] = jnp.zeros_like(hc_ref)

    x = x_ref[...].astype(F32)
    xs_ref[8:8 + ts, :] = x
    cw = cw_ref[...]
    xc = cb_ref[...] + cw[CONV_W - 1:CONV_W] * x
    for k in range(1, CONV_W):
        xc = xc + cw[CONV_W - 1 - k:CONV_W - k] * xs_ref[8 - k:8 - k + ts, :]
    xs_ref[0:8, :] = x[ts - 8:ts, :]

    xcb = xc.astype(BF)
    r_parts, i_parts = [], []
    for hd in range(LRU_HEADS):
        z = _dot(xcb[:, hd * hw:(hd + 1) * hw], wax_ref[hd])
        r_parts.append(z[:, :hw])
        i_parts.append(z[:, hw:])
    r = _sigmoid(jnp.concatenate(r_parts, axis=1) + ba_ref[...])
    gi = _sigmoid(jnp.concatenate(i_parts, axis=1) + bx_ref[...])
    nl = -lam_ref[...]
    softplus = jnp.maximum(nl, 0.0) + jnp.log1p(jnp.exp(-jnp.abs(nl)))
    log_a = (-LRU_C * r) * softplus
    a = jnp.exp(log_a)
    th = jnp.tanh(log_a)
    e2 = -2.0 * th / (1.0 - th)
    b = (xc * gi) * (e2 * lax.rsqrt(jnp.maximum(e2, 1e-30)))

    sub = lax.broadcasted_iota(jnp.int32, a.shape, 0) & 7
    for d in (1, 2, 4):
        keep = sub >= d
        a_sh = jnp.where(keep, pltpu.roll(a, d, 0), 1.0)
        b_sh = jnp.where(keep, pltpu.roll(b, d, 0), 0.0)
        b = b + a * b_sh
        a = a * a_sh
    carry = hc_ref[...]
    groups = []
    for g in range(ts // 8):
        hg = a[8 * g:8 * g + 8] * carry + b[8 * g:8 * g + 8]
        groups.append(hg)
        carry = hg[7:8]
    h = jnp.concatenate(groups, axis=0)
    hc_ref[...] = carry
    o_ref[...] = (h * _gelu(y_ref[...].astype(F32))).astype(BF)


def _rglru(proj, bsz, cw, cb, wax, ba, bx, lam, ts=256):
    t = proj.shape[0]
    ns = t // bsz // ts
    return pl.pallas_call(
        _rglru_kernel,
        grid=(bsz, ns),
        in_specs=[pl.BlockSpec((ts, BRANCH_W), lambda b, j: (b * ns + j, COL_BX)),
                  pl.BlockSpec((ts, BRANCH_W), lambda b, j: (b * ns + j, COL_BY)),
                  _const_spec((CONV_W, BRANCH_W)),
                  _const_spec((1, BRANCH_W)),
                  _const_spec((LRU_HEADS, BRANCH_W // LRU_HEADS, 2 * BRANCH_W // LRU_HEADS)),
                  _const_spec((1, BRANCH_W)),
                  _const_spec((1, BRANCH_W)),
                  _const_spec((1, BRANCH_W))],
        out_specs=pl.BlockSpec((ts, BRANCH_W), lambda b, j: (b * ns + j, 0)),
        out_shape=jax.ShapeDtypeStruct((t, BRANCH_W), BF),
        scratch_shapes=[pltpu.VMEM((ts + 8, BRANCH_W), F32), pltpu.VMEM((1, BRANCH_W), F32)],
        compiler_params=_params("parallel", "arbitrary"),
        name="rglru",
    )(proj, proj, cw, cb, wax, ba, bx, lam)


def _swa_tables(sinks):
    t_idx = np.arange(CHUNK)[None, :] + CHUNK
    dist = (t_idx - np.arange(2 * CHUNK)[:, None]).astype(np.float32)
    in_win = (dist >= 0) & (dist < CHUNK)
    slopes = 2.0 ** (-(8.0 / SWA_HEADS) * np.arange(1, SWA_HEADS + 1, dtype=np.float32))
    bias = np.where(in_win[None], -slopes[:, None, None] * dist[None] * LOG2E, NEG)
    grp = SWA_HEADS // SWA_KV
    bias = bias.reshape(SWA_KV, grp, 2 * CHUNK, CHUNK).transpose(0, 2, 1, 3).reshape(SWA_KV, 2 * CHUNK, grp * CHUNK)
    sink = jnp.repeat(sinks.astype(F32) * LOG2E, CHUNK).reshape(SWA_KV, 1, grp * CHUNK)
    return jnp.asarray(bias, F32), sink


def _swa_kernel(qt_ref, kv_ref, kvp_ref, vt_ref, vtp_ref, bias_ref, sink_ref, o_ref):
    tq = kv_ref.shape[0]
    grp = SWA_HEADS // SWA_KV
    first_key = jnp.where(pl.program_id(1) == 0, CHUNK, 0)
    k_all = jnp.concatenate([kvp_ref[...], kv_ref[...]], axis=0)
    vt_all = jnp.concatenate([vtp_ref[...], vt_ref[...]], axis=1)
    qt = (qt_ref[...].astype(F32) * (HEAD_DIM ** -0.5 * LOG2E)).astype(BF)
    key_row = lax.broadcasted_iota(jnp.int32, (2 * CHUNK, grp * CHUNK), 0)
    kw = 2 * SWA_KV * HEAD_DIM
    for qb in range(tq // CHUNK):
        cols = slice(qb * CHUNK, (qb + 1) * CHUNK)
        kb = k_all[qb * CHUNK:(qb + 2) * CHUNK]
        vtb = vt_all[:, qb * CHUNK:(qb + 2) * CHUNK]
        outs = []
        for kh in range(SWA_KV):
            qg = jnp.concatenate([qt[(kh * grp + g) * HEAD_DIM:(kh * grp + g + 1) * HEAD_DIM, cols]
                                  for g in range(grp)], axis=1)
            parts = []
            if kh > 0:
                parts.append(jnp.zeros((kh * HEAD_DIM, grp * CHUNK), BF))
            parts.append(qg)
            parts.append(jnp.zeros((kw - (kh + 1) * HEAD_DIM, grp * CHUNK), BF))
            st = _dot(kb, jnp.concatenate(parts, axis=0)) + bias_ref[kh]
            if qb == 0:
                st = jnp.where(key_row >= first_key, st, NEG)
            sink = sink_ref[kh]
            m = jnp.maximum(jnp.max(st, axis=0, keepdims=True), sink)
            pt = jnp.exp2(st - m).astype(BF)
            acc = _dot(vtb[kh * VT_ROWS:(kh + 1) * VT_ROWS], pt)
            ot = acc[:HEAD_DIM] / (acc[HEAD_DIM:HEAD_DIM + 1] + jnp.exp2(sink - m))
            outs.extend(ot[:, g * CHUNK:(g + 1) * CHUNK] for g in range(grp))
        o_ref[cols, :] = jnp.concatenate(outs, axis=0).T.astype(BF)


def _swa(proj, sqt, svt, bsz, sinks, tq=512):
    t = proj.shape[0]
    seq = t // bsz
    nq = seq // tq
    per = tq // CHUNK
    kv_blk = N_WIDE * BRANCH_W // (2 * SWA_KV * HEAD_DIM)
    prev = lambda b, i: b * (seq // CHUNK) + jnp.maximum(i * per - 1, 0)
    bias, sink = _swa_tables(sinks)
    grp = SWA_HEADS // SWA_KV
    return pl.pallas_call(
        _swa_kernel,
        grid=(bsz, nq),
        in_specs=[pl.BlockSpec((BRANCH_W, tq), lambda b, i: (0, b * nq + i)),
                  pl.BlockSpec((tq, 2 * SWA_KV * HEAD_DIM), lambda b, i: (b * nq + i, kv_blk)),
                  pl.BlockSpec((CHUNK, 2 * SWA_KV * HEAD_DIM), lambda b, i: (prev(b, i), kv_blk)),
                  pl.BlockSpec((SWA_VT, tq), lambda b, i: (0, b * nq + i)),
                  pl.BlockSpec((SWA_VT, CHUNK), lambda b, i: (0, prev(b, i))),
                  _const_spec((SWA_KV, 2 * CHUNK, grp * CHUNK)),
                  _const_spec((SWA_KV, 1, grp * CHUNK))],
        out_specs=pl.BlockSpec((tq, BRANCH_W), lambda b, i: (b * nq + i, 0)),
        out_shape=jax.ShapeDtypeStruct((t, BRANCH_W), BF),
        compiler_params=_params("parallel", "parallel"),
        name="swa",
    )(sqt, proj, proj, svt, svt, bias, sink)


def _aug_tables():
    eq = np.zeros((3, LANES, FOX_PAIRS * LANES), np.float32)
    ek = np.zeros((3, LANES, FOX_PAIRS * LANES), np.float32)
    oq = np.zeros((1, FOX_PAIRS * LANES), np.float32)
    ok = np.zeros((1, FOX_PAIRS * LANES), np.float32)
    for h in range(FOX_HEADS):
        base = (h // 2) * LANES + 6 * (h % 2)
        for s in range(3):
            eq[s, h, base + s] = 1.0
            ek[s, h, base + 3 + s] = -1.0
            ok[0, base + s] = 1.0
            oq[0, base + 3 + s] = 1.0
    return eq, ek, oq, ok


def _fox_prep_kernel(q_ref, k_ref, f_ref, bf_ref, eq_ref, ek_ref, oq_ref, ok_ref, qp_ref, kp_ref, cum_ref):
    ts = q_ref.shape[0]

    @pl.when(pl.program_id(1) == 0)
    def _():
        cum_ref[...] = jnp.zeros_like(cum_ref)

    z = f_ref[...] + bf_ref[...]
    c = jnp.minimum(z, 0.0) - jnp.log1p(jnp.exp(-jnp.abs(z)))
    d = 1
    while d < ts:
        c = c + _shift_rows(c, d, 0.0)
        d *= 2
    c = c + cum_ref[...]
    cum_ref[...] = c[ts - 1:ts, :]
    c = c * LOG2E
    c1 = c.astype(BF)
    r1 = c - c1.astype(F32)
    c2 = r1.astype(BF)
    c3 = (r1 - c2.astype(F32)).astype(BF)
    augq = _dot(c1, eq_ref[0]) + _dot(c2, eq_ref[1]) + _dot(c3, eq_ref[2]) + oq_ref[...]
    augk = _dot(c1, ek_ref[0]) + _dot(c2, ek_ref[1]) + _dot(c3, ek_ref[2]) + ok_ref[...]
    qt = (q_ref[...].astype(F32) * (HEAD_DIM ** -0.5 * LOG2E)).T.astype(BF)
    augqt = augq.T.astype(BF)
    k = k_ref[...]
    for p in range(FOX_PAIRS):
        qp_ref[2 * p * LANES:(2 * p + 1) * LANES, :] = qt[p * LANES:(p + 1) * LANES]
        qp_ref[(2 * p + 1) * LANES:(2 * p + 2) * LANES, :] = augqt[p * LANES:(p + 1) * LANES]
        kp_ref[:, 2 * p * LANES:(2 * p + 1) * LANES] = k[:, p * LANES:(p + 1) * LANES]
        kp_ref[:, (2 * p + 1) * LANES:(2 * p + 2) * LANES] = augk[:, p * LANES:(p + 1) * LANES].astype(BF)


def _fox_prep(proj, f, bsz, bf_pad, ts=512):
    t = proj.shape[0]
    ns = t // bsz // ts
    eq, ek, oq, ok = _aug_tables()
    wide = 2 * FOX_PAIRS * LANES
    return pl.pallas_call(
        _fox_prep_kernel,
        grid=(bsz, ns),
        in_specs=[pl.BlockSpec((ts, BRANCH_W), lambda b, j: (b * ns + j, COL_DQ)),
                  pl.BlockSpec((ts, BRANCH_W), lambda b, j: (b * ns + j, COL_DK)),
                  pl.BlockSpec((ts, LANES), lambda b, j: (b * ns + j, 0)),
                  _const_spec((1, LANES)),
                  _const_spec(eq.shape), _const_spec(ek.shape), _const_spec(oq.shape), _const_spec(ok.shape)],
        out_specs=[pl.BlockSpec((wide, ts), lambda b, j: (0, b * ns + j)),
                   pl.BlockSpec((ts, wide), lambda b, j: (b * ns + j, 0))],
        out_shape=[jax.ShapeDtypeStruct((wide, t), BF), jax.ShapeDtypeStruct((t, wide), BF)],
        scratch_shapes=[pltpu.VMEM((1, LANES), F32)],
        compiler_params=_params("parallel", "arbitrary"),
        name="fox_prep",
    )(proj, proj, f, bf_pad, jnp.asarray(eq, BF), jnp.asarray(ek, BF), jnp.asarray(oq), jnp.asarray(ok))


def _fox_kernel(q_ref, k_ref, vt_ref, o_ref, st_ref, pt_ref, al_ref, m_ref, acc_ref):
    tq = q_ref.shape[1]
    tk = tq // 2
    qi = pl.program_id(2)
    row = lax.broadcasted_iota(jnp.int32, (2 * LANES, tq), 0)
    q = q_ref[...]
    zero = jnp.zeros_like(q)
    in0 = (row < HEAD_DIM) | ((row >= LANES) & (row < LANES + 6))
    in1 = ((row >= HEAD_DIM) & (row < LANES)) | ((row >= LANES + 6) & (row < LANES + 12))
    qh = (jnp.where(in0, q, zero), jnp.where(in1, q, zero))
    for h in range(2):
        m_ref[h] = jnp.full(m_ref.shape[1:], NEG, F32)
        acc_ref[h] = jnp.zeros(acc_ref.shape[1:], F32)
        pt_ref[1, h] = jnp.zeros(pt_ref.shape[2:], BF)
        al_ref[1, h] = jnp.ones(al_ref.shape[2:], F32)

    def scores(t, slot):
        k = k_ref[pl.ds(pl.multiple_of(t * tk, tk), tk), :]
        for h in range(2):
            st_ref[slot, h] = _dot(k, qh[h])

    def numerators(slot, key_offset):
        for h in range(2):
            st = st_ref[slot, h]
            if key_offset is not None:
                key = lax.broadcasted_iota(jnp.int32, (tk, tq), 0) + key_offset
                qry = lax.broadcasted_iota(jnp.int32, (tk, tq), 1)
                st = jnp.where(key <= qry, st, NEG)
            m_old = m_ref[h]
            m_new = jnp.maximum(m_old, jnp.max(st, axis=0, keepdims=True))
            al_ref[slot, h] = jnp.exp2(m_old - m_new)
            pt_ref[slot, h] = jnp.exp2(st - m_new).astype(BF)
            m_ref[h] = m_new

    def accumulate(t, slot):
        vt = vt_ref[:, pl.ds(pl.multiple_of(t * tk, tk), tk)]
        for h in range(2):
            acc_ref[h] = al_ref[slot, h] * acc_ref[h] + _dot(vt[h * VT_ROWS:(h + 1) * VT_ROWS], pt_ref[slot, h])

    def tile_pair(j, diagonal):
        scores(2 * j + 1, 1)
        accumulate(jnp.maximum(2 * j - 1, 0), 1)
        numerators(0, 0 if diagonal else None)
        if not diagonal:
            scores(2 * j + 2, 0)
        accumulate(2 * j, 0)
        numerators(1, tk if diagonal else None)

    def body(jj, carry):
        for u in range(FOX_UNROLL):
            tile_pair(FOX_UNROLL * jj + u, False)
        return carry

    def remainder(j, carry):
        tile_pair(j, False)
        return carry

    scores(0, 0)
    lax.fori_loop(0, qi // FOX_UNROLL, body, 0)
    lax.fori_loop((qi // FOX_UNROLL) * FOX_UNROLL, qi, remainder, 0)
    tile_pair(qi, True)
    accumulate(2 * qi + 1, 1)
    outs = []
    for h in range(2):
        acc = acc_ref[h]
        outs.append(acc[:HEAD_DIM] / acc[HEAD_DIM:HEAD_DIM + 1])
    o_ref[...] = jnp.concatenate(outs, axis=0).T.astype(BF)


def _fox(qp, kp, vt, bsz, tq=512):
    t = kp.shape[0]
    seq = t // bsz
    nq = seq // tq
    return pl.pallas_call(
        _fox_kernel,
        grid=(bsz, FOX_PAIRS, nq),
        in_specs=[pl.BlockSpec((2 * LANES, tq), lambda b, p, i: (p, b * nq + i)),
                  pl.BlockSpec((seq, 2 * LANES), lambda b, p, i: (b, p)),
                  pl.BlockSpec((2 * VT_ROWS, seq), lambda b, p, i: (p, b))],
        out_specs=pl.BlockSpec((tq, LANES), lambda b, p, i: (b * nq + i, p)),
        out_shape=jax.ShapeDtypeStruct((t, BRANCH_W), BF),
        scratch_shapes=[pltpu.VMEM((2, 2, tq // 2, tq), F32),
                        pltpu.VMEM((2, 2, tq // 2, tq), BF),
                        pltpu.VMEM((2, 2, 1, tq), F32),
                        pltpu.VMEM((2, 1, tq), F32),
                        pltpu.VMEM((2, VT_ROWS, tq), F32)],
        compiler_params=_params("parallel", "parallel", "arbitrary"),
        name="fox",
    )(qp, kp, vt)


def _merge_kernel(h_ref, g_ref, oa_ref, ob_ref, oc_ref, od_ref, wg_ref, bg_ref, wb_ref, wo_ref, out_ref):
    h = h_ref[...]
    xn = _rms(h, g_ref[...]).astype(BF)
    merged = None
    for br, o_ref in enumerate((oa_ref, ob_ref, oc_ref, od_ref)):
        gate = _sigmoid(_dot(xn, wg_ref[br]) + bg_ref[br])
        term = gate * _dot(o_ref[...], wb_ref[br])
        merged = term if merged is None else merged + term
    out_ref[...] = h + _dot(merged.astype(BF), wo_ref[...])


def _merge(h, g, oa, ob, oc, od, wg, bg, wb, wo, tm=512):
    t = h.shape[0]
    row = lambda i: (i, 0)
    return pl.pallas_call(
        _merge_kernel,
        grid=(t // tm,),
        in_specs=[pl.BlockSpec((tm, D_MODEL), row),
                  _const_spec((1, D_MODEL)),
                  pl.BlockSpec((tm, BRANCH_W), row), pl.BlockSpec((tm, BRANCH_W), row),
                  pl.BlockSpec((tm, BRANCH_W), row), pl.BlockSpec((tm, BRANCH_W), row),
                  _const_spec((4, D_MODEL, D_MODEL)),
                  _const_spec((4, 1, D_MODEL)),
                  _const_spec((4, BRANCH_W, D_MODEL)),
                  _const_spec((D_MODEL, D_MODEL))],
        out_specs=pl.BlockSpec((tm, D_MODEL), row),
        out_shape=jax.ShapeDtypeStruct((t, D_MODEL), F32),
        compiler_params=_params("parallel"),
        name="merge",
    )(h, g, oa, ob, oc, od, wg, bg, wb, wo)


def _memkv_kernel(mem_ref, g_ref, w_ref, kv_ref):
    mn = _rms(mem_ref[...], g_ref[...]).astype(BF)
    kv_ref[...] = _dot(mn, w_ref[...]).astype(BF)


def _memkv(mem2, g, w, m_len):
    n = mem2.shape[0]
    width = 2 * X_HEADS * X_HEAD_DIM
    return pl.pallas_call(
        _memkv_kernel,
        grid=(n // m_len,),
        in_specs=[pl.BlockSpec((m_len, D_MODEL), lambda b: (b, 0)),
                  _const_spec((1, D_MODEL)),
                  _const_spec((D_MODEL, width))],
        out_specs=pl.BlockSpec((m_len, width), lambda b: (b, 0)),
        out_shape=jax.ShapeDtypeStruct((n, width), BF),
        compiler_params=_params("parallel"),
        name="memkv",
    )(mem2, g, w)


def _cross_kernel(h_ref, g_ref, wq_ref, kv_ref, wo_ref, out_ref):
    h = h_ref[...]
    hn = _rms(h, g_ref[...]).astype(BF)
    q = _dot(hn, wq_ref[...]).astype(BF)
    kv = kv_ref[...]
    width = X_HEADS * X_HEAD_DIM
    outs = []
    for hd in range(X_HEADS):
        cs = slice(hd * X_HEAD_DIM, (hd + 1) * X_HEAD_DIM)
        s = _dot_nt(q[:, cs], kv[:, cs]) * (X_HEAD_DIM ** -0.5)
        m = jnp.max(s, axis=-1, keepdims=True)
        p = jnp.exp(s - m)
        denom = jnp.sum(p, axis=-1, keepdims=True)
        v = kv[:, width + hd * X_HEAD_DIM:width + (hd + 1) * X_HEAD_DIM]
        outs.append((_dot(p.astype(BF), v) / denom).astype(BF))
    o = jnp.concatenate(outs, axis=1)
    out_ref[...] = h + _dot(o, wo_ref[...])


def _cross(h, g, wq, kv, wo, bsz, m_len, tm=512):
    t = h.shape[0]
    per = t // bsz // tm
    width = X_HEADS * X_HEAD_DIM
    return pl.pallas_call(
        _cross_kernel,
        grid=(bsz, per),
        in_specs=[pl.BlockSpec((tm, D_MODEL), lambda b, i: (b * per + i, 0)),
                  _const_spec((1, D_MODEL)),
                  _const_spec((D_MODEL, width)),
                  pl.BlockSpec((m_len, 2 * width), lambda b, i: (b, 0)),
                  _const_spec((width, D_MODEL))],
        out_specs=pl.BlockSpec((tm, D_MODEL), lambda b, i: (b * per + i, 0)),
        out_shape=jax.ShapeDtypeStruct((t, D_MODEL), F32),
        compiler_params=_params("parallel", "parallel"),
        name="cross",
    )(h, g, wq, kv, wo)


def _swiglu(xb, w13_ref, w2_ref):
    out = None
    for lo, hi in FF_SPLITS:
        gate = _dot(xb, w13_ref[:, lo:hi])
        up = _dot(xb, w13_ref[:, D_FF + lo:D_FF + hi])
        act = (gate * _sigmoid(gate) * up).astype(BF)
        part = _dot(act, w2_ref[lo:hi, :])
        out = part if out is None else out + part
    return out


def _ffn_kernel(h_ref, g_ref, w13_ref, w2_ref, out_ref):
    h = h_ref[...]
    out_ref[...] = h + _swiglu(_rms(h, g_ref[...]).astype(BF), w13_ref, w2_ref)


def _ffn(h, g, w13, w2, tm=512):
    t = h.shape[0]
    return pl.pallas_call(
        _ffn_kernel,
        grid=(t // tm,),
        in_specs=[pl.BlockSpec((tm, D_MODEL), lambda i: (i, 0)),
                  _const_spec((1, D_MODEL)),
                  _const_spec((D_MODEL, 2 * D_FF)),
                  _const_spec((D_FF, D_MODEL))],
        out_specs=pl.BlockSpec((tm, D_MODEL), lambda i: (i, 0)),
        out_shape=jax.ShapeDtypeStruct((t, D_MODEL), F32),
        compiler_params=_params("parallel"),
        name="ffn",
    )(h, g, w13, w2)


def _split_slabs(ref):
    rows = ref.shape[0] // ROW_SLABS
    return jnp.concatenate([ref[pl.ds(c, rows, stride=ROW_SLABS), :] for c in range(ROW_SLABS)], axis=1)


def _store_slabs(ref, x):
    rows = ref.shape[0] // ROW_SLABS
    for c in range(ROW_SLABS):
        ref[pl.ds(c, rows, stride=ROW_SLABS), :] = x[:, c * LANES:(c + 1) * LANES]


def _slab_spec(tm, index_map):
    return pl.BlockSpec((tm * ROW_SLABS, LANES), index_map)


def _router_kernel(h_ref, g_ref, whi_ref, wlo_ref, br_ref, hn_ref, idx_ref, wts_ref):
    hn = _rms(h_ref[...], g_ref[...])
    _store_slabs(hn_ref, hn)
    hi = hn.astype(BF)
    lo = (hn - hi.astype(F32)).astype(BF)
    logits = _dot(hi, whi_ref[...]) + (_dot(lo, whi_ref[...]) + _dot(hi, wlo_ref[...])) + br_ref[...]
    lane = lax.broadcasted_iota(jnp.int32, logits.shape, 1)
    logits = jnp.where(lane < N_EXPERTS, logits, NEG)
    v1 = jnp.max(logits, axis=-1, keepdims=True)
    i1 = jnp.min(jnp.where(logits == v1, lane, LANES), axis=-1, keepdims=True)
    rest = jnp.where(lane == i1, NEG, logits)
    v2 = jnp.max(rest, axis=-1, keepdims=True)
    i2 = jnp.min(jnp.where(rest == v2, lane, LANES), axis=-1, keepdims=True)
    e2 = jnp.exp(v2 - v1)
    w1 = 1.0 / (1.0 + e2)
    w2 = e2 / (1.0 + e2)
    idx_ref[...] = jnp.where(lane == 0, i1, jnp.where(lane == 1, i2, 0))
    wts_ref[...] = jnp.where(lane == 0, w1, jnp.where(lane == 1, w2, 0.0))


def _router(h, g, whi, wlo, br, chunk, n_chunks, tm=512):
    t = h.shape[0] // n_chunks
    first = chunk * (t // tm)
    return pl.pallas_call(
        _router_kernel,
        grid=(t // tm,),
        in_specs=[pl.BlockSpec((tm, D_MODEL), lambda i: (first + i, 0)),
                  _const_spec((1, D_MODEL)),
                  _const_spec((D_MODEL, LANES)),
                  _const_spec((D_MODEL, LANES)),
                  _const_spec((1, LANES))],
        out_specs=[_slab_spec(tm, lambda i: (i, 0)),
                   pl.BlockSpec((tm, LANES), lambda i: (i, 0)),
                   pl.BlockSpec((tm, LANES), lambda i: (i, 0))],
        out_shape=[jax.ShapeDtypeStruct((t * ROW_SLABS, LANES), F32),
                   jax.ShapeDtypeStruct((t, LANES), jnp.int32),
                   jax.ShapeDtypeStruct((t, LANES), F32)],
        compiler_params=_params("parallel"),
        name="router",
    )(h, g, whi, wlo, br)


def _route_plan(idx, tm):
    t = idx.shape[0]
    n_pairs = 2 * t
    n_rows = n_pairs + N_EXPERTS * tm
    e_flat = jnp.concatenate([idx[:, 0], idx[:, 1]])
    onehot = (e_flat[:, None] == jnp.arange(N_EXPERTS, dtype=jnp.int32)[None, :]).astype(jnp.int32)
    csum = jnp.cumsum(onehot, axis=0)
    rank = jnp.sum(onehot * csum, axis=1) - 1
    counts = csum[-1]
    padded = ((counts + tm - 1) // tm) * tm
    ends = jnp.cumsum(padded)
    starts = ends - padded
    pos = starts[e_flat] + rank
    order = jnp.argsort(e_flat, stable=True).astype(jnp.int32)
    first = jnp.cumsum(counts) - counts
    r = jnp.minimum(jnp.arange(n_rows, dtype=jnp.int32), ends[-1] - 1)
    e_r = jnp.minimum(jnp.searchsorted(ends, r, side="right").astype(jnp.int32), N_EXPERTS - 1)
    local = r - starts[e_r]
    src_pair = order[jnp.clip(first[e_r] + local, 0, n_pairs - 1)]
    src_tok = jnp.where(local < counts[e_r], src_pair % t, 0).astype(jnp.int32)
    tile_expert = e_r[::tm]
    n_valid = (ends[-1] // tm).astype(jnp.int32).reshape(1)
    return pos.astype(jnp.int32), src_tok, tile_expert, n_valid


def _gather_rows(table, idx):
    n = idx.shape[0]
    info = plsc.get_sparse_core_info()
    n_workers = info.num_cores * info.num_subcores
    per_worker = n // n_workers
    assert per_worker * n_workers == n and per_worker % SC_CHUNK == 0, (n, n_workers)
    mesh = plsc.VectorSubcoreMesh(core_axis_name="c", subcore_axis_name="s")

    @functools.partial(
        pl.kernel, mesh=mesh,
        out_type=jax.ShapeDtypeStruct((n,) + table.shape[1:], table.dtype),
        scratch_types=[pltpu.VMEM((SC_CHUNK,), jnp.int32),
                       pltpu.VMEM((SC_CHUNK,) + table.shape[1:], table.dtype),
                       pltpu.SemaphoreType.DMA],
    )
    def gather(table_hbm, idx_hbm, out_hbm, idx_v, rows_v, sem):
        worker = lax.axis_index("s") * info.num_cores + lax.axis_index("c")
        base = worker * per_worker

        @pl.loop(0, per_worker // SC_CHUNK)
        def _(i):
            off = pl.multiple_of(base + i * SC_CHUNK, SC_CHUNK)
            pltpu.sync_copy(idx_hbm.at[pl.ds(off, SC_CHUNK)], idx_v)
            pltpu.async_copy(table_hbm.at[idx_v], rows_v, sem).wait()
            pltpu.sync_copy(rows_v, out_hbm.at[pl.ds(off, SC_CHUNK)])

    return gather(table, idx)


def _expert_ffn_kernel(te_ref, nv_ref, x_ref, w13_ref, w2_ref, y_ref):
    valid = pl.program_id(0) < nv_ref[0]

    @pl.when(valid)
    def _():
        _store_slabs(y_ref, _swiglu(_split_slabs(x_ref).astype(BF), w13_ref.at[0], w2_ref.at[0]))

    @pl.when(jnp.logical_not(valid))
    def _():
        y_ref[...] = jnp.zeros_like(y_ref)


def _expert_ffn(xs, tile_expert, n_valid, w13, w2, tm):
    n_rows = xs.shape[0] // ROW_SLABS
    last = lambda nv: jnp.maximum(nv[0] - 1, 0)
    grid_spec = pltpu.PrefetchScalarGridSpec(
        num_scalar_prefetch=2,
        grid=(n_rows // tm,),
        in_specs=[_slab_spec(tm, lambda i, te, nv: (jnp.minimum(i, last(nv)), 0)),
                  pl.BlockSpec((1, D_MODEL, 2 * D_FF), lambda i, te, nv: (te[i], 0, 0), pipeline_mode=pl.Buffered(1)),
                  pl.BlockSpec((1, D_FF, D_MODEL), lambda i, te, nv: (te[i], 0, 0), pipeline_mode=pl.Buffered(1))],
        out_specs=_slab_spec(tm, lambda i, te, nv: (i, 0)),
    )
    return pl.pallas_call(
        _expert_ffn_kernel,
        grid_spec=grid_spec,
        out_shape=jax.ShapeDtypeStruct((n_rows * ROW_SLABS, LANES), F32),
        compiler_params=_params("arbitrary"),
        name="expert_ffn",
    )(tile_expert, n_valid, xs, w13, w2)


def _combine_kernel(h_ref, y0_ref, y1_ref, wts_ref, gf_ref, *rest):
    out_ref = rest[-1]
    wts = wts_ref[...]
    tot = h_ref[...] + wts[:, 0:1] * _split_slabs(y0_ref) + wts[:, 1:2] * _split_slabs(y1_ref)
    out_ref[...] = _rms(tot, gf_ref[...])


def _combine(h, yg, wts, g_final, chunk, n_chunks, out_so_far, tm=512):
    t = h.shape[0]
    nt = t // n_chunks // tm
    first = chunk * nt
    in_specs = [pl.BlockSpec((tm, D_MODEL), lambda i: (first + i, 0)),
                _slab_spec(tm, lambda i: (i, 0)),
                _slab_spec(tm, lambda i: (nt + i, 0)),
                pl.BlockSpec((tm, LANES), lambda i: (i, 0)),
                _const_spec((1, D_MODEL))]
    args = [h, yg, yg, wts, g_final]
    aliases = {}
    if out_so_far is not None:
        in_specs.append(pl.BlockSpec(memory_space=pl.ANY))
        args.append(out_so_far)
        aliases = {len(args) - 1: 0}
    return pl.pallas_call(
        _combine_kernel,
        grid=(nt,),
        in_specs=in_specs,
        out_specs=pl.BlockSpec((tm, D_MODEL), lambda i: (first + i, 0)),
        out_shape=jax.ShapeDtypeStruct((t, D_MODEL), F32),
        input_output_aliases=aliases,
        compiler_params=_params("parallel"),
        name="combine",
    )(*args)


def _moe(h, g, router_w, router_b, w13, w2, g_final, tm=512):
    wr = jnp.pad(router_w, ((0, 0), (0, LANES - N_EXPERTS)))
    whi = wr.astype(BF)
    wlo = (wr - whi.astype(F32)).astype(BF)
    br = jnp.pad(router_b, (0, LANES - N_EXPERTS)).reshape(1, LANES)
    as_rows = lambda a: a.reshape(-1, ROW_SLABS, LANES)
    as_slabs = lambda a: a.reshape(-1, LANES)
    chunks = range(MOE_CHUNKS)
    routed = [_router(h, g, whi, wlo, br, c, MOE_CHUNKS) for c in chunks]
    plans = [_route_plan(idx, tm) for _, idx, _ in routed]
    xs = [as_slabs(_gather_rows(as_rows(routed[c][0]), plans[c][1])) for c in chunks]
    ys = [_expert_ffn(xs[c], plans[c][2], plans[c][3], w13, w2, tm) for c in chunks]
    yg = [as_slabs(_gather_rows(as_rows(ys[c]), plans[c][0])) for c in chunks]
    out = None
    for c in chunks:
        out = _combine(h, yg[c], routed[c][2], g_final, c, MOE_CHUNKS, out)
    return out


def _feature_major(w, heads):
    wt = w.T.reshape(heads, HEAD_DIM, D_MODEL)
    return jnp.pad(wt, ((0, 0), (0, VT_ROWS - HEAD_DIM), (0, 0))).reshape(heads * VT_ROWS, D_MODEL)


def _pack_w_in(w_in):
    cuts = np.cumsum((512, 512, 512, 512, 512, 128, 128, 512, 512, 512, 8))[:-1].tolist()
    a_u, a_v, b_x, b_y, c_q, c_k, c_v, d_q, d_k, d_v, d_f = jnp.split(w_in, cuts, axis=-1)
    w = jnp.concatenate([b_x, b_y, d_q, d_k, c_k, c_v], axis=-1).astype(BF)
    wa = jnp.concatenate([a_u, a_v], axis=-1).astype(BF)
    wf = jnp.pad(d_f, ((0, 0), (0, LANES - FOX_HEADS))).astype(BF)
    wt = jnp.concatenate([_feature_major(d_v, FOX_HEADS), c_q.T, _feature_major(c_v, SWA_KV)], axis=0).astype(BF)
    return w, wa, wf, wt


def _fm_ones():
    ones = np.zeros((FM_ROWS, LANES), np.float32)
    for base, heads in ((0, FOX_HEADS), (VT_ALL + BRANCH_W, SWA_KV)):
        for h in range(heads):
            ones[base + h * VT_ROWS + HEAD_DIM, :] = 1.0
    return jnp.asarray(ones)


def _row(v):
    return v.reshape(1, -1)


def _hybrid_mixer(h, bsz, norm_mix, w_in, sgu_g, sgu_w, sgu_b, conv_w, conv_b, rg_wa, rg_ba, rg_wx, rg_bx,
                  rg_lambda, swa_sinks, fox_bf, w_branch, w_gate, b_gate, w_out):
    w, wa, wf, wt = _pack_w_in(w_in)
    proj, f, o_a, vt, sqt, svt = _inproj(h, _row(norm_mix), w, wa, wf, wt, _fm_ones(), _row(sgu_g), sgu_w, sgu_b.T)
    wax = jnp.concatenate([rg_wa, rg_wx], axis=-1).astype(BF)
    o_b = _rglru(proj, bsz, conv_w, _row(conv_b), wax, _row(rg_ba), _row(rg_bx), _row(rg_lambda))
    o_c = _swa(proj, sqt, svt, bsz, swa_sinks)
    bf_pad = jnp.pad(fox_bf, (0, LANES - FOX_HEADS)).reshape(1, LANES)
    qp, kp = _fox_prep(proj, f, bsz, bf_pad)
    o_d = _fox(qp, kp, vt, bsz)
    return _merge(h, _row(norm_mix), o_a, o_b, o_c, o_d, w_gate.astype(BF), b_gate[:, None, :],
                  w_branch.astype(BF), w_out.astype(BF))


def kernel(x, mem, norm_mix, w_in, sgu_g, sgu_w, sgu_b, conv_w, conv_b, rg_wa, rg_ba, rg_wx, rg_bx, rg_lambda, swa_sinks, fox_bf, w_branch, w_gate, b_gate, w_out, norm_cross, norm_mem, wq_c, wkv_c, wo_c, norm_ffn, dense_w13, dense_w2, router_w, router_b, moe_w13, moe_w2, norm_final):
    bsz, seq, d = x.shape
    m_len = mem.shape[1]
    depth = norm_mix.shape[0]
    assert depth == 2, "the final RMSNorm is fused into the routed layer, which must be the last one"
    h = x.reshape(bsz * seq, d)
    mem2 = mem.reshape(bsz * m_len, d)
    for l in range(depth):
        h = _hybrid_mixer(h, bsz, norm_mix[l], w_in[l], sgu_g[l], sgu_w[l], sgu_b[l], conv_w[l], conv_b[l],
                          rg_wa[l], rg_ba[l], rg_wx[l], rg_bx[l], rg_lambda[l], swa_sinks[l], fox_bf[l],
                          w_branch[l], w_gate[l], b_gate[l], w_out[l])
        kv = _memkv(mem2, _row(norm_mem[l]), wkv_c[l].astype(BF), m_len)
        h = _cross(h, _row(norm_cross[l]), wq_c[l].astype(BF), kv, wo_c[l].astype(BF), bsz, m_len)
        if l % 2 == 0:
            h = _ffn(h, _row(norm_ffn[l]), dense_w13[l // 2].astype(BF), dense_w2[l // 2].astype(BF))
        else:
            h = _moe(h, _row(norm_ffn[l]), router_w[l // 2], router_b[l // 2], moe_w13[l // 2].astype(BF),
                     moe_w2[l // 2].astype(BF), _row(norm_final))
    return h.reshape(bsz, seq, d)
```

```python
import functools
import math

import numpy as np
import jax
import jax.numpy as jnp
from jax import lax
from jax.experimental import pallas as pl
from jax.experimental.pallas import tpu as pltpu
from jax.experimental.pallas import tpu_sc as plsc

F32 = jnp.float32
BF = jnp.bfloat16

D_MODEL = 1024
BRANCH_W = 512
HEAD_DIM = 64
CHUNK = 128
SGU_GROUPS = 4
LRU_HEADS = 4
LRU_C = 8.0
CONV_W = 4
SWA_HEADS = 8
SWA_KV = 2
FOX_HEADS = 8
FOX_PAIRS = FOX_HEADS // 2
X_HEADS = 4
X_HEAD_DIM = 128
D_FF = 2816
N_EXPERTS = 8
EPS = 1e-6
LOG2E = math.log2(math.e)
NEG = -1e30
LANES = 128
ROW_SLABS = D_MODEL // LANES
SC_CHUNK = 64
MOE_CHUNKS = 4
VMEM_LIMIT = 56 * 1024 * 1024

COL_BX, COL_BY, COL_DQ, COL_DK = range(4)
N_WIDE = 4
N_PROJ = N_WIDE * BRANCH_W + 2 * SWA_KV * HEAD_DIM
VT_ROWS = 80
VT_ALL = FOX_HEADS * VT_ROWS
SWA_VT = SWA_KV * VT_ROWS
FM_ROWS = VT_ALL + BRANCH_W + SWA_VT
FOX_UNROLL = 4
FOX_STEP_PAIRS = 1
FF_SPLITS = ((0, 1536), (1536, D_FF))


def _rms(x, g):
    return x * lax.rsqrt(jnp.mean(x * x, axis=-1, keepdims=True) + EPS) * g


def _sigmoid(x):
    return 1.0 / (1.0 + jnp.exp(-x))


def _gelu(x):
    return 0.5 * x * (1.0 + jnp.tanh(math.sqrt(2.0 / math.pi) * (x + 0.044715 * (x * x * x))))


def _dot(a, b):
    return jnp.dot(a, b, preferred_element_type=F32)


def _dot_nt(a, b):
    return lax.dot_general(a, b, (((1,), (1,)), ((), ())), preferred_element_type=F32)


def _shift_rows(x, d, fill):
    row = lax.broadcasted_iota(jnp.int32, x.shape, 0)
    return jnp.where(row >= d, pltpu.roll(x, d, 0), fill)


def _params(*sem):
    return pltpu.CompilerParams(dimension_semantics=sem, vmem_limit_bytes=VMEM_LIMIT)


def _const_spec(shape):
    nd = len(shape)
    return pl.BlockSpec(shape, lambda *_: (0,) * nd, pipeline_mode=pl.Buffered(1))


def _inproj_kernel(h_ref, g_ref, w_ref, wa_ref, wf_ref, wt_ref, ones_ref, sg_ref, sw_ref, sbt_ref,
                   proj_ref, f_ref, oa_ref, vt_ref, sqt_ref, svt_ref):
    xn = _rms(h_ref[...], g_ref[...]).astype(BF)
    tm = xn.shape[0]
    n_col = w_ref.shape[1]
    for c in range(0, n_col, BRANCH_W):
        w = min(BRANCH_W, n_col - c)
        proj_ref[:, c:c + w] = _dot(xn, w_ref[:, c:c + w]).astype(BF)
    f_ref[...] = _dot(xn, wf_ref[...])
    ones = jnp.concatenate([ones_ref[...]] * (tm // LANES), axis=1)
    fm = (_dot_nt(wt_ref[...], xn) + ones).astype(BF)
    vt_ref[...] = fm[:VT_ALL]
    sqt_ref[...] = fm[VT_ALL:VT_ALL + BRANCH_W]
    svt_ref[...] = fm[VT_ALL + BRANCH_W:]

    u = _gelu(_dot(xn, wa_ref[:, :BRANCH_W]))
    v = _gelu(_dot(xn, wa_ref[:, BRANCH_W:]))
    vn = _rms(v, sg_ref[...]).astype(BF)
    row = lax.broadcasted_iota(jnp.int32, (CHUNK, CHUNK), 0)
    col = lax.broadcasted_iota(jnp.int32, (CHUNK, CHUNK), 1)
    gw = BRANCH_W // SGU_GROUPS
    for g in range(SGU_GROUPS):
        wg = jnp.where(col <= row, sw_ref[g], 0.0).astype(BF)
        bg = sbt_ref[:, g:g + 1]
        for c in range(tm // CHUNK):
            rs = slice(c * CHUNK, (c + 1) * CHUNK)
            cs = slice(g * gw, (g + 1) * gw)
            mixed = _dot(wg, vn[rs, cs]) + bg
            oa_ref[rs, cs] = (u[rs, cs] * mixed).astype(BF)


def _inproj(h, g, w, wa, wf, wt, ones, sgu_g, sgu_w, sgu_bt, tm=512):
    t = h.shape[0]
    row = lambda i: (i, 0)
    col = lambda i: (0, i)
    return pl.pallas_call(
        _inproj_kernel,
        grid=(t // tm,),
        in_specs=[pl.BlockSpec((tm, D_MODEL), row),
                  _const_spec((1, D_MODEL)),
                  _const_spec((D_MODEL, N_PROJ)),
                  _const_spec((D_MODEL, 2 * BRANCH_W)),
                  _const_spec((D_MODEL, LANES)),
                  _const_spec((FM_ROWS, D_MODEL)),
                  _const_spec((FM_ROWS, LANES)),
                  _const_spec((1, BRANCH_W)),
                  _const_spec((SGU_GROUPS, CHUNK, CHUNK)),
                  _const_spec((CHUNK, SGU_GROUPS))],
        out_specs=[pl.BlockSpec((tm, N_PROJ), row),
                   pl.BlockSpec((tm, LANES), row),
                   pl.BlockSpec((tm, BRANCH_W), row),
                   pl.BlockSpec((VT_ALL, tm), col),
                   pl.BlockSpec((BRANCH_W, tm), col),
                   pl.BlockSpec((SWA_VT, tm), col)],
        out_shape=[jax.ShapeDtypeStruct((t, N_PROJ), BF), jax.ShapeDtypeStruct((t, LANES), F32),
                   jax.ShapeDtypeStruct((t, BRANCH_W), BF), jax.ShapeDtypeStruct((VT_ALL, t), BF),
                   jax.ShapeDtypeStruct((BRANCH_W, t), BF), jax.ShapeDtypeStruct((SWA_VT, t), BF)],
        compiler_params=_params("parallel"),
        name="inproj",
    )(h, g, w, wa, wf, wt, ones, sgu_g, sgu_w, sgu_bt)


def _rglru_kernel(x_ref, y_ref, cw_ref, cb_ref, wax_ref, ba_ref, bx_ref, lam_ref, o_ref, xs_ref, hc_ref):
    ts = x_ref.shape[0]
    hw = BRANCH_W // LRU_HEADS

    @pl.when(pl.program_id(1) == 0)
    def _():
        xs_ref[0:8, :] = jnp.zeros((8, BRANCH_W), F32)
        hc_ref[...] = jnp.zeros_like(hc_ref)

    x = x_ref[...].astype(F32)
    xs_ref[8:8 + ts, :] = x
    cw = cw_ref[...]
    xc = cb_ref[...] + cw[CONV_W - 1:CONV_W] * x
    for k in range(1, CONV_W):
        xc = xc + cw[CONV_W - 1 - k:CONV_W - k] * xs_ref[8 - k:8 - k + ts, :]
    xs_ref[0:8, :] = x[ts - 8:ts, :]

    xcb = xc.astype(BF)
    r_parts, i_parts = [], []
    for hd in range(LRU_HEADS):
        z = _dot(xcb[:, hd * hw:(hd + 1) * hw], wax_ref[hd])
        r_parts.append(z[:, :hw])
        i_parts.append(z[:, hw:])
    r = _sigmoid(jnp.concatenate(r_parts, axis=1) + ba_ref[...])
    gi = _sigmoid(jnp.concatenate(i_parts, axis=1) + bx_ref[...])
    nl = -lam_ref[...]
    softplus = jnp.maximum(nl, 0.0) + jnp.log1p(jnp.exp(-jnp.abs(nl)))
    log_a = (-LRU_C * r) * softplus
    a = jnp.exp(log_a)
    th = jnp.tanh(log_a)
    e2 = -2.0 * th / (1.0 - th)
    b = (xc * gi) * (e2 * lax.rsqrt(jnp.maximum(e2, 1e-30)))

    sub = lax.broadcasted_iota(jnp.int32, a.shape, 0) & 7
    for d in (1, 2, 4):
        keep = sub >= d
        a_sh = jnp.where(keep, pltpu.roll(a, d, 0), 1.0)
        b_sh = jnp.where(keep, pltpu.roll(b, d, 0), 0.0)
        b = b + a * b_sh
        a = a * a_sh
    carry = hc_ref[...]
    groups = []
    for g in range(ts // 8):
        hg = a[8 * g:8 * g + 8] * carry + b[8 * g:8 * g + 8]
        groups.append(hg)
        carry = hg[7:8]
    h = jnp.concatenate(groups, axis=0)
    hc_ref[...] = carry
    o_ref[...] = (h * _gelu(y_ref[...].astype(F32))).astype(BF)


def _rglru(proj, bsz, cw, cb, wax, ba, bx, lam, ts=256):
    t = proj.shape[0]
    ns = t // bsz // ts
    return pl.pallas_call(
        _rglru_kernel,
        grid=(bsz, ns),
        in_specs=[pl.BlockSpec((ts, BRANCH_W), lambda b, j: (b * ns + j, COL_BX)),
                  pl.BlockSpec((ts, BRANCH_W), lambda b, j: (b * ns + j, COL_BY)),
                  _const_spec((CONV_W, BRANCH_W)),
                  _const_spec((1, BRANCH_W)),
                  _const_spec((LRU_HEADS, BRANCH_W // LRU_HEADS, 2 * BRANCH_W // LRU_HEADS)),
                  _const_spec((1, BRANCH_W)),
                  _const_spec((1, BRANCH_W)),
                  _const_spec((1, BRANCH_W))],
        out_specs=pl.BlockSpec((ts, BRANCH_W), lambda b, j: (b * ns + j, 0)),
        out_shape=jax.ShapeDtypeStruct((t, BRANCH_W), BF),
        scratch_shapes=[pltpu.VMEM((ts + 8, BRANCH_W), F32), pltpu.VMEM((1, BRANCH_W), F32)],
        compiler_params=_params("parallel", "arbitrary"),
        name="rglru",
    )(proj, proj, cw, cb, wax, ba, bx, lam)


def _swa_tables(sinks):
    t_idx = np.arange(CHUNK)[None, :] + CHUNK
    dist = (t_idx - np.arange(2 * CHUNK)[:, None]).astype(np.float32)
    in_win = (dist >= 0) & (dist < CHUNK)
    slopes = 2.0 ** (-(8.0 / SWA_HEADS) * np.arange(1, SWA_HEADS + 1, dtype=np.float32))
    bias = np.where(in_win[None], -slopes[:, None, None] * dist[None] * LOG2E, NEG)
    grp = SWA_HEADS // SWA_KV
    bias = bias.reshape(SWA_KV, grp, 2 * CHUNK, CHUNK).transpose(0, 2, 1, 3).reshape(SWA_KV, 2 * CHUNK, grp * CHUNK)
    sink = jnp.repeat(sinks.astype(F32) * LOG2E, CHUNK).reshape(SWA_KV, 1, grp * CHUNK)
    return jnp.asarray(bias, F32), sink


def _swa_kernel(qt_ref, kv_ref, kvp_ref, vt_ref, vtp_ref, bias_ref, sink_ref, o_ref):
    tq = kv_ref.shape[0]
    grp = SWA_HEADS // SWA_KV
    first_key = jnp.where(pl.program_id(1) == 0, CHUNK, 0)
    k_all = jnp.concatenate([kvp_ref[...], kv_ref[...]], axis=0)
    vt_all = jnp.concatenate([vtp_ref[...], vt_ref[...]], axis=1)
    qt = (qt_ref[...].astype(F32) * (HEAD_DIM ** -0.5 * LOG2E)).astype(BF)
    key_row = lax.broadcasted_iota(jnp.int32, (2 * CHUNK, grp * CHUNK), 0)
    kw = 2 * SWA_KV * HEAD_DIM
    for qb in range(tq // CHUNK):
        cols = slice(qb * CHUNK, (qb + 1) * CHUNK)
        kb = k_all[qb * CHUNK:(qb + 2) * CHUNK]
        vtb = vt_all[:, qb * CHUNK:(qb + 2) * CHUNK]
        outs = []
        for kh in range(SWA_KV):
            qg = jnp.concatenate([qt[(kh * grp + g) * HEAD_DIM:(kh * grp + g + 1) * HEAD_DIM, cols]
                                  for g in range(grp)], axis=1)
            parts = []
            if kh > 0:
                parts.append(jnp.zeros((kh * HEAD_DIM, grp * CHUNK), BF))
            parts.append(qg)
            parts.append(jnp.zeros((kw - (kh + 1) * HEAD_DIM, grp * CHUNK), BF))
            st = _dot(kb, jnp.concatenate(parts, axis=0)) + bias_ref[kh]
            if qb == 0:
                st = jnp.where(key_row >= first_key, st, NEG)
            sink = sink_ref[kh]
            m = jnp.maximum(jnp.max(st, axis=0, keepdims=True), sink)
            pt = jnp.exp2(st - m).astype(BF)
            acc = _dot(vtb[kh * VT_ROWS:(kh + 1) * VT_ROWS], pt)
            ot = acc[:HEAD_DIM] / (acc[HEAD_DIM:HEAD_DIM + 1] + jnp.exp2(sink - m))
            outs.extend(ot[:, g * CHUNK:(g + 1) * CHUNK] for g in range(grp))
        o_ref[cols, :] = jnp.concatenate(outs, axis=0).T.astype(BF)


def _swa(proj, sqt, svt, bsz, sinks, tq=1024):
    t = proj.shape[0]
    seq = t // bsz
    nq = seq // tq
    per = tq // CHUNK
    kv_blk = N_WIDE * BRANCH_W // (2 * SWA_KV * HEAD_DIM)
    prev = lambda b, i: b * (seq // CHUNK) + jnp.maximum(i * per - 1, 0)
    bias, sink = _swa_tables(sinks)
    grp = SWA_HEADS // SWA_KV
    return pl.pallas_call(
        _swa_kernel,
        grid=(bsz, nq),
        in_specs=[pl.BlockSpec((BRANCH_W, tq), lambda b, i: (0, b * nq + i)),
                  pl.BlockSpec((tq, 2 * SWA_KV * HEAD_DIM), lambda b, i: (b * nq + i, kv_blk)),
                  pl.BlockSpec((CHUNK, 2 * SWA_KV * HEAD_DIM), lambda b, i: (prev(b, i), kv_blk)),
                  pl.BlockSpec((SWA_VT, tq), lambda b, i: (0, b * nq + i)),
                  pl.BlockSpec((SWA_VT, CHUNK), lambda b, i: (0, prev(b, i))),
                  _const_spec((SWA_KV, 2 * CHUNK, grp * CHUNK)),
                  _const_spec((SWA_KV, 1, grp * CHUNK))],
        out_specs=pl.BlockSpec((tq, BRANCH_W), lambda b, i: (b * nq + i, 0)),
        out_shape=jax.ShapeDtypeStruct((t, BRANCH_W), BF),
        compiler_params=_params("parallel", "parallel"),
        name="swa",
    )(sqt, proj, proj, svt, svt, bias, sink)


def _aug_tables():
    eq = np.zeros((3, LANES, FOX_PAIRS * LANES), np.float32)
    ek = np.zeros((3, LANES, FOX_PAIRS * LANES), np.float32)
    oq = np.zeros((1, FOX_PAIRS * LANES), np.float32)
    ok = np.zeros((1, FOX_PAIRS * LANES), np.float32)
    for h in range(FOX_HEADS):
        base = (h // 2) * LANES + 6 * (h % 2)
        for s in range(3):
            eq[s, h, base + s] = 1.0
            ek[s, h, base + 3 + s] = -1.0
            ok[0, base + s] = 1.0
            oq[0, base + 3 + s] = 1.0
    return eq, ek, oq, ok


def _fox_prep_kernel(q_ref, k_ref, f_ref, bf_ref, eq_ref, ek_ref, oq_ref, ok_ref, qp_ref, kp_ref, cum_ref):
    ts = q_ref.shape[0]

    @pl.when(pl.program_id(1) == 0)
    def _():
        cum_ref[...] = jnp.zeros_like(cum_ref)

    z = f_ref[...] + bf_ref[...]
    c = jnp.minimum(z, 0.0) - jnp.log1p(jnp.exp(-jnp.abs(z)))
    d = 1
    while d < ts:
        c = c + _shift_rows(c, d, 0.0)
        d *= 2
    c = c + cum_ref[...]
    cum_ref[...] = c[ts - 1:ts, :]
    c = c * LOG2E
    c1 = c.astype(BF)
    r1 = c - c1.astype(F32)
    c2 = r1.astype(BF)
    c3 = (r1 - c2.astype(F32)).astype(BF)
    augq = _dot(c1, eq_ref[0]) + _dot(c2, eq_ref[1]) + _dot(c3, eq_ref[2]) + oq_ref[...]
    augk = _dot(c1, ek_ref[0]) + _dot(c2, ek_ref[1]) + _dot(c3, ek_ref[2]) + ok_ref[...]
    qt = (q_ref[...].astype(F32) * (HEAD_DIM ** -0.5 * LOG2E)).T.astype(BF)
    augqt = augq.T.astype(BF)
    k = k_ref[...]
    for p in range(FOX_PAIRS):
        qp_ref[2 * p * LANES:(2 * p + 1) * LANES, :] = qt[p * LANES:(p + 1) * LANES]
        qp_ref[(2 * p + 1) * LANES:(2 * p + 2) * LANES, :] = augqt[p * LANES:(p + 1) * LANES]
        kp_ref[:, 2 * p * LANES:(2 * p + 1) * LANES] = k[:, p * LANES:(p + 1) * LANES]
        kp_ref[:, (2 * p + 1) * LANES:(2 * p + 2) * LANES] = augk[:, p * LANES:(p + 1) * LANES].astype(BF)


def _fox_prep(proj, f, bsz, bf_pad, ts=1024):
    t = proj.shape[0]
    ns = t // bsz // ts
    eq, ek, oq, ok = _aug_tables()
    wide = 2 * FOX_PAIRS * LANES
    return pl.pallas_call(
        _fox_prep_kernel,
        grid=(bsz, ns),
        in_specs=[pl.BlockSpec((ts, BRANCH_W), lambda b, j: (b * ns + j, COL_DQ)),
                  pl.BlockSpec((ts, BRANCH_W), lambda b, j: (b * ns + j, COL_DK)),
                  pl.BlockSpec((ts, LANES), lambda b, j: (b * ns + j, 0)),
                  _const_spec((1, LANES)),
                  _const_spec(eq.shape), _const_spec(ek.shape), _const_spec(oq.shape), _const_spec(ok.shape)],
        out_specs=[pl.BlockSpec((wide, ts), lambda b, j: (0, b * ns + j)),
                   pl.BlockSpec((ts, wide), lambda b, j: (b * ns + j, 0))],
        out_shape=[jax.ShapeDtypeStruct((wide, t), BF), jax.ShapeDtypeStruct((t, wide), BF)],
        scratch_shapes=[pltpu.VMEM((1, LANES), F32)],
        compiler_params=_params("parallel", "arbitrary"),
        name="fox_prep",
    )(proj, proj, f, bf_pad, jnp.asarray(eq, BF), jnp.asarray(ek, BF), jnp.asarray(oq), jnp.asarray(ok))


def _fox_kernel(q_ref, k_ref, vt_ref, o_ref, st_ref, pt_ref, al_ref, m_ref, acc_ref):
    tq = q_ref.shape[1]
    tk = tq // 2
    n_heads = m_ref.shape[0]
    qi = pl.program_id(2)
    row = lax.broadcasted_iota(jnp.int32, (2 * LANES, tq), 0)
    in0 = (row < HEAD_DIM) | ((row >= LANES) & (row < LANES + 6))
    in1 = ((row >= HEAD_DIM) & (row < LANES)) | ((row >= LANES + 6) & (row < LANES + 12))
    qh = []
    for p in range(n_heads // 2):
        q = q_ref[2 * p * LANES:(2 * p + 2) * LANES, :]
        zero = jnp.zeros_like(q)
        qh += [jnp.where(in0, q, zero), jnp.where(in1, q, zero)]
    for h in range(n_heads):
        m_ref[h] = jnp.full(m_ref.shape[1:], NEG, F32)
        acc_ref[h] = jnp.zeros(acc_ref.shape[1:], F32)
        pt_ref[1, h] = jnp.zeros(pt_ref.shape[2:], BF)
        al_ref[1, h] = jnp.ones(al_ref.shape[2:], F32)

    def scores(t, slot):
        rows = pl.ds(pl.multiple_of(t * tk, tk), tk)
        for p in range(n_heads // 2):
            kp = k_ref[rows, 2 * p * LANES:(2 * p + 2) * LANES]
            for h in (2 * p, 2 * p + 1):
                st_ref[slot, h] = _dot(kp, qh[h])

    def numerators(slot, key_offset):
        for h in range(n_heads):
            st = st_ref[slot, h]
            if key_offset is not None:
                key = lax.broadcasted_iota(jnp.int32, (tk, tq), 0) + key_offset
                qry = lax.broadcasted_iota(jnp.int32, (tk, tq), 1)
                st = jnp.where(key <= qry, st, NEG)
            m_old = m_ref[h]
            m_new = jnp.maximum(m_old, jnp.max(st, axis=0, keepdims=True))
            al_ref[slot, h] = jnp.exp2(m_old - m_new)
            pt_ref[slot, h] = jnp.exp2(st - m_new).astype(BF)
            m_ref[h] = m_new

    def accumulate(t, slot):
        cols = pl.ds(pl.multiple_of(t * tk, tk), tk)
        for h in range(n_heads):
            vt = vt_ref[h * VT_ROWS:(h + 1) * VT_ROWS, cols]
            acc_ref[h] = al_ref[slot, h] * acc_ref[h] + _dot(vt, pt_ref[slot, h])

    def tile_pair(j, diagonal):
        scores(2 * j + 1, 1)
        accumulate(jnp.maximum(2 * j - 1, 0), 1)
        numerators(0, 0 if diagonal else None)
        if not diagonal:
            scores(2 * j + 2, 0)
        accumulate(2 * j, 0)
        numerators(1, tk if diagonal else None)

    def body(jj, carry):
        for u in range(FOX_UNROLL):
            tile_pair(FOX_UNROLL * jj + u, False)
        return carry

    def remainder(j, carry):
        tile_pair(j, False)
        return carry

    scores(0, 0)
    lax.fori_loop(0, qi // FOX_UNROLL, body, 0)
    lax.fori_loop((qi // FOX_UNROLL) * FOX_UNROLL, qi, remainder, 0)
    tile_pair(qi, True)
    accumulate(2 * qi + 1, 1)
    outs = []
    for h in range(n_heads):
        acc = acc_ref[h]
        outs.append(acc[:HEAD_DIM] / acc[HEAD_DIM:HEAD_DIM + 1])
    o_ref[...] = jnp.concatenate(outs, axis=0).T.astype(BF)


def _fox(qp, kp, vt, bsz, tq=512):
    t = kp.shape[0]
    seq = t // bsz
    nq = seq // tq
    np_ = FOX_STEP_PAIRS
    nh = 2 * np_
    return pl.pallas_call(
        _fox_kernel,
        grid=(bsz, FOX_PAIRS // np_, nq),
        in_specs=[pl.BlockSpec((np_ * 2 * LANES, tq), lambda b, p, i: (p, b * nq + i)),
                  pl.BlockSpec((seq, np_ * 2 * LANES), lambda b, p, i: (b, p)),
                  pl.BlockSpec((nh * VT_ROWS, seq), lambda b, p, i: (p, b))],
        out_specs=pl.BlockSpec((tq, np_ * LANES), lambda b, p, i: (b * nq + i, p)),
        out_shape=jax.ShapeDtypeStruct((t, BRANCH_W), BF),
        scratch_shapes=[pltpu.VMEM((2, nh, tq // 2, tq), F32),
                        pltpu.VMEM((2, nh, tq // 2, tq), BF),
                        pltpu.VMEM((2, nh, 1, tq), F32),
                        pltpu.VMEM((nh, 1, tq), F32),
                        pltpu.VMEM((nh, VT_ROWS, tq), F32)],
        compiler_params=_params("parallel", "parallel", "arbitrary"),
        name="fox",
    )(qp, kp, vt)


def _merge_kernel(h_ref, g_ref, oa_ref, ob_ref, oc_ref, od_ref, wg_ref, bg_ref, wb_ref, wo_ref, out_ref):
    h = h_ref[...]
    xn = _rms(h, g_ref[...]).astype(BF)
    merged = None
    for br, o_ref in enumerate((oa_ref, ob_ref, oc_ref, od_ref)):
        gate = _sigmoid(_dot(xn, wg_ref[br]) + bg_ref[br])
        term = gate * _dot(o_ref[...], wb_ref[br])
        merged = term if merged is None else merged + term
    out_ref[...] = h + _dot(merged.astype(BF), wo_ref[...])


def _merge(h, g, oa, ob, oc, od, wg, bg, wb, wo, tm=512):
    t = h.shape[0]
    row = lambda i: (i, 0)
    return pl.pallas_call(
        _merge_kernel,
        grid=(t // tm,),
        in_specs=[pl.BlockSpec((tm, D_MODEL), row),
                  _const_spec((1, D_MODEL)),
                  pl.BlockSpec((tm, BRANCH_W), row), pl.BlockSpec((tm, BRANCH_W), row),
                  pl.BlockSpec((tm, BRANCH_W), row), pl.BlockSpec((tm, BRANCH_W), row),
                  _const_spec((4, D_MODEL, D_MODEL)),
                  _const_spec((4, 1, D_MODEL)),
                  _const_spec((4, BRANCH_W, D_MODEL)),
                  _const_spec((D_MODEL, D_MODEL))],
        out_specs=pl.BlockSpec((tm, D_MODEL), row),
        out_shape=jax.ShapeDtypeStruct((t, D_MODEL), F32),
        compiler_params=_params("parallel"),
        name="merge",
    )(h, g, oa, ob, oc, od, wg, bg, wb, wo)


def _memkv_kernel(mem_ref, g_ref, w_ref, kv_ref):
    mn = _rms(mem_ref[...], g_ref[...]).astype(BF)
    kv_ref[...] = _dot(mn, w_ref[...]).astype(BF)


def _memkv(mem2, g, w, m_len):
    n = mem2.shape[0]
    width = 2 * X_HEADS * X_HEAD_DIM
    return pl.pallas_call(
        _memkv_kernel,
        grid=(n // m_len,),
        in_specs=[pl.BlockSpec((m_len, D_MODEL), lambda b: (b, 0)),
                  _const_spec((1, D_MODEL)),
                  _const_spec((D_MODEL, width))],
        out_specs=pl.BlockSpec((m_len, width), lambda b: (b, 0)),
        out_shape=jax.ShapeDtypeStruct((n, width), BF),
        compiler_params=_params("parallel"),
        name="memkv",
    )(mem2, g, w)


def _cross_kernel(h_ref, g_ref, wq_ref, kv_ref, wo_ref, out_ref):
    h = h_ref[...]
    hn = _rms(h, g_ref[...]).astype(BF)
    q = _dot(hn, wq_ref[...]).astype(BF)
    kv = kv_ref[...]
    width = X_HEADS * X_HEAD_DIM
    outs = []
    for hd in range(X_HEADS):
        cs = slice(hd * X_HEAD_DIM, (hd + 1) * X_HEAD_DIM)
        s = _dot_nt(q[:, cs], kv[:, cs]) * (X_HEAD_DIM ** -0.5)
        m = jnp.max(s, axis=-1, keepdims=True)
        p = jnp.exp(s - m)
        denom = jnp.sum(p, axis=-1, keepdims=True)
        v = kv[:, width + hd * X_HEAD_DIM:width + (hd + 1) * X_HEAD_DIM]
        outs.append((_dot(p.astype(BF), v) / denom).astype(BF))
    o = jnp.concatenate(outs, axis=1)
    out_ref[...] = h + _dot(o, wo_ref[...])


def _cross(h, g, wq, kv, wo, bsz, m_len, tm=1024):
    t = h.shape[0]
    per = t // bsz // tm
    width = X_HEADS * X_HEAD_DIM
    return pl.pallas_call(
        _cross_kernel,
        grid=(bsz, per),
        in_specs=[pl.BlockSpec((tm, D_MODEL), lambda b, i: (b * per + i, 0)),
                  _const_spec((1, D_MODEL)),
                  _const_spec((D_MODEL, width)),
                  pl.BlockSpec((m_len, 2 * width), lambda b, i: (b, 0)),
                  _const_spec((width, D_MODEL))],
        out_specs=pl.BlockSpec((tm, D_MODEL), lambda b, i: (b * per + i, 0)),
        out_shape=jax.ShapeDtypeStruct((t, D_MODEL), F32),
        compiler_params=_params("parallel", "parallel"),
        name="cross",
    )(h, g, wq, kv, wo)


def _swiglu(xb, w13_ref, w2_ref):
    out = None
    for lo, hi in FF_SPLITS:
        gate = _dot(xb, w13_ref[:, lo:hi])
        up = _dot(xb, w13_ref[:, D_FF + lo:D_FF + hi])
        act = (gate * _sigmoid(gate) * up).astype(BF)
        part = _dot(act, w2_ref[lo:hi, :])
        out = part if out is None else out + part
    return out


def _ffn_kernel(h_ref, g_ref, w13_ref, w2_ref, out_ref):
    h = h_ref[...]
    out_ref[...] = h + _swiglu(_rms(h, g_ref[...]).astype(BF), w13_ref, w2_ref)


def _ffn(h, g, w13, w2, tm=512):
    t = h.shape[0]
    return pl.pallas_call(
        _ffn_kernel,
        grid=(t // tm,),
        in_specs=[pl.BlockSpec((tm, D_MODEL), lambda i: (i, 0)),
                  _const_spec((1, D_MODEL)),
                  _const_spec((D_MODEL, 2 * D_FF)),
                  _const_spec((D_FF, D_MODEL))],
        out_specs=pl.BlockSpec((tm, D_MODEL), lambda i: (i, 0)),
        out_shape=jax.ShapeDtypeStruct((t, D_MODEL), F32),
        compiler_params=_params("parallel"),
        name="ffn",
    )(h, g, w13, w2)


def _split_slabs(ref):
    rows = ref.shape[0] // ROW_SLABS
    return jnp.concatenate([ref[pl.ds(c, rows, stride=ROW_SLABS), :] for c in range(ROW_SLABS)], axis=1)


def _store_slabs(ref, x):
    rows = ref.shape[0] // ROW_SLABS
    for c in range(ROW_SLABS):
        ref[pl.ds(c, rows, stride=ROW_SLABS), :] = x[:, c * LANES:(c + 1) * LANES]


def _slab_spec(tm, index_map):
    return pl.BlockSpec((tm * ROW_SLABS, LANES), index_map)


def _router_kernel(h_ref, g_ref, whi_ref, wlo_ref, br_ref, hn_ref, idx_ref, wts_ref):
    hn = _rms(h_ref[...], g_ref[...])
    _store_slabs(hn_ref, hn)
    hi = hn.astype(BF)
    lo = (hn - hi.astype(F32)).astype(BF)
    logits = _dot(hi, whi_ref[...]) + (_dot(lo, whi_ref[...]) + _dot(hi, wlo_ref[...])) + br_ref[...]
    lane = lax.broadcasted_iota(jnp.int32, logits.shape, 1)
    logits = jnp.where(lane < N_EXPERTS, logits, NEG)
    v1 = jnp.max(logits, axis=-1, keepdims=True)
    i1 = jnp.min(jnp.where(logits == v1, lane, LANES), axis=-1, keepdims=True)
    rest = jnp.where(lane == i1, NEG, logits)
    v2 = jnp.max(rest, axis=-1, keepdims=True)
    i2 = jnp.min(jnp.where(rest == v2, lane, LANES), axis=-1, keepdims=True)
    e2 = jnp.exp(v2 - v1)
    w1 = 1.0 / (1.0 + e2)
    w2 = e2 / (1.0 + e2)
    idx_ref[...] = jnp.where(lane == 0, i1, jnp.where(lane == 1, i2, 0))
    wts_ref[...] = jnp.where(lane == 0, w1, jnp.where(lane == 1, w2, 0.0))


def _router(h, g, whi, wlo, br, chunk, n_chunks, tm=1024):
    t = h.shape[0] // n_chunks
    first = chunk * (t // tm)
    return pl.pallas_call(
        _router_kernel,
        grid=(t // tm,),
        in_specs=[pl.BlockSpec((tm, D_MODEL), lambda i: (first + i, 0)),
                  _const_spec((1, D_MODEL)),
                  _const_spec((D_MODEL, LANES)),
                  _const_spec((D_MODEL, LANES)),
                  _const_spec((1, LANES))],
        out_specs=[_slab_spec(tm, lambda i: (i, 0)),
                   pl.BlockSpec((tm, LANES), lambda i: (i, 0)),
                   pl.BlockSpec((tm, LANES), lambda i: (i, 0))],
        out_shape=[jax.ShapeDtypeStruct((t * ROW_SLABS, LANES), F32),
                   jax.ShapeDtypeStruct((t, LANES), jnp.int32),
                   jax.ShapeDtypeStruct((t, LANES), F32)],
        compiler_params=_params("parallel"),
        name="router",
    )(h, g, whi, wlo, br)


def _route_plan(idx, tm):
    t = idx.shape[0]
    n_pairs = 2 * t
    n_rows = n_pairs + N_EXPERTS * tm
    e_flat = jnp.concatenate([idx[:, 0], idx[:, 1]])
    onehot = (e_flat[:, None] == jnp.arange(N_EXPERTS, dtype=jnp.int32)[None, :]).astype(jnp.int32)
    csum = jnp.cumsum(onehot, axis=0)
    rank = jnp.sum(onehot * csum, axis=1) - 1
    counts = csum[-1]
    padded = ((counts + tm - 1) // tm) * tm
    ends = jnp.cumsum(padded)
    starts = ends - padded
    pos = starts[e_flat] + rank
    order = jnp.argsort(e_flat, stable=True).astype(jnp.int32)
    first = jnp.cumsum(counts) - counts
    r = jnp.minimum(jnp.arange(n_rows, dtype=jnp.int32), ends[-1] - 1)
    e_r = jnp.minimum(jnp.searchsorted(ends, r, side="right").astype(jnp.int32), N_EXPERTS - 1)
    local = r - starts[e_r]
    src_pair = order[jnp.clip(first[e_r] + local, 0, n_pairs - 1)]
    src_tok = jnp.where(local < counts[e_r], src_pair % t, 0).astype(jnp.int32)
    tile_expert = e_r[::tm]
    n_valid = (ends[-1] // tm).astype(jnp.int32).reshape(1)
    return pos.astype(jnp.int32), src_tok, tile_expert, n_valid


def _gather_rows(table, idx):
    n = idx.shape[0]
    info = plsc.get_sparse_core_info()
    n_workers = info.num_cores * info.num_subcores
    per_worker = n // n_workers
    assert per_worker * n_workers == n and per_worker % SC_CHUNK == 0, (n, n_workers)
    mesh = plsc.VectorSubcoreMesh(core_axis_name="c", subcore_axis_name="s")

    @functools.partial(
        pl.kernel, mesh=mesh,
        out_type=jax.ShapeDtypeStruct((n,) + table.shape[1:], table.dtype),
        scratch_types=[pltpu.VMEM((SC_CHUNK,), jnp.int32),
                       pltpu.VMEM((SC_CHUNK,) + table.shape[1:], table.dtype),
                       pltpu.SemaphoreType.DMA],
    )
    def gather(table_hbm, idx_hbm, out_hbm, idx_v, rows_v, sem):
        worker = lax.axis_index("s") * info.num_cores + lax.axis_index("c")
        base = worker * per_worker

        @pl.loop(0, per_worker // SC_CHUNK)
        def _(i):
            off = pl.multiple_of(base + i * SC_CHUNK, SC_CHUNK)
            pltpu.sync_copy(idx_hbm.at[pl.ds(off, SC_CHUNK)], idx_v)
            pltpu.async_copy(table_hbm.at[idx_v], rows_v, sem).wait()
            pltpu.sync_copy(rows_v, out_hbm.at[pl.ds(off, SC_CHUNK)])

    return gather(table, idx)


def _expert_ffn_kernel(te_ref, nv_ref, x_ref, w13_ref, w2_ref, y_ref):
    valid = pl.program_id(0) < nv_ref[0]

    @pl.when(valid)
    def _():
        _store_slabs(y_ref, _swiglu(_split_slabs(x_ref).astype(BF), w13_ref.at[0], w2_ref.at[0]))

    @pl.when(jnp.logical_not(valid))
    def _():
        y_ref[...] = jnp.zeros_like(y_ref)


def _expert_ffn(xs, tile_expert, n_valid, w13, w2, tm):
    n_rows = xs.shape[0] // ROW_SLABS
    last = lambda nv: jnp.maximum(nv[0] - 1, 0)
    grid_spec = pltpu.PrefetchScalarGridSpec(
        num_scalar_prefetch=2,
        grid=(n_rows // tm,),
        in_specs=[_slab_spec(tm, lambda i, te, nv: (jnp.minimum(i, last(nv)), 0)),
                  pl.BlockSpec((1, D_MODEL, 2 * D_FF), lambda i, te, nv: (te[i], 0, 0), pipeline_mode=pl.Buffered(1)),
                  pl.BlockSpec((1, D_FF, D_MODEL), lambda i, te, nv: (te[i], 0, 0), pipeline_mode=pl.Buffered(1))],
        out_specs=_slab_spec(tm, lambda i, te, nv: (i, 0)),
    )
    return pl.pallas_call(
        _expert_ffn_kernel,
        grid_spec=grid_spec,
        out_shape=jax.ShapeDtypeStruct((n_rows * ROW_SLABS, LANES), F32),
        compiler_params=_params("arbitrary"),
        name="expert_ffn",
    )(tile_expert, n_valid, xs, w13, w2)


def _combine_kernel(h_ref, y0_ref, y1_ref, wts_ref, gf_ref, *rest):
    out_ref = rest[-1]
    wts = wts_ref[...]
    tot = h_ref[...] + wts[:, 0:1] * _split_slabs(y0_ref) + wts[:, 1:2] * _split_slabs(y1_ref)
    out_ref[...] = _rms(tot, gf_ref[...])


def _combine(h, yg, wts, g_final, chunk, n_chunks, out_so_far, tm=1024):
    t = h.shape[0]
    nt = t // n_chunks // tm
    first = chunk * nt
    in_specs = [pl.BlockSpec((tm, D_MODEL), lambda i: (first + i, 0)),
                _slab_spec(tm, lambda i: (i, 0)),
                _slab_spec(tm, lambda i: (nt + i, 0)),
                pl.BlockSpec((tm, LANES), lambda i: (i, 0)),
                _const_spec((1, D_MODEL))]
    args = [h, yg, yg, wts, g_final]
    aliases = {}
    if out_so_far is not None:
        in_specs.append(pl.BlockSpec(memory_space=pl.ANY))
        args.append(out_so_far)
        aliases = {len(args) - 1: 0}
    return pl.pallas_call(
        _combine_kernel,
        grid=(nt,),
        in_specs=in_specs,
        out_specs=pl.BlockSpec((tm, D_MODEL), lambda i: (first + i, 0)),
        out_shape=jax.ShapeDtypeStruct((t, D_MODEL), F32),
        input_output_aliases=aliases,
        compiler_params=_params("parallel"),
        name="combine",
    )(*args)


def _moe(h, g, router_w, router_b, w13, w2, g_final, tm=512):
    wr = jnp.pad(router_w, ((0, 0), (0, LANES - N_EXPERTS)))
    whi = wr.astype(BF)
    wlo = (wr - whi.astype(F32)).astype(BF)
    br = jnp.pad(router_b, (0, LANES - N_EXPERTS)).reshape(1, LANES)
    as_rows = lambda a: a.reshape(-1, ROW_SLABS, LANES)
    as_slabs = lambda a: a.reshape(-1, LANES)
    chunks = range(MOE_CHUNKS)
    routed = [_router(h, g, whi, wlo, br, c, MOE_CHUNKS) for c in chunks]
    plans = [_route_plan(idx, tm) for _, idx, _ in routed]
    xs = [as_slabs(_gather_rows(as_rows(routed[c][0]), plans[c][1])) for c in chunks]
    ys = [_expert_ffn(xs[c], plans[c][2], plans[c][3], w13, w2, tm) for c in chunks]
    yg = [as_slabs(_gather_rows(as_rows(ys[c]), plans[c][0])) for c in chunks]
    out = None
    for c in chunks:
        out = _combine(h, yg[c], routed[c][2], g_final, c, MOE_CHUNKS, out)
    return out


def _feature_major(w, heads):
    wt = w.T.reshape(heads, HEAD_DIM, D_MODEL)
    return jnp.pad(wt, ((0, 0), (0, VT_ROWS - HEAD_DIM), (0, 0))).reshape(heads * VT_ROWS, D_MODEL)


def _pack_w_in(w_in):
    cuts = np.cumsum((512, 512, 512, 512, 512, 128, 128, 512, 512, 512, 8))[:-1].tolist()
    a_u, a_v, b_x, b_y, c_q, c_k, c_v, d_q, d_k, d_v, d_f = jnp.split(w_in, cuts, axis=-1)
    w = jnp.concatenate([b_x, b_y, d_q, d_k, c_k, c_v], axis=-1).astype(BF)
    wa = jnp.concatenate([a_u, a_v], axis=-1).astype(BF)
    wf = jnp.pad(d_f, ((0, 0), (0, LANES - FOX_HEADS))).astype(BF)
    wt = jnp.concatenate([_feature_major(d_v, FOX_HEADS), c_q.T, _feature_major(c_v, SWA_KV)], axis=0).astype(BF)
    return w, wa, wf, wt


def _fm_ones():
    ones = np.zeros((FM_ROWS, LANES), np.float32)
    for base, heads in ((0, FOX_HEADS), (VT_ALL + BRANCH_W, SWA_KV)):
        for h in range(heads):
            ones[base + h * VT_ROWS + HEAD_DIM, :] = 1.0
    return jnp.asarray(ones)


def _row(v):
    return v.reshape(1, -1)


def _hybrid_mixer(h, bsz, norm_mix, w_in, sgu_g, sgu_w, sgu_b, conv_w, conv_b, rg_wa, rg_ba, rg_wx, rg_bx,
                  rg_lambda, swa_sinks, fox_bf, w_branch, w_gate, b_gate, w_out):
    w, wa, wf, wt = _pack_w_in(w_in)
    proj, f, o_a, vt, sqt, svt = _inproj(h, _row(norm_mix), w, wa, wf, wt, _fm_ones(), _row(sgu_g), sgu_w, sgu_b.T)
    wax = jnp.concatenate([rg_wa, rg_wx], axis=-1).astype(BF)
    o_b = _rglru(proj, bsz, conv_w, _row(conv_b), wax, _row(rg_ba), _row(rg_bx), _row(rg_lambda))
    o_c = _swa(proj, sqt, svt, bsz, swa_sinks)
    bf_pad = jnp.pad(fox_bf, (0, LANES - FOX_HEADS)).reshape(1, LANES)
    qp, kp = _fox_prep(proj, f, bsz, bf_pad)
    o_d = _fox(qp, kp, vt, bsz)
    return _merge(h, _row(norm_mix), o_a, o_b, o_c, o_d, w_gate.astype(BF), b_gate[:, None, :],
                  w_branch.astype(BF), w_out.astype(BF))


def kernel(x, mem, norm_mix, w_in, sgu_g, sgu_w, sgu_b, conv_w, conv_b, rg_wa, rg_ba, rg_wx, rg_bx, rg_lambda, swa_sinks, fox_bf, w_branch, w_gate, b_gate, w_out, norm_cross, norm_mem, wq_c, wkv_c, wo_c, norm_ffn, dense_w13, dense_w2, router_w, router_b, moe_w13, moe_w2, norm_final):
    bsz, seq, d = x.shape
    m_len = mem.shape[1]
    depth = norm_mix.shape[0]
    assert depth == 2, "the final RMSNorm is fused into the routed layer, which must be the last one"
    h = x.reshape(bsz * seq, d)
    mem2 = mem.reshape(bsz * m_len, d)
    for l in range(depth):
        h = _hybrid_mixer(h, bsz, norm_mix[l], w_in[l], sgu_g[l], sgu_w[l], sgu_b[l], conv_w[l], conv_b[l],
                          rg_wa[l], rg_ba[l], rg_wx[l], rg_bx[l], rg_lambda[l], swa_sinks[l], fox_bf[l],
                          w_branch[l], w_gate[l], b_gate[l], w_out[l])
        kv = _memkv(mem2, _row(norm_mem[l]), wkv_c[l].astype(BF), m_len)
        h = _cross(h, _row(norm_cross[l]), wq_c[l].astype(BF), kv, wo_c[l].astype(BF), bsz, m_len)
        if l % 2 == 0:
            h = _ffn(h, _row(norm_ffn[l]), dense_w13[l // 2].astype(BF), dense_w2[l // 2].astype(BF))
        else:
            h = _moe(h, _row(norm_ffn[l]), router_w[l // 2], router_b[l // 2], moe_w13[l // 2].astype(BF),
                     moe_w2[l // 2].astype(BF), _row(norm_final))
    return h.reshape(bsz, seq, d)
```

```python
import functools
import math

import numpy as np
import jax
import jax.numpy as jnp
from jax import lax
from jax.experimental import pallas as pl
from jax.experimental.pallas import tpu as pltpu
from jax.experimental.pallas import tpu_sc as plsc

F32 = jnp.float32
BF = jnp.bfloat16

D_MODEL = 1024
BRANCH_W = 512
HEAD_DIM = 64
CHUNK = 128
SGU_GROUPS = 4
LRU_HEADS = 4
LRU_C = 8.0
CONV_W = 4
SWA_HEADS = 8
SWA_KV = 2
FOX_HEADS = 8
FOX_PAIRS = FOX_HEADS // 2
X_HEADS = 4
X_HEAD_DIM = 128
D_FF = 2816
N_EXPERTS = 8
EPS = 1e-6
LOG2E = math.log2(math.e)
NEG = -1e30
LANES = 128
ROW_SLABS = D_MODEL // LANES
SC_CHUNK = 64
MOE_CHUNKS = 2
VMEM_LIMIT = 56 * 1024 * 1024

COL_BX, COL_BY, COL_DQ, COL_DK = range(4)
N_WIDE = 4
N_PROJ = N_WIDE * BRANCH_W + 2 * SWA_KV * HEAD_DIM
VT_ROWS = 80
VT_ALL = FOX_HEADS * VT_ROWS
SWA_VT = SWA_KV * VT_ROWS
FM_ROWS = VT_ALL + BRANCH_W + SWA_VT
FOX_UNROLL = 4
FOX_STEP_PAIRS = 1
FF_SPLITS = ((0, 1536), (1536, D_FF))


def _rms(x, g):
    return x * lax.rsqrt(jnp.mean(x * x, axis=-1, keepdims=True) + EPS) * g


def _sigmoid(x):
    return 1.0 / (1.0 + jnp.exp(-x))


def _gelu(x):
    return 0.5 * x * (1.0 + jnp.tanh(math.sqrt(2.0 / math.pi) * (x + 0.044715 * (x * x * x))))


def _dot(a, b):
    return jnp.dot(a, b, preferred_element_type=F32)


def _dot_nt(a, b):
    return lax.dot_general(a, b, (((1,), (1,)), ((), ())), preferred_element_type=F32)


def _shift_rows(x, d, fill):
    row = lax.broadcasted_iota(jnp.int32, x.shape, 0)
    return jnp.where(row >= d, pltpu.roll(x, d, 0), fill)


def _params(*sem):
    return pltpu.CompilerParams(dimension_semantics=sem, vmem_limit_bytes=VMEM_LIMIT)


def _const_spec(shape):
    nd = len(shape)
    return pl.BlockSpec(shape, lambda *_: (0,) * nd, pipeline_mode=pl.Buffered(1))


def _inproj_kernel(h_ref, g_ref, w_ref, wa_ref, wf_ref, wt_ref, ones_ref, sg_ref, sw_ref, sbt_ref,
                   proj_ref, f_ref, oa_ref, vt_ref, sqt_ref, svt_ref):
    xn = _rms(h_ref[...], g_ref[...]).astype(BF)
    tm = xn.shape[0]
    n_col = w_ref.shape[1]
    for c in range(0, n_col, BRANCH_W):
        w = min(BRANCH_W, n_col - c)
        proj_ref[:, c:c + w] = _dot(xn, w_ref[:, c:c + w]).astype(BF)
    f_ref[...] = _dot(xn, wf_ref[...])
    ones = jnp.concatenate([ones_ref[...]] * (tm // LANES), axis=1)
    fm = (_dot_nt(wt_ref[...], xn) + ones).astype(BF)
    vt_ref[...] = fm[:VT_ALL]
    sqt_ref[...] = fm[VT_ALL:VT_ALL + BRANCH_W]
    svt_ref[...] = fm[VT_ALL + BRANCH_W:]

    u = _gelu(_dot(xn, wa_ref[:, :BRANCH_W]))
    v = _gelu(_dot(xn, wa_ref[:, BRANCH_W:]))
    vn = _rms(v, sg_ref[...]).astype(BF)
    row = lax.broadcasted_iota(jnp.int32, (CHUNK, CHUNK), 0)
    col = lax.broadcasted_iota(jnp.int32, (CHUNK, CHUNK), 1)
    gw = BRANCH_W // SGU_GROUPS
    for g in range(SGU_GROUPS):
        wg = jnp.where(col <= row, sw_ref[g], 0.0).astype(BF)
        bg = sbt_ref[:, g:g + 1]
        for c in range(tm // CHUNK):
            rs = slice(c * CHUNK, (c + 1) * CHUNK)
            cs = slice(g * gw, (g + 1) * gw)
            mixed = _dot(wg, vn[rs, cs]) + bg
            oa_ref[rs, cs] = (u[rs, cs] * mixed).astype(BF)


def _inproj(h, g, w, wa, wf, wt, ones, sgu_g, sgu_w, sgu_bt, tm=512):
    t = h.shape[0]
    row = lambda i: (i, 0)
    col = lambda i: (0, i)
    return pl.pallas_call(
        _inproj_kernel,
        grid=(t // tm,),
        in_specs=[pl.BlockSpec((tm, D_MODEL), row),
                  _const_spec((1, D_MODEL)),
                  _const_spec((D_MODEL, N_PROJ)),
                  _const_spec((D_MODEL, 2 * BRANCH_W)),
                  _const_spec((D_MODEL, LANES)),
                  _const_spec((FM_ROWS, D_MODEL)),
                  _const_spec((FM_ROWS, LANES)),
                  _const_spec((1, BRANCH_W)),
                  _const_spec((SGU_GROUPS, CHUNK, CHUNK)),
                  _const_spec((CHUNK, SGU_GROUPS))],
        out_specs=[pl.BlockSpec((tm, N_PROJ), row),
                   pl.BlockSpec((tm, LANES), row),
                   pl.BlockSpec((tm, BRANCH_W), row),
                   pl.BlockSpec((VT_ALL, tm), col),
                   pl.BlockSpec((BRANCH_W, tm), col),
                   pl.BlockSpec((SWA_VT, tm), col)],
        out_shape=[jax.ShapeDtypeStruct((t, N_PROJ), BF), jax.ShapeDtypeStruct((t, LANES), F32),
                   jax.ShapeDtypeStruct((t, BRANCH_W), BF), jax.ShapeDtypeStruct((VT_ALL, t), BF),
                   jax.ShapeDtypeStruct((BRANCH_W, t), BF), jax.ShapeDtypeStruct((SWA_VT, t), BF)],
        compiler_params=_params("parallel"),
        name="inproj",
    )(h, g, w, wa, wf, wt, ones, sgu_g, sgu_w, sgu_bt)


def _rglru_kernel(x_ref, y_ref, cw_ref, cb_ref, wax_ref, ba_ref, bx_ref, lam_ref, o_ref, xs_ref, hc_ref):
    ts = x_ref.shape[0]
    hw = BRANCH_W // LRU_HEADS

    @pl.when(pl.program_id(1) == 0)
    def _():
        xs_ref[0:8, :] = jnp.zeros((8, BRANCH_W), F32)
        hc_ref[...] = jnp.zeros_like(hc_ref)

    x = x_ref[...].astype(F32)
    xs_ref[8:8 + ts, :] = x
    cw = cw_ref[...]
    xc = cb_ref[...] + cw[CONV_W - 1:CONV_W] * x
    for k in range(1, CONV_W):
        xc = xc + cw[CONV_W - 1 - k:CONV_W - k] * xs_ref[8 - k:8 - k + ts, :]
    xs_ref[0:8, :] = x[ts - 8:ts, :]

    xcb = xc.astype(BF)
    r_parts, i_parts = [], []
    for hd in range(LRU_HEADS):
        z = _dot(xcb[:, hd * hw:(hd + 1) * hw], wax_ref[hd])
        r_parts.append(z[:, :hw])
        i_parts.append(z[:, hw:])
    r = _sigmoid(jnp.concatenate(r_parts, axis=1) + ba_ref[...])
    gi = _sigmoid(jnp.concatenate(i_parts, axis=1) + bx_ref[...])
    nl = -lam_ref[...]
    softplus = jnp.maximum(nl, 0.0) + jnp.log1p(jnp.exp(-jnp.abs(nl)))
    log_a = (-LRU_C * r) * softplus
    a = jnp.exp(log_a)
    th = jnp.tanh(log_a)
    e2 = -2.0 * th / (1.0 - th)
    b = (xc * gi) * (e2 * lax.rsqrt(jnp.maximum(e2, 1e-30)))

    sub = lax.broadcasted_iota(jnp.int32, a.shape, 0) & 7
    for d in (1, 2, 4):
        keep = sub >= d
        a_sh = jnp.where(keep, pltpu.roll(a, d, 0), 1.0)
        b_sh = jnp.where(keep, pltpu.roll(b, d, 0), 0.0)
        b = b + a * b_sh
        a = a * a_sh
    carry = hc_ref[...]
    groups = []
    for g in range(ts // 8):
        hg = a[8 * g:8 * g + 8] * carry + b[8 * g:8 * g + 8]
        groups.append(hg)
        carry = hg[7:8]
    h = jnp.concatenate(groups, axis=0)
    hc_ref[...] = carry
    o_ref[...] = (h * _gelu(y_ref[...].astype(F32))).astype(BF)


def _rglru(proj, bsz, cw, cb, wax, ba, bx, lam, ts=256):
    t = proj.shape[0]
    ns = t // bsz // ts
    return pl.pallas_call(
        _rglru_kernel,
        grid=(bsz, ns),
        in_specs=[pl.BlockSpec((ts, BRANCH_W), lambda b, j: (b * ns + j, COL_BX)),
                  pl.BlockSpec((ts, BRANCH_W), lambda b, j: (b * ns + j, COL_BY)),
                  _const_spec((CONV_W, BRANCH_W)),
                  _const_spec((1, BRANCH_W)),
                  _const_spec((LRU_HEADS, BRANCH_W // LRU_HEADS, 2 * BRANCH_W // LRU_HEADS)),
                  _const_spec((1, BRANCH_W)),
                  _const_spec((1, BRANCH_W)),
                  _const_spec((1, BRANCH_W))],
        out_specs=pl.BlockSpec((ts, BRANCH_W), lambda b, j: (b * ns + j, 0)),
        out_shape=jax.ShapeDtypeStruct((t, BRANCH_W), BF),
        scratch_shapes=[pltpu.VMEM((ts + 8, BRANCH_W), F32), pltpu.VMEM((1, BRANCH_W), F32)],
        compiler_params=_params("parallel", "arbitrary"),
        name="rglru",
    )(proj, proj, cw, cb, wax, ba, bx, lam)


def _swa_tables(sinks):
    t_idx = np.arange(CHUNK)[None, :] + CHUNK
    dist = (t_idx - np.arange(2 * CHUNK)[:, None]).astype(np.float32)
    in_win = (dist >= 0) & (dist < CHUNK)
    slopes = 2.0 ** (-(8.0 / SWA_HEADS) * np.arange(1, SWA_HEADS + 1, dtype=np.float32))
    bias = np.where(in_win[None], -slopes[:, None, None] * dist[None] * LOG2E, NEG)
    grp = SWA_HEADS // SWA_KV
    bias = bias.reshape(SWA_KV, grp, 2 * CHUNK, CHUNK).transpose(0, 2, 1, 3).reshape(SWA_KV, 2 * CHUNK, grp * CHUNK)
    sink = jnp.repeat(sinks.astype(F32) * LOG2E, CHUNK).reshape(SWA_KV, 1, grp * CHUNK)
    return jnp.asarray(bias, F32), sink


def _swa_kernel(qt_ref, kv_ref, kvp_ref, vt_ref, vtp_ref, bias_ref, sink_ref, o_ref):
    tq = kv_ref.shape[0]
    grp = SWA_HEADS // SWA_KV
    first_key = jnp.where(pl.program_id(1) == 0, CHUNK, 0)
    k_all = jnp.concatenate([kvp_ref[...], kv_ref[...]], axis=0)
    vt_all = jnp.concatenate([vtp_ref[...], vt_ref[...]], axis=1)
    qt = (qt_ref[...].astype(F32) * (HEAD_DIM ** -0.5 * LOG2E)).astype(BF)
    key_row = lax.broadcasted_iota(jnp.int32, (2 * CHUNK, grp * CHUNK), 0)
    kw = 2 * SWA_KV * HEAD_DIM
    for qb in range(tq // CHUNK):
        cols = slice(qb * CHUNK, (qb + 1) * CHUNK)
        kb = k_all[qb * CHUNK:(qb + 2) * CHUNK]
        vtb = vt_all[:, qb * CHUNK:(qb + 2) * CHUNK]
        outs = []
        for kh in range(SWA_KV):
            qg = jnp.concatenate([qt[(kh * grp + g) * HEAD_DIM:(kh * grp + g + 1) * HEAD_DIM, cols]
                                  for g in range(grp)], axis=1)
            parts = []
            if kh > 0:
                parts.append(jnp.zeros((kh * HEAD_DIM, grp * CHUNK), BF))
            parts.append(qg)
            parts.append(jnp.zeros((kw - (kh + 1) * HEAD_DIM, grp * CHUNK), BF))
            st = _dot(kb, jnp.concatenate(parts, axis=0)) + bias_ref[kh]
            if qb == 0:
                st = jnp.where(key_row >= first_key, st, NEG)
            sink = sink_ref[kh]
            m = jnp.maximum(jnp.max(st, axis=0, keepdims=True), sink)
            pt = jnp.exp2(st - m).astype(BF)
            acc = _dot(vtb[kh * VT_ROWS:(kh + 1) * VT_ROWS], pt)
            ot = acc[:HEAD_DIM] / (acc[HEAD_DIM:HEAD_DIM + 1] + jnp.exp2(sink - m))
            outs.extend(ot[:, g * CHUNK:(g + 1) * CHUNK] for g in range(grp))
        o_ref[cols, :] = jnp.concatenate(outs, axis=0).T.astype(BF)


def _swa(proj, sqt, svt, bsz, sinks, tq=1024):
    t = proj.shape[0]
    seq = t // bsz
    nq = seq // tq
    per = tq // CHUNK
    kv_blk = N_WIDE * BRANCH_W // (2 * SWA_KV * HEAD_DIM)
    prev = lambda b, i: b * (seq // CHUNK) + jnp.maximum(i * per - 1, 0)
    bias, sink = _swa_tables(sinks)
    grp = SWA_HEADS // SWA_KV
    return pl.pallas_call(
        _swa_kernel,
        grid=(bsz, nq),
        in_specs=[pl.BlockSpec((BRANCH_W, tq), lambda b, i: (0, b * nq + i)),
                  pl.BlockSpec((tq, 2 * SWA_KV * HEAD_DIM), lambda b, i: (b * nq + i, kv_blk)),
                  pl.BlockSpec((CHUNK, 2 * SWA_KV * HEAD_DIM), lambda b, i: (prev(b, i), kv_blk)),
                  pl.BlockSpec((SWA_VT, tq), lambda b, i: (0, b * nq + i)),
                  pl.BlockSpec((SWA_VT, CHUNK), lambda b, i: (0, prev(b, i))),
                  _const_spec((SWA_KV, 2 * CHUNK, grp * CHUNK)),
                  _const_spec((SWA_KV, 1, grp * CHUNK))],
        out_specs=pl.BlockSpec((tq, BRANCH_W), lambda b, i: (b * nq + i, 0)),
        out_shape=jax.ShapeDtypeStruct((t, BRANCH_W), BF),
        compiler_params=_params("parallel", "parallel"),
        name="swa",
    )(sqt, proj, proj, svt, svt, bias, sink)


def _aug_tables():
    eq = np.zeros((3, LANES, FOX_PAIRS * LANES), np.float32)
    ek = np.zeros((3, LANES, FOX_PAIRS * LANES), np.float32)
    oq = np.zeros((1, FOX_PAIRS * LANES), np.float32)
    ok = np.zeros((1, FOX_PAIRS * LANES), np.float32)
    for h in range(FOX_HEADS):
        base = (h // 2) * LANES + 6 * (h % 2)
        for s in range(3):
            eq[s, h, base + s] = 1.0
            ek[s, h, base + 3 + s] = -1.0
            ok[0, base + s] = 1.0
            oq[0, base + 3 + s] = 1.0
    return eq, ek, oq, ok


def _fox_prep_kernel(q_ref, k_ref, f_ref, bf_ref, eq_ref, ek_ref, oq_ref, ok_ref, qp_ref, kp_ref, cum_ref):
    ts = q_ref.shape[0]

    @pl.when(pl.program_id(1) == 0)
    def _():
        cum_ref[...] = jnp.zeros_like(cum_ref)

    z = f_ref[...] + bf_ref[...]
    c = jnp.minimum(z, 0.0) - jnp.log1p(jnp.exp(-jnp.abs(z)))
    d = 1
    while d < ts:
        c = c + _shift_rows(c, d, 0.0)
        d *= 2
    c = c + cum_ref[...]
    cum_ref[...] = c[ts - 1:ts, :]
    c = c * LOG2E
    c1 = c.astype(BF)
    r1 = c - c1.astype(F32)
    c2 = r1.astype(BF)
    c3 = (r1 - c2.astype(F32)).astype(BF)
    augq = _dot(c1, eq_ref[0]) + _dot(c2, eq_ref[1]) + _dot(c3, eq_ref[2]) + oq_ref[...]
    augk = _dot(c1, ek_ref[0]) + _dot(c2, ek_ref[1]) + _dot(c3, ek_ref[2]) + ok_ref[...]
    qt = (q_ref[...].astype(F32) * (HEAD_DIM ** -0.5 * LOG2E)).T.astype(BF)
    augqt = augq.T.astype(BF)
    k = k_ref[...]
    for p in range(FOX_PAIRS):
        qp_ref[2 * p * LANES:(2 * p + 1) * LANES, :] = qt[p * LANES:(p + 1) * LANES]
        qp_ref[(2 * p + 1) * LANES:(2 * p + 2) * LANES, :] = augqt[p * LANES:(p + 1) * LANES]
        kp_ref[:, 2 * p * LANES:(2 * p + 1) * LANES] = k[:, p * LANES:(p + 1) * LANES]
        kp_ref[:, (2 * p + 1) * LANES:(2 * p + 2) * LANES] = augk[:, p * LANES:(p + 1) * LANES].astype(BF)


def _fox_prep(proj, f, bsz, bf_pad, ts=1024):
    t = proj.shape[0]
    ns = t // bsz // ts
    eq, ek, oq, ok = _aug_tables()
    wide = 2 * FOX_PAIRS * LANES
    return pl.pallas_call(
        _fox_prep_kernel,
        grid=(bsz, ns),
        in_specs=[pl.BlockSpec((ts, BRANCH_W), lambda b, j: (b * ns + j, COL_DQ)),
                  pl.BlockSpec((ts, BRANCH_W), lambda b, j: (b * ns + j, COL_DK)),
                  pl.BlockSpec((ts, LANES), lambda b, j: (b * ns + j, 0)),
                  _const_spec((1, LANES)),
                  _const_spec(eq.shape), _const_spec(ek.shape), _const_spec(oq.shape), _const_spec(ok.shape)],
        out_specs=[pl.BlockSpec((wide, ts), lambda b, j: (0, b * ns + j)),
                   pl.BlockSpec((ts, wide), lambda b, j: (b * ns + j, 0))],
        out_shape=[jax.ShapeDtypeStruct((wide, t), BF), jax.ShapeDtypeStruct((t, wide), BF)],
        scratch_shapes=[pltpu.VMEM((1, LANES), F32)],
        compiler_params=_params("parallel", "arbitrary"),
        name="fox_prep",
    )(proj, proj, f, bf_pad, jnp.asarray(eq, BF), jnp.asarray(ek, BF), jnp.asarray(oq), jnp.asarray(ok))


def _fox_kernel(q_ref, k_ref, vt_ref, o_ref, st_ref, pt_ref, al_ref, m_ref, acc_ref):
    tq = q_ref.shape[1]
    tk = tq // 2
    n_heads = m_ref.shape[0]
    qi = pl.program_id(2)
    row = lax.broadcasted_iota(jnp.int32, (2 * LANES, tq), 0)
    in0 = (row < HEAD_DIM) | ((row >= LANES) & (row < LANES + 6))
    in1 = ((row >= HEAD_DIM) & (row < LANES)) | ((row >= LANES + 6) & (row < LANES + 12))
    qh = []
    for p in range(n_heads // 2):
        q = q_ref[2 * p * LANES:(2 * p + 2) * LANES, :]
        zero = jnp.zeros_like(q)
        qh += [jnp.where(in0, q, zero), jnp.where(in1, q, zero)]
    for h in range(n_heads):
        m_ref[h] = jnp.full(m_ref.shape[1:], NEG, F32)
        acc_ref[h] = jnp.zeros(acc_ref.shape[1:], F32)
        pt_ref[1, h] = jnp.zeros(pt_ref.shape[2:], BF)
        al_ref[1, h] = jnp.ones(al_ref.shape[2:], F32)

    def scores(t, slot):
        rows = pl.ds(pl.multiple_of(t * tk, tk), tk)
        for p in range(n_heads // 2):
            kp = k_ref[rows, 2 * p * LANES:(2 * p + 2) * LANES]
            for h in (2 * p, 2 * p + 1):
                st_ref[slot, h] = _dot(kp, qh[h])

    def numerators(slot, key_offset):
        for h in range(n_heads):
            st = st_ref[slot, h]
            if key_offset is not None:
                key = lax.broadcasted_iota(jnp.int32, (tk, tq), 0) + key_offset
                qry = lax.broadcasted_iota(jnp.int32, (tk, tq), 1)
                st = jnp.where(key <= qry, st, NEG)
            m_old = m_ref[h]
            m_new = jnp.maximum(m_old, jnp.max(st, axis=0, keepdims=True))
            al_ref[slot, h] = jnp.exp2(m_old - m_new)
            pt_ref[slot, h] = jnp.exp2(st - m_new).astype(BF)
            m_ref[h] = m_new

    def accumulate(t, slot):
        cols = pl.ds(pl.multiple_of(t * tk, tk), tk)
        for h in range(n_heads):
            vt = vt_ref[h * VT_ROWS:(h + 1) * VT_ROWS, cols]
            acc_ref[h] = al_ref[slot, h] * acc_ref[h] + _dot(vt, pt_ref[slot, h])

    def tile_pair(j, diagonal):
        scores(2 * j + 1, 1)
        accumulate(jnp.maximum(2 * j - 1, 0), 1)
        numerators(0, 0 if diagonal else None)
        if not diagonal:
            scores(2 * j + 2, 0)
        accumulate(2 * j, 0)
        numerators(1, tk if diagonal else None)

    def body(jj, carry):
        for u in range(FOX_UNROLL):
            tile_pair(FOX_UNROLL * jj + u, False)
        return carry

    def remainder(j, carry):
        tile_pair(j, False)
        return carry

    scores(0, 0)
    lax.fori_loop(0, qi // FOX_UNROLL, body, 0)
    lax.fori_loop((qi // FOX_UNROLL) * FOX_UNROLL, qi, remainder, 0)
    tile_pair(qi, True)
    accumulate(2 * qi + 1, 1)
    outs = []
    for h in range(n_heads):
        acc = acc_ref[h]
        outs.append(acc[:HEAD_DIM] / acc[HEAD_DIM:HEAD_DIM + 1])
    o_ref[...] = jnp.concatenate(outs, axis=0).T.astype(BF)


def _fox(qp, kp, vt, bsz, tq=512):
    t = kp.shape[0]
    seq = t // bsz
    nq = seq // tq
    np_ = FOX_STEP_PAIRS
    nh = 2 * np_
    return pl.pallas_call(
        _fox_kernel,
        grid=(bsz, FOX_PAIRS // np_, nq),
        in_specs=[pl.BlockSpec((np_ * 2 * LANES, tq), lambda b, p, i: (p, b * nq + i)),
                  pl.BlockSpec((seq, np_ * 2 * LANES), lambda b, p, i: (b, p)),
                  pl.BlockSpec((nh * VT_ROWS, seq), lambda b, p, i: (p, b))],
        out_specs=pl.BlockSpec((tq, np_ * LANES), lambda b, p, i: (b * nq + i, p)),
        out_shape=jax.ShapeDtypeStruct((t, BRANCH_W), BF),
        scratch_shapes=[pltpu.VMEM((2, nh, tq // 2, tq), F32),
                        pltpu.VMEM((2, nh, tq // 2, tq), BF),
                        pltpu.VMEM((2, nh, 1, tq), F32),
                        pltpu.VMEM((nh, 1, tq), F32),
                        pltpu.VMEM((nh, VT_ROWS, tq), F32)],
        compiler_params=_params("parallel", "parallel", "arbitrary"),
        name="fox",
    )(qp, kp, vt)


def _merge_kernel(h_ref, g_ref, oa_ref, ob_ref, oc_ref, od_ref, wg_ref, bg_ref, wb_ref, wo_ref, out_ref):
    h = h_ref[...]
    xn = _rms(h, g_ref[...]).astype(BF)
    merged = None
    for br, o_ref in enumerate((oa_ref, ob_ref, oc_ref, od_ref)):
        gate = _sigmoid(_dot(xn, wg_ref[br]) + bg_ref[br])
        term = gate * _dot(o_ref[...], wb_ref[br])
        merged = term if merged is None else merged + term
    out_ref[...] = h + _dot(merged.astype(BF), wo_ref[...])


def _merge(h, g, oa, ob, oc, od, wg, bg, wb, wo, tm=512):
    t = h.shape[0]
    row = lambda i: (i, 0)
    return pl.pallas_call(
        _merge_kernel,
        grid=(t // tm,),
        in_specs=[pl.BlockSpec((tm, D_MODEL), row),
                  _const_spec((1, D_MODEL)),
                  pl.BlockSpec((tm, BRANCH_W), row), pl.BlockSpec((tm, BRANCH_W), row),
                  pl.BlockSpec((tm, BRANCH_W), row), pl.BlockSpec((tm, BRANCH_W), row),
                  _const_spec((4, D_MODEL, D_MODEL)),
                  _const_spec((4, 1, D_MODEL)),
                  _const_spec((4, BRANCH_W, D_MODEL)),
                  _const_spec((D_MODEL, D_MODEL))],
        out_specs=pl.BlockSpec((tm, D_MODEL), row),
        out_shape=jax.ShapeDtypeStruct((t, D_MODEL), F32),
        compiler_params=_params("parallel"),
        name="merge",
    )(h, g, oa, ob, oc, od, wg, bg, wb, wo)


def _memkv_kernel(mem_ref, g_ref, w_ref, kv_ref):
    mn = _rms(mem_ref[...], g_ref[...]).astype(BF)
    kv_ref[...] = _dot(mn, w_ref[...]).astype(BF)


def _memkv(mem2, g, w, m_len):
    n = mem2.shape[0]
    width = 2 * X_HEADS * X_HEAD_DIM
    return pl.pallas_call(
        _memkv_kernel,
        grid=(n // m_len,),
        in_specs=[pl.BlockSpec((m_len, D_MODEL), lambda b: (b, 0)),
                  _const_spec((1, D_MODEL)),
                  _const_spec((D_MODEL, width))],
        out_specs=pl.BlockSpec((m_len, width), lambda b: (b, 0)),
        out_shape=jax.ShapeDtypeStruct((n, width), BF),
        compiler_params=_params("parallel"),
        name="memkv",
    )(mem2, g, w)


def _cross_kernel(h_ref, g_ref, wq_ref, kv_ref, wo_ref, out_ref):
    h = h_ref[...]
    hn = _rms(h, g_ref[...]).astype(BF)
    q = _dot(hn, wq_ref[...]).astype(BF)
    kv = kv_ref[...]
    width = X_HEADS * X_HEAD_DIM
    outs = []
    for hd in range(X_HEADS):
        cs = slice(hd * X_HEAD_DIM, (hd + 1) * X_HEAD_DIM)
        s = _dot_nt(q[:, cs], kv[:, cs]) * (X_HEAD_DIM ** -0.5)
        m = jnp.max(s, axis=-1, keepdims=True)
        p = jnp.exp(s - m)
        denom = jnp.sum(p, axis=-1, keepdims=True)
        v = kv[:, width + hd * X_HEAD_DIM:width + (hd + 1) * X_HEAD_DIM]
        outs.append((_dot(p.astype(BF), v) / denom).astype(BF))
    o = jnp.concatenate(outs, axis=1)
    out_ref[...] = h + _dot(o, wo_ref[...])


def _cross(h, g, wq, kv, wo, bsz, m_len, tm=1024):
    t = h.shape[0]
    per = t // bsz // tm
    width = X_HEADS * X_HEAD_DIM
    return pl.pallas_call(
        _cross_kernel,
        grid=(bsz, per),
        in_specs=[pl.BlockSpec((tm, D_MODEL), lambda b, i: (b * per + i, 0)),
                  _const_spec((1, D_MODEL)),
                  _const_spec((D_MODEL, width)),
                  pl.BlockSpec((m_len, 2 * width), lambda b, i: (b, 0)),
                  _const_spec((width, D_MODEL))],
        out_specs=pl.BlockSpec((tm, D_MODEL), lambda b, i: (b * per + i, 0)),
        out_shape=jax.ShapeDtypeStruct((t, D_MODEL), F32),
        compiler_params=_params("parallel", "parallel"),
        name="cross",
    )(h, g, wq, kv, wo)


def _swiglu(xb, w13_ref, w2_ref):
    out = None
    for lo, hi in FF_SPLITS:
        gate = _dot(xb, w13_ref[:, lo:hi])
        up = _dot(xb, w13_ref[:, D_FF + lo:D_FF + hi])
        act = (gate * _sigmoid(gate) * up).astype(BF)
        part = _dot(act, w2_ref[lo:hi, :])
        out = part if out is None else out + part
    return out


def _ffn_kernel(h_ref, g_ref, w13_ref, w2_ref, out_ref):
    h = h_ref[...]
    out_ref[...] = h + _swiglu(_rms(h, g_ref[...]).astype(BF), w13_ref, w2_ref)


def _ffn(h, g, w13, w2, tm=512):
    t = h.shape[0]
    return pl.pallas_call(
        _ffn_kernel,
        grid=(t // tm,),
        in_specs=[pl.BlockSpec((tm, D_MODEL), lambda i: (i, 0)),
                  _const_spec((1, D_MODEL)),
                  _const_spec((D_MODEL, 2 * D_FF)),
                  _const_spec((D_FF, D_MODEL))],
        out_specs=pl.BlockSpec((tm, D_MODEL), lambda i: (i, 0)),
        out_shape=jax.ShapeDtypeStruct((t, D_MODEL), F32),
        compiler_params=_params("parallel"),
        name="ffn",
    )(h, g, w13, w2)


def _split_slabs(ref):
    rows = ref.shape[0] // ROW_SLABS
    return jnp.concatenate([ref[pl.ds(c, rows, stride=ROW_SLABS), :] for c in range(ROW_SLABS)], axis=1)


def _store_slabs(ref, x):
    rows = ref.shape[0] // ROW_SLABS
    for c in range(ROW_SLABS):
        ref[pl.ds(c, rows, stride=ROW_SLABS), :] = x[:, c * LANES:(c + 1) * LANES]


def _slab_spec(tm, index_map):
    return pl.BlockSpec((tm * ROW_SLABS, LANES), index_map)


def _router_kernel(h_ref, g_ref, whi_ref, wlo_ref, br_ref, hn_ref, idx_ref, wts_ref):
    hn = _rms(h_ref[...], g_ref[...])
    _store_slabs(hn_ref, hn)
    hi = hn.astype(BF)
    lo = (hn - hi.astype(F32)).astype(BF)
    logits = _dot(hi, whi_ref[...]) + (_dot(lo, whi_ref[...]) + _dot(hi, wlo_ref[...])) + br_ref[...]
    lane = lax.broadcasted_iota(jnp.int32, logits.shape, 1)
    logits = jnp.where(lane < N_EXPERTS, logits, NEG)
    v1 = jnp.max(logits, axis=-1, keepdims=True)
    i1 = jnp.min(jnp.where(logits == v1, lane, LANES), axis=-1, keepdims=True)
    rest = jnp.where(lane == i1, NEG, logits)
    v2 = jnp.max(rest, axis=-1, keepdims=True)
    i2 = jnp.min(jnp.where(rest == v2, lane, LANES), axis=-1, keepdims=True)
    e2 = jnp.exp(v2 - v1)
    w1 = 1.0 / (1.0 + e2)
    w2 = e2 / (1.0 + e2)
    idx_ref[...] = jnp.where(lane == 0, i1, jnp.where(lane == 1, i2, 0))
    wts_ref[...] = jnp.where(lane == 0, w1, jnp.where(lane == 1, w2, 0.0))


def _router(h, g, whi, wlo, br, chunk, n_chunks, tm=1024):
    t = h.shape[0] // n_chunks
    first = chunk * (t // tm)
    return pl.pallas_call(
        _router_kernel,
        grid=(t // tm,),
        in_specs=[pl.BlockSpec((tm, D_MODEL), lambda i: (first + i, 0)),
                  _const_spec((1, D_MODEL)),
                  _const_spec((D_MODEL, LANES)),
                  _const_spec((D_MODEL, LANES)),
                  _const_spec((1, LANES))],
        out_specs=[_slab_spec(tm, lambda i: (i, 0)),
                   pl.BlockSpec((tm, LANES), lambda i: (i, 0)),
                   pl.BlockSpec((tm, LANES), lambda i: (i, 0))],
        out_shape=[jax.ShapeDtypeStruct((t * ROW_SLABS, LANES), F32),
                   jax.ShapeDtypeStruct((t, LANES), jnp.int32),
                   jax.ShapeDtypeStruct((t, LANES), F32)],
        compiler_params=_params("parallel"),
        name="router",
    )(h, g, whi, wlo, br)


def _route_plan(idx, tm):
    t = idx.shape[0]
    n_pairs = 2 * t
    n_rows = n_pairs + N_EXPERTS * tm
    e_flat = jnp.concatenate([idx[:, 0], idx[:, 1]])
    onehot = (e_flat[:, None] == jnp.arange(N_EXPERTS, dtype=jnp.int32)[None, :]).astype(jnp.int32)
    csum = jnp.cumsum(onehot, axis=0)
    rank = jnp.sum(onehot * csum, axis=1) - 1
    counts = csum[-1]
    padded = ((counts + tm - 1) // tm) * tm
    ends = jnp.cumsum(padded)
    starts = ends - padded
    pos = starts[e_flat] + rank
    order = jnp.argsort(e_flat, stable=True).astype(jnp.int32)
    first = jnp.cumsum(counts) - counts
    r = jnp.minimum(jnp.arange(n_rows, dtype=jnp.int32), ends[-1] - 1)
    e_r = jnp.minimum(jnp.searchsorted(ends, r, side="right").astype(jnp.int32), N_EXPERTS - 1)
    local = r - starts[e_r]
    src_pair = order[jnp.clip(first[e_r] + local, 0, n_pairs - 1)]
    src_tok = jnp.where(local < counts[e_r], src_pair % t, 0).astype(jnp.int32)
    tile_expert = e_r[::tm]
    n_valid = (ends[-1] // tm).astype(jnp.int32).reshape(1)
    return pos.astype(jnp.int32), src_tok, tile_expert, n_valid


def _gather_rows(table, idx):
    n = idx.shape[0]
    info = plsc.get_sparse_core_info()
    n_workers = info.num_cores * info.num_subcores
    per_worker = n // n_workers
    assert per_worker * n_workers == n and per_worker % SC_CHUNK == 0, (n, n_workers)
    mesh = plsc.VectorSubcoreMesh(core_axis_name="c", subcore_axis_name="s")

    @functools.partial(
        pl.kernel, mesh=mesh,
        out_type=jax.ShapeDtypeStruct((n,) + table.shape[1:], table.dtype),
        scratch_types=[pltpu.VMEM((SC_CHUNK,), jnp.int32),
                       pltpu.VMEM((SC_CHUNK,) + table.shape[1:], table.dtype),
                       pltpu.SemaphoreType.DMA],
    )
    def gather(table_hbm, idx_hbm, out_hbm, idx_v, rows_v, sem):
        worker = lax.axis_index("s") * info.num_cores + lax.axis_index("c")
        base = worker * per_worker

        @pl.loop(0, per_worker // SC_CHUNK)
        def _(i):
            off = pl.multiple_of(base + i * SC_CHUNK, SC_CHUNK)
            pltpu.sync_copy(idx_hbm.at[pl.ds(off, SC_CHUNK)], idx_v)
            pltpu.async_copy(table_hbm.at[idx_v], rows_v, sem).wait()
            pltpu.sync_copy(rows_v, out_hbm.at[pl.ds(off, SC_CHUNK)])

    return gather(table, idx)


def _expert_ffn_kernel(te_ref, nv_ref, x_ref, w13_ref, w2_ref, y_ref):
    valid = pl.program_id(0) < nv_ref[0]

    @pl.when(valid)
    def _():
        _store_slabs(y_ref, _swiglu(_split_slabs(x_ref).astype(BF), w13_ref.at[0], w2_ref.at[0]))

    @pl.when(jnp.logical_not(valid))
    def _():
        y_ref[...] = jnp.zeros_like(y_ref)


def _expert_ffn(xs, tile_expert, n_valid, w13, w2, tm):
    n_rows = xs.shape[0] // ROW_SLABS
    last = lambda nv: jnp.maximum(nv[0] - 1, 0)
    grid_spec = pltpu.PrefetchScalarGridSpec(
        num_scalar_prefetch=2,
        grid=(n_rows // tm,),
        in_specs=[_slab_spec(tm, lambda i, te, nv: (jnp.minimum(i, last(nv)), 0)),
                  pl.BlockSpec((1, D_MODEL, 2 * D_FF), lambda i, te, nv: (te[i], 0, 0), pipeline_mode=pl.Buffered(1)),
                  pl.BlockSpec((1, D_FF, D_MODEL), lambda i, te, nv: (te[i], 0, 0), pipeline_mode=pl.Buffered(1))],
        out_specs=_slab_spec(tm, lambda i, te, nv: (i, 0)),
    )
    return pl.pallas_call(
        _expert_ffn_kernel,
        grid_spec=grid_spec,
        out_shape=jax.ShapeDtypeStruct((n_rows * ROW_SLABS, LANES), F32),
        compiler_params=_params("arbitrary"),
        name="expert_ffn",
    )(tile_expert, n_valid, xs, w13, w2)


def _combine_kernel(h_ref, y0_ref, y1_ref, wts_ref, gf_ref, *rest):
    out_ref = rest[-1]
    wts = wts_ref[...]
    tot = h_ref[...] + wts[:, 0:1] * _split_slabs(y0_ref) + wts[:, 1:2] * _split_slabs(y1_ref)
    out_ref[...] = _rms(tot, gf_ref[...])


def _combine(h, yg, wts, g_final, chunk, n_chunks, out_so_far, tm=1024):
    t = h.shape[0]
    nt = t // n_chunks // tm
    first = chunk * nt
    in_specs = [pl.BlockSpec((tm, D_MODEL), lambda i: (first + i, 0)),
                _slab_spec(tm, lambda i: (i, 0)),
                _slab_spec(tm, lambda i: (nt + i, 0)),
                pl.BlockSpec((tm, LANES), lambda i: (i, 0)),
                _const_spec((1, D_MODEL))]
    args = [h, yg, yg, wts, g_final]
    aliases = {}
    if out_so_far is not None:
        in_specs.append(pl.BlockSpec(memory_space=pl.ANY))
        args.append(out_so_far)
        aliases = {len(args) - 1: 0}
    return pl.pallas_call(
        _combine_kernel,
        grid=(nt,),
        in_specs=in_specs,
        out_specs=pl.BlockSpec((tm, D_MODEL), lambda i: (first + i, 0)),
        out_shape=jax.ShapeDtypeStruct((t, D_MODEL), F32),
        input_output_aliases=aliases,
        compiler_params=_params("parallel"),
        name="combine",
    )(*args)


def _moe(h, g, router_w, router_b, w13, w2, g_final, tm=512):
    wr = jnp.pad(router_w, ((0, 0), (0, LANES - N_EXPERTS)))
    whi = wr.astype(BF)
    wlo = (wr - whi.astype(F32)).astype(BF)
    br = jnp.pad(router_b, (0, LANES - N_EXPERTS)).reshape(1, LANES)
    as_rows = lambda a: a.reshape(-1, ROW_SLABS, LANES)
    as_slabs = lambda a: a.reshape(-1, LANES)
    chunks = range(MOE_CHUNKS)
    routed = [_router(h, g, whi, wlo, br, c, MOE_CHUNKS) for c in chunks]
    plans = [_route_plan(idx, tm) for _, idx, _ in routed]
    xs = [as_slabs(_gather_rows(as_rows(routed[c][0]), plans[c][1])) for c in chunks]
    ys = [_expert_ffn(xs[c], plans[c][2], plans[c][3], w13, w2, tm) for c in chunks]
    yg = [as_slabs(_gather_rows(as_rows(ys[c]), plans[c][0])) for c in chunks]
    out = None
    for c in chunks:
        out = _combine(h, yg[c], routed[c][2], g_final, c, MOE_CHUNKS, out)
    return out


def _feature_major(w, heads):
    wt = w.T.reshape(heads, HEAD_DIM, D_MODEL)
    return jnp.pad(wt, ((0, 0), (0, VT_ROWS - HEAD_DIM), (0, 0))).reshape(heads * VT_ROWS, D_MODEL)


def _pack_w_in(w_in):
    cuts = np.cumsum((512, 512, 512, 512, 512, 128, 128, 512, 512, 512, 8))[:-1].tolist()
    a_u, a_v, b_x, b_y, c_q, c_k, c_v, d_q, d_k, d_v, d_f = jnp.split(w_in, cuts, axis=-1)
    w = jnp.concatenate([b_x, b_y, d_q, d_k, c_k, c_v], axis=-1).astype(BF)
    wa = jnp.concatenate([a_u, a_v], axis=-1).astype(BF)
    wf = jnp.pad(d_f, ((0, 0), (0, LANES - FOX_HEADS))).astype(BF)
    wt = jnp.concatenate([_feature_major(d_v, FOX_HEADS), c_q.T, _feature_major(c_v, SWA_KV)], axis=0).astype(BF)
    return w, wa, wf, wt


def _fm_ones():
    ones = np.zeros((FM_ROWS, LANES), np.float32)
    for base, heads in ((0, FOX_HEADS), (VT_ALL + BRANCH_W, SWA_KV)):
        for h in range(heads):
            ones[base + h * VT_ROWS + HEAD_DIM, :] = 1.0
    return jnp.asarray(ones)


def _row(v):
    return v.reshape(1, -1)


def _hybrid_mixer(h, bsz, norm_mix, w_in, sgu_g, sgu_w, sgu_b, conv_w, conv_b, rg_wa, rg_ba, rg_wx, rg_bx,
                  rg_lambda, swa_sinks, fox_bf, w_branch, w_gate, b_gate, w_out):
    w, wa, wf, wt = _pack_w_in(w_in)
    proj, f, o_a, vt, sqt, svt = _inproj(h, _row(norm_mix), w, wa, wf, wt, _fm_ones(), _row(sgu_g), sgu_w, sgu_b.T)
    wax = jnp.concatenate([rg_wa, rg_wx], axis=-1).astype(BF)
    o_b = _rglru(proj, bsz, conv_w, _row(conv_b), wax, _row(rg_ba), _row(rg_bx), _row(rg_lambda))
    o_c = _swa(proj, sqt, svt, bsz, swa_sinks)
    bf_pad = jnp.pad(fox_bf, (0, LANES - FOX_HEADS)).reshape(1, LANES)
    qp, kp = _fox_prep(proj, f, bsz, bf_pad)
    o_d = _fox(qp, kp, vt, bsz)
    return _merge(h, _row(norm_mix), o_a, o_b, o_c, o_d, w_gate.astype(BF), b_gate[:, None, :],
                  w_branch.astype(BF), w_out.astype(BF))


def kernel(x, mem, norm_mix, w_in, sgu_g, sgu_w, sgu_b, conv_w, conv_b, rg_wa, rg_ba, rg_wx, rg_bx, rg_lambda, swa_sinks, fox_bf, w_branch, w_gate, b_gate, w_out, norm_cross, norm_mem, wq_c, wkv_c, wo_c, norm_ffn, dense_w13, dense_w2, router_w, router_b, moe_w13, moe_w2, norm_final):
    bsz, seq, d = x.shape
    m_len = mem.shape[1]
    depth = norm_mix.shape[0]
    assert depth == 2, "the final RMSNorm is fused into the routed layer, which must be the last one"
    h = x.reshape(bsz * seq, d)
    mem2 = mem.reshape(bsz * m_len, d)
    for l in range(depth):
        h = _hybrid_mixer(h, bsz, norm_mix[l], w_in[l], sgu_g[l], sgu_w[l], sgu_b[l], conv_w[l], conv_b[l],
                          rg_wa[l], rg_ba[l], rg_wx[l], rg_bx[l], rg_lambda[l], swa_sinks[l], fox_bf[l],
                          w_branch[l], w_gate[l], b_gate[l], w_out[l])
        kv = _memkv(mem2, _row(norm_mem[l]), wkv_c[l].astype(BF), m_len)
        h = _cross(h, _row(norm_cross[l]), wq_c[l].astype(BF), kv, wo_c[l].astype(BF), bsz, m_len)
        if l % 2 == 0:
            h = _ffn(h, _row(norm_ffn[l]), dense_w13[l // 2].astype(BF), dense_w2[l // 2].astype(BF))
        else:
            h = _moe(h, _row(norm_ffn[l]), router_w[l // 2], router_b[l // 2], moe_w13[l // 2].astype(BF),
                     moe_w2[l // 2].astype(BF), _row(norm_final))
    return h.reshape(bsz, seq, d)
```

```python
import functools
import math

import numpy as np
import jax
import jax.numpy as jnp
from jax import lax
from jax.experimental import pallas as pl
from jax.experimental.pallas import tpu as pltpu
from jax.experimental.pallas import tpu_sc as plsc

F32 = jnp.float32
BF = jnp.bfloat16

D_MODEL = 1024
BRANCH_W = 512
HEAD_DIM = 64
CHUNK = 128
SGU_GROUPS = 4
LRU_HEADS = 4
LRU_C = 8.0
CONV_W = 4
SWA_HEADS = 8
SWA_KV = 2
FOX_HEADS = 8
FOX_PAIRS = FOX_HEADS // 2
X_HEADS = 4
X_HEAD_DIM = 128
D_FF = 2816
N_EXPERTS = 8
EPS = 1e-6
LOG2E = math.log2(math.e)
NEG = -1e30
LANES = 128
ROW_SLABS = D_MODEL // LANES
SC_CHUNK = 64
MOE_CHUNKS = 2
VMEM_LIMIT = 56 * 1024 * 1024

COL_BX, COL_BY, COL_DQ, COL_DK = range(4)
N_WIDE = 4
N_PROJ = N_WIDE * BRANCH_W + 2 * SWA_KV * HEAD_DIM
VT_ROWS = 80
VT_ALL = FOX_HEADS * VT_ROWS
SWA_VT = SWA_KV * VT_ROWS
FM_ROWS = VT_ALL + BRANCH_W + SWA_VT
FOX_UNROLL = 4
FOX_STEP_PAIRS = 1
FF_SPLITS = ((0, 1536), (1536, D_FF))


def _rms(x, g):
    return x * lax.rsqrt(jnp.mean(x * x, axis=-1, keepdims=True) + EPS) * g


def _sigmoid(x):
    return 1.0 / (1.0 + jnp.exp(-x))


def _gelu(x):
    return 0.5 * x * (1.0 + jnp.tanh(math.sqrt(2.0 / math.pi) * (x + 0.044715 * (x * x * x))))


def _dot(a, b):
    return jnp.dot(a, b, preferred_element_type=F32)


def _dot_nt(a, b):
    return lax.dot_general(a, b, (((1,), (1,)), ((), ())), preferred_element_type=F32)


def _shift_rows(x, d, fill):
    row = lax.broadcasted_iota(jnp.int32, x.shape, 0)
    return jnp.where(row >= d, pltpu.roll(x, d, 0), fill)


def _params(*sem):
    return pltpu.CompilerParams(dimension_semantics=sem, vmem_limit_bytes=VMEM_LIMIT)


def _const_spec(shape):
    nd = len(shape)
    return pl.BlockSpec(shape, lambda *_: (0,) * nd, pipeline_mode=pl.Buffered(1))


def _inproj_kernel(h_ref, g_ref, w_ref, wa_ref, wf_ref, wt_ref, ones_ref, sg_ref, sw_ref, sbt_ref,
                   proj_ref, f_ref, oa_ref, vt_ref, sqt_ref, svt_ref):
    xn = _rms(h_ref[...], g_ref[...]).astype(BF)
    tm = xn.shape[0]
    n_col = w_ref.shape[1]
    for c in range(0, n_col, BRANCH_W):
        w = min(BRANCH_W, n_col - c)
        proj_ref[:, c:c + w] = _dot(xn, w_ref[:, c:c + w]).astype(BF)
    f_ref[...] = _dot(xn, wf_ref[...])
    ones = jnp.concatenate([ones_ref[...]] * (tm // LANES), axis=1)
    fm = (_dot_nt(wt_ref[...], xn) + ones).astype(BF)
    vt_ref[...] = fm[:VT_ALL]
    sqt_ref[...] = fm[VT_ALL:VT_ALL + BRANCH_W]
    svt_ref[...] = fm[VT_ALL + BRANCH_W:]

    u = _gelu(_dot(xn, wa_ref[:, :BRANCH_W]))
    v = _gelu(_dot(xn, wa_ref[:, BRANCH_W:]))
    vn = _rms(v, sg_ref[...]).astype(BF)
    row = lax.broadcasted_iota(jnp.int32, (CHUNK, CHUNK), 0)
    col = lax.broadcasted_iota(jnp.int32, (CHUNK, CHUNK), 1)
    gw = BRANCH_W // SGU_GROUPS
    for g in range(SGU_GROUPS):
        wg = jnp.where(col <= row, sw_ref[g], 0.0).astype(BF)
        bg = sbt_ref[:, g:g + 1]
        for c in range(tm // CHUNK):
            rs = slice(c * CHUNK, (c + 1) * CHUNK)
            cs = slice(g * gw, (g + 1) * gw)
            mixed = _dot(wg, vn[rs, cs]) + bg
            oa_ref[rs, cs] = (u[rs, cs] * mixed).astype(BF)


def _inproj(h, g, w, wa, wf, wt, ones, sgu_g, sgu_w, sgu_bt, tm=512):
    t = h.shape[0]
    row = lambda i: (i, 0)
    col = lambda i: (0, i)
    return pl.pallas_call(
        _inproj_kernel,
        grid=(t // tm,),
        in_specs=[pl.BlockSpec((tm, D_MODEL), row),
                  _const_spec((1, D_MODEL)),
                  _const_spec((D_MODEL, N_PROJ)),
                  _const_spec((D_MODEL, 2 * BRANCH_W)),
                  _const_spec((D_MODEL, LANES)),
                  _const_spec((FM_ROWS, D_MODEL)),
                  _const_spec((FM_ROWS, LANES)),
                  _const_spec((1, BRANCH_W)),
                  _const_spec((SGU_GROUPS, CHUNK, CHUNK)),
                  _const_spec((CHUNK, SGU_GROUPS))],
        out_specs=[pl.BlockSpec((tm, N_PROJ), row),
                   pl.BlockSpec((tm, LANES), row),
                   pl.BlockSpec((tm, BRANCH_W), row),
                   pl.BlockSpec((VT_ALL, tm), col),
                   pl.BlockSpec((BRANCH_W, tm), col),
                   pl.BlockSpec((SWA_VT, tm), col)],
        out_shape=[jax.ShapeDtypeStruct((t, N_PROJ), BF), jax.ShapeDtypeStruct((t, LANES), F32),
                   jax.ShapeDtypeStruct((t, BRANCH_W), BF), jax.ShapeDtypeStruct((VT_ALL, t), BF),
                   jax.ShapeDtypeStruct((BRANCH_W, t), BF), jax.ShapeDtypeStruct((SWA_VT, t), BF)],
        compiler_params=_params("parallel"),
        name="inproj",
    )(h, g, w, wa, wf, wt, ones, sgu_g, sgu_w, sgu_bt)


def _rglru_kernel(x_ref, y_ref, cw_ref, cb_ref, wax_ref, ba_ref, bx_ref, lam_ref, o_ref, xs_ref, hc_ref):
    ts = x_ref.shape[0]
    hw = BRANCH_W // LRU_HEADS

    @pl.when(pl.program_id(1) == 0)
    def _():
        xs_ref[0:8, :] = jnp.zeros((8, BRANCH_W), F32)
        hc_ref[...] = jnp.zeros_like(hc_ref)

    x = x_ref[...].astype(F32)
    xs_ref[8:8 + ts, :] = x
    cw = cw_ref[...]
    xc = cb_ref[...] + cw[CONV_W - 1:CONV_W] * x
    for k in range(1, CONV_W):
        xc = xc + cw[CONV_W - 1 - k:CONV_W - k] * xs_ref[8 - k:8 - k + ts, :]
    xs_ref[0:8, :] = x[ts - 8:ts, :]

    xcb = xc.astype(BF)
    r_parts, i_parts = [], []
    for hd in range(LRU_HEADS):
        z = _dot(xcb[:, hd * hw:(hd + 1) * hw], wax_ref[hd])
        r_parts.append(z[:, :hw])
        i_parts.append(z[:, hw:])
    r = _sigmoid(jnp.concatenate(r_parts, axis=1) + ba_ref[...])
    gi = _sigmoid(jnp.concatenate(i_parts, axis=1) + bx_ref[...])
    nl = -lam_ref[...]
    softplus = jnp.maximum(nl, 0.0) + jnp.log1p(jnp.exp(-jnp.abs(nl)))
    log_a = (-LRU_C * r) * softplus
    a = jnp.exp(log_a)
    th = jnp.tanh(log_a)
    e2 = -2.0 * th / (1.0 - th)
    b = (xc * gi) * (e2 * lax.rsqrt(jnp.maximum(e2, 1e-30)))

    sub = lax.broadcasted_iota(jnp.int32, a.shape, 0) & 7
    for d in (1, 2, 4):
        keep = sub >= d
        a_sh = jnp.where(keep, pltpu.roll(a, d, 0), 1.0)
        b_sh = jnp.where(keep, pltpu.roll(b, d, 0), 0.0)
        b = b + a * b_sh
        a = a * a_sh
    carry = hc_ref[...]
    groups = []
    for g in range(ts // 8):
        hg = a[8 * g:8 * g + 8] * carry + b[8 * g:8 * g + 8]
        groups.append(hg)
        carry = hg[7:8]
    h = jnp.concatenate(groups, axis=0)
    hc_ref[...] = carry
    o_ref[...] = (h * _gelu(y_ref[...].astype(F32))).astype(BF)


def _rglru(proj, bsz, cw, cb, wax, ba, bx, lam, ts=512):
    t = proj.shape[0]
    ns = t // bsz // ts
    return pl.pallas_call(
        _rglru_kernel,
        grid=(bsz, ns),
        in_specs=[pl.BlockSpec((ts, BRANCH_W), lambda b, j: (b * ns + j, COL_BX)),
                  pl.BlockSpec((ts, BRANCH_W), lambda b, j: (b * ns + j, COL_BY)),
                  _const_spec((CONV_W, BRANCH_W)),
                  _const_spec((1, BRANCH_W)),
                  _const_spec((LRU_HEADS, BRANCH_W // LRU_HEADS, 2 * BRANCH_W // LRU_HEADS)),
                  _const_spec((1, BRANCH_W)),
                  _const_spec((1, BRANCH_W)),
                  _const_spec((1, BRANCH_W))],
        out_specs=pl.BlockSpec((ts, BRANCH_W), lambda b, j: (b * ns + j, 0)),
        out_shape=jax.ShapeDtypeStruct((t, BRANCH_W), BF),
        scratch_shapes=[pltpu.VMEM((ts + 8, BRANCH_W), F32), pltpu.VMEM((1, BRANCH_W), F32)],
        compiler_params=_params("parallel", "arbitrary"),
        name="rglru",
    )(proj, proj, cw, cb, wax, ba, bx, lam)


def _swa_tables(sinks):
    t_idx = np.arange(CHUNK)[None, :] + CHUNK
    dist = (t_idx - np.arange(2 * CHUNK)[:, None]).astype(np.float32)
    in_win = (dist >= 0) & (dist < CHUNK)
    slopes = 2.0 ** (-(8.0 / SWA_HEADS) * np.arange(1, SWA_HEADS + 1, dtype=np.float32))
    bias = np.where(in_win[None], -slopes[:, None, None] * dist[None] * LOG2E, NEG)
    grp = SWA_HEADS // SWA_KV
    bias = bias.reshape(SWA_KV, grp, 2 * CHUNK, CHUNK).transpose(0, 2, 1, 3).reshape(SWA_KV, 2 * CHUNK, grp * CHUNK)
    sink = jnp.repeat(sinks.astype(F32) * LOG2E, CHUNK).reshape(SWA_KV, 1, grp * CHUNK)
    return jnp.asarray(bias, F32), sink


def _swa_kernel(qt_ref, kv_ref, kvp_ref, vt_ref, vtp_ref, bias_ref, sink_ref, o_ref):
    tq = kv_ref.shape[0]
    grp = SWA_HEADS // SWA_KV
    first_key = jnp.where(pl.program_id(1) == 0, CHUNK, 0)
    k_all = jnp.concatenate([kvp_ref[...], kv_ref[...]], axis=0)
    vt_all = jnp.concatenate([vtp_ref[...], vt_ref[...]], axis=1)
    qt = (qt_ref[...].astype(F32) * (HEAD_DIM ** -0.5 * LOG2E)).astype(BF)
    key_row = lax.broadcasted_iota(jnp.int32, (2 * CHUNK, grp * CHUNK), 0)
    kw = 2 * SWA_KV * HEAD_DIM
    for qb in range(tq // CHUNK):
        cols = slice(qb * CHUNK, (qb + 1) * CHUNK)
        kb = k_all[qb * CHUNK:(qb + 2) * CHUNK]
        vtb = vt_all[:, qb * CHUNK:(qb + 2) * CHUNK]
        outs = []
        for kh in range(SWA_KV):
            qg = jnp.concatenate([qt[(kh * grp + g) * HEAD_DIM:(kh * grp + g + 1) * HEAD_DIM, cols]
                                  for g in range(grp)], axis=1)
            parts = []
            if kh > 0:
                parts.append(jnp.zeros((kh * HEAD_DIM, grp * CHUNK), BF))
            parts.append(qg)
            parts.append(jnp.zeros((kw - (kh + 1) * HEAD_DIM, grp * CHUNK), BF))
            st = _dot(kb, jnp.concatenate(parts, axis=0)) + bias_ref[kh]
            if qb == 0:
                st = jnp.where(key_row >= first_key, st, NEG)
            sink = sink_ref[kh]
            m = jnp.maximum(jnp.max(st, axis=0, keepdims=True), sink)
            pt = jnp.exp2(st - m).astype(BF)
            acc = _dot(vtb[kh * VT_ROWS:(kh + 1) * VT_ROWS], pt)
            ot = acc[:HEAD_DIM] / (acc[HEAD_DIM:HEAD_DIM + 1] + jnp.exp2(sink - m))
            outs.extend(ot[:, g * CHUNK:(g + 1) * CHUNK] for g in range(grp))
        o_ref[cols, :] = jnp.concatenate(outs, axis=0).T.astype(BF)


def _swa(proj, sqt, svt, bsz, sinks, tq=1024):
    t = proj.shape[0]
    seq = t // bsz
    nq = seq // tq
    per = tq // CHUNK
    kv_blk = N_WIDE * BRANCH_W // (2 * SWA_KV * HEAD_DIM)
    prev = lambda b, i: b * (seq // CHUNK) + jnp.maximum(i * per - 1, 0)
    bias, sink = _swa_tables(sinks)
    grp = SWA_HEADS // SWA_KV
    return pl.pallas_call(
        _swa_kernel,
        grid=(bsz, nq),
        in_specs=[pl.BlockSpec((BRANCH_W, tq), lambda b, i: (0, b * nq + i)),
                  pl.BlockSpec((tq, 2 * SWA_KV * HEAD_DIM), lambda b, i: (b * nq + i, kv_blk)),
                  pl.BlockSpec((CHUNK, 2 * SWA_KV * HEAD_DIM), lambda b, i: (prev(b, i), kv_blk)),
                  pl.BlockSpec((SWA_VT, tq), lambda b, i: (0, b * nq + i)),
                  pl.BlockSpec((SWA_VT, CHUNK), lambda b, i: (0, prev(b, i))),
                  _const_spec((SWA_KV, 2 * CHUNK, grp * CHUNK)),
                  _const_spec((SWA_KV, 1, grp * CHUNK))],
        out_specs=pl.BlockSpec((tq, BRANCH_W), lambda b, i: (b * nq + i, 0)),
        out_shape=jax.ShapeDtypeStruct((t, BRANCH_W), BF),
        compiler_params=_params("parallel", "parallel"),
        name="swa",
    )(sqt, proj, proj, svt, svt, bias, sink)


def _aug_tables():
    eq = np.zeros((3, LANES, FOX_PAIRS * LANES), np.float32)
    ek = np.zeros((3, LANES, FOX_PAIRS * LANES), np.float32)
    oq = np.zeros((1, FOX_PAIRS * LANES), np.float32)
    ok = np.zeros((1, FOX_PAIRS * LANES), np.float32)
    for h in range(FOX_HEADS):
        base = (h // 2) * LANES + 6 * (h % 2)
        for s in range(3):
            eq[s, h, base + s] = 1.0
            ek[s, h, base + 3 + s] = -1.0
            ok[0, base + s] = 1.0
            oq[0, base + 3 + s] = 1.0
    return eq, ek, oq, ok


def _fox_prep_kernel(q_ref, k_ref, f_ref, bf_ref, eq_ref, ek_ref, oq_ref, ok_ref, qp_ref, kp_ref, cum_ref):
    ts = q_ref.shape[0]

    @pl.when(pl.program_id(1) == 0)
    def _():
        cum_ref[...] = jnp.zeros_like(cum_ref)

    z = f_ref[...] + bf_ref[...]
    c = jnp.minimum(z, 0.0) - jnp.log1p(jnp.exp(-jnp.abs(z)))
    d = 1
    while d < ts:
        c = c + _shift_rows(c, d, 0.0)
        d *= 2
    c = c + cum_ref[...]
    cum_ref[...] = c[ts - 1:ts, :]
    c = c * LOG2E
    c1 = c.astype(BF)
    r1 = c - c1.astype(F32)
    c2 = r1.astype(BF)
    c3 = (r1 - c2.astype(F32)).astype(BF)
    augq = _dot(c1, eq_ref[0]) + _dot(c2, eq_ref[1]) + _dot(c3, eq_ref[2]) + oq_ref[...]
    augk = _dot(c1, ek_ref[0]) + _dot(c2, ek_ref[1]) + _dot(c3, ek_ref[2]) + ok_ref[...]
    qt = (q_ref[...].astype(F32) * (HEAD_DIM ** -0.5 * LOG2E)).T.astype(BF)
    augqt = augq.T.astype(BF)
    k = k_ref[...]
    for p in range(FOX_PAIRS):
        qp_ref[2 * p * LANES:(2 * p + 1) * LANES, :] = qt[p * LANES:(p + 1) * LANES]
        qp_ref[(2 * p + 1) * LANES:(2 * p + 2) * LANES, :] = augqt[p * LANES:(p + 1) * LANES]
        kp_ref[:, 2 * p * LANES:(2 * p + 1) * LANES] = k[:, p * LANES:(p + 1) * LANES]
        kp_ref[:, (2 * p + 1) * LANES:(2 * p + 2) * LANES] = augk[:, p * LANES:(p + 1) * LANES].astype(BF)


def _fox_prep(proj, f, bsz, bf_pad, ts=1024):
    t = proj.shape[0]
    ns = t // bsz // ts
    eq, ek, oq, ok = _aug_tables()
    wide = 2 * FOX_PAIRS * LANES
    return pl.pallas_call(
        _fox_prep_kernel,
        grid=(bsz, ns),
        in_specs=[pl.BlockSpec((ts, BRANCH_W), lambda b, j: (b * ns + j, COL_DQ)),
                  pl.BlockSpec((ts, BRANCH_W), lambda b, j: (b * ns + j, COL_DK)),
                  pl.BlockSpec((ts, LANES), lambda b, j: (b * ns + j, 0)),
                  _const_spec((1, LANES)),
                  _const_spec(eq.shape), _const_spec(ek.shape), _const_spec(oq.shape), _const_spec(ok.shape)],
        out_specs=[pl.BlockSpec((wide, ts), lambda b, j: (0, b * ns + j)),
                   pl.BlockSpec((ts, wide), lambda b, j: (b * ns + j, 0))],
        out_shape=[jax.ShapeDtypeStruct((wide, t), BF), jax.ShapeDtypeStruct((t, wide), BF)],
        scratch_shapes=[pltpu.VMEM((1, LANES), F32)],
        compiler_params=_params("parallel", "arbitrary"),
        name="fox_prep",
    )(proj, proj, f, bf_pad, jnp.asarray(eq, BF), jnp.asarray(ek, BF), jnp.asarray(oq), jnp.asarray(ok))


def _fox_kernel(q_ref, k_ref, vt_ref, o_ref, st_ref, pt_ref, al_ref, m_ref, acc_ref):
    tq = q_ref.shape[1]
    tk = tq // 2
    n_heads = m_ref.shape[0]
    qi = pl.program_id(2)
    row = lax.broadcasted_iota(jnp.int32, (2 * LANES, tq), 0)
    in0 = (row < HEAD_DIM) | ((row >= LANES) & (row < LANES + 6))
    in1 = ((row >= HEAD_DIM) & (row < LANES)) | ((row >= LANES + 6) & (row < LANES + 12))
    qh = []
    for p in range(n_heads // 2):
        q = q_ref[2 * p * LANES:(2 * p + 2) * LANES, :]
        zero = jnp.zeros_like(q)
        qh += [jnp.where(in0, q, zero), jnp.where(in1, q, zero)]
    for h in range(n_heads):
        m_ref[h] = jnp.full(m_ref.shape[1:], NEG, F32)
        acc_ref[h] = jnp.zeros(acc_ref.shape[1:], F32)
        pt_ref[1, h] = jnp.zeros(pt_ref.shape[2:], BF)
        al_ref[1, h] = jnp.ones(al_ref.shape[2:], F32)

    def scores(t, slot):
        rows = pl.ds(pl.multiple_of(t * tk, tk), tk)
        for p in range(n_heads // 2):
            kp = k_ref[rows, 2 * p * LANES:(2 * p + 2) * LANES]
            for h in (2 * p, 2 * p + 1):
                st_ref[slot, h] = _dot(kp, qh[h])

    def numerators(slot, key_offset):
        for h in range(n_heads):
            st = st_ref[slot, h]
            if key_offset is not None:
                key = lax.broadcasted_iota(jnp.int32, (tk, tq), 0) + key_offset
                qry = lax.broadcasted_iota(jnp.int32, (tk, tq), 1)
                st = jnp.where(key <= qry, st, NEG)
            m_old = m_ref[h]
            m_new = jnp.maximum(m_old, jnp.max(st, axis=0, keepdims=True))
            al_ref[slot, h] = jnp.exp2(m_old - m_new)
            pt_ref[slot, h] = jnp.exp2(st - m_new).astype(BF)
            m_ref[h] = m_new

    def accumulate(t, slot):
        cols = pl.ds(pl.multiple_of(t * tk, tk), tk)
        for h in range(n_heads):
            vt = vt_ref[h * VT_ROWS:(h + 1) * VT_ROWS, cols]
            acc_ref[h] = al_ref[slot, h] * acc_ref[h] + _dot(vt, pt_ref[slot, h])

    def tile_pair(j, diagonal):
        scores(2 * j + 1, 1)
        accumulate(jnp.maximum(2 * j - 1, 0), 1)
        numerators(0, 0 if diagonal else None)
        if not diagonal:
            scores(2 * j + 2, 0)
        accumulate(2 * j, 0)
        numerators(1, tk if diagonal else None)

    def body(jj, carry):
        for u in range(FOX_UNROLL):
            tile_pair(FOX_UNROLL * jj + u, False)
        return carry

    def remainder(j, carry):
        tile_pair(j, False)
        return carry

    scores(0, 0)
    lax.fori_loop(0, qi // FOX_UNROLL, body, 0)
    lax.fori_loop((qi // FOX_UNROLL) * FOX_UNROLL, qi, remainder, 0)
    tile_pair(qi, True)
    accumulate(2 * qi + 1, 1)
    outs = []
    for h in range(n_heads):
        acc = acc_ref[h]
        outs.append(acc[:HEAD_DIM] / acc[HEAD_DIM:HEAD_DIM + 1])
    o_ref[...] = jnp.concatenate(outs, axis=0).T.astype(BF)


def _fox(qp, kp, vt, bsz, tq=512):
    t = kp.shape[0]
    seq = t // bsz
    nq = seq // tq
    np_ = FOX_STEP_PAIRS
    nh = 2 * np_
    return pl.pallas_call(
        _fox_kernel,
        grid=(bsz, FOX_PAIRS // np_, nq),
        in_specs=[pl.BlockSpec((np_ * 2 * LANES, tq), lambda b, p, i: (p, b * nq + i)),
                  pl.BlockSpec((seq, np_ * 2 * LANES), lambda b, p, i: (b, p)),
                  pl.BlockSpec((nh * VT_ROWS, seq), lambda b, p, i: (p, b))],
        out_specs=pl.BlockSpec((tq, np_ * LANES), lambda b, p, i: (b * nq + i, p)),
        out_shape=jax.ShapeDtypeStruct((t, BRANCH_W), BF),
        scratch_shapes=[pltpu.VMEM((2, nh, tq // 2, tq), F32),
                        pltpu.VMEM((2, nh, tq // 2, tq), BF),
                        pltpu.VMEM((2, nh, 1, tq), F32),
                        pltpu.VMEM((nh, 1, tq), F32),
                        pltpu.VMEM((nh, VT_ROWS, tq), F32)],
        compiler_params=_params("parallel", "parallel", "arbitrary"),
        name="fox",
    )(qp, kp, vt)


def _merge_kernel(h_ref, g_ref, oa_ref, ob_ref, oc_ref, od_ref, wg_ref, bg_ref, wb_ref, wo_ref, out_ref):
    h = h_ref[...]
    xn = _rms(h, g_ref[...]).astype(BF)
    merged = None
    for br, o_ref in enumerate((oa_ref, ob_ref, oc_ref, od_ref)):
        gate = _sigmoid(_dot(xn, wg_ref[br]) + bg_ref[br])
        term = gate * _dot(o_ref[...], wb_ref[br])
        merged = term if merged is None else merged + term
    out_ref[...] = h + _dot(merged.astype(BF), wo_ref[...])


def _merge(h, g, oa, ob, oc, od, wg, bg, wb, wo, tm=512):
    t = h.shape[0]
    row = lambda i: (i, 0)
    return pl.pallas_call(
        _merge_kernel,
        grid=(t // tm,),
        in_specs=[pl.BlockSpec((tm, D_MODEL), row),
                  _const_spec((1, D_MODEL)),
                  pl.BlockSpec((tm, BRANCH_W), row), pl.BlockSpec((tm, BRANCH_W), row),
                  pl.BlockSpec((tm, BRANCH_W), row), pl.BlockSpec((tm, BRANCH_W), row),
                  _const_spec((4, D_MODEL, D_MODEL)),
                  _const_spec((4, 1, D_MODEL)),
                  _const_spec((4, BRANCH_W, D_MODEL)),
                  _const_spec((D_MODEL, D_MODEL))],
        out_specs=pl.BlockSpec((tm, D_MODEL), row),
        out_shape=jax.ShapeDtypeStruct((t, D_MODEL), F32),
        compiler_params=_params("parallel"),
        name="merge",
    )(h, g, oa, ob, oc, od, wg, bg, wb, wo)


def _memkv_kernel(mem_ref, g_ref, w_ref, kv_ref):
    mn = _rms(mem_ref[...], g_ref[...]).astype(BF)
    kv_ref[...] = _dot(mn, w_ref[...]).astype(BF)


def _memkv(mem2, g, w, m_len):
    n = mem2.shape[0]
    width = 2 * X_HEADS * X_HEAD_DIM
    return pl.pallas_call(
        _memkv_kernel,
        grid=(n // m_len,),
        in_specs=[pl.BlockSpec((m_len, D_MODEL), lambda b: (b, 0)),
                  _const_spec((1, D_MODEL)),
                  _const_spec((D_MODEL, width))],
        out_specs=pl.BlockSpec((m_len, width), lambda b: (b, 0)),
        out_shape=jax.ShapeDtypeStruct((n, width), BF),
        compiler_params=_params("parallel"),
        name="memkv",
    )(mem2, g, w)


def _cross_kernel(h_ref, g_ref, wq_ref, kv_ref, wo_ref, out_ref):
    h = h_ref[...]
    hn = _rms(h, g_ref[...]).astype(BF)
    q = _dot(hn, wq_ref[...]).astype(BF)
    kv = kv_ref[...]
    width = X_HEADS * X_HEAD_DIM
    outs = []
    for hd in range(X_HEADS):
        cs = slice(hd * X_HEAD_DIM, (hd + 1) * X_HEAD_DIM)
        s = _dot_nt(q[:, cs], kv[:, cs]) * (X_HEAD_DIM ** -0.5)
        m = jnp.max(s, axis=-1, keepdims=True)
        p = jnp.exp(s - m)
        denom = jnp.sum(p, axis=-1, keepdims=True)
        v = kv[:, width + hd * X_HEAD_DIM:width + (hd + 1) * X_HEAD_DIM]
        outs.append((_dot(p.astype(BF), v) / denom).astype(BF))
    o = jnp.concatenate(outs, axis=1)
    out_ref[...] = h + _dot(o, wo_ref[...])


def _cross(h, g, wq, kv, wo, bsz, m_len, tm=1024):
    t = h.shape[0]
    per = t // bsz // tm
    width = X_HEADS * X_HEAD_DIM
    return pl.pallas_call(
        _cross_kernel,
        grid=(bsz, per),
        in_specs=[pl.BlockSpec((tm, D_MODEL), lambda b, i: (b * per + i, 0)),
                  _const_spec((1, D_MODEL)),
                  _const_spec((D_MODEL, width)),
                  pl.BlockSpec((m_len, 2 * width), lambda b, i: (b, 0)),
                  _const_spec((width, D_MODEL))],
        out_specs=pl.BlockSpec((tm, D_MODEL), lambda b, i: (b * per + i, 0)),
        out_shape=jax.ShapeDtypeStruct((t, D_MODEL), F32),
        compiler_params=_params("parallel", "parallel"),
        name="cross",
    )(h, g, wq, kv, wo)


def _swiglu(xb, w13_ref, w2_ref):
    out = None
    for lo, hi in FF_SPLITS:
        gate = _dot(xb, w13_ref[:, lo:hi])
        up = _dot(xb, w13_ref[:, D_FF + lo:D_FF + hi])
        act = (gate * _sigmoid(gate) * up).astype(BF)
        part = _dot(act, w2_ref[lo:hi, :])
        out = part if out is None else out + part
    return out


def _ffn_kernel(h_ref, g_ref, w13_ref, w2_ref, out_ref):
    h = h_ref[...]
    out_ref[...] = h + _swiglu(_rms(h, g_ref[...]).astype(BF), w13_ref, w2_ref)


def _ffn(h, g, w13, w2, tm=512):
    t = h.shape[0]
    return pl.pallas_call(
        _ffn_kernel,
        grid=(t // tm,),
        in_specs=[pl.BlockSpec((tm, D_MODEL), lambda i: (i, 0)),
                  _const_spec((1, D_MODEL)),
                  _const_spec((D_MODEL, 2 * D_FF)),
                  _const_spec((D_FF, D_MODEL))],
        out_specs=pl.BlockSpec((tm, D_MODEL), lambda i: (i, 0)),
        out_shape=jax.ShapeDtypeStruct((t, D_MODEL), F32),
        compiler_params=_params("parallel"),
        name="ffn",
    )(h, g, w13, w2)


def _split_slabs(ref):
    rows = ref.shape[0] // ROW_SLABS
    return jnp.concatenate([ref[pl.ds(c, rows, stride=ROW_SLABS), :] for c in range(ROW_SLABS)], axis=1)


def _store_slabs(ref, x):
    rows = ref.shape[0] // ROW_SLABS
    for c in range(ROW_SLABS):
        ref[pl.ds(c, rows, stride=ROW_SLABS), :] = x[:, c * LANES:(c + 1) * LANES]


def _slab_spec(tm, index_map):
    return pl.BlockSpec((tm * ROW_SLABS, LANES), index_map)


def _router_kernel(h_ref, g_ref, whi_ref, wlo_ref, br_ref, hn_ref, idx_ref, wts_ref):
    hn = _rms(h_ref[...], g_ref[...])
    _store_slabs(hn_ref, hn)
    hi = hn.astype(BF)
    lo = (hn - hi.astype(F32)).astype(BF)
    logits = _dot(hi, whi_ref[...]) + (_dot(lo, whi_ref[...]) + _dot(hi, wlo_ref[...])) + br_ref[...]
    lane = lax.broadcasted_iota(jnp.int32, logits.shape, 1)
    logits = jnp.where(lane < N_EXPERTS, logits, NEG)
    v1 = jnp.max(logits, axis=-1, keepdims=True)
    i1 = jnp.min(jnp.where(logits == v1, lane, LANES), axis=-1, keepdims=True)
    rest = jnp.where(lane == i1, NEG, logits)
    v2 = jnp.max(rest, axis=-1, keepdims=True)
    i2 = jnp.min(jnp.where(rest == v2, lane, LANES), axis=-1, keepdims=True)
    e2 = jnp.exp(v2 - v1)
    w1 = 1.0 / (1.0 + e2)
    w2 = e2 / (1.0 + e2)
    idx_ref[...] = jnp.where(lane == 0, i1, jnp.where(lane == 1, i2, 0))
    wts_ref[...] = jnp.where(lane == 0, w1, jnp.where(lane == 1, w2, 0.0))


def _router(h, g, whi, wlo, br, chunk, n_chunks, tm=1024):
    t = h.shape[0] // n_chunks
    first = chunk * (t // tm)
    return pl.pallas_call(
        _router_kernel,
        grid=(t // tm,),
        in_specs=[pl.BlockSpec((tm, D_MODEL), lambda i: (first + i, 0)),
                  _const_spec((1, D_MODEL)),
                  _const_spec((D_MODEL, LANES)),
                  _const_spec((D_MODEL, LANES)),
                  _const_spec((1, LANES))],
        out_specs=[_slab_spec(tm, lambda i: (i, 0)),
                   pl.BlockSpec((tm, LANES), lambda i: (i, 0)),
                   pl.BlockSpec((tm, LANES), lambda i: (i, 0))],
        out_shape=[jax.ShapeDtypeStruct((t * ROW_SLABS, LANES), F32),
                   jax.ShapeDtypeStruct((t, LANES), jnp.int32),
                   jax.ShapeDtypeStruct((t, LANES), F32)],
        compiler_params=_params("parallel"),
        name="router",
    )(h, g, whi, wlo, br)


def _route_plan(idx, tm):
    t = idx.shape[0]
    n_pairs = 2 * t
    n_rows = n_pairs + N_EXPERTS * tm
    e_flat = jnp.concatenate([idx[:, 0], idx[:, 1]])
    onehot = (e_flat[:, None] == jnp.arange(N_EXPERTS, dtype=jnp.int32)[None, :]).astype(jnp.int32)
    csum = jnp.cumsum(onehot, axis=0)
    rank = jnp.sum(onehot * csum, axis=1) - 1
    counts = csum[-1]
    padded = ((counts + tm - 1) // tm) * tm
    ends = jnp.cumsum(padded)
    starts = ends - padded
    pos = starts[e_flat] + rank
    order = jnp.argsort(e_flat, stable=True).astype(jnp.int32)
    first = jnp.cumsum(counts) - counts
    r = jnp.minimum(jnp.arange(n_rows, dtype=jnp.int32), ends[-1] - 1)
    e_r = jnp.minimum(jnp.searchsorted(ends, r, side="right").astype(jnp.int32), N_EXPERTS - 1)
    local = r - starts[e_r]
    src_pair = order[jnp.clip(first[e_r] + local, 0, n_pairs - 1)]
    src_tok = jnp.where(local < counts[e_r], src_pair % t, 0).astype(jnp.int32)
    tile_expert = e_r[::tm]
    n_valid = (ends[-1] // tm).astype(jnp.int32).reshape(1)
    return pos.astype(jnp.int32), src_tok, tile_expert, n_valid


def _gather_rows(table, idx):
    n = idx.shape[0]
    info = plsc.get_sparse_core_info()
    n_workers = info.num_cores * info.num_subcores
    per_worker = n // n_workers
    assert per_worker * n_workers == n and per_worker % SC_CHUNK == 0, (n, n_workers)
    mesh = plsc.VectorSubcoreMesh(core_axis_name="c", subcore_axis_name="s")

    @functools.partial(
        pl.kernel, mesh=mesh,
        out_type=jax.ShapeDtypeStruct((n,) + table.shape[1:], table.dtype),
        scratch_types=[pltpu.VMEM((SC_CHUNK,), jnp.int32),
                       pltpu.VMEM((SC_CHUNK,) + table.shape[1:], table.dtype),
                       pltpu.SemaphoreType.DMA],
    )
    def gather(table_hbm, idx_hbm, out_hbm, idx_v, rows_v, sem):
        worker = lax.axis_index("s") * info.num_cores + lax.axis_index("c")
        base = worker * per_worker

        @pl.loop(0, per_worker // SC_CHUNK)
        def _(i):
            off = pl.multiple_of(base + i * SC_CHUNK, SC_CHUNK)
            pltpu.sync_copy(idx_hbm.at[pl.ds(off, SC_CHUNK)], idx_v)
            pltpu.async_copy(table_hbm.at[idx_v], rows_v, sem).wait()
            pltpu.sync_copy(rows_v, out_hbm.at[pl.ds(off, SC_CHUNK)])

    return gather(table, idx)


def _expert_ffn_kernel(te_ref, nv_ref, x_ref, w13_ref, w2_ref, y_ref):
    valid = pl.program_id(0) < nv_ref[0]

    @pl.when(valid)
    def _():
        _store_slabs(y_ref, _swiglu(_split_slabs(x_ref).astype(BF), w13_ref.at[0], w2_ref.at[0]))

    @pl.when(jnp.logical_not(valid))
    def _():
        y_ref[...] = jnp.zeros_like(y_ref)


def _expert_ffn(xs, tile_expert, n_valid, w13, w2, tm):
    n_rows = xs.shape[0] // ROW_SLABS
    last = lambda nv: jnp.maximum(nv[0] - 1, 0)
    grid_spec = pltpu.PrefetchScalarGridSpec(
        num_scalar_prefetch=2,
        grid=(n_rows // tm,),
        in_specs=[_slab_spec(tm, lambda i, te, nv: (jnp.minimum(i, last(nv)), 0)),
                  pl.BlockSpec((1, D_MODEL, 2 * D_FF), lambda i, te, nv: (te[i], 0, 0), pipeline_mode=pl.Buffered(1)),
                  pl.BlockSpec((1, D_FF, D_MODEL), lambda i, te, nv: (te[i], 0, 0), pipeline_mode=pl.Buffered(1))],
        out_specs=_slab_spec(tm, lambda i, te, nv: (i, 0)),
    )
    return pl.pallas_call(
        _expert_ffn_kernel,
        grid_spec=grid_spec,
        out_shape=jax.ShapeDtypeStruct((n_rows * ROW_SLABS, LANES), F32),
        compiler_params=_params("arbitrary"),
        name="expert_ffn",
    )(tile_expert, n_valid, xs, w13, w2)


def _combine_kernel(h_ref, y0_ref, y1_ref, wts_ref, gf_ref, *rest):
    out_ref = rest[-1]
    wts = wts_ref[...]
    tot = h_ref[...] + wts[:, 0:1] * _split_slabs(y0_ref) + wts[:, 1:2] * _split_slabs(y1_ref)
    out_ref[...] = _rms(tot, gf_ref[...])


def _combine(h, yg, wts, g_final, chunk, n_chunks, out_so_far, tm=1024):
    t = h.shape[0]
    nt = t // n_chunks // tm
    first = chunk * nt
    in_specs = [pl.BlockSpec((tm, D_MODEL), lambda i: (first + i, 0)),
                _slab_spec(tm, lambda i: (i, 0)),
                _slab_spec(tm, lambda i: (nt + i, 0)),
                pl.BlockSpec((tm, LANES), lambda i: (i, 0)),
                _const_spec((1, D_MODEL))]
    args = [h, yg, yg, wts, g_final]
    aliases = {}
    if out_so_far is not None:
        in_specs.append(pl.BlockSpec(memory_space=pl.ANY))
        args.append(out_so_far)
        aliases = {len(args) - 1: 0}
    return pl.pallas_call(
        _combine_kernel,
        grid=(nt,),
        in_specs=in_specs,
        out_specs=pl.BlockSpec((tm, D_MODEL), lambda i: (first + i, 0)),
        out_shape=jax.ShapeDtypeStruct((t, D_MODEL), F32),
        input_output_aliases=aliases,
        compiler_params=_params("parallel"),
        name="combine",
    )(*args)


def _moe(h, g, router_w, router_b, w13, w2, g_final, tm=512):
    wr = jnp.pad(router_w, ((0, 0), (0, LANES - N_EXPERTS)))
    whi = wr.astype(BF)
    wlo = (wr - whi.astype(F32)).astype(BF)
    br = jnp.pad(router_b, (0, LANES - N_EXPERTS)).reshape(1, LANES)
    as_rows = lambda a: a.reshape(-1, ROW_SLABS, LANES)
    as_slabs = lambda a: a.reshape(-1, LANES)
    chunks = range(MOE_CHUNKS)
    routed = [_router(h, g, whi, wlo, br, c, MOE_CHUNKS) for c in chunks]
    plans = [_route_plan(idx, tm) for _, idx, _ in routed]
    xs = [as_slabs(_gather_rows(as_rows(routed[c][0]), plans[c][1])) for c in chunks]
    ys = [_expert_ffn(xs[c], plans[c][2], plans[c][3], w13, w2, tm) for c in chunks]
    yg = [as_slabs(_gather_rows(as_rows(ys[c]), plans[c][0])) for c in chunks]
    out = None
    for c in chunks:
        out = _combine(h, yg[c], routed[c][2], g_final, c, MOE_CHUNKS, out)
    return out


def _feature_major(w, heads):
    wt = w.T.reshape(heads, HEAD_DIM, D_MODEL)
    return jnp.pad(wt, ((0, 0), (0, VT_ROWS - HEAD_DIM), (0, 0))).reshape(heads * VT_ROWS, D_MODEL)


def _pack_w_in(w_in):
    cuts = np.cumsum((512, 512, 512, 512, 512, 128, 128, 512, 512, 512, 8))[:-1].tolist()
    a_u, a_v, b_x, b_y, c_q, c_k, c_v, d_q, d_k, d_v, d_f = jnp.split(w_in, cuts, axis=-1)
    w = jnp.concatenate([b_x, b_y, d_q, d_k, c_k, c_v], axis=-1).astype(BF)
    wa = jnp.concatenate([a_u, a_v], axis=-1).astype(BF)
    wf = jnp.pad(d_f, ((0, 0), (0, LANES - FOX_HEADS))).astype(BF)
    wt = jnp.concatenate([_feature_major(d_v, FOX_HEADS), c_q.T, _feature_major(c_v, SWA_KV)], axis=0).astype(BF)
    return w, wa, wf, wt


def _fm_ones():
    ones = np.zeros((FM_ROWS, LANES), np.float32)
    for base, heads in ((0, FOX_HEADS), (VT_ALL + BRANCH_W, SWA_KV)):
        for h in range(heads):
            ones[base + h * VT_ROWS + HEAD_DIM, :] = 1.0
    return jnp.asarray(ones)


def _row(v):
    return v.reshape(1, -1)


def _hybrid_mixer(h, bsz, norm_mix, w_in, sgu_g, sgu_w, sgu_b, conv_w, conv_b, rg_wa, rg_ba, rg_wx, rg_bx,
                  rg_lambda, swa_sinks, fox_bf, w_branch, w_gate, b_gate, w_out):
    w, wa, wf, wt = _pack_w_in(w_in)
    proj, f, o_a, vt, sqt, svt = _inproj(h, _row(norm_mix), w, wa, wf, wt, _fm_ones(), _row(sgu_g), sgu_w, sgu_b.T)
    wax = jnp.concatenate([rg_wa, rg_wx], axis=-1).astype(BF)
    o_b = _rglru(proj, bsz, conv_w, _row(conv_b), wax, _row(rg_ba), _row(rg_bx), _row(rg_lambda))
    o_c = _swa(proj, sqt, svt, bsz, swa_sinks)
    bf_pad = jnp.pad(fox_bf, (0, LANES - FOX_HEADS)).reshape(1, LANES)
    qp, kp = _fox_prep(proj, f, bsz, bf_pad)
    o_d = _fox(qp, kp, vt, bsz)
    return _merge(h, _row(norm_mix), o_a, o_b, o_c, o_d, w_gate.astype(BF), b_gate[:, None, :],
                  w_branch.astype(BF), w_out.astype(BF))


def kernel(x, mem, norm_mix, w_in, sgu_g, sgu_w, sgu_b, conv_w, conv_b, rg_wa, rg_ba, rg_wx, rg_bx, rg_lambda, swa_sinks, fox_bf, w_branch, w_gate, b_gate, w_out, norm_cross, norm_mem, wq_c, wkv_c, wo_c, norm_ffn, dense_w13, dense_w2, router_w, router_b, moe_w13, moe_w2, norm_final):
    bsz, seq, d = x.shape
    m_len = mem.shape[1]
    depth = norm_mix.shape[0]
    assert depth == 2, "the final RMSNorm is fused into the routed layer, which must be the last one"
    h = x.reshape(bsz * seq, d)
    mem2 = mem.reshape(bsz * m_len, d)
    for l in range(depth):
        h = _hybrid_mixer(h, bsz, norm_mix[l], w_in[l], sgu_g[l], sgu_w[l], sgu_b[l], conv_w[l], conv_b[l],
                          rg_wa[l], rg_ba[l], rg_wx[l], rg_bx[l], rg_lambda[l], swa_sinks[l], fox_bf[l],
                          w_branch[l], w_gate[l], b_gate[l], w_out[l])
        kv = _memkv(mem2, _row(norm_mem[l]), wkv_c[l].astype(BF), m_len)
        h = _cross(h, _row(norm_cross[l]), wq_c[l].astype(BF), kv, wo_c[l].astype(BF), bsz, m_len)
        if l % 2 == 0:
            h = _ffn(h, _row(norm_ffn[l]), dense_w13[l // 2].astype(BF), dense_w2[l // 2].astype(BF))
        else:
            h = _moe(h, _row(norm_ffn[l]), router_w[l // 2], router_b[l // 2], moe_w13[l // 2].astype(BF),
                     moe_w2[l // 2].astype(BF), _row(norm_final))
    return h.reshape(bsz, seq, d)
```

```python
import functools
import math

import numpy as np
import jax
import jax.numpy as jnp
from jax import lax
from jax.experimental import pallas as pl
from jax.experimental.pallas import tpu as pltpu
from jax.experimental.pallas import tpu_sc as plsc

F32 = jnp.float32
BF = jnp.bfloat16

D_MODEL = 1024
BRANCH_W = 512
HEAD_DIM = 64
CHUNK = 128
SGU_GROUPS = 4
LRU_HEADS = 4
LRU_C = 8.0
CONV_W = 4
SWA_HEADS = 8
SWA_KV = 2
FOX_HEADS = 8
FOX_PAIRS = FOX_HEADS // 2
X_HEADS = 4
X_HEAD_DIM = 128
D_FF = 2816
N_EXPERTS = 8
EPS = 1e-6
LOG2E = math.log2(math.e)
NEG = -1e30
LANES = 128
ROW_SLABS = D_MODEL // LANES
SC_CHUNK = 64
MOE_CHUNKS = 2
VMEM_LIMIT = 56 * 1024 * 1024
CAST_TILE_BYTES = 6 * 1024 * 1024

COL_BX, COL_BY, COL_DQ, COL_DK = range(4)
N_WIDE = 4
N_PROJ = N_WIDE * BRANCH_W + 2 * SWA_KV * HEAD_DIM
VT_ROWS = 80
VT_ALL = FOX_HEADS * VT_ROWS
SWA_VT = SWA_KV * VT_ROWS
FM_ROWS = VT_ALL + BRANCH_W + SWA_VT
FOX_UNROLL = 4
FOX_STEP_PAIRS = 1
FF_SPLITS = ((0, 1536), (1536, D_FF))


def _rms(x, g):
    return x * lax.rsqrt(jnp.mean(x * x, axis=-1, keepdims=True) + EPS) * g


def _sigmoid(x):
    return 1.0 / (1.0 + jnp.exp(-x))


def _gelu(x):
    return 0.5 * x * (1.0 + jnp.tanh(math.sqrt(2.0 / math.pi) * (x + 0.044715 * (x * x * x))))


def _dot(a, b):
    return jnp.dot(a, b, preferred_element_type=F32)


def _dot_nt(a, b):
    return lax.dot_general(a, b, (((1,), (1,)), ((), ())), preferred_element_type=F32)


def _shift_rows(x, d, fill):
    row = lax.broadcasted_iota(jnp.int32, x.shape, 0)
    return jnp.where(row >= d, pltpu.roll(x, d, 0), fill)


def _params(*sem):
    return pltpu.CompilerParams(dimension_semantics=sem, vmem_limit_bytes=VMEM_LIMIT)


def _const_spec(shape):
    nd = len(shape)
    return pl.BlockSpec(shape, lambda *_: (0,) * nd, pipeline_mode=pl.Buffered(1))


def _inproj_kernel(h_ref, g_ref, w_ref, wa_ref, wf_ref, wt_ref, ones_ref, sg_ref, sw_ref, sbt_ref,
                   proj_ref, f_ref, oa_ref, vt_ref, sqt_ref, svt_ref):
    xn = _rms(h_ref[...], g_ref[...]).astype(BF)
    tm = xn.shape[0]
    n_col = w_ref.shape[1]
    for c in range(0, n_col, BRANCH_W):
        w = min(BRANCH_W, n_col - c)
        proj_ref[:, c:c + w] = _dot(xn, w_ref[:, c:c + w]).astype(BF)
    f_ref[...] = _dot(xn, wf_ref[...])
    ones = jnp.concatenate([ones_ref[...]] * (tm // LANES), axis=1)
    fm = (_dot_nt(wt_ref[...], xn) + ones).astype(BF)
    vt_ref[...] = fm[:VT_ALL]
    sqt_ref[...] = fm[VT_ALL:VT_ALL + BRANCH_W]
    svt_ref[...] = fm[VT_ALL + BRANCH_W:]

    u = _gelu(_dot(xn, wa_ref[:, :BRANCH_W]))
    v = _gelu(_dot(xn, wa_ref[:, BRANCH_W:]))
    vn = _rms(v, sg_ref[...]).astype(BF)
    row = lax.broadcasted_iota(jnp.int32, (CHUNK, CHUNK), 0)
    col = lax.broadcasted_iota(jnp.int32, (CHUNK, CHUNK), 1)
    gw = BRANCH_W // SGU_GROUPS
    for g in range(SGU_GROUPS):
        wg = jnp.where(col <= row, sw_ref[g], 0.0).astype(BF)
        bg = sbt_ref[:, g:g + 1]
        for c in range(tm // CHUNK):
            rs = slice(c * CHUNK, (c + 1) * CHUNK)
            cs = slice(g * gw, (g + 1) * gw)
            mixed = _dot(wg, vn[rs, cs]) + bg
            oa_ref[rs, cs] = (u[rs, cs] * mixed).astype(BF)


def _inproj(h, g, w, wa, wf, wt, ones, sgu_g, sgu_w, sgu_bt, tm=512):
    t = h.shape[0]
    row = lambda i: (i, 0)
    col = lambda i: (0, i)
    return pl.pallas_call(
        _inproj_kernel,
        grid=(t // tm,),
        in_specs=[pl.BlockSpec((tm, D_MODEL), row),
                  _const_spec((1, D_MODEL)),
                  _const_spec((D_MODEL, N_PROJ)),
                  _const_spec((D_MODEL, 2 * BRANCH_W)),
                  _const_spec((D_MODEL, LANES)),
                  _const_spec((FM_ROWS, D_MODEL)),
                  _const_spec((FM_ROWS, LANES)),
                  _const_spec((1, BRANCH_W)),
                  _const_spec((SGU_GROUPS, CHUNK, CHUNK)),
                  _const_spec((CHUNK, SGU_GROUPS))],
        out_specs=[pl.BlockSpec((tm, N_PROJ), row),
                   pl.BlockSpec((tm, LANES), row),
                   pl.BlockSpec((tm, BRANCH_W), row),
                   pl.BlockSpec((VT_ALL, tm), col),
                   pl.BlockSpec((BRANCH_W, tm), col),
                   pl.BlockSpec((SWA_VT, tm), col)],
        out_shape=[jax.ShapeDtypeStruct((t, N_PROJ), BF), jax.ShapeDtypeStruct((t, LANES), F32),
                   jax.ShapeDtypeStruct((t, BRANCH_W), BF), jax.ShapeDtypeStruct((VT_ALL, t), BF),
                   jax.ShapeDtypeStruct((BRANCH_W, t), BF), jax.ShapeDtypeStruct((SWA_VT, t), BF)],
        compiler_params=_params("parallel"),
        name="inproj",
    )(h, g, w, wa, wf, wt, ones, sgu_g, sgu_w, sgu_bt)


def _rglru_kernel(x_ref, y_ref, cw_ref, cb_ref, wax_ref, ba_ref, bx_ref, lam_ref, o_ref, xs_ref, hc_ref):
    ts = x_ref.shape[0]
    hw = BRANCH_W // LRU_HEADS

    @pl.when(pl.program_id(1) == 0)
    def _():
        xs_ref[0:8, :] = jnp.zeros((8, BRANCH_W), F32)
        hc_ref[...] = jnp.zeros_like(hc_ref)

    x = x_ref[...].astype(F32)
    xs_ref[8:8 + ts, :] = x
    cw = cw_ref[...]
    xc = cb_ref[...] + cw[CONV_W - 1:CONV_W] * x
    for k in range(1, CONV_W):
        xc = xc + cw[CONV_W - 1 - k:CONV_W - k] * xs_ref[8 - k:8 - k + ts, :]
    xs_ref[0:8, :] = x[ts - 8:ts, :]

    xcb = xc.astype(BF)
    r_parts, i_parts = [], []
    for hd in range(LRU_HEADS):
        z = _dot(xcb[:, hd * hw:(hd + 1) * hw], wax_ref[hd])
        r_parts.append(z[:, :hw])
        i_parts.append(z[:, hw:])
    r = _sigmoid(jnp.concatenate(r_parts, axis=1) + ba_ref[...])
    gi = _sigmoid(jnp.concatenate(i_parts, axis=1) + bx_ref[...])
    nl = -lam_ref[...]
    softplus = jnp.maximum(nl, 0.0) + jnp.log1p(jnp.exp(-jnp.abs(nl)))
    log_a = (-LRU_C * r) * softplus
    a = jnp.exp(log_a)
    th = jnp.tanh(log_a)
    e2 = -2.0 * th / (1.0 - th)
    b = (xc * gi) * (e2 * lax.rsqrt(jnp.maximum(e2, 1e-30)))

    sub = lax.broadcasted_iota(jnp.int32, a.shape, 0) & 7
    for d in (1, 2, 4):
        keep = sub >= d
        a_sh = jnp.where(keep, pltpu.roll(a, d, 0), 1.0)
        b_sh = jnp.where(keep, pltpu.roll(b, d, 0), 0.0)
        b = b + a * b_sh
        a = a * a_sh
    carry = hc_ref[...]
    groups = []
    for g in range(ts // 8):
        hg = a[8 * g:8 * g + 8] * carry + b[8 * g:8 * g + 8]
        groups.append(hg)
        carry = hg[7:8]
    h = jnp.concatenate(groups, axis=0)
    hc_ref[...] = carry
    o_ref[...] = (h * _gelu(y_ref[...].astype(F32))).astype(BF)


def _rglru(proj, bsz, cw, cb, wax, ba, bx, lam, ts=512):
    t = proj.shape[0]
    ns = t // bsz // ts
    return pl.pallas_call(
        _rglru_kernel,
        grid=(bsz, ns),
        in_specs=[pl.BlockSpec((ts, BRANCH_W), lambda b, j: (b * ns + j, COL_BX)),
                  pl.BlockSpec((ts, BRANCH_W), lambda b, j: (b * ns + j, COL_BY)),
                  _const_spec((CONV_W, BRANCH_W)),
                  _const_spec((1, BRANCH_W)),
                  _const_spec((LRU_HEADS, BRANCH_W // LRU_HEADS, 2 * BRANCH_W // LRU_HEADS)),
                  _const_spec((1, BRANCH_W)),
                  _const_spec((1, BRANCH_W)),
                  _const_spec((1, BRANCH_W))],
        out_specs=pl.BlockSpec((ts, BRANCH_W), lambda b, j: (b * ns + j, 0)),
        out_shape=jax.ShapeDtypeStruct((t, BRANCH_W), BF),
        scratch_shapes=[pltpu.VMEM((ts + 8, BRANCH_W), F32), pltpu.VMEM((1, BRANCH_W), F32)],
        compiler_params=_params("parallel", "arbitrary"),
        name="rglru",
    )(proj, proj, cw, cb, wax, ba, bx, lam)


def _swa_tables(sinks):
    t_idx = np.arange(CHUNK)[None, :] + CHUNK
    dist = (t_idx - np.arange(2 * CHUNK)[:, None]).astype(np.float32)
    in_win = (dist >= 0) & (dist < CHUNK)
    slopes = 2.0 ** (-(8.0 / SWA_HEADS) * np.arange(1, SWA_HEADS + 1, dtype=np.float32))
    bias = np.where(in_win[None], -slopes[:, None, None] * dist[None] * LOG2E, NEG)
    grp = SWA_HEADS // SWA_KV
    bias = bias.reshape(SWA_KV, grp, 2 * CHUNK, CHUNK).transpose(0, 2, 1, 3).reshape(SWA_KV, 2 * CHUNK, grp * CHUNK)
    sink = jnp.repeat(sinks.astype(F32) * LOG2E, CHUNK).reshape(SWA_KV, 1, grp * CHUNK)
    return jnp.asarray(bias, F32), sink


def _swa_kernel(qt_ref, kv_ref, kvp_ref, vt_ref, vtp_ref, bias_ref, sink_ref, o_ref):
    tq = kv_ref.shape[0]
    grp = SWA_HEADS // SWA_KV
    first_key = jnp.where(pl.program_id(1) == 0, CHUNK, 0)
    k_all = jnp.concatenate([kvp_ref[...], kv_ref[...]], axis=0)
    vt_all = jnp.concatenate([vtp_ref[...], vt_ref[...]], axis=1)
    qt = (qt_ref[...].astype(F32) * (HEAD_DIM ** -0.5 * LOG2E)).astype(BF)
    key_row = lax.broadcasted_iota(jnp.int32, (2 * CHUNK, grp * CHUNK), 0)
    kw = 2 * SWA_KV * HEAD_DIM
    for qb in range(tq // CHUNK):
        cols = slice(qb * CHUNK, (qb + 1) * CHUNK)
        kb = k_all[qb * CHUNK:(qb + 2) * CHUNK]
        vtb = vt_all[:, qb * CHUNK:(qb + 2) * CHUNK]
        outs = []
        for kh in range(SWA_KV):
            qg = jnp.concatenate([qt[(kh * grp + g) * HEAD_DIM:(kh * grp + g + 1) * HEAD_DIM, cols]
                                  for g in range(grp)], axis=1)
            parts = []
            if kh > 0:
                parts.append(jnp.zeros((kh * HEAD_DIM, grp * CHUNK), BF))
            parts.append(qg)
            parts.append(jnp.zeros((kw - (kh + 1) * HEAD_DIM, grp * CHUNK), BF))
            st = _dot(kb, jnp.concatenate(parts, axis=0)) + bias_ref[kh]
            if qb == 0:
                st = jnp.where(key_row >= first_key, st, NEG)
            sink = sink_ref[kh]
            m = jnp.maximum(jnp.max(st, axis=0, keepdims=True), sink)
            pt = jnp.exp2(st - m).astype(BF)
            acc = _dot(vtb[kh * VT_ROWS:(kh + 1) * VT_ROWS], pt)
            ot = acc[:HEAD_DIM] / (acc[HEAD_DIM:HEAD_DIM + 1] + jnp.exp2(sink - m))
            outs.extend(ot[:, g * CHUNK:(g + 1) * CHUNK] for g in range(grp))
        o_ref[cols, :] = jnp.concatenate(outs, axis=0).T.astype(BF)


def _swa(proj, sqt, svt, bsz, sinks, tq=1024):
    t = proj.shape[0]
    seq = t // bsz
    nq = seq // tq
    per = tq // CHUNK
    kv_blk = N_WIDE * BRANCH_W // (2 * SWA_KV * HEAD_DIM)
    prev = lambda b, i: b * (seq // CHUNK) + jnp.maximum(i * per - 1, 0)
    bias, sink = _swa_tables(sinks)
    grp = SWA_HEADS // SWA_KV
    return pl.pallas_call(
        _swa_kernel,
        grid=(bsz, nq),
        in_specs=[pl.BlockSpec((BRANCH_W, tq), lambda b, i: (0, b * nq + i)),
                  pl.BlockSpec((tq, 2 * SWA_KV * HEAD_DIM), lambda b, i: (b * nq + i, kv_blk)),
                  pl.BlockSpec((CHUNK, 2 * SWA_KV * HEAD_DIM), lambda b, i: (prev(b, i), kv_blk)),
                  pl.BlockSpec((SWA_VT, tq), lambda b, i: (0, b * nq + i)),
                  pl.BlockSpec((SWA_VT, CHUNK), lambda b, i: (0, prev(b, i))),
                  _const_spec((SWA_KV, 2 * CHUNK, grp * CHUNK)),
                  _const_spec((SWA_KV, 1, grp * CHUNK))],
        out_specs=pl.BlockSpec((tq, BRANCH_W), lambda b, i: (b * nq + i, 0)),
        out_shape=jax.ShapeDtypeStruct((t, BRANCH_W), BF),
        compiler_params=_params("parallel", "parallel"),
        name="swa",
    )(sqt, proj, proj, svt, svt, bias, sink)


def _aug_tables():
    eq = np.zeros((3, LANES, FOX_PAIRS * LANES), np.float32)
    ek = np.zeros((3, LANES, FOX_PAIRS * LANES), np.float32)
    oq = np.zeros((1, FOX_PAIRS * LANES), np.float32)
    ok = np.zeros((1, FOX_PAIRS * LANES), np.float32)
    for h in range(FOX_HEADS):
        base = (h // 2) * LANES + 6 * (h % 2)
        for s in range(3):
            eq[s, h, base + s] = 1.0
            ek[s, h, base + 3 + s] = -1.0
            ok[0, base + s] = 1.0
            oq[0, base + 3 + s] = 1.0
    return eq, ek, oq, ok


def _fox_prep_kernel(q_ref, k_ref, f_ref, bf_ref, eq_ref, ek_ref, oq_ref, ok_ref, qp_ref, kp_ref, cum_ref):
    ts = q_ref.shape[0]

    @pl.when(pl.program_id(1) == 0)
    def _():
        cum_ref[...] = jnp.zeros_like(cum_ref)

    z = f_ref[...] + bf_ref[...]
    c = jnp.minimum(z, 0.0) - jnp.log1p(jnp.exp(-jnp.abs(z)))
    d = 1
    while d < ts:
        c = c + _shift_rows(c, d, 0.0)
        d *= 2
    c = c + cum_ref[...]
    cum_ref[...] = c[ts - 1:ts, :]
    c = c * LOG2E
    c1 = c.astype(BF)
    r1 = c - c1.astype(F32)
    c2 = r1.astype(BF)
    c3 = (r1 - c2.astype(F32)).astype(BF)
    augq = _dot(c1, eq_ref[0]) + _dot(c2, eq_ref[1]) + _dot(c3, eq_ref[2]) + oq_ref[...]
    augk = _dot(c1, ek_ref[0]) + _dot(c2, ek_ref[1]) + _dot(c3, ek_ref[2]) + ok_ref[...]
    qt = (q_ref[...].astype(F32) * (HEAD_DIM ** -0.5 * LOG2E)).T.astype(BF)
    augqt = augq.T.astype(BF)
    k = k_ref[...]
    for p in range(FOX_PAIRS):
        qp_ref[2 * p * LANES:(2 * p + 1) * LANES, :] = qt[p * LANES:(p + 1) * LANES]
        qp_ref[(2 * p + 1) * LANES:(2 * p + 2) * LANES, :] = augqt[p * LANES:(p + 1) * LANES]
        kp_ref[:, 2 * p * LANES:(2 * p + 1) * LANES] = k[:, p * LANES:(p + 1) * LANES]
        kp_ref[:, (2 * p + 1) * LANES:(2 * p + 2) * LANES] = augk[:, p * LANES:(p + 1) * LANES].astype(BF)


def _fox_prep(proj, f, bsz, bf_pad, ts=1024):
    t = proj.shape[0]
    ns = t // bsz // ts
    eq, ek, oq, ok = _aug_tables()
    wide = 2 * FOX_PAIRS * LANES
    return pl.pallas_call(
        _fox_prep_kernel,
        grid=(bsz, ns),
        in_specs=[pl.BlockSpec((ts, BRANCH_W), lambda b, j: (b * ns + j, COL_DQ)),
                  pl.BlockSpec((ts, BRANCH_W), lambda b, j: (b * ns + j, COL_DK)),
                  pl.BlockSpec((ts, LANES), lambda b, j: (b * ns + j, 0)),
                  _const_spec((1, LANES)),
                  _const_spec(eq.shape), _const_spec(ek.shape), _const_spec(oq.shape), _const_spec(ok.shape)],
        out_specs=[pl.BlockSpec((wide, ts), lambda b, j: (0, b * ns + j)),
                   pl.BlockSpec((ts, wide), lambda b, j: (b * ns + j, 0))],
        out_shape=[jax.ShapeDtypeStruct((wide, t), BF), jax.ShapeDtypeStruct((t, wide), BF)],
        scratch_shapes=[pltpu.VMEM((1, LANES), F32)],
        compiler_params=_params("parallel", "arbitrary"),
        name="fox_prep",
    )(proj, proj, f, bf_pad, jnp.asarray(eq, BF), jnp.asarray(ek, BF), jnp.asarray(oq), jnp.asarray(ok))


def _fox_kernel(q_ref, k_ref, vt_ref, o_ref, st_ref, pt_ref, al_ref, m_ref, acc_ref):
    tq = q_ref.shape[1]
    tk = tq // 2
    n_heads = m_ref.shape[0]
    qi = pl.program_id(2)
    row = lax.broadcasted_iota(jnp.int32, (2 * LANES, tq), 0)
    in0 = (row < HEAD_DIM) | ((row >= LANES) & (row < LANES + 6))
    in1 = ((row >= HEAD_DIM) & (row < LANES)) | ((row >= LANES + 6) & (row < LANES + 12))
    qh = []
    for p in range(n_heads // 2):
        q = q_ref[2 * p * LANES:(2 * p + 2) * LANES, :]
        zero = jnp.zeros_like(q)
        qh += [jnp.where(in0, q, zero), jnp.where(in1, q, zero)]
    for h in range(n_heads):
        m_ref[h] = jnp.full(m_ref.shape[1:], NEG, F32)
        acc_ref[h] = jnp.zeros(acc_ref.shape[1:], F32)
        pt_ref[1, h] = jnp.zeros(pt_ref.shape[2:], BF)
        al_ref[1, h] = jnp.ones(al_ref.shape[2:], F32)

    def scores(t, slot):
        rows = pl.ds(pl.multiple_of(t * tk, tk), tk)
        for p in range(n_heads // 2):
            kp = k_ref[rows, 2 * p * LANES:(2 * p + 2) * LANES]
            for h in (2 * p, 2 * p + 1):
                st_ref[slot, h] = _dot(kp, qh[h])

    def numerators(slot, key_offset):
        for h in range(n_heads):
            st = st_ref[slot, h]
            if key_offset is not None:
                key = lax.broadcasted_iota(jnp.int32, (tk, tq), 0) + key_offset
                qry = lax.broadcasted_iota(jnp.int32, (tk, tq), 1)
                st = jnp.where(key <= qry, st, NEG)
            m_old = m_ref[h]
            m_new = jnp.maximum(m_old, jnp.max(st, axis=0, keepdims=True))
            al_ref[slot, h] = jnp.exp2(m_old - m_new)
            pt_ref[slot, h] = jnp.exp2(st - m_new).astype(BF)
            m_ref[h] = m_new

    def accumulate(t, slot):
        cols = pl.ds(pl.multiple_of(t * tk, tk), tk)
        for h in range(n_heads):
            vt = vt_ref[h * VT_ROWS:(h + 1) * VT_ROWS, cols]
            acc_ref[h] = al_ref[slot, h] * acc_ref[h] + _dot(vt, pt_ref[slot, h])

    def tile_pair(j, diagonal):
        scores(2 * j + 1, 1)
        accumulate(jnp.maximum(2 * j - 1, 0), 1)
        numerators(0, 0 if diagonal else None)
        if not diagonal:
            scores(2 * j + 2, 0)
        accumulate(2 * j, 0)
        numerators(1, tk if diagonal else None)

    def body(jj, carry):
        for u in range(FOX_UNROLL):
            tile_pair(FOX_UNROLL * jj + u, False)
        return carry

    def remainder(j, carry):
        tile_pair(j, False)
        return carry

    scores(0, 0)
    lax.fori_loop(0, qi // FOX_UNROLL, body, 0)
    lax.fori_loop((qi // FOX_UNROLL) * FOX_UNROLL, qi, remainder, 0)
    tile_pair(qi, True)
    accumulate(2 * qi + 1, 1)
    outs = []
    for h in range(n_heads):
        acc = acc_ref[h]
        outs.append(acc[:HEAD_DIM] / acc[HEAD_DIM:HEAD_DIM + 1])
    o_ref[...] = jnp.concatenate(outs, axis=0).T.astype(BF)


def _fox(qp, kp, vt, bsz, tq=512):
    t = kp.shape[0]
    seq = t // bsz
    nq = seq // tq
    np_ = FOX_STEP_PAIRS
    nh = 2 * np_
    return pl.pallas_call(
        _fox_kernel,
        grid=(bsz, FOX_PAIRS // np_, nq),
        in_specs=[pl.BlockSpec((np_ * 2 * LANES, tq), lambda b, p, i: (p, b * nq + i)),
                  pl.BlockSpec((seq, np_ * 2 * LANES), lambda b, p, i: (b, p)),
                  pl.BlockSpec((nh * VT_ROWS, seq), lambda b, p, i: (p, b))],
        out_specs=pl.BlockSpec((tq, np_ * LANES), lambda b, p, i: (b * nq + i, p)),
        out_shape=jax.ShapeDtypeStruct((t, BRANCH_W), BF),
        scratch_shapes=[pltpu.VMEM((2, nh, tq // 2, tq), F32),
                        pltpu.VMEM((2, nh, tq // 2, tq), BF),
                        pltpu.VMEM((2, nh, 1, tq), F32),
                        pltpu.VMEM((nh, 1, tq), F32),
                        pltpu.VMEM((nh, VT_ROWS, tq), F32)],
        compiler_params=_params("parallel", "parallel", "arbitrary"),
        name="fox",
    )(qp, kp, vt)


def _merge_kernel(h_ref, g_ref, oa_ref, ob_ref, oc_ref, od_ref, wg_ref, bg_ref, wb_ref, wo_ref, out_ref):
    h = h_ref[...]
    xn = _rms(h, g_ref[...]).astype(BF)
    merged = None
    for br, o_ref in enumerate((oa_ref, ob_ref, oc_ref, od_ref)):
        gate = _sigmoid(_dot(xn, wg_ref[br]) + bg_ref[br])
        term = gate * _dot(o_ref[...], wb_ref[br])
        merged = term if merged is None else merged + term
    out_ref[...] = h + _dot(merged.astype(BF), wo_ref[...])


def _merge(h, g, oa, ob, oc, od, wg, bg, wb, wo, tm=512):
    t = h.shape[0]
    row = lambda i: (i, 0)
    return pl.pallas_call(
        _merge_kernel,
        grid=(t // tm,),
        in_specs=[pl.BlockSpec((tm, D_MODEL), row),
                  _const_spec((1, D_MODEL)),
                  pl.BlockSpec((tm, BRANCH_W), row), pl.BlockSpec((tm, BRANCH_W), row),
                  pl.BlockSpec((tm, BRANCH_W), row), pl.BlockSpec((tm, BRANCH_W), row),
                  _const_spec((4, D_MODEL, D_MODEL)),
                  _const_spec((4, 1, D_MODEL)),
                  _const_spec((4, BRANCH_W, D_MODEL)),
                  _const_spec((D_MODEL, D_MODEL))],
        out_specs=pl.BlockSpec((tm, D_MODEL), row),
        out_shape=jax.ShapeDtypeStruct((t, D_MODEL), F32),
        compiler_params=_params("parallel"),
        name="merge",
    )(h, g, oa, ob, oc, od, wg, bg, wb, wo)


def _memkv_kernel(mem_ref, g_ref, w_ref, kv_ref):
    mn = _rms(mem_ref[...], g_ref[...]).astype(BF)
    kv_ref[...] = _dot(mn, w_ref[...]).astype(BF)


def _memkv(mem2, g, w, m_len):
    n = mem2.shape[0]
    width = 2 * X_HEADS * X_HEAD_DIM
    return pl.pallas_call(
        _memkv_kernel,
        grid=(n // m_len,),
        in_specs=[pl.BlockSpec((m_len, D_MODEL), lambda b: (b, 0)),
                  _const_spec((1, D_MODEL)),
                  _const_spec((D_MODEL, width))],
        out_specs=pl.BlockSpec((m_len, width), lambda b: (b, 0)),
        out_shape=jax.ShapeDtypeStruct((n, width), BF),
        compiler_params=_params("parallel"),
        name="memkv",
    )(mem2, g, w)


def _cross_kernel(h_ref, g_ref, wq_ref, kv_ref, wo_ref, out_ref):
    h = h_ref[...]
    hn = _rms(h, g_ref[...]).astype(BF)
    q = _dot(hn, wq_ref[...]).astype(BF)
    kv = kv_ref[...]
    width = X_HEADS * X_HEAD_DIM
    outs = []
    for hd in range(X_HEADS):
        cs = slice(hd * X_HEAD_DIM, (hd + 1) * X_HEAD_DIM)
        s = _dot_nt(q[:, cs], kv[:, cs]) * (X_HEAD_DIM ** -0.5)
        m = jnp.max(s, axis=-1, keepdims=True)
        p = jnp.exp(s - m)
        denom = jnp.sum(p, axis=-1, keepdims=True)
        v = kv[:, width + hd * X_HEAD_DIM:width + (hd + 1) * X_HEAD_DIM]
        outs.append((_dot(p.astype(BF), v) / denom).astype(BF))
    o = jnp.concatenate(outs, axis=1)
    out_ref[...] = h + _dot(o, wo_ref[...])


def _cross(h, g, wq, kv, wo, bsz, m_len, tm=1024):
    t = h.shape[0]
    per = t // bsz // tm
    width = X_HEADS * X_HEAD_DIM
    return pl.pallas_call(
        _cross_kernel,
        grid=(bsz, per),
        in_specs=[pl.BlockSpec((tm, D_MODEL), lambda b, i: (b * per + i, 0)),
                  _const_spec((1, D_MODEL)),
                  _const_spec((D_MODEL, width)),
                  pl.BlockSpec((m_len, 2 * width), lambda b, i: (b, 0)),
                  _const_spec((width, D_MODEL))],
        out_specs=pl.BlockSpec((tm, D_MODEL), lambda b, i: (b * per + i, 0)),
        out_shape=jax.ShapeDtypeStruct((t, D_MODEL), F32),
        compiler_params=_params("parallel", "parallel"),
        name="cross",
    )(h, g, wq, kv, wo)


def _swiglu(xb, w13_ref, w2_ref):
    out = None
    for lo, hi in FF_SPLITS:
        gate = _dot(xb, w13_ref[:, lo:hi])
        up = _dot(xb, w13_ref[:, D_FF + lo:D_FF + hi])
        act = (gate * _sigmoid(gate) * up).astype(BF)
        part = _dot(act, w2_ref[lo:hi, :])
        out = part if out is None else out + part
    return out


def _ffn_kernel(h_ref, g_ref, w13_ref, w2_ref, out_ref):
    h = h_ref[...]
    out_ref[...] = h + _swiglu(_rms(h, g_ref[...]).astype(BF), w13_ref, w2_ref)


def _ffn(h, g, w13, w2, tm=512):
    t = h.shape[0]
    return pl.pallas_call(
        _ffn_kernel,
        grid=(t // tm,),
        in_specs=[pl.BlockSpec((tm, D_MODEL), lambda i: (i, 0)),
                  _const_spec((1, D_MODEL)),
                  _const_spec((D_MODEL, 2 * D_FF)),
                  _const_spec((D_FF, D_MODEL))],
        out_specs=pl.BlockSpec((tm, D_MODEL), lambda i: (i, 0)),
        out_shape=jax.ShapeDtypeStruct((t, D_MODEL), F32),
        compiler_params=_params("parallel"),
        name="ffn",
    )(h, g, w13, w2)


def _split_slabs(ref):
    rows = ref.shape[0] // ROW_SLABS
    return jnp.concatenate([ref[pl.ds(c, rows, stride=ROW_SLABS), :] for c in range(ROW_SLABS)], axis=1)


def _store_slabs(ref, x):
    rows = ref.shape[0] // ROW_SLABS
    for c in range(ROW_SLABS):
        ref[pl.ds(c, rows, stride=ROW_SLABS), :] = x[:, c * LANES:(c + 1) * LANES]


def _slab_spec(tm, index_map):
    return pl.BlockSpec((tm * ROW_SLABS, LANES), index_map)


def _router_kernel(h_ref, g_ref, whi_ref, wlo_ref, br_ref, hn_ref, idx_ref, wts_ref):
    hn = _rms(h_ref[...], g_ref[...])
    _store_slabs(hn_ref, hn)
    hi = hn.astype(BF)
    lo = (hn - hi.astype(F32)).astype(BF)
    logits = _dot(hi, whi_ref[...]) + (_dot(lo, whi_ref[...]) + _dot(hi, wlo_ref[...])) + br_ref[...]
    lane = lax.broadcasted_iota(jnp.int32, logits.shape, 1)
    logits = jnp.where(lane < N_EXPERTS, logits, NEG)
    v1 = jnp.max(logits, axis=-1, keepdims=True)
    i1 = jnp.min(jnp.where(logits == v1, lane, LANES), axis=-1, keepdims=True)
    rest = jnp.where(lane == i1, NEG, logits)
    v2 = jnp.max(rest, axis=-1, keepdims=True)
    i2 = jnp.min(jnp.where(rest == v2, lane, LANES), axis=-1, keepdims=True)
    e2 = jnp.exp(v2 - v1)
    w1 = 1.0 / (1.0 + e2)
    w2 = e2 / (1.0 + e2)
    idx_ref[...] = jnp.where(lane == 0, i1, jnp.where(lane == 1, i2, 0))
    wts_ref[...] = jnp.where(lane == 0, w1, jnp.where(lane == 1, w2, 0.0))


def _router(h, g, whi, wlo, br, chunk, n_chunks, tm=1024):
    t = h.shape[0] // n_chunks
    first = chunk * (t // tm)
    return pl.pallas_call(
        _router_kernel,
        grid=(t // tm,),
        in_specs=[pl.BlockSpec((tm, D_MODEL), lambda i: (first + i, 0)),
                  _const_spec((1, D_MODEL)),
                  _const_spec((D_MODEL, LANES)),
                  _const_spec((D_MODEL, LANES)),
                  _const_spec((1, LANES))],
        out_specs=[_slab_spec(tm, lambda i: (i, 0)),
                   pl.BlockSpec((tm, LANES), lambda i: (i, 0)),
                   pl.BlockSpec((tm, LANES), lambda i: (i, 0))],
        out_shape=[jax.ShapeDtypeStruct((t * ROW_SLABS, LANES), F32),
                   jax.ShapeDtypeStruct((t, LANES), jnp.int32),
                   jax.ShapeDtypeStruct((t, LANES), F32)],
        compiler_params=_params("parallel"),
        name="router",
    )(h, g, whi, wlo, br)


def _route_plan(idx, tm):
    t = idx.shape[0]
    n_pairs = 2 * t
    n_rows = n_pairs + N_EXPERTS * tm
    e_flat = jnp.concatenate([idx[:, 0], idx[:, 1]])
    onehot = (e_flat[:, None] == jnp.arange(N_EXPERTS, dtype=jnp.int32)[None, :]).astype(jnp.int32)
    csum = jnp.cumsum(onehot, axis=0)
    rank = jnp.sum(onehot * csum, axis=1) - 1
    counts = csum[-1]
    padded = ((counts + tm - 1) // tm) * tm
    ends = jnp.cumsum(padded)
    starts = ends - padded
    pos = starts[e_flat] + rank
    order = jnp.argsort(e_flat, stable=True).astype(jnp.int32)
    first = jnp.cumsum(counts) - counts
    r = jnp.minimum(jnp.arange(n_rows, dtype=jnp.int32), ends[-1] - 1)
    e_r = jnp.minimum(jnp.searchsorted(ends, r, side="right").astype(jnp.int32), N_EXPERTS - 1)
    local = r - starts[e_r]
    src_pair = order[jnp.clip(first[e_r] + local, 0, n_pairs - 1)]
    src_tok = jnp.where(local < counts[e_r], src_pair % t, 0).astype(jnp.int32)
    tile_expert = e_r[::tm]
    n_valid = (ends[-1] // tm).astype(jnp.int32).reshape(1)
    return pos.astype(jnp.int32), src_tok, tile_expert, n_valid


def _gather_rows(table, idx):
    n = idx.shape[0]
    info = plsc.get_sparse_core_info()
    n_workers = info.num_cores * info.num_subcores
    per_worker = n // n_workers
    assert per_worker * n_workers == n and per_worker % SC_CHUNK == 0, (n, n_workers)
    mesh = plsc.VectorSubcoreMesh(core_axis_name="c", subcore_axis_name="s")

    @functools.partial(
        pl.kernel, mesh=mesh,
        out_type=jax.ShapeDtypeStruct((n,) + table.shape[1:], table.dtype),
        scratch_types=[pltpu.VMEM((SC_CHUNK,), jnp.int32),
                       pltpu.VMEM((SC_CHUNK,) + table.shape[1:], table.dtype),
                       pltpu.SemaphoreType.DMA],
    )
    def gather(table_hbm, idx_hbm, out_hbm, idx_v, rows_v, sem):
        worker = lax.axis_index("s") * info.num_cores + lax.axis_index("c")
        base = worker * per_worker

        @pl.loop(0, per_worker // SC_CHUNK)
        def _(i):
            off = pl.multiple_of(base + i * SC_CHUNK, SC_CHUNK)
            pltpu.sync_copy(idx_hbm.at[pl.ds(off, SC_CHUNK)], idx_v)
            pltpu.async_copy(table_hbm.at[idx_v], rows_v, sem).wait()
            pltpu.sync_copy(rows_v, out_hbm.at[pl.ds(off, SC_CHUNK)])

    return gather(table, idx)


def _expert_ffn_kernel(te_ref, nv_ref, x_ref, w13_ref, w2_ref, y_ref):
    valid = pl.program_id(0) < nv_ref[0]

    @pl.when(valid)
    def _():
        _store_slabs(y_ref, _swiglu(_split_slabs(x_ref).astype(BF), w13_ref.at[0], w2_ref.at[0]))

    @pl.when(jnp.logical_not(valid))
    def _():
        y_ref[...] = jnp.zeros_like(y_ref)


def _expert_ffn(xs, tile_expert, n_valid, w13, w2, tm):
    n_rows = xs.shape[0] // ROW_SLABS
    last = lambda nv: jnp.maximum(nv[0] - 1, 0)
    grid_spec = pltpu.PrefetchScalarGridSpec(
        num_scalar_prefetch=2,
        grid=(n_rows // tm,),
        in_specs=[_slab_spec(tm, lambda i, te, nv: (jnp.minimum(i, last(nv)), 0)),
                  pl.BlockSpec((1, D_MODEL, 2 * D_FF), lambda i, te, nv: (te[i], 0, 0), pipeline_mode=pl.Buffered(1)),
                  pl.BlockSpec((1, D_FF, D_MODEL), lambda i, te, nv: (te[i], 0, 0), pipeline_mode=pl.Buffered(1))],
        out_specs=_slab_spec(tm, lambda i, te, nv: (i, 0)),
    )
    return pl.pallas_call(
        _expert_ffn_kernel,
        grid_spec=grid_spec,
        out_shape=jax.ShapeDtypeStruct((n_rows * ROW_SLABS, LANES), F32),
        compiler_params=_params("arbitrary"),
        name="expert_ffn",
    )(tile_expert, n_valid, xs, w13, w2)


def _combine_kernel(h_ref, y0_ref, y1_ref, wts_ref, gf_ref, *rest):
    out_ref = rest[-1]
    wts = wts_ref[...]
    tot = h_ref[...] + wts[:, 0:1] * _split_slabs(y0_ref) + wts[:, 1:2] * _split_slabs(y1_ref)
    out_ref[...] = _rms(tot, gf_ref[...])


def _combine(h, yg, wts, g_final, chunk, n_chunks, out_so_far, tm=1024):
    t = h.shape[0]
    nt = t // n_chunks // tm
    first = chunk * nt
    in_specs = [pl.BlockSpec((tm, D_MODEL), lambda i: (first + i, 0)),
                _slab_spec(tm, lambda i: (i, 0)),
                _slab_spec(tm, lambda i: (nt + i, 0)),
                pl.BlockSpec((tm, LANES), lambda i: (i, 0)),
                _const_spec((1, D_MODEL))]
    args = [h, yg, yg, wts, g_final]
    aliases = {}
    if out_so_far is not None:
        in_specs.append(pl.BlockSpec(memory_space=pl.ANY))
        args.append(out_so_far)
        aliases = {len(args) - 1: 0}
    return pl.pallas_call(
        _combine_kernel,
        grid=(nt,),
        in_specs=in_specs,
        out_specs=pl.BlockSpec((tm, D_MODEL), lambda i: (first + i, 0)),
        out_shape=jax.ShapeDtypeStruct((t, D_MODEL), F32),
        input_output_aliases=aliases,
        compiler_params=_params("parallel"),
        name="combine",
    )(*args)


def _moe(h, g, router_w, router_b, w13, w2, g_final, tm=512):
    wr = jnp.pad(router_w, ((0, 0), (0, LANES - N_EXPERTS)))
    whi = wr.astype(BF)
    wlo = (wr - whi.astype(F32)).astype(BF)
    br = jnp.pad(router_b, (0, LANES - N_EXPERTS)).reshape(1, LANES)
    as_rows = lambda a: a.reshape(-1, ROW_SLABS, LANES)
    as_slabs = lambda a: a.reshape(-1, LANES)
    chunks = range(MOE_CHUNKS)
    routed = [_router(h, g, whi, wlo, br, c, MOE_CHUNKS) for c in chunks]
    plans = [_route_plan(idx, tm) for _, idx, _ in routed]
    xs = [as_slabs(_gather_rows(as_rows(routed[c][0]), plans[c][1])) for c in chunks]
    ys = [_expert_ffn(xs[c], plans[c][2], plans[c][3], w13, w2, tm) for c in chunks]
    yg = [as_slabs(_gather_rows(as_rows(ys[c]), plans[c][0])) for c in chunks]
    out = None
    for c in chunks:
        out = _combine(h, yg[c], routed[c][2], g_final, c, MOE_CHUNKS, out)
    return out


def _feature_major(w, heads):
    wt = w.T.reshape(heads, HEAD_DIM, D_MODEL)
    return jnp.pad(wt, ((0, 0), (0, VT_ROWS - HEAD_DIM), (0, 0))).reshape(heads * VT_ROWS, D_MODEL)


def _pack_w_in(w_in):
    cuts = np.cumsum((512, 512, 512, 512, 512, 128, 128, 512, 512, 512, 8))[:-1].tolist()
    a_u, a_v, b_x, b_y, c_q, c_k, c_v, d_q, d_k, d_v, d_f = jnp.split(w_in, cuts, axis=-1)
    w = jnp.concatenate([b_x, b_y, d_q, d_k, c_k, c_v], axis=-1).astype(BF)
    wa = jnp.concatenate([a_u, a_v], axis=-1).astype(BF)
    wf = jnp.pad(d_f, ((0, 0), (0, LANES - FOX_HEADS))).astype(BF)
    wt = jnp.concatenate([_feature_major(d_v, FOX_HEADS), c_q.T, _feature_major(c_v, SWA_KV)], axis=0).astype(BF)
    return w, wa, wf, wt


def _fm_ones():
    ones = np.zeros((FM_ROWS, LANES), np.float32)
    for base, heads in ((0, FOX_HEADS), (VT_ALL + BRANCH_W, SWA_KV)):
        for h in range(heads):
            ones[base + h * VT_ROWS + HEAD_DIM, :] = 1.0
    return jnp.asarray(ones)


def _cast_kernel(x_ref, o_ref):
    o_ref[...] = x_ref[...].astype(o_ref.dtype)


def _to_bf16(w):
    shape = w.shape
    w3 = w.reshape((-1,) + shape[-2:])
    e, r, c = w3.shape
    tr = r
    while tr * c * 4 > CAST_TILE_BYTES and tr % 32 == 0:
        tr //= 2
    out = pl.pallas_call(
        _cast_kernel,
        grid=(e, r // tr),
        in_specs=[pl.BlockSpec((1, tr, c), lambda i, j: (i, j, 0))],
        out_specs=pl.BlockSpec((1, tr, c), lambda i, j: (i, j, 0)),
        out_shape=jax.ShapeDtypeStruct(w3.shape, BF),
        compiler_params=_params("parallel", "parallel"),
        name="to_bf16",
    )(w3)
    return out.reshape(shape)


def _row(v):
    return v.reshape(1, -1)


def _hybrid_mixer(h, bsz, norm_mix, w_in, sgu_g, sgu_w, sgu_b, conv_w, conv_b, rg_wa, rg_ba, rg_wx, rg_bx,
                  rg_lambda, swa_sinks, fox_bf, w_branch, w_gate, b_gate, w_out):
    w, wa, wf, wt = _pack_w_in(w_in)
    proj, f, o_a, vt, sqt, svt = _inproj(h, _row(norm_mix), w, wa, wf, wt, _fm_ones(), _row(sgu_g), sgu_w, sgu_b.T)
    wax = jnp.concatenate([rg_wa, rg_wx], axis=-1).astype(BF)
    o_b = _rglru(proj, bsz, conv_w, _row(conv_b), wax, _row(rg_ba), _row(rg_bx), _row(rg_lambda))
    o_c = _swa(proj, sqt, svt, bsz, swa_sinks)
    bf_pad = jnp.pad(fox_bf, (0, LANES - FOX_HEADS)).reshape(1, LANES)
    qp, kp = _fox_prep(proj, f, bsz, bf_pad)
    o_d = _fox(qp, kp, vt, bsz)
    return _merge(h, _row(norm_mix), o_a, o_b, o_c, o_d, _to_bf16(w_gate), b_gate[:, None, :],
                  _to_bf16(w_branch), w_out.astype(BF))


def kernel(x, mem, norm_mix, w_in, sgu_g, sgu_w, sgu_b, conv_w, conv_b, rg_wa, rg_ba, rg_wx, rg_bx, rg_lambda, swa_sinks, fox_bf, w_branch, w_gate, b_gate, w_out, norm_cross, norm_mem, wq_c, wkv_c, wo_c, norm_ffn, dense_w13, dense_w2, router_w, router_b, moe_w13, moe_w2, norm_final):
    bsz, seq, d = x.shape
    m_len = mem.shape[1]
    depth = norm_mix.shape[0]
    assert depth == 2, "the final RMSNorm is fused into the routed layer, which must be the last one"
    h = x.reshape(bsz * seq, d)
    mem2 = mem.reshape(bsz * m_len, d)
    for l in range(depth):
        h = _hybrid_mixer(h, bsz, norm_mix[l], w_in[l], sgu_g[l], sgu_w[l], sgu_b[l], conv_w[l], conv_b[l],
                          rg_wa[l], rg_ba[l], rg_wx[l], rg_bx[l], rg_lambda[l], swa_sinks[l], fox_bf[l],
                          w_branch[l], w_gate[l], b_gate[l], w_out[l])
        kv = _memkv(mem2, _row(norm_mem[l]), wkv_c[l].astype(BF), m_len)
        h = _cross(h, _row(norm_cross[l]), wq_c[l].astype(BF), kv, wo_c[l].astype(BF), bsz, m_len)
        if l % 2 == 0:
            h = _ffn(h, _row(norm_ffn[l]), _to_bf16(dense_w13[l // 2]), _to_bf16(dense_w2[l // 2]))
        else:
            h = _moe(h, _row(norm_ffn[l]), router_w[l // 2], router_b[l // 2], _to_bf16(moe_w13[l // 2]),
                     _to_bf16(moe_w2[l // 2]), _row(norm_final))
    return h.reshape(bsz, seq, d)
```

```python
import functools
import math

import numpy as np
import jax
import jax.numpy as jnp
from jax import lax
from jax.experimental import pallas as pl
from jax.experimental.pallas import tpu as pltpu
from jax.experimental.pallas import tpu_sc as plsc

F32 = jnp.float32
BF = jnp.bfloat16

D_MODEL = 1024
BRANCH_W = 512
HEAD_DIM = 64
CHUNK = 128
SGU_GROUPS = 4
LRU_HEADS = 4
LRU_C = 8.0
CONV_W = 4
SWA_HEADS = 8
SWA_KV = 2
FOX_HEADS = 8
FOX_PAIRS = FOX_HEADS // 2
X_HEADS = 4
X_HEAD_DIM = 128
D_FF = 2816
N_EXPERTS = 8
EPS = 1e-6
LOG2E = math.log2(math.e)
NEG = -1e30
LANES = 128
ROW_SLABS = D_MODEL // LANES
SC_CHUNK = 64
MOE_CHUNKS = 2
VMEM_LIMIT = 56 * 1024 * 1024
EXPERT_VMEM_LIMIT = 62 * 1024 * 1024

COL_BX, COL_BY, COL_DQ, COL_DK = range(4)
N_WIDE = 4
N_PROJ = N_WIDE * BRANCH_W + 2 * SWA_KV * HEAD_DIM
VT_ROWS = 80
VT_ALL = FOX_HEADS * VT_ROWS
SWA_VT = SWA_KV * VT_ROWS
FM_ROWS = VT_ALL + BRANCH_W + SWA_VT
FOX_UNROLL = 4
FOX_STEP_PAIRS = 1
FF_SPLITS = ((0, 1536), (1536, D_FF))


def _rms(x, g):
    return x * lax.rsqrt(jnp.mean(x * x, axis=-1, keepdims=True) + EPS) * g


def _sigmoid(x):
    return 1.0 / (1.0 + jnp.exp(-x))


def _gelu(x):
    return 0.5 * x * (1.0 + jnp.tanh(math.sqrt(2.0 / math.pi) * (x + 0.044715 * (x * x * x))))


def _dot(a, b):
    return jnp.dot(a, b, preferred_element_type=F32)


def _dot_nt(a, b):
    return lax.dot_general(a, b, (((1,), (1,)), ((), ())), preferred_element_type=F32)


def _shift_rows(x, d, fill):
    row = lax.broadcasted_iota(jnp.int32, x.shape, 0)
    return jnp.where(row >= d, pltpu.roll(x, d, 0), fill)


def _params(*sem):
    return pltpu.CompilerParams(dimension_semantics=sem, vmem_limit_bytes=VMEM_LIMIT)


def _const_spec(shape):
    nd = len(shape)
    return pl.BlockSpec(shape, lambda *_: (0,) * nd, pipeline_mode=pl.Buffered(1))


def _inproj_kernel(h_ref, g_ref, w_ref, wa_ref, wf_ref, wt_ref, ones_ref, sg_ref, sw_ref, sbt_ref,
                   proj_ref, f_ref, oa_ref, vt_ref, sqt_ref, svt_ref):
    xn = _rms(h_ref[...], g_ref[...]).astype(BF)
    tm = xn.shape[0]
    n_col = w_ref.shape[1]
    for c in range(0, n_col, BRANCH_W):
        w = min(BRANCH_W, n_col - c)
        proj_ref[:, c:c + w] = _dot(xn, w_ref[:, c:c + w]).astype(BF)
    f_ref[...] = _dot(xn, wf_ref[...])
    ones = jnp.concatenate([ones_ref[...]] * (tm // LANES), axis=1)
    fm = (_dot_nt(wt_ref[...], xn) + ones).astype(BF)
    vt_ref[...] = fm[:VT_ALL]
    sqt_ref[...] = fm[VT_ALL:VT_ALL + BRANCH_W]
    svt_ref[...] = fm[VT_ALL + BRANCH_W:]

    u = _gelu(_dot(xn, wa_ref[:, :BRANCH_W]))
    v = _gelu(_dot(xn, wa_ref[:, BRANCH_W:]))
    vn = _rms(v, sg_ref[...]).astype(BF)
    row = lax.broadcasted_iota(jnp.int32, (CHUNK, CHUNK), 0)
    col = lax.broadcasted_iota(jnp.int32, (CHUNK, CHUNK), 1)
    gw = BRANCH_W // SGU_GROUPS
    for g in range(SGU_GROUPS):
        wg = jnp.where(col <= row, sw_ref[g], 0.0).astype(BF)
        bg = sbt_ref[:, g:g + 1]
        for c in range(tm // CHUNK):
            rs = slice(c * CHUNK, (c + 1) * CHUNK)
            cs = slice(g * gw, (g + 1) * gw)
            mixed = _dot(wg, vn[rs, cs]) + bg
            oa_ref[rs, cs] = (u[rs, cs] * mixed).astype(BF)


def _inproj(h, g, w, wa, wf, wt, ones, sgu_g, sgu_w, sgu_bt, tm=512):
    t = h.shape[0]
    row = lambda i: (i, 0)
    col = lambda i: (0, i)
    return pl.pallas_call(
        _inproj_kernel,
        grid=(t // tm,),
        in_specs=[pl.BlockSpec((tm, D_MODEL), row),
                  _const_spec((1, D_MODEL)),
                  _const_spec((D_MODEL, N_PROJ)),
                  _const_spec((D_MODEL, 2 * BRANCH_W)),
                  _const_spec((D_MODEL, LANES)),
                  _const_spec((FM_ROWS, D_MODEL)),
                  _const_spec((FM_ROWS, LANES)),
                  _const_spec((1, BRANCH_W)),
                  _const_spec((SGU_GROUPS, CHUNK, CHUNK)),
                  _const_spec((CHUNK, SGU_GROUPS))],
        out_specs=[pl.BlockSpec((tm, N_PROJ), row),
                   pl.BlockSpec((tm, LANES), row),
                   pl.BlockSpec((tm, BRANCH_W), row),
                   pl.BlockSpec((VT_ALL, tm), col),
                   pl.BlockSpec((BRANCH_W, tm), col),
                   pl.BlockSpec((SWA_VT, tm), col)],
        out_shape=[jax.ShapeDtypeStruct((t, N_PROJ), BF), jax.ShapeDtypeStruct((t, LANES), F32),
                   jax.ShapeDtypeStruct((t, BRANCH_W), BF), jax.ShapeDtypeStruct((VT_ALL, t), BF),
                   jax.ShapeDtypeStruct((BRANCH_W, t), BF), jax.ShapeDtypeStruct((SWA_VT, t), BF)],
        compiler_params=_params("parallel"),
        name="inproj",
    )(h, g, w, wa, wf, wt, ones, sgu_g, sgu_w, sgu_bt)


def _rglru_kernel(x_ref, y_ref, cw_ref, cb_ref, wax_ref, ba_ref, bx_ref, lam_ref, o_ref, xs_ref, hc_ref):
    ts = x_ref.shape[0]
    hw = BRANCH_W // LRU_HEADS

    @pl.when(pl.program_id(1) == 0)
    def _():
        xs_ref[0:8, :] = jnp.zeros((8, BRANCH_W), F32)
        hc_ref[...] = jnp.zeros_like(hc_ref)

    x = x_ref[...].astype(F32)
    xs_ref[8:8 + ts, :] = x
    cw = cw_ref[...]
    xc = cb_ref[...] + cw[CONV_W - 1:CONV_W] * x
    for k in range(1, CONV_W):
        xc = xc + cw[CONV_W - 1 - k:CONV_W - k] * xs_ref[8 - k:8 - k + ts, :]
    xs_ref[0:8, :] = x[ts - 8:ts, :]

    xcb = xc.astype(BF)
    r_parts, i_parts = [], []
    for hd in range(LRU_HEADS):
        z = _dot(xcb[:, hd * hw:(hd + 1) * hw], wax_ref[hd])
        r_parts.append(z[:, :hw])
        i_parts.append(z[:, hw:])
    r = _sigmoid(jnp.concatenate(r_parts, axis=1) + ba_ref[...])
    gi = _sigmoid(jnp.concatenate(i_parts, axis=1) + bx_ref[...])
    nl = -lam_ref[...]
    softplus = jnp.maximum(nl, 0.0) + jnp.log1p(jnp.exp(-jnp.abs(nl)))
    log_a = (-LRU_C * r) * softplus
    a = jnp.exp(log_a)
    th = jnp.tanh(log_a)
    e2 = -2.0 * th / (1.0 - th)
    b = (xc * gi) * (e2 * lax.rsqrt(jnp.maximum(e2, 1e-30)))

    sub = lax.broadcasted_iota(jnp.int32, a.shape, 0) & 7
    for d in (1, 2, 4):
        keep = sub >= d
        a_sh = jnp.where(keep, pltpu.roll(a, d, 0), 1.0)
        b_sh = jnp.where(keep, pltpu.roll(b, d, 0), 0.0)
        b = b + a * b_sh
        a = a * a_sh
    carry = hc_ref[...]
    groups = []
    for g in range(ts // 8):
        hg = a[8 * g:8 * g + 8] * carry + b[8 * g:8 * g + 8]
        groups.append(hg)
        carry = hg[7:8]
    h = jnp.concatenate(groups, axis=0)
    hc_ref[...] = carry
    o_ref[...] = (h * _gelu(y_ref[...].astype(F32))).astype(BF)


def _rglru(proj, bsz, cw, cb, wax, ba, bx, lam, ts=512):
    t = proj.shape[0]
    ns = t // bsz // ts
    return pl.pallas_call(
        _rglru_kernel,
        grid=(bsz, ns),
        in_specs=[pl.BlockSpec((ts, BRANCH_W), lambda b, j: (b * ns + j, COL_BX)),
                  pl.BlockSpec((ts, BRANCH_W), lambda b, j: (b * ns + j, COL_BY)),
                  _const_spec((CONV_W, BRANCH_W)),
                  _const_spec((1, BRANCH_W)),
                  _const_spec((LRU_HEADS, BRANCH_W // LRU_HEADS, 2 * BRANCH_W // LRU_HEADS)),
                  _const_spec((1, BRANCH_W)),
                  _const_spec((1, BRANCH_W)),
                  _const_spec((1, BRANCH_W))],
        out_specs=pl.BlockSpec((ts, BRANCH_W), lambda b, j: (b * ns + j, 0)),
        out_shape=jax.ShapeDtypeStruct((t, BRANCH_W), BF),
        scratch_shapes=[pltpu.VMEM((ts + 8, BRANCH_W), F32), pltpu.VMEM((1, BRANCH_W), F32)],
        compiler_params=_params("parallel", "arbitrary"),
        name="rglru",
    )(proj, proj, cw, cb, wax, ba, bx, lam)


def _swa_tables(sinks):
    t_idx = np.arange(CHUNK)[None, :] + CHUNK
    dist = (t_idx - np.arange(2 * CHUNK)[:, None]).astype(np.float32)
    in_win = (dist >= 0) & (dist < CHUNK)
    slopes = 2.0 ** (-(8.0 / SWA_HEADS) * np.arange(1, SWA_HEADS + 1, dtype=np.float32))
    bias = np.where(in_win[None], -slopes[:, None, None] * dist[None] * LOG2E, NEG)
    grp = SWA_HEADS // SWA_KV
    bias = bias.reshape(SWA_KV, grp, 2 * CHUNK, CHUNK).transpose(0, 2, 1, 3).reshape(SWA_KV, 2 * CHUNK, grp * CHUNK)
    sink = jnp.repeat(sinks.astype(F32) * LOG2E, CHUNK).reshape(SWA_KV, 1, grp * CHUNK)
    return jnp.asarray(bias, F32), sink


def _swa_kernel(qt_ref, kv_ref, kvp_ref, vt_ref, vtp_ref, bias_ref, sink_ref, o_ref):
    tq = kv_ref.shape[0]
    grp = SWA_HEADS // SWA_KV
    first_key = jnp.where(pl.program_id(1) == 0, CHUNK, 0)
    k_all = jnp.concatenate([kvp_ref[...], kv_ref[...]], axis=0)
    vt_all = jnp.concatenate([vtp_ref[...], vt_ref[...]], axis=1)
    qt = (qt_ref[...].astype(F32) * (HEAD_DIM ** -0.5 * LOG2E)).astype(BF)
    key_row = lax.broadcasted_iota(jnp.int32, (2 * CHUNK, grp * CHUNK), 0)
    kw = 2 * SWA_KV * HEAD_DIM
    for qb in range(tq // CHUNK):
        cols = slice(qb * CHUNK, (qb + 1) * CHUNK)
        kb = k_all[qb * CHUNK:(qb + 2) * CHUNK]
        vtb = vt_all[:, qb * CHUNK:(qb + 2) * CHUNK]
        outs = []
        for kh in range(SWA_KV):
            qg = jnp.concatenate([qt[(kh * grp + g) * HEAD_DIM:(kh * grp + g + 1) * HEAD_DIM, cols]
                                  for g in range(grp)], axis=1)
            parts = []
            if kh > 0:
                parts.append(jnp.zeros((kh * HEAD_DIM, grp * CHUNK), BF))
            parts.append(qg)
            parts.append(jnp.zeros((kw - (kh + 1) * HEAD_DIM, grp * CHUNK), BF))
            st = _dot(kb, jnp.concatenate(parts, axis=0)) + bias_ref[kh]
            if qb == 0:
                st = jnp.where(key_row >= first_key, st, NEG)
            sink = sink_ref[kh]
            m = jnp.maximum(jnp.max(st, axis=0, keepdims=True), sink)
            pt = jnp.exp2(st - m).astype(BF)
            acc = _dot(vtb[kh * VT_ROWS:(kh + 1) * VT_ROWS], pt)
            ot = acc[:HEAD_DIM] / (acc[HEAD_DIM:HEAD_DIM + 1] + jnp.exp2(sink - m))
            outs.extend(ot[:, g * CHUNK:(g + 1) * CHUNK] for g in range(grp))
        o_ref[cols, :] = jnp.concatenate(outs, axis=0).T.astype(BF)


def _swa(proj, sqt, svt, bsz, sinks, tq=1024):
    t = proj.shape[0]
    seq = t // bsz
    nq = seq // tq
    per = tq // CHUNK
    kv_blk = N_WIDE * BRANCH_W // (2 * SWA_KV * HEAD_DIM)
    prev = lambda b, i: b * (seq // CHUNK) + jnp.maximum(i * per - 1, 0)
    bias, sink = _swa_tables(sinks)
    grp = SWA_HEADS // SWA_KV
    return pl.pallas_call(
        _swa_kernel,
        grid=(bsz, nq),
        in_specs=[pl.BlockSpec((BRANCH_W, tq), lambda b, i: (0, b * nq + i)),
                  pl.BlockSpec((tq, 2 * SWA_KV * HEAD_DIM), lambda b, i: (b * nq + i, kv_blk)),
                  pl.BlockSpec((CHUNK, 2 * SWA_KV * HEAD_DIM), lambda b, i: (prev(b, i), kv_blk)),
                  pl.BlockSpec((SWA_VT, tq), lambda b, i: (0, b * nq + i)),
                  pl.BlockSpec((SWA_VT, CHUNK), lambda b, i: (0, prev(b, i))),
                  _const_spec((SWA_KV, 2 * CHUNK, grp * CHUNK)),
                  _const_spec((SWA_KV, 1, grp * CHUNK))],
        out_specs=pl.BlockSpec((tq, BRANCH_W), lambda b, i: (b * nq + i, 0)),
        out_shape=jax.ShapeDtypeStruct((t, BRANCH_W), BF),
        compiler_params=_params("parallel", "parallel"),
        name="swa",
    )(sqt, proj, proj, svt, svt, bias, sink)


def _aug_tables():
    eq = np.zeros((3, LANES, FOX_PAIRS * LANES), np.float32)
    ek = np.zeros((3, LANES, FOX_PAIRS * LANES), np.float32)
    oq = np.zeros((1, FOX_PAIRS * LANES), np.float32)
    ok = np.zeros((1, FOX_PAIRS * LANES), np.float32)
    for h in range(FOX_HEADS):
        base = (h // 2) * LANES + 6 * (h % 2)
        for s in range(3):
            eq[s, h, base + s] = 1.0
            ek[s, h, base + 3 + s] = -1.0
            ok[0, base + s] = 1.0
            oq[0, base + 3 + s] = 1.0
    return eq, ek, oq, ok


def _fox_prep_kernel(q_ref, k_ref, f_ref, bf_ref, eq_ref, ek_ref, oq_ref, ok_ref, qp_ref, kp_ref, cum_ref):
    ts = q_ref.shape[0]

    @pl.when(pl.program_id(1) == 0)
    def _():
        cum_ref[...] = jnp.zeros_like(cum_ref)

    z = f_ref[...] + bf_ref[...]
    c = jnp.minimum(z, 0.0) - jnp.log1p(jnp.exp(-jnp.abs(z)))
    d = 1
    while d < ts:
        c = c + _shift_rows(c, d, 0.0)
        d *= 2
    c = c + cum_ref[...]
    cum_ref[...] = c[ts - 1:ts, :]
    c = c * LOG2E
    c1 = c.astype(BF)
    r1 = c - c1.astype(F32)
    c2 = r1.astype(BF)
    c3 = (r1 - c2.astype(F32)).astype(BF)
    augq = _dot(c1, eq_ref[0]) + _dot(c2, eq_ref[1]) + _dot(c3, eq_ref[2]) + oq_ref[...]
    augk = _dot(c1, ek_ref[0]) + _dot(c2, ek_ref[1]) + _dot(c3, ek_ref[2]) + ok_ref[...]
    qt = (q_ref[...].astype(F32) * (HEAD_DIM ** -0.5 * LOG2E)).T.astype(BF)
    augqt = augq.T.astype(BF)
    k = k_ref[...]
    for p in range(FOX_PAIRS):
        qp_ref[2 * p * LANES:(2 * p + 1) * LANES, :] = qt[p * LANES:(p + 1) * LANES]
        qp_ref[(2 * p + 1) * LANES:(2 * p + 2) * LANES, :] = augqt[p * LANES:(p + 1) * LANES]
        kp_ref[:, 2 * p * LANES:(2 * p + 1) * LANES] = k[:, p * LANES:(p + 1) * LANES]
        kp_ref[:, (2 * p + 1) * LANES:(2 * p + 2) * LANES] = augk[:, p * LANES:(p + 1) * LANES].astype(BF)


def _fox_prep(proj, f, bsz, bf_pad, ts=1024):
    t = proj.shape[0]
    ns = t // bsz // ts
    eq, ek, oq, ok = _aug_tables()
    wide = 2 * FOX_PAIRS * LANES
    return pl.pallas_call(
        _fox_prep_kernel,
        grid=(bsz, ns),
        in_specs=[pl.BlockSpec((ts, BRANCH_W), lambda b, j: (b * ns + j, COL_DQ)),
                  pl.BlockSpec((ts, BRANCH_W), lambda b, j: (b * ns + j, COL_DK)),
                  pl.BlockSpec((ts, LANES), lambda b, j: (b * ns + j, 0)),
                  _const_spec((1, LANES)),
                  _const_spec(eq.shape), _const_spec(ek.shape), _const_spec(oq.shape), _const_spec(ok.shape)],
        out_specs=[pl.BlockSpec((wide, ts), lambda b, j: (0, b * ns + j)),
                   pl.BlockSpec((ts, wide), lambda b, j: (b * ns + j, 0))],
        out_shape=[jax.ShapeDtypeStruct((wide, t), BF), jax.ShapeDtypeStruct((t, wide), BF)],
        scratch_shapes=[pltpu.VMEM((1, LANES), F32)],
        compiler_params=_params("parallel", "arbitrary"),
        name="fox_prep",
    )(proj, proj, f, bf_pad, jnp.asarray(eq, BF), jnp.asarray(ek, BF), jnp.asarray(oq), jnp.asarray(ok))


def _fox_kernel(q_ref, k_ref, vt_ref, o_ref, st_ref, pt_ref, al_ref, m_ref, acc_ref):
    tq = q_ref.shape[1]
    tk = tq // 2
    n_heads = m_ref.shape[0]
    qi = pl.program_id(2)
    row = lax.broadcasted_iota(jnp.int32, (2 * LANES, tq), 0)
    in0 = (row < HEAD_DIM) | ((row >= LANES) & (row < LANES + 6))
    in1 = ((row >= HEAD_DIM) & (row < LANES)) | ((row >= LANES + 6) & (row < LANES + 12))
    qh = []
    for p in range(n_heads // 2):
        q = q_ref[2 * p * LANES:(2 * p + 2) * LANES, :]
        zero = jnp.zeros_like(q)
        qh += [jnp.where(in0, q, zero), jnp.where(in1, q, zero)]
    for h in range(n_heads):
        m_ref[h] = jnp.full(m_ref.shape[1:], NEG, F32)
        acc_ref[h] = jnp.zeros(acc_ref.shape[1:], F32)
        pt_ref[1, h] = jnp.zeros(pt_ref.shape[2:], BF)
        al_ref[1, h] = jnp.ones(al_ref.shape[2:], F32)

    def scores(t, slot):
        rows = pl.ds(pl.multiple_of(t * tk, tk), tk)
        for p in range(n_heads // 2):
            kp = k_ref[rows, 2 * p * LANES:(2 * p + 2) * LANES]
            for h in (2 * p, 2 * p + 1):
                st_ref[slot, h] = _dot(kp, qh[h])

    def numerators(slot, key_offset):
        for h in range(n_heads):
            st = st_ref[slot, h]
            if key_offset is not None:
                key = lax.broadcasted_iota(jnp.int32, (tk, tq), 0) + key_offset
                qry = lax.broadcasted_iota(jnp.int32, (tk, tq), 1)
                st = jnp.where(key <= qry, st, NEG)
            m_old = m_ref[h]
            m_new = jnp.maximum(m_old, jnp.max(st, axis=0, keepdims=True))
            al_ref[slot, h] = jnp.exp2(m_old - m_new)
            pt_ref[slot, h] = jnp.exp2(st - m_new).astype(BF)
            m_ref[h] = m_new

    def accumulate(t, slot):
        cols = pl.ds(pl.multiple_of(t * tk, tk), tk)
        for h in range(n_heads):
            vt = vt_ref[h * VT_ROWS:(h + 1) * VT_ROWS, cols]
            acc_ref[h] = al_ref[slot, h] * acc_ref[h] + _dot(vt, pt_ref[slot, h])

    def tile_pair(j, diagonal):
        scores(2 * j + 1, 1)
        accumulate(jnp.maximum(2 * j - 1, 0), 1)
        numerators(0, 0 if diagonal else None)
        if not diagonal:
            scores(2 * j + 2, 0)
        accumulate(2 * j, 0)
        numerators(1, tk if diagonal else None)

    def body(jj, carry):
        for u in range(FOX_UNROLL):
            tile_pair(FOX_UNROLL * jj + u, False)
        return carry

    def remainder(j, carry):
        tile_pair(j, False)
        return carry

    scores(0, 0)
    lax.fori_loop(0, qi // FOX_UNROLL, body, 0)
    lax.fori_loop((qi // FOX_UNROLL) * FOX_UNROLL, qi, remainder, 0)
    tile_pair(qi, True)
    accumulate(2 * qi + 1, 1)
    outs = []
    for h in range(n_heads):
        acc = acc_ref[h]
        outs.append(acc[:HEAD_DIM] / acc[HEAD_DIM:HEAD_DIM + 1])
    o_ref[...] = jnp.concatenate(outs, axis=0).T.astype(BF)


def _fox(qp, kp, vt, bsz, tq=512):
    t = kp.shape[0]
    seq = t // bsz
    nq = seq // tq
    np_ = FOX_STEP_PAIRS
    nh = 2 * np_
    return pl.pallas_call(
        _fox_kernel,
        grid=(bsz, FOX_PAIRS // np_, nq),
        in_specs=[pl.BlockSpec((np_ * 2 * LANES, tq), lambda b, p, i: (p, b * nq + i)),
                  pl.BlockSpec((seq, np_ * 2 * LANES), lambda b, p, i: (b, p)),
                  pl.BlockSpec((nh * VT_ROWS, seq), lambda b, p, i: (p, b))],
        out_specs=pl.BlockSpec((tq, np_ * LANES), lambda b, p, i: (b * nq + i, p)),
        out_shape=jax.ShapeDtypeStruct((t, BRANCH_W), BF),
        scratch_shapes=[pltpu.VMEM((2, nh, tq // 2, tq), F32),
                        pltpu.VMEM((2, nh, tq // 2, tq), BF),
                        pltpu.VMEM((2, nh, 1, tq), F32),
                        pltpu.VMEM((nh, 1, tq), F32),
                        pltpu.VMEM((nh, VT_ROWS, tq), F32)],
        compiler_params=_params("parallel", "parallel", "arbitrary"),
        name="fox",
    )(qp, kp, vt)


def _merge_kernel(h_ref, g_ref, oa_ref, ob_ref, oc_ref, od_ref, wg_ref, bg_ref, wb_ref, wo_ref, out_ref):
    h = h_ref[...]
    xn = _rms(h, g_ref[...]).astype(BF)
    merged = None
    for br, o_ref in enumerate((oa_ref, ob_ref, oc_ref, od_ref)):
        gate = _sigmoid(_dot(xn, wg_ref[br]) + bg_ref[br])
        term = gate * _dot(o_ref[...], wb_ref[br])
        merged = term if merged is None else merged + term
    out_ref[...] = h + _dot(merged.astype(BF), wo_ref[...])


def _merge(h, g, oa, ob, oc, od, wg, bg, wb, wo, tm=512):
    t = h.shape[0]
    row = lambda i: (i, 0)
    return pl.pallas_call(
        _merge_kernel,
        grid=(t // tm,),
        in_specs=[pl.BlockSpec((tm, D_MODEL), row),
                  _const_spec((1, D_MODEL)),
                  pl.BlockSpec((tm, BRANCH_W), row), pl.BlockSpec((tm, BRANCH_W), row),
                  pl.BlockSpec((tm, BRANCH_W), row), pl.BlockSpec((tm, BRANCH_W), row),
                  _const_spec((4, D_MODEL, D_MODEL)),
                  _const_spec((4, 1, D_MODEL)),
                  _const_spec((4, BRANCH_W, D_MODEL)),
                  _const_spec((D_MODEL, D_MODEL))],
        out_specs=pl.BlockSpec((tm, D_MODEL), row),
        out_shape=jax.ShapeDtypeStruct((t, D_MODEL), F32),
        compiler_params=_params("parallel"),
        name="merge",
    )(h, g, oa, ob, oc, od, wg, bg, wb, wo)


def _memkv_kernel(mem_ref, g_ref, w_ref, kv_ref):
    mn = _rms(mem_ref[...], g_ref[...]).astype(BF)
    kv_ref[...] = _dot(mn, w_ref[...]).astype(BF)


def _memkv(mem2, g, w, m_len):
    n = mem2.shape[0]
    width = 2 * X_HEADS * X_HEAD_DIM
    return pl.pallas_call(
        _memkv_kernel,
        grid=(n // m_len,),
        in_specs=[pl.BlockSpec((m_len, D_MODEL), lambda b: (b, 0)),
                  _const_spec((1, D_MODEL)),
                  _const_spec((D_MODEL, width))],
        out_specs=pl.BlockSpec((m_len, width), lambda b: (b, 0)),
        out_shape=jax.ShapeDtypeStruct((n, width), BF),
        compiler_params=_params("parallel"),
        name="memkv",
    )(mem2, g, w)


def _cross_kernel(h_ref, g_ref, wq_ref, kv_ref, wo_ref, out_ref):
    h = h_ref[...]
    hn = _rms(h, g_ref[...]).astype(BF)
    q = _dot(hn, wq_ref[...]).astype(BF)
    kv = kv_ref[...]
    width = X_HEADS * X_HEAD_DIM
    outs = []
    for hd in range(X_HEADS):
        cs = slice(hd * X_HEAD_DIM, (hd + 1) * X_HEAD_DIM)
        s = _dot_nt(q[:, cs], kv[:, cs]) * (X_HEAD_DIM ** -0.5)
        m = jnp.max(s, axis=-1, keepdims=True)
        p = jnp.exp(s - m)
        denom = jnp.sum(p, axis=-1, keepdims=True)
        v = kv[:, width + hd * X_HEAD_DIM:width + (hd + 1) * X_HEAD_DIM]
        outs.append((_dot(p.astype(BF), v) / denom).astype(BF))
    o = jnp.concatenate(outs, axis=1)
    out_ref[...] = h + _dot(o, wo_ref[...])


def _cross(h, g, wq, kv, wo, bsz, m_len, tm=1024):
    t = h.shape[0]
    per = t // bsz // tm
    width = X_HEADS * X_HEAD_DIM
    return pl.pallas_call(
        _cross_kernel,
        grid=(bsz, per),
        in_specs=[pl.BlockSpec((tm, D_MODEL), lambda b, i: (b * per + i, 0)),
                  _const_spec((1, D_MODEL)),
                  _const_spec((D_MODEL, width)),
                  pl.BlockSpec((m_len, 2 * width), lambda b, i: (b, 0)),
                  _const_spec((width, D_MODEL))],
        out_specs=pl.BlockSpec((tm, D_MODEL), lambda b, i: (b * per + i, 0)),
        out_shape=jax.ShapeDtypeStruct((t, D_MODEL), F32),
        compiler_params=_params("parallel", "parallel"),
        name="cross",
    )(h, g, wq, kv, wo)


def _swiglu(xb, w13_ref, w2_ref):
    out = None
    for lo, hi in FF_SPLITS:
        gate = _dot(xb, w13_ref[:, lo:hi])
        up = _dot(xb, w13_ref[:, D_FF + lo:D_FF + hi])
        act = (gate * _sigmoid(gate) * up).astype(BF)
        part = _dot(act, w2_ref[lo:hi, :])
        out = part if out is None else out + part
    return out


def _ffn_kernel(h_ref, g_ref, w13_ref, w2_ref, out_ref):
    h = h_ref[...]
    out_ref[...] = h + _swiglu(_rms(h, g_ref[...]).astype(BF), w13_ref, w2_ref)


def _ffn(h, g, w13, w2, tm=512):
    t = h.shape[0]
    return pl.pallas_call(
        _ffn_kernel,
        grid=(t // tm,),
        in_specs=[pl.BlockSpec((tm, D_MODEL), lambda i: (i, 0)),
                  _const_spec((1, D_MODEL)),
                  _const_spec((D_MODEL, 2 * D_FF)),
                  _const_spec((D_FF, D_MODEL))],
        out_specs=pl.BlockSpec((tm, D_MODEL), lambda i: (i, 0)),
        out_shape=jax.ShapeDtypeStruct((t, D_MODEL), F32),
        compiler_params=_params("parallel"),
        name="ffn",
    )(h, g, w13, w2)


def _split_slabs(ref):
    rows = ref.shape[0] // ROW_SLABS
    return jnp.concatenate([ref[pl.ds(c, rows, stride=ROW_SLABS), :] for c in range(ROW_SLABS)], axis=1)


def _store_slabs(ref, x):
    rows = ref.shape[0] // ROW_SLABS
    for c in range(ROW_SLABS):
        ref[pl.ds(c, rows, stride=ROW_SLABS), :] = x[:, c * LANES:(c + 1) * LANES]


def _slab_spec(tm, index_map):
    return pl.BlockSpec((tm * ROW_SLABS, LANES), index_map)


def _router_kernel(h_ref, g_ref, whi_ref, wlo_ref, br_ref, hn_ref, idx_ref, wts_ref):
    hn = _rms(h_ref[...], g_ref[...])
    _store_slabs(hn_ref, hn)
    hi = hn.astype(BF)
    lo = (hn - hi.astype(F32)).astype(BF)
    logits = _dot(hi, whi_ref[...]) + (_dot(lo, whi_ref[...]) + _dot(hi, wlo_ref[...])) + br_ref[...]
    lane = lax.broadcasted_iota(jnp.int32, logits.shape, 1)
    logits = jnp.where(lane < N_EXPERTS, logits, NEG)
    v1 = jnp.max(logits, axis=-1, keepdims=True)
    i1 = jnp.min(jnp.where(logits == v1, lane, LANES), axis=-1, keepdims=True)
    rest = jnp.where(lane == i1, NEG, logits)
    v2 = jnp.max(rest, axis=-1, keepdims=True)
    i2 = jnp.min(jnp.where(rest == v2, lane, LANES), axis=-1, keepdims=True)
    e2 = jnp.exp(v2 - v1)
    w1 = 1.0 / (1.0 + e2)
    w2 = e2 / (1.0 + e2)
    idx_ref[...] = jnp.where(lane == 0, i1, jnp.where(lane == 1, i2, 0))
    wts_ref[...] = jnp.where(lane == 0, w1, jnp.where(lane == 1, w2, 0.0))


def _router(h, g, whi, wlo, br, chunk, n_chunks, tm=1024):
    t = h.shape[0] // n_chunks
    first = chunk * (t // tm)
    return pl.pallas_call(
        _router_kernel,
        grid=(t // tm,),
        in_specs=[pl.BlockSpec((tm, D_MODEL), lambda i: (first + i, 0)),
                  _const_spec((1, D_MODEL)),
                  _const_spec((D_MODEL, LANES)),
                  _const_spec((D_MODEL, LANES)),
                  _const_spec((1, LANES))],
        out_specs=[_slab_spec(tm, lambda i: (i, 0)),
                   pl.BlockSpec((tm, LANES), lambda i: (i, 0)),
                   pl.BlockSpec((tm, LANES), lambda i: (i, 0))],
        out_shape=[jax.ShapeDtypeStruct((t * ROW_SLABS, LANES), F32),
                   jax.ShapeDtypeStruct((t, LANES), jnp.int32),
                   jax.ShapeDtypeStruct((t, LANES), F32)],
        compiler_params=_params("parallel"),
        name="router",
    )(h, g, whi, wlo, br)


def _route_plan(idx, tm):
    t = idx.shape[0]
    n_pairs = 2 * t
    n_rows = n_pairs + N_EXPERTS * tm
    e_flat = jnp.concatenate([idx[:, 0], idx[:, 1]])
    onehot = (e_flat[:, None] == jnp.arange(N_EXPERTS, dtype=jnp.int32)[None, :]).astype(jnp.int32)
    csum = jnp.cumsum(onehot, axis=0)
    rank = jnp.sum(onehot * csum, axis=1) - 1
    counts = csum[-1]
    padded = ((counts + tm - 1) // tm) * tm
    ends = jnp.cumsum(padded)
    starts = ends - padded
    pos = starts[e_flat] + rank
    order = jnp.argsort(e_flat, stable=True).astype(jnp.int32)
    first = jnp.cumsum(counts) - counts
    r = jnp.minimum(jnp.arange(n_rows, dtype=jnp.int32), ends[-1] - 1)
    e_r = jnp.minimum(jnp.searchsorted(ends, r, side="right").astype(jnp.int32), N_EXPERTS - 1)
    local = r - starts[e_r]
    src_pair = order[jnp.clip(first[e_r] + local, 0, n_pairs - 1)]
    src_tok = jnp.where(local < counts[e_r], src_pair % t, 0).astype(jnp.int32)
    tile_expert = e_r[::tm]
    n_valid = (ends[-1] // tm).astype(jnp.int32).reshape(1)
    return pos.astype(jnp.int32), src_tok, tile_expert, n_valid


def _gather_rows(table, idx):
    n = idx.shape[0]
    info = plsc.get_sparse_core_info()
    n_workers = info.num_cores * info.num_subcores
    per_worker = n // n_workers
    assert per_worker * n_workers == n and per_worker % SC_CHUNK == 0, (n, n_workers)
    mesh = plsc.VectorSubcoreMesh(core_axis_name="c", subcore_axis_name="s")

    @functools.partial(
        pl.kernel, mesh=mesh,
        out_type=jax.ShapeDtypeStruct((n,) + table.shape[1:], table.dtype),
        scratch_types=[pltpu.VMEM((SC_CHUNK,), jnp.int32),
                       pltpu.VMEM((SC_CHUNK,) + table.shape[1:], table.dtype),
                       pltpu.SemaphoreType.DMA],
    )
    def gather(table_hbm, idx_hbm, out_hbm, idx_v, rows_v, sem):
        worker = lax.axis_index("s") * info.num_cores + lax.axis_index("c")
        base = worker * per_worker

        @pl.loop(0, per_worker // SC_CHUNK)
        def _(i):
            off = pl.multiple_of(base + i * SC_CHUNK, SC_CHUNK)
            pltpu.sync_copy(idx_hbm.at[pl.ds(off, SC_CHUNK)], idx_v)
            pltpu.async_copy(table_hbm.at[idx_v], rows_v, sem).wait()
            pltpu.sync_copy(rows_v, out_hbm.at[pl.ds(off, SC_CHUNK)])

    return gather(table, idx)


def _expert_ffn_kernel(te_ref, nv_ref, x_ref, w13_ref, w2_ref, y_ref):
    valid = pl.program_id(0) < nv_ref[0]

    @pl.when(valid)
    def _():
        _store_slabs(y_ref, _swiglu(_split_slabs(x_ref).astype(BF), w13_ref.at[0], w2_ref.at[0]))

    @pl.when(jnp.logical_not(valid))
    def _():
        y_ref[...] = jnp.zeros_like(y_ref)


def _expert_ffn(xs, tile_expert, n_valid, w13, w2, tm):
    n_rows = xs.shape[0] // ROW_SLABS
    last = lambda nv: jnp.maximum(nv[0] - 1, 0)
    grid_spec = pltpu.PrefetchScalarGridSpec(
        num_scalar_prefetch=2,
        grid=(n_rows // tm,),
        in_specs=[_slab_spec(tm, lambda i, te, nv: (jnp.minimum(i, last(nv)), 0)),
                  pl.BlockSpec((1, D_MODEL, 2 * D_FF), lambda i, te, nv: (te[i], 0, 0)),
                  pl.BlockSpec((1, D_FF, D_MODEL), lambda i, te, nv: (te[i], 0, 0))],
        out_specs=_slab_spec(tm, lambda i, te, nv: (i, 0)),
    )
    return pl.pallas_call(
        _expert_ffn_kernel,
        grid_spec=grid_spec,
        out_shape=jax.ShapeDtypeStruct((n_rows * ROW_SLABS, LANES), F32),
        compiler_params=pltpu.CompilerParams(dimension_semantics=("arbitrary",),
                                             vmem_limit_bytes=EXPERT_VMEM_LIMIT),
        name="expert_ffn",
    )(tile_expert, n_valid, xs, w13, w2)


def _combine_kernel(h_ref, y0_ref, y1_ref, wts_ref, gf_ref, *rest):
    out_ref = rest[-1]
    wts = wts_ref[...]
    tot = h_ref[...] + wts[:, 0:1] * _split_slabs(y0_ref) + wts[:, 1:2] * _split_slabs(y1_ref)
    out_ref[...] = _rms(tot, gf_ref[...])


def _combine(h, yg, wts, g_final, chunk, n_chunks, out_so_far, tm=1024):
    t = h.shape[0]
    nt = t // n_chunks // tm
    first = chunk * nt
    in_specs = [pl.BlockSpec((tm, D_MODEL), lambda i: (first + i, 0)),
                _slab_spec(tm, lambda i: (i, 0)),
                _slab_spec(tm, lambda i: (nt + i, 0)),
                pl.BlockSpec((tm, LANES), lambda i: (i, 0)),
                _const_spec((1, D_MODEL))]
    args = [h, yg, yg, wts, g_final]
    aliases = {}
    if out_so_far is not None:
        in_specs.append(pl.BlockSpec(memory_space=pl.ANY))
        args.append(out_so_far)
        aliases = {len(args) - 1: 0}
    return pl.pallas_call(
        _combine_kernel,
        grid=(nt,),
        in_specs=in_specs,
        out_specs=pl.BlockSpec((tm, D_MODEL), lambda i: (first + i, 0)),
        out_shape=jax.ShapeDtypeStruct((t, D_MODEL), F32),
        input_output_aliases=aliases,
        compiler_params=_params("parallel"),
        name="combine",
    )(*args)


def _moe(h, g, router_w, router_b, w13, w2, g_final, tm=512):
    wr = jnp.pad(router_w, ((0, 0), (0, LANES - N_EXPERTS)))
    whi = wr.astype(BF)
    wlo = (wr - whi.astype(F32)).astype(BF)
    br = jnp.pad(router_b, (0, LANES - N_EXPERTS)).reshape(1, LANES)
    as_rows = lambda a: a.reshape(-1, ROW_SLABS, LANES)
    as_slabs = lambda a: a.reshape(-1, LANES)
    chunks = range(MOE_CHUNKS)
    routed = [_router(h, g, whi, wlo, br, c, MOE_CHUNKS) for c in chunks]
    plans = [_route_plan(idx, tm) for _, idx, _ in routed]
    xs = [as_slabs(_gather_rows(as_rows(routed[c][0]), plans[c][1])) for c in chunks]
    ys = [_expert_ffn(xs[c], plans[c][2], plans[c][3], w13, w2, tm) for c in chunks]
    yg = [as_slabs(_gather_rows(as_rows(ys[c]), plans[c][0])) for c in chunks]
    out = None
    for c in chunks:
        out = _combine(h, yg[c], routed[c][2], g_final, c, MOE_CHUNKS, out)
    return out


def _feature_major(w, heads):
    wt = w.T.reshape(heads, HEAD_DIM, D_MODEL)
    return jnp.pad(wt, ((0, 0), (0, VT_ROWS - HEAD_DIM), (0, 0))).reshape(heads * VT_ROWS, D_MODEL)


def _pack_w_in(w_in):
    cuts = np.cumsum((512, 512, 512, 512, 512, 128, 128, 512, 512, 512, 8))[:-1].tolist()
    a_u, a_v, b_x, b_y, c_q, c_k, c_v, d_q, d_k, d_v, d_f = jnp.split(w_in, cuts, axis=-1)
    w = jnp.concatenate([b_x, b_y, d_q, d_k, c_k, c_v], axis=-1).astype(BF)
    wa = jnp.concatenate([a_u, a_v], axis=-1).astype(BF)
    wf = jnp.pad(d_f, ((0, 0), (0, LANES - FOX_HEADS))).astype(BF)
    wt = jnp.concatenate([_feature_major(d_v, FOX_HEADS), c_q.T, _feature_major(c_v, SWA_KV)], axis=0).astype(BF)
    return w, wa, wf, wt


def _fm_ones():
    ones = np.zeros((FM_ROWS, LANES), np.float32)
    for base, heads in ((0, FOX_HEADS), (VT_ALL + BRANCH_W, SWA_KV)):
        for h in range(heads):
            ones[base + h * VT_ROWS + HEAD_DIM, :] = 1.0
    return jnp.asarray(ones)


def _row(v):
    return v.reshape(1, -1)


def _hybrid_mixer(h, bsz, norm_mix, w_in, sgu_g, sgu_w, sgu_b, conv_w, conv_b, rg_wa, rg_ba, rg_wx, rg_bx,
                  rg_lambda, swa_sinks, fox_bf, w_branch, w_gate, b_gate, w_out):
    w, wa, wf, wt = _pack_w_in(w_in)
    proj, f, o_a, vt, sqt, svt = _inproj(h, _row(norm_mix), w, wa, wf, wt, _fm_ones(), _row(sgu_g), sgu_w, sgu_b.T)
    wax = jnp.concatenate([rg_wa, rg_wx], axis=-1).astype(BF)
    o_b = _rglru(proj, bsz, conv_w, _row(conv_b), wax, _row(rg_ba), _row(rg_bx), _row(rg_lambda))
    o_c = _swa(proj, sqt, svt, bsz, swa_sinks)
    bf_pad = jnp.pad(fox_bf, (0, LANES - FOX_HEADS)).reshape(1, LANES)
    qp, kp = _fox_prep(proj, f, bsz, bf_pad)
    o_d = _fox(qp, kp, vt, bsz)
    return _merge(h, _row(norm_mix), o_a, o_b, o_c, o_d, w_gate.astype(BF), b_gate[:, None, :],
                  w_branch.astype(BF), w_out.astype(BF))


def kernel(x, mem, norm_mix, w_in, sgu_g, sgu_w, sgu_b, conv_w, conv_b, rg_wa, rg_ba, rg_wx, rg_bx, rg_lambda, swa_sinks, fox_bf, w_branch, w_gate, b_gate, w_out, norm_cross, norm_mem, wq_c, wkv_c, wo_c, norm_ffn, dense_w13, dense_w2, router_w, router_b, moe_w13, moe_w2, norm_final):
    bsz, seq, d = x.shape
    m_len = mem.shape[1]
    depth = norm_mix.shape[0]
    assert depth == 2, "the final RMSNorm is fused into the routed layer, which must be the last one"
    h = x.reshape(bsz * seq, d)
    mem2 = mem.reshape(bsz * m_len, d)
    for l in range(depth):
        h = _hybrid_mixer(h, bsz, norm_mix[l], w_in[l], sgu_g[l], sgu_w[l], sgu_b[l], conv_w[l], conv_b[l],
                          rg_wa[l], rg_ba[l], rg_wx[l], rg_bx[l], rg_lambda[l], swa_sinks[l], fox_bf[l],
                          w_branch[l], w_gate[l], b_gate[l], w_out[l])
        kv = _memkv(mem2, _row(norm_mem[l]), wkv_c[l].astype(BF), m_len)
        h = _cross(h, _row(norm_cross[l]), wq_c[l].astype(BF), kv, wo_c[l].astype(BF), bsz, m_len)
        if l % 2 == 0:
            h = _ffn(h, _row(norm_ffn[l]), dense_w13[l // 2].astype(BF), dense_w2[l // 2].astype(BF))
        else:
            h = _moe(h, _row(norm_ffn[l]), router_w[l // 2], router_b[l // 2], moe_w13[l // 2].astype(BF),
                     moe_w2[l // 2].astype(BF), _row(norm_final))
    return h.reshape(bsz, seq, d)
```

```python
import functools
import math

import numpy as np
import jax
import jax.numpy as jnp
from jax import lax
from jax.experimental import pallas as pl
from jax.experimental.pallas import tpu as pltpu
from jax.experimental.pallas import tpu_sc as plsc

F32 = jnp.float32
BF = jnp.bfloat16

D_MODEL = 1024
BRANCH_W = 512
HEAD_DIM = 64
CHUNK = 128
SGU_GROUPS = 4
LRU_HEADS = 4
LRU_C = 8.0
CONV_W = 4
SWA_HEADS = 8
SWA_KV = 2
FOX_HEADS = 8
FOX_PAIRS = FOX_HEADS // 2
X_HEADS = 4
X_HEAD_DIM = 128
D_FF = 2816
N_EXPERTS = 8
EPS = 1e-6
LOG2E = math.log2(math.e)
NEG = -1e30
LANES = 128
ROW_SLABS = D_MODEL // LANES
SC_CHUNK = 64
MOE_CHUNKS = 2
VMEM_LIMIT = 56 * 1024 * 1024
BIG_VMEM_LIMIT = 62 * 1024 * 1024

COL_BX, COL_BY, COL_DQ, COL_DK = range(4)
N_WIDE = 4
N_PROJ = N_WIDE * BRANCH_W + 2 * SWA_KV * HEAD_DIM
VT_ROWS = 80
VT_ALL = FOX_HEADS * VT_ROWS
SWA_VT = SWA_KV * VT_ROWS
FM_ROWS = VT_ALL + BRANCH_W + SWA_VT
FOX_UNROLL = 4
FOX_STEP_PAIRS = 1
FF_SPLITS = ((0, 1536), (1536, D_FF))


def _rms(x, g):
    return x * lax.rsqrt(jnp.mean(x * x, axis=-1, keepdims=True) + EPS) * g


def _sigmoid(x):
    return 1.0 / (1.0 + jnp.exp(-x))


def _gelu(x):
    return 0.5 * x * (1.0 + jnp.tanh(math.sqrt(2.0 / math.pi) * (x + 0.044715 * (x * x * x))))


def _dot(a, b):
    return jnp.dot(a, b, preferred_element_type=F32)


def _dot_nt(a, b):
    return lax.dot_general(a, b, (((1,), (1,)), ((), ())), preferred_element_type=F32)


def _shift_rows(x, d, fill):
    row = lax.broadcasted_iota(jnp.int32, x.shape, 0)
    return jnp.where(row >= d, pltpu.roll(x, d, 0), fill)


def _params(*sem, vmem=VMEM_LIMIT):
    return pltpu.CompilerParams(dimension_semantics=sem, vmem_limit_bytes=vmem)


def _const_spec(shape):
    nd = len(shape)
    return pl.BlockSpec(shape, lambda *_: (0,) * nd, pipeline_mode=pl.Buffered(1))


def _inproj_kernel(h_ref, g_ref, w_ref, wa_ref, wf_ref, wt_ref, ones_ref, sg_ref, sw_ref, sbt_ref,
                   proj_ref, f_ref, oa_ref, vt_ref, sqt_ref, svt_ref):
    xn = _rms(h_ref[...], g_ref[...]).astype(BF)
    tm = xn.shape[0]
    n_col = w_ref.shape[1]
    for c in range(0, n_col, BRANCH_W):
        w = min(BRANCH_W, n_col - c)
        proj_ref[:, c:c + w] = _dot(xn, w_ref[:, c:c + w]).astype(BF)
    f_ref[...] = _dot(xn, wf_ref[...])
    ones = jnp.concatenate([ones_ref[...]] * (tm // LANES), axis=1)
    fm = (_dot_nt(wt_ref[...], xn) + ones).astype(BF)
    vt_ref[...] = fm[:VT_ALL]
    sqt_ref[...] = fm[VT_ALL:VT_ALL + BRANCH_W]
    svt_ref[...] = fm[VT_ALL + BRANCH_W:]

    u = _gelu(_dot(xn, wa_ref[:, :BRANCH_W]))
    v = _gelu(_dot(xn, wa_ref[:, BRANCH_W:]))
    vn = _rms(v, sg_ref[...]).astype(BF)
    row = lax.broadcasted_iota(jnp.int32, (CHUNK, CHUNK), 0)
    col = lax.broadcasted_iota(jnp.int32, (CHUNK, CHUNK), 1)
    gw = BRANCH_W // SGU_GROUPS
    for g in range(SGU_GROUPS):
        wg = jnp.where(col <= row, sw_ref[g], 0.0).astype(BF)
        bg = sbt_ref[:, g:g + 1]
        for c in range(tm // CHUNK):
            rs = slice(c * CHUNK, (c + 1) * CHUNK)
            cs = slice(g * gw, (g + 1) * gw)
            mixed = _dot(wg, vn[rs, cs]) + bg
            oa_ref[rs, cs] = (u[rs, cs] * mixed).astype(BF)


def _inproj(h, g, w, wa, wf, wt, ones, sgu_g, sgu_w, sgu_bt, tm=1024):
    t = h.shape[0]
    row = lambda i: (i, 0)
    col = lambda i: (0, i)
    return pl.pallas_call(
        _inproj_kernel,
        grid=(t // tm,),
        in_specs=[pl.BlockSpec((tm, D_MODEL), row),
                  _const_spec((1, D_MODEL)),
                  _const_spec((D_MODEL, N_PROJ)),
                  _const_spec((D_MODEL, 2 * BRANCH_W)),
                  _const_spec((D_MODEL, LANES)),
                  _const_spec((FM_ROWS, D_MODEL)),
                  _const_spec((FM_ROWS, LANES)),
                  _const_spec((1, BRANCH_W)),
                  _const_spec((SGU_GROUPS, CHUNK, CHUNK)),
                  _const_spec((CHUNK, SGU_GROUPS))],
        out_specs=[pl.BlockSpec((tm, N_PROJ), row),
                   pl.BlockSpec((tm, LANES), row),
                   pl.BlockSpec((tm, BRANCH_W), row),
                   pl.BlockSpec((VT_ALL, tm), col),
                   pl.BlockSpec((BRANCH_W, tm), col),
                   pl.BlockSpec((SWA_VT, tm), col)],
        out_shape=[jax.ShapeDtypeStruct((t, N_PROJ), BF), jax.ShapeDtypeStruct((t, LANES), F32),
                   jax.ShapeDtypeStruct((t, BRANCH_W), BF), jax.ShapeDtypeStruct((VT_ALL, t), BF),
                   jax.ShapeDtypeStruct((BRANCH_W, t), BF), jax.ShapeDtypeStruct((SWA_VT, t), BF)],
        compiler_params=_params("parallel", vmem=BIG_VMEM_LIMIT),
        name="inproj",
    )(h, g, w, wa, wf, wt, ones, sgu_g, sgu_w, sgu_bt)


def _rglru_kernel(x_ref, y_ref, cw_ref, cb_ref, wax_ref, ba_ref, bx_ref, lam_ref, o_ref, xs_ref, hc_ref):
    ts = x_ref.shape[0]
    hw = BRANCH_W // LRU_HEADS

    @pl.when(pl.program_id(1) == 0)
    def _():
        xs_ref[0:8, :] = jnp.zeros((8, BRANCH_W), F32)
        hc_ref[...] = jnp.zeros_like(hc_ref)

    x = x_ref[...].astype(F32)
    xs_ref[8:8 + ts, :] = x
    cw = cw_ref[...]
    xc = cb_ref[...] + cw[CONV_W - 1:CONV_W] * x
    for k in range(1, CONV_W):
        xc = xc + cw[CONV_W - 1 - k:CONV_W - k] * xs_ref[8 - k:8 - k + ts, :]
    xs_ref[0:8, :] = x[ts - 8:ts, :]

    xcb = xc.astype(BF)
    r_parts, i_parts = [], []
    for hd in range(LRU_HEADS):
        z = _dot(xcb[:, hd * hw:(hd + 1) * hw], wax_ref[hd])
        r_parts.append(z[:, :hw])
        i_parts.append(z[:, hw:])
    r = _sigmoid(jnp.concatenate(r_parts, axis=1) + ba_ref[...])
    gi = _sigmoid(jnp.concatenate(i_parts, axis=1) + bx_ref[...])
    nl = -lam_ref[...]
    softplus = jnp.maximum(nl, 0.0) + jnp.log1p(jnp.exp(-jnp.abs(nl)))
    log_a = (-LRU_C * r) * softplus
    a = jnp.exp(log_a)
    th = jnp.tanh(log_a)
    e2 = -2.0 * th / (1.0 - th)
    b = (xc * gi) * (e2 * lax.rsqrt(jnp.maximum(e2, 1e-30)))

    sub = lax.broadcasted_iota(jnp.int32, a.shape, 0) & 7
    for d in (1, 2, 4):
        keep = sub >= d
        a_sh = jnp.where(keep, pltpu.roll(a, d, 0), 1.0)
        b_sh = jnp.where(keep, pltpu.roll(b, d, 0), 0.0)
        b = b + a * b_sh
        a = a * a_sh
    carry = hc_ref[...]
    groups = []
    for g in range(ts // 8):
        hg = a[8 * g:8 * g + 8] * carry + b[8 * g:8 * g + 8]
        groups.append(hg)
        carry = hg[7:8]
    h = jnp.concatenate(groups, axis=0)
    hc_ref[...] = carry
    o_ref[...] = (h * _gelu(y_ref[...].astype(F32))).astype(BF)


def _rglru(proj, bsz, cw, cb, wax, ba, bx, lam, ts=512):
    t = proj.shape[0]
    ns = t // bsz // ts
    return pl.pallas_call(
        _rglru_kernel,
        grid=(bsz, ns),
        in_specs=[pl.BlockSpec((ts, BRANCH_W), lambda b, j: (b * ns + j, COL_BX)),
                  pl.BlockSpec((ts, BRANCH_W), lambda b, j: (b * ns + j, COL_BY)),
                  _const_spec((CONV_W, BRANCH_W)),
                  _const_spec((1, BRANCH_W)),
                  _const_spec((LRU_HEADS, BRANCH_W // LRU_HEADS, 2 * BRANCH_W // LRU_HEADS)),
                  _const_spec((1, BRANCH_W)),
                  _const_spec((1, BRANCH_W)),
                  _const_spec((1, BRANCH_W))],
        out_specs=pl.BlockSpec((ts, BRANCH_W), lambda b, j: (b * ns + j, 0)),
        out_shape=jax.ShapeDtypeStruct((t, BRANCH_W), BF),
        scratch_shapes=[pltpu.VMEM((ts + 8, BRANCH_W), F32), pltpu.VMEM((1, BRANCH_W), F32)],
        compiler_params=_params("parallel", "arbitrary"),
        name="rglru",
    )(proj, proj, cw, cb, wax, ba, bx, lam)


def _swa_tables(sinks):
    t_idx = np.arange(CHUNK)[None, :] + CHUNK
    dist = (t_idx - np.arange(2 * CHUNK)[:, None]).astype(np.float32)
    in_win = (dist >= 0) & (dist < CHUNK)
    slopes = 2.0 ** (-(8.0 / SWA_HEADS) * np.arange(1, SWA_HEADS + 1, dtype=np.float32))
    bias = np.where(in_win[None], -slopes[:, None, None] * dist[None] * LOG2E, NEG)
    grp = SWA_HEADS // SWA_KV
    bias = bias.reshape(SWA_KV, grp, 2 * CHUNK, CHUNK).transpose(0, 2, 1, 3).reshape(SWA_KV, 2 * CHUNK, grp * CHUNK)
    sink = jnp.repeat(sinks.astype(F32) * LOG2E, CHUNK).reshape(SWA_KV, 1, grp * CHUNK)
    return jnp.asarray(bias, F32), sink


def _swa_kernel(qt_ref, kv_ref, kvp_ref, vt_ref, vtp_ref, bias_ref, sink_ref, o_ref):
    tq = kv_ref.shape[0]
    grp = SWA_HEADS // SWA_KV
    first_key = jnp.where(pl.program_id(1) == 0, CHUNK, 0)
    k_all = jnp.concatenate([kvp_ref[...], kv_ref[...]], axis=0)
    vt_all = jnp.concatenate([vtp_ref[...], vt_ref[...]], axis=1)
    qt = (qt_ref[...].astype(F32) * (HEAD_DIM ** -0.5 * LOG2E)).astype(BF)
    key_row = lax.broadcasted_iota(jnp.int32, (2 * CHUNK, grp * CHUNK), 0)
    kw = 2 * SWA_KV * HEAD_DIM
    for qb in range(tq // CHUNK):
        cols = slice(qb * CHUNK, (qb + 1) * CHUNK)
        kb = k_all[qb * CHUNK:(qb + 2) * CHUNK]
        vtb = vt_all[:, qb * CHUNK:(qb + 2) * CHUNK]
        outs = []
        for kh in range(SWA_KV):
            qg = jnp.concatenate([qt[(kh * grp + g) * HEAD_DIM:(kh * grp + g + 1) * HEAD_DIM, cols]
                                  for g in range(grp)], axis=1)
            parts = []
            if kh > 0:
                parts.append(jnp.zeros((kh * HEAD_DIM, grp * CHUNK), BF))
            parts.append(qg)
            parts.append(jnp.zeros((kw - (kh + 1) * HEAD_DIM, grp * CHUNK), BF))
            st = _dot(kb, jnp.concatenate(parts, axis=0)) + bias_ref[kh]
            if qb == 0:
                st = jnp.where(key_row >= first_key, st, NEG)
            sink = sink_ref[kh]
            m = jnp.maximum(jnp.max(st, axis=0, keepdims=True), sink)
            pt = jnp.exp2(st - m).astype(BF)
            acc = _dot(vtb[kh * VT_ROWS:(kh + 1) * VT_ROWS], pt)
            ot = acc[:HEAD_DIM] / (acc[HEAD_DIM:HEAD_DIM + 1] + jnp.exp2(sink - m))
            outs.extend(ot[:, g * CHUNK:(g + 1) * CHUNK] for g in range(grp))
        o_ref[cols, :] = jnp.concatenate(outs, axis=0).T.astype(BF)


def _swa(proj, sqt, svt, bsz, sinks, tq=1024):
    t = proj.shape[0]
    seq = t // bsz
    nq = seq // tq
    per = tq // CHUNK
    kv_blk = N_WIDE * BRANCH_W // (2 * SWA_KV * HEAD_DIM)
    prev = lambda b, i: b * (seq // CHUNK) + jnp.maximum(i * per - 1, 0)
    bias, sink = _swa_tables(sinks)
    grp = SWA_HEADS // SWA_KV
    return pl.pallas_call(
        _swa_kernel,
        grid=(bsz, nq),
        in_specs=[pl.BlockSpec((BRANCH_W, tq), lambda b, i: (0, b * nq + i)),
                  pl.BlockSpec((tq, 2 * SWA_KV * HEAD_DIM), lambda b, i: (b * nq + i, kv_blk)),
                  pl.BlockSpec((CHUNK, 2 * SWA_KV * HEAD_DIM), lambda b, i: (prev(b, i), kv_blk)),
                  pl.BlockSpec((SWA_VT, tq), lambda b, i: (0, b * nq + i)),
                  pl.BlockSpec((SWA_VT, CHUNK), lambda b, i: (0, prev(b, i))),
                  _const_spec((SWA_KV, 2 * CHUNK, grp * CHUNK)),
                  _const_spec((SWA_KV, 1, grp * CHUNK))],
        out_specs=pl.BlockSpec((tq, BRANCH_W), lambda b, i: (b * nq + i, 0)),
        out_shape=jax.ShapeDtypeStruct((t, BRANCH_W), BF),
        compiler_params=_params("parallel", "parallel"),
        name="swa",
    )(sqt, proj, proj, svt, svt, bias, sink)


def _aug_tables():
    eq = np.zeros((3, LANES, FOX_PAIRS * LANES), np.float32)
    ek = np.zeros((3, LANES, FOX_PAIRS * LANES), np.float32)
    oq = np.zeros((1, FOX_PAIRS * LANES), np.float32)
    ok = np.zeros((1, FOX_PAIRS * LANES), np.float32)
    for h in range(FOX_HEADS):
        base = (h // 2) * LANES + 6 * (h % 2)
        for s in range(3):
            eq[s, h, base + s] = 1.0
            ek[s, h, base + 3 + s] = -1.0
            ok[0, base + s] = 1.0
            oq[0, base + 3 + s] = 1.0
    return eq, ek, oq, ok


def _fox_prep_kernel(q_ref, k_ref, f_ref, bf_ref, eq_ref, ek_ref, oq_ref, ok_ref, qp_ref, kp_ref, cum_ref):
    ts = q_ref.shape[0]

    @pl.when(pl.program_id(1) == 0)
    def _():
        cum_ref[...] = jnp.zeros_like(cum_ref)

    z = f_ref[...] + bf_ref[...]
    c = jnp.minimum(z, 0.0) - jnp.log1p(jnp.exp(-jnp.abs(z)))
    d = 1
    while d < ts:
        c = c + _shift_rows(c, d, 0.0)
        d *= 2
    c = c + cum_ref[...]
    cum_ref[...] = c[ts - 1:ts, :]
    c = c * LOG2E
    c1 = c.astype(BF)
    r1 = c - c1.astype(F32)
    c2 = r1.astype(BF)
    c3 = (r1 - c2.astype(F32)).astype(BF)
    augq = _dot(c1, eq_ref[0]) + _dot(c2, eq_ref[1]) + _dot(c3, eq_ref[2]) + oq_ref[...]
    augk = _dot(c1, ek_ref[0]) + _dot(c2, ek_ref[1]) + _dot(c3, ek_ref[2]) + ok_ref[...]
    qt = (q_ref[...].astype(F32) * (HEAD_DIM ** -0.5 * LOG2E)).T.astype(BF)
    augqt = augq.T.astype(BF)
    k = k_ref[...]
    for p in range(FOX_PAIRS):
        qp_ref[2 * p * LANES:(2 * p + 1) * LANES, :] = qt[p * LANES:(p + 1) * LANES]
        qp_ref[(2 * p + 1) * LANES:(2 * p + 2) * LANES, :] = augqt[p * LANES:(p + 1) * LANES]
        kp_ref[:, 2 * p * LANES:(2 * p + 1) * LANES] = k[:, p * LANES:(p + 1) * LANES]
        kp_ref[:, (2 * p + 1) * LANES:(2 * p + 2) * LANES] = augk[:, p * LANES:(p + 1) * LANES].astype(BF)


def _fox_prep(proj, f, bsz, bf_pad, ts=1024):
    t = proj.shape[0]
    ns = t // bsz // ts
    eq, ek, oq, ok = _aug_tables()
    wide = 2 * FOX_PAIRS * LANES
    return pl.pallas_call(
        _fox_prep_kernel,
        grid=(bsz, ns),
        in_specs=[pl.BlockSpec((ts, BRANCH_W), lambda b, j: (b * ns + j, COL_DQ)),
                  pl.BlockSpec((ts, BRANCH_W), lambda b, j: (b * ns + j, COL_DK)),
                  pl.BlockSpec((ts, LANES), lambda b, j: (b * ns + j, 0)),
                  _const_spec((1, LANES)),
                  _const_spec(eq.shape), _const_spec(ek.shape), _const_spec(oq.shape), _const_spec(ok.shape)],
        out_specs=[pl.BlockSpec((wide, ts), lambda b, j: (0, b * ns + j)),
                   pl.BlockSpec((ts, wide), lambda b, j: (b * ns + j, 0))],
        out_shape=[jax.ShapeDtypeStruct((wide, t), BF), jax.ShapeDtypeStruct((t, wide), BF)],
        scratch_shapes=[pltpu.VMEM((1, LANES), F32)],
        compiler_params=_params("parallel", "arbitrary"),
        name="fox_prep",
    )(proj, proj, f, bf_pad, jnp.asarray(eq, BF), jnp.asarray(ek, BF), jnp.asarray(oq), jnp.asarray(ok))


def _fox_kernel(q_ref, k_ref, vt_ref, o_ref, st_ref, pt_ref, al_ref, m_ref, acc_ref):
    tq = q_ref.shape[1]
    tk = tq // 2
    n_heads = m_ref.shape[0]
    qi = pl.program_id(2)
    row = lax.broadcasted_iota(jnp.int32, (2 * LANES, tq), 0)
    in0 = (row < HEAD_DIM) | ((row >= LANES) & (row < LANES + 6))
    in1 = ((row >= HEAD_DIM) & (row < LANES)) | ((row >= LANES + 6) & (row < LANES + 12))
    qh = []
    for p in range(n_heads // 2):
        q = q_ref[2 * p * LANES:(2 * p + 2) * LANES, :]
        zero = jnp.zeros_like(q)
        qh += [jnp.where(in0, q, zero), jnp.where(in1, q, zero)]
    for h in range(n_heads):
        m_ref[h] = jnp.full(m_ref.shape[1:], NEG, F32)
        acc_ref[h] = jnp.zeros(acc_ref.shape[1:], F32)
        pt_ref[1, h] = jnp.zeros(pt_ref.shape[2:], BF)
        al_ref[1, h] = jnp.ones(al_ref.shape[2:], F32)

    def scores(t, slot):
        rows = pl.ds(pl.multiple_of(t * tk, tk), tk)
        for p in range(n_heads // 2):
            kp = k_ref[rows, 2 * p * LANES:(2 * p + 2) * LANES]
            for h in (2 * p, 2 * p + 1):
                st_ref[slot, h] = _dot(kp, qh[h])

    def numerators(slot, key_offset):
        for h in range(n_heads):
            st = st_ref[slot, h]
            if key_offset is not None:
                key = lax.broadcasted_iota(jnp.int32, (tk, tq), 0) + key_offset
                qry = lax.broadcasted_iota(jnp.int32, (tk, tq), 1)
                st = jnp.where(key <= qry, st, NEG)
            m_old = m_ref[h]
            m_new = jnp.maximum(m_old, jnp.max(st, axis=0, keepdims=True))
            al_ref[slot, h] = jnp.exp2(m_old - m_new)
            pt_ref[slot, h] = jnp.exp2(st - m_new).astype(BF)
            m_ref[h] = m_new

    def accumulate(t, slot):
        cols = pl.ds(pl.multiple_of(t * tk, tk), tk)
        for h in range(n_heads):
            vt = vt_ref[h * VT_ROWS:(h + 1) * VT_ROWS, cols]
            acc_ref[h] = al_ref[slot, h] * acc_ref[h] + _dot(vt, pt_ref[slot, h])

    def tile_pair(j, diagonal):
        scores(2 * j + 1, 1)
        accumulate(jnp.maximum(2 * j - 1, 0), 1)
        numerators(0, 0 if diagonal else None)
        if not diagonal:
            scores(2 * j + 2, 0)
        accumulate(2 * j, 0)
        numerators(1, tk if diagonal else None)

    def body(jj, carry):
        for u in range(FOX_UNROLL):
            tile_pair(FOX_UNROLL * jj + u, False)
        return carry

    def remainder(j, carry):
        tile_pair(j, False)
        return carry

    scores(0, 0)
    lax.fori_loop(0, qi // FOX_UNROLL, body, 0)
    lax.fori_loop((qi // FOX_UNROLL) * FOX_UNROLL, qi, remainder, 0)
    tile_pair(qi, True)
    accumulate(2 * qi + 1, 1)
    outs = []
    for h in range(n_heads):
        acc = acc_ref[h]
        outs.append(acc[:HEAD_DIM] / acc[HEAD_DIM:HEAD_DIM + 1])
    o_ref[...] = jnp.concatenate(outs, axis=0).T.astype(BF)


def _fox(qp, kp, vt, bsz, tq=512):
    t = kp.shape[0]
    seq = t // bsz
    nq = seq // tq
    np_ = FOX_STEP_PAIRS
    nh = 2 * np_
    return pl.pallas_call(
        _fox_kernel,
        grid=(bsz, FOX_PAIRS // np_, nq),
        in_specs=[pl.BlockSpec((np_ * 2 * LANES, tq), lambda b, p, i: (p, b * nq + i)),
                  pl.BlockSpec((seq, np_ * 2 * LANES), lambda b, p, i: (b, p)),
                  pl.BlockSpec((nh * VT_ROWS, seq), lambda b, p, i: (p, b))],
        out_specs=pl.BlockSpec((tq, np_ * LANES), lambda b, p, i: (b * nq + i, p)),
        out_shape=jax.ShapeDtypeStruct((t, BRANCH_W), BF),
        scratch_shapes=[pltpu.VMEM((2, nh, tq // 2, tq), F32),
                        pltpu.VMEM((2, nh, tq // 2, tq), BF),
                        pltpu.VMEM((2, nh, 1, tq), F32),
                        pltpu.VMEM((nh, 1, tq), F32),
                        pltpu.VMEM((nh, VT_ROWS, tq), F32)],
        compiler_params=_params("parallel", "parallel", "arbitrary"),
        name="fox",
    )(qp, kp, vt)


def _merge_kernel(h_ref, g_ref, oa_ref, ob_ref, oc_ref, od_ref, wg_ref, bg_ref, wb_ref, wo_ref, out_ref):
    h = h_ref[...]
    xn = _rms(h, g_ref[...]).astype(BF)
    merged = None
    for br, o_ref in enumerate((oa_ref, ob_ref, oc_ref, od_ref)):
        gate = _sigmoid(_dot(xn, wg_ref[br]) + bg_ref[br])
        term = gate * _dot(o_ref[...], wb_ref[br])
        merged = term if merged is None else merged + term
    out_ref[...] = h + _dot(merged.astype(BF), wo_ref[...])


def _merge(h, g, oa, ob, oc, od, wg, bg, wb, wo, tm=1024):
    t = h.shape[0]
    row = lambda i: (i, 0)
    return pl.pallas_call(
        _merge_kernel,
        grid=(t // tm,),
        in_specs=[pl.BlockSpec((tm, D_MODEL), row),
                  _const_spec((1, D_MODEL)),
                  pl.BlockSpec((tm, BRANCH_W), row), pl.BlockSpec((tm, BRANCH_W), row),
                  pl.BlockSpec((tm, BRANCH_W), row), pl.BlockSpec((tm, BRANCH_W), row),
                  _const_spec((4, D_MODEL, D_MODEL)),
                  _const_spec((4, 1, D_MODEL)),
                  _const_spec((4, BRANCH_W, D_MODEL)),
                  _const_spec((D_MODEL, D_MODEL))],
        out_specs=pl.BlockSpec((tm, D_MODEL), row),
        out_shape=jax.ShapeDtypeStruct((t, D_MODEL), F32),
        compiler_params=_params("parallel", vmem=BIG_VMEM_LIMIT),
        name="merge",
    )(h, g, oa, ob, oc, od, wg, bg, wb, wo)


def _memkv_kernel(mem_ref, g_ref, w_ref, kv_ref):
    mn = _rms(mem_ref[...], g_ref[...]).astype(BF)
    kv_ref[...] = _dot(mn, w_ref[...]).astype(BF)


def _memkv(mem2, g, w, m_len):
    n = mem2.shape[0]
    width = 2 * X_HEADS * X_HEAD_DIM
    return pl.pallas_call(
        _memkv_kernel,
        grid=(n // m_len,),
        in_specs=[pl.BlockSpec((m_len, D_MODEL), lambda b: (b, 0)),
                  _const_spec((1, D_MODEL)),
                  _const_spec((D_MODEL, width))],
        out_specs=pl.BlockSpec((m_len, width), lambda b: (b, 0)),
        out_shape=jax.ShapeDtypeStruct((n, width), BF),
        compiler_params=_params("parallel"),
        name="memkv",
    )(mem2, g, w)


def _cross_kernel(h_ref, g_ref, wq_ref, kv_ref, wo_ref, out_ref):
    h = h_ref[...]
    hn = _rms(h, g_ref[...]).astype(BF)
    q = _dot(hn, wq_ref[...]).astype(BF)
    kv = kv_ref[...]
    width = X_HEADS * X_HEAD_DIM
    outs = []
    for hd in range(X_HEADS):
        cs = slice(hd * X_HEAD_DIM, (hd + 1) * X_HEAD_DIM)
        s = _dot_nt(q[:, cs], kv[:, cs]) * (X_HEAD_DIM ** -0.5)
        m = jnp.max(s, axis=-1, keepdims=True)
        p = jnp.exp(s - m)
        denom = jnp.sum(p, axis=-1, keepdims=True)
        v = kv[:, width + hd * X_HEAD_DIM:width + (hd + 1) * X_HEAD_DIM]
        outs.append((_dot(p.astype(BF), v) / denom).astype(BF))
    o = jnp.concatenate(outs, axis=1)
    out_ref[...] = h + _dot(o, wo_ref[...])


def _cross(h, g, wq, kv, wo, bsz, m_len, tm=1024):
    t = h.shape[0]
    per = t // bsz // tm
    width = X_HEADS * X_HEAD_DIM
    return pl.pallas_call(
        _cross_kernel,
        grid=(bsz, per),
        in_specs=[pl.BlockSpec((tm, D_MODEL), lambda b, i: (b * per + i, 0)),
                  _const_spec((1, D_MODEL)),
                  _const_spec((D_MODEL, width)),
                  pl.BlockSpec((m_len, 2 * width), lambda b, i: (b, 0)),
                  _const_spec((width, D_MODEL))],
        out_specs=pl.BlockSpec((tm, D_MODEL), lambda b, i: (b * per + i, 0)),
        out_shape=jax.ShapeDtypeStruct((t, D_MODEL), F32),
        compiler_params=_params("parallel", "parallel"),
        name="cross",
    )(h, g, wq, kv, wo)


def _swiglu(xb, w13_ref, w2_ref):
    out = None
    for lo, hi in FF_SPLITS:
        gate = _dot(xb, w13_ref[:, lo:hi])
        up = _dot(xb, w13_ref[:, D_FF + lo:D_FF + hi])
        act = (gate * _sigmoid(gate) * up).astype(BF)
        part = _dot(act, w2_ref[lo:hi, :])
        out = part if out is None else out + part
    return out


def _ffn_kernel(h_ref, g_ref, w13_ref, w2_ref, out_ref):
    h = h_ref[...]
    out_ref[...] = h + _swiglu(_rms(h, g_ref[...]).astype(BF), w13_ref, w2_ref)


def _ffn(h, g, w13, w2, tm=1024):
    t = h.shape[0]
    return pl.pallas_call(
        _ffn_kernel,
        grid=(t // tm,),
        in_specs=[pl.BlockSpec((tm, D_MODEL), lambda i: (i, 0)),
                  _const_spec((1, D_MODEL)),
                  _const_spec((D_MODEL, 2 * D_FF)),
                  _const_spec((D_FF, D_MODEL))],
        out_specs=pl.BlockSpec((tm, D_MODEL), lambda i: (i, 0)),
        out_shape=jax.ShapeDtypeStruct((t, D_MODEL), F32),
        compiler_params=_params("parallel", vmem=BIG_VMEM_LIMIT),
        name="ffn",
    )(h, g, w13, w2)


def _split_slabs(ref):
    rows = ref.shape[0] // ROW_SLABS
    return jnp.concatenate([ref[pl.ds(c, rows, stride=ROW_SLABS), :] for c in range(ROW_SLABS)], axis=1)


def _store_slabs(ref, x):
    rows = ref.shape[0] // ROW_SLABS
    for c in range(ROW_SLABS):
        ref[pl.ds(c, rows, stride=ROW_SLABS), :] = x[:, c * LANES:(c + 1) * LANES]


def _slab_spec(tm, index_map):
    return pl.BlockSpec((tm * ROW_SLABS, LANES), index_map)


def _router_kernel(h_ref, g_ref, whi_ref, wlo_ref, br_ref, hn_ref, idx_ref, wts_ref):
    hn = _rms(h_ref[...], g_ref[...])
    _store_slabs(hn_ref, hn)
    hi = hn.astype(BF)
    lo = (hn - hi.astype(F32)).astype(BF)
    logits = _dot(hi, whi_ref[...]) + (_dot(lo, whi_ref[...]) + _dot(hi, wlo_ref[...])) + br_ref[...]
    lane = lax.broadcasted_iota(jnp.int32, logits.shape, 1)
    logits = jnp.where(lane < N_EXPERTS, logits, NEG)
    v1 = jnp.max(logits, axis=-1, keepdims=True)
    i1 = jnp.min(jnp.where(logits == v1, lane, LANES), axis=-1, keepdims=True)
    rest = jnp.where(lane == i1, NEG, logits)
    v2 = jnp.max(rest, axis=-1, keepdims=True)
    i2 = jnp.min(jnp.where(rest == v2, lane, LANES), axis=-1, keepdims=True)
    e2 = jnp.exp(v2 - v1)
    w1 = 1.0 / (1.0 + e2)
    w2 = e2 / (1.0 + e2)
    idx_ref[...] = jnp.where(lane == 0, i1, jnp.where(lane == 1, i2, 0))
    wts_ref[...] = jnp.where(lane == 0, w1, jnp.where(lane == 1, w2, 0.0))


def _router(h, g, whi, wlo, br, chunk, n_chunks, tm=1024):
    t = h.shape[0] // n_chunks
    first = chunk * (t // tm)
    return pl.pallas_call(
        _router_kernel,
        grid=(t // tm,),
        in_specs=[pl.BlockSpec((tm, D_MODEL), lambda i: (first + i, 0)),
                  _const_spec((1, D_MODEL)),
                  _const_spec((D_MODEL, LANES)),
                  _const_spec((D_MODEL, LANES)),
                  _const_spec((1, LANES))],
        out_specs=[_slab_spec(tm, lambda i: (i, 0)),
                   pl.BlockSpec((tm, LANES), lambda i: (i, 0)),
                   pl.BlockSpec((tm, LANES), lambda i: (i, 0))],
        out_shape=[jax.ShapeDtypeStruct((t * ROW_SLABS, LANES), F32),
                   jax.ShapeDtypeStruct((t, LANES), jnp.int32),
                   jax.ShapeDtypeStruct((t, LANES), F32)],
        compiler_params=_params("parallel"),
        name="router",
    )(h, g, whi, wlo, br)


def _route_plan(idx, tm):
    t = idx.shape[0]
    n_pairs = 2 * t
    n_rows = n_pairs + N_EXPERTS * tm
    e_flat = jnp.concatenate([idx[:, 0], idx[:, 1]])
    onehot = (e_flat[:, None] == jnp.arange(N_EXPERTS, dtype=jnp.int32)[None, :]).astype(jnp.int32)
    csum = jnp.cumsum(onehot, axis=0)
    rank = jnp.sum(onehot * csum, axis=1) - 1
    counts = csum[-1]
    padded = ((counts + tm - 1) // tm) * tm
    ends = jnp.cumsum(padded)
    starts = ends - padded
    pos = starts[e_flat] + rank
    order = jnp.argsort(e_flat, stable=True).astype(jnp.int32)
    first = jnp.cumsum(counts) - counts
    r = jnp.minimum(jnp.arange(n_rows, dtype=jnp.int32), ends[-1] - 1)
    e_r = jnp.minimum(jnp.searchsorted(ends, r, side="right").astype(jnp.int32), N_EXPERTS - 1)
    local = r - starts[e_r]
    src_pair = order[jnp.clip(first[e_r] + local, 0, n_pairs - 1)]
    src_tok = jnp.where(local < counts[e_r], src_pair % t, 0).astype(jnp.int32)
    tile_expert = e_r[::tm]
    n_valid = (ends[-1] // tm).astype(jnp.int32).reshape(1)
    return pos.astype(jnp.int32), src_tok, tile_expert, n_valid


def _gather_rows(table, idx):
    n = idx.shape[0]
    info = plsc.get_sparse_core_info()
    n_workers = info.num_cores * info.num_subcores
    per_worker = n // n_workers
    assert per_worker * n_workers == n and per_worker % SC_CHUNK == 0, (n, n_workers)
    mesh = plsc.VectorSubcoreMesh(core_axis_name="c", subcore_axis_name="s")

    @functools.partial(
        pl.kernel, mesh=mesh,
        out_type=jax.ShapeDtypeStruct((n,) + table.shape[1:], table.dtype),
        scratch_types=[pltpu.VMEM((SC_CHUNK,), jnp.int32),
                       pltpu.VMEM((SC_CHUNK,) + table.shape[1:], table.dtype),
                       pltpu.SemaphoreType.DMA],
    )
    def gather(table_hbm, idx_hbm, out_hbm, idx_v, rows_v, sem):
        worker = lax.axis_index("s") * info.num_cores + lax.axis_index("c")
        base = worker * per_worker

        @pl.loop(0, per_worker // SC_CHUNK)
        def _(i):
            off = pl.multiple_of(base + i * SC_CHUNK, SC_CHUNK)
            pltpu.sync_copy(idx_hbm.at[pl.ds(off, SC_CHUNK)], idx_v)
            pltpu.async_copy(table_hbm.at[idx_v], rows_v, sem).wait()
            pltpu.sync_copy(rows_v, out_hbm.at[pl.ds(off, SC_CHUNK)])

    return gather(table, idx)


def _expert_ffn_kernel(te_ref, nv_ref, x_ref, w13_ref, w2_ref, y_ref):
    valid = pl.program_id(0) < nv_ref[0]

    @pl.when(valid)
    def _():
        _store_slabs(y_ref, _swiglu(_split_slabs(x_ref).astype(BF), w13_ref.at[0], w2_ref.at[0]))

    @pl.when(jnp.logical_not(valid))
    def _():
        y_ref[...] = jnp.zeros_like(y_ref)


def _expert_ffn(xs, tile_expert, n_valid, w13, w2, tm):
    n_rows = xs.shape[0] // ROW_SLABS
    last = lambda nv: jnp.maximum(nv[0] - 1, 0)
    grid_spec = pltpu.PrefetchScalarGridSpec(
        num_scalar_prefetch=2,
        grid=(n_rows // tm,),
        in_specs=[_slab_spec(tm, lambda i, te, nv: (jnp.minimum(i, last(nv)), 0)),
                  pl.BlockSpec((1, D_MODEL, 2 * D_FF), lambda i, te, nv: (te[i], 0, 0)),
                  pl.BlockSpec((1, D_FF, D_MODEL), lambda i, te, nv: (te[i], 0, 0))],
        out_specs=_slab_spec(tm, lambda i, te, nv: (i, 0)),
    )
    return pl.pallas_call(
        _expert_ffn_kernel,
        grid_spec=grid_spec,
        out_shape=jax.ShapeDtypeStruct((n_rows * ROW_SLABS, LANES), F32),
        compiler_params=_params("arbitrary", vmem=BIG_VMEM_LIMIT),
        name="expert_ffn",
    )(tile_expert, n_valid, xs, w13, w2)


def _combine_kernel(h_ref, y0_ref, y1_ref, wts_ref, gf_ref, *rest):
    out_ref = rest[-1]
    wts = wts_ref[...]
    tot = h_ref[...] + wts[:, 0:1] * _split_slabs(y0_ref) + wts[:, 1:2] * _split_slabs(y1_ref)
    out_ref[...] = _rms(tot, gf_ref[...])


def _combine(h, yg, wts, g_final, chunk, n_chunks, out_so_far, tm=1024):
    t = h.shape[0]
    nt = t // n_chunks // tm
    first = chunk * nt
    in_specs = [pl.BlockSpec((tm, D_MODEL), lambda i: (first + i, 0)),
                _slab_spec(tm, lambda i: (i, 0)),
                _slab_spec(tm, lambda i: (nt + i, 0)),
                pl.BlockSpec((tm, LANES), lambda i: (i, 0)),
                _const_spec((1, D_MODEL))]
    args = [h, yg, yg, wts, g_final]
    aliases = {}
    if out_so_far is not None:
        in_specs.append(pl.BlockSpec(memory_space=pl.ANY))
        args.append(out_so_far)
        aliases = {len(args) - 1: 0}
    return pl.pallas_call(
        _combine_kernel,
        grid=(nt,),
        in_specs=in_specs,
        out_specs=pl.BlockSpec((tm, D_MODEL), lambda i: (first + i, 0)),
        out_shape=jax.ShapeDtypeStruct((t, D_MODEL), F32),
        input_output_aliases=aliases,
        compiler_params=_params("parallel"),
        name="combine",
    )(*args)


def _moe(h, g, router_w, router_b, w13, w2, g_final, tm=512):
    wr = jnp.pad(router_w, ((0, 0), (0, LANES - N_EXPERTS)))
    whi = wr.astype(BF)
    wlo = (wr - whi.astype(F32)).astype(BF)
    br = jnp.pad(router_b, (0, LANES - N_EXPERTS)).reshape(1, LANES)
    as_rows = lambda a: a.reshape(-1, ROW_SLABS, LANES)
    as_slabs = lambda a: a.reshape(-1, LANES)
    chunks = range(MOE_CHUNKS)
    routed = [_router(h, g, whi, wlo, br, c, MOE_CHUNKS) for c in chunks]
    plans = [_route_plan(idx, tm) for _, idx, _ in routed]
    xs = [as_slabs(_gather_rows(as_rows(routed[c][0]), plans[c][1])) for c in chunks]
    ys = [_expert_ffn(xs[c], plans[c][2], plans[c][3], w13, w2, tm) for c in chunks]
    yg = [as_slabs(_gather_rows(as_rows(ys[c]), plans[c][0])) for c in chunks]
    out = None
    for c in chunks:
        out = _combine(h, yg[c], routed[c][2], g_final, c, MOE_CHUNKS, out)
    return out


def _feature_major(w, heads):
    wt = w.T.reshape(heads, HEAD_DIM, D_MODEL)
    return jnp.pad(wt, ((0, 0), (0, VT_ROWS - HEAD_DIM), (0, 0))).reshape(heads * VT_ROWS, D_MODEL)


def _pack_w_in(w_in):
    cuts = np.cumsum((512, 512, 512, 512, 512, 128, 128, 512, 512, 512, 8))[:-1].tolist()
    a_u, a_v, b_x, b_y, c_q, c_k, c_v, d_q, d_k, d_v, d_f = jnp.split(w_in, cuts, axis=-1)
    w = jnp.concatenate([b_x, b_y, d_q, d_k, c_k, c_v], axis=-1).astype(BF)
    wa = jnp.concatenate([a_u, a_v], axis=-1).astype(BF)
    wf = jnp.pad(d_f, ((0, 0), (0, LANES - FOX_HEADS))).astype(BF)
    wt = jnp.concatenate([_feature_major(d_v, FOX_HEADS), c_q.T, _feature_major(c_v, SWA_KV)], axis=0).astype(BF)
    return w, wa, wf, wt


def _fm_ones():
    ones = np.zeros((FM_ROWS, LANES), np.float32)
    for base, heads in ((0, FOX_HEADS), (VT_ALL + BRANCH_W, SWA_KV)):
        for h in range(heads):
            ones[base + h * VT_ROWS + HEAD_DIM, :] = 1.0
    return jnp.asarray(ones)


def _row(v):
    return v.reshape(1, -1)


def _hybrid_mixer(h, bsz, norm_mix, w_in, sgu_g, sgu_w, sgu_b, conv_w, conv_b, rg_wa, rg_ba, rg_wx, rg_bx,
                  rg_lambda, swa_sinks, fox_bf, w_branch, w_gate, b_gate, w_out):
    w, wa, wf, wt = _pack_w_in(w_in)
    proj, f, o_a, vt, sqt, svt = _inproj(h, _row(norm_mix), w, wa, wf, wt, _fm_ones(), _row(sgu_g), sgu_w, sgu_b.T)
    wax = jnp.concatenate([rg_wa, rg_wx], axis=-1).astype(BF)
    o_b = _rglru(proj, bsz, conv_w, _row(conv_b), wax, _row(rg_ba), _row(rg_bx), _row(rg_lambda))
    o_c = _swa(proj, sqt, svt, bsz, swa_sinks)
    bf_pad = jnp.pad(fox_bf, (0, LANES - FOX_HEADS)).reshape(1, LANES)
    qp, kp = _fox_prep(proj, f, bsz, bf_pad)
    o_d = _fox(qp, kp, vt, bsz)
    return _merge(h, _row(norm_mix), o_a, o_b, o_c, o_d, w_gate.astype(BF), b_gate[:, None, :],
                  w_branch.astype(BF), w_out.astype(BF))


def kernel(x, mem, norm_mix, w_in, sgu_g, sgu_w, sgu_b, conv_w, conv_b, rg_wa, rg_ba, rg_wx, rg_bx, rg_lambda, swa_sinks, fox_bf, w_branch, w_gate, b_gate, w_out, norm_cross, norm_mem, wq_c, wkv_c, wo_c, norm_ffn, dense_w13, dense_w2, router_w, router_b, moe_w13, moe_w2, norm_final):
    bsz, seq, d = x.shape
    m_len = mem.shape[1]
    depth = norm_mix.shape[0]
    assert depth == 2, "the final RMSNorm is fused into the routed layer, which must be the last one"
    h = x.reshape(bsz * seq, d)
    mem2 = mem.reshape(bsz * m_len, d)
    for l in range(depth):
        h = _hybrid_mixer(h, bsz, norm_mix[l], w_in[l], sgu_g[l], sgu_w[l], sgu_b[l], conv_w[l], conv_b[l],
                          rg_wa[l], rg_ba[l], rg_wx[l], rg_bx[l], rg_lambda[l], swa_sinks[l], fox_bf[l],
                          w_branch[l], w_gate[l], b_gate[l], w_out[l])
        kv = _memkv(mem2, _row(norm_mem[l]), wkv_c[l].astype(BF), m_len)
        h = _cross(h, _row(norm_cross[l]), wq_c[l].astype(BF), kv, wo_c[l].astype(BF), bsz, m_len)
        if l % 2 == 0:
            h = _ffn(h, _row(norm_ffn[l]), dense_w13[l // 2].astype(BF), dense_w2[l // 2].astype(BF))
        else:
            h = _moe(h, _row(norm_ffn[l]), router_w[l // 2], router_b[l // 2], moe_w13[l // 2].astype(BF),
                     moe_w2[l // 2].astype(BF), _row(norm_final))
    return h.reshape(bsz, seq, d)
```

```python
import functools
import math

import numpy as np
import jax
import jax.numpy as jnp
from jax import lax
from jax.experimental import pallas as pl
from jax.experimental.pallas import tpu as pltpu
from jax.experimental.pallas import tpu_sc as plsc

F32 = jnp.float32
BF = jnp.bfloat16

D_MODEL = 1024
BRANCH_W = 512
HEAD_DIM = 64
CHUNK = 128
SGU_GROUPS = 4
LRU_HEADS = 4
LRU_C = 8.0
CONV_W = 4
SWA_HEADS = 8
SWA_KV = 2
FOX_HEADS = 8
FOX_PAIRS = FOX_HEADS // 2
X_HEADS = 4
X_HEAD_DIM = 128
D_FF = 2816
N_EXPERTS = 8
EPS = 1e-6
LOG2E = math.log2(math.e)
NEG = -1e30
LANES = 128
ROW_SLABS = D_MODEL // LANES
SC_CHUNK = 64
MOE_CHUNKS = 2
VMEM_LIMIT = 56 * 1024 * 1024
BIG_VMEM_LIMIT = 62 * 1024 * 1024

COL_BX, COL_BY, COL_DQ, COL_DK = range(4)
N_WIDE = 4
N_PROJ = N_WIDE * BRANCH_W + 2 * SWA_KV * HEAD_DIM
VT_ROWS = 80
VT_ALL = FOX_HEADS * VT_ROWS
SWA_VT = SWA_KV * VT_ROWS
FM_ROWS = VT_ALL + BRANCH_W + SWA_VT
FOX_UNROLL = 4
FOX_STEP_PAIRS = 1
FF_SPLITS = ((0, 1536), (1536, D_FF))


def _rms(x, g):
    return x * lax.rsqrt(jnp.mean(x * x, axis=-1, keepdims=True) + EPS) * g


def _sigmoid(x):
    return 1.0 / (1.0 + jnp.exp(-x))


def _gelu(x):
    return 0.5 * x * (1.0 + jnp.tanh(math.sqrt(2.0 / math.pi) * (x + 0.044715 * (x * x * x))))


def _dot(a, b):
    return jnp.dot(a, b, preferred_element_type=F32)


def _dot_nt(a, b):
    return lax.dot_general(a, b, (((1,), (1,)), ((), ())), preferred_element_type=F32)


def _shift_rows(x, d, fill):
    row = lax.broadcasted_iota(jnp.int32, x.shape, 0)
    return jnp.where(row >= d, pltpu.roll(x, d, 0), fill)


def _params(*sem, vmem=VMEM_LIMIT):
    return pltpu.CompilerParams(dimension_semantics=sem, vmem_limit_bytes=vmem)


def _const_spec(shape):
    nd = len(shape)
    return pl.BlockSpec(shape, lambda *_: (0,) * nd, pipeline_mode=pl.Buffered(1))


def _inproj_kernel(h_ref, g_ref, w_ref, wa_ref, wf_ref, wt_ref, ones_ref, sg_ref, sw_ref, sbt_ref,
                   proj_ref, f_ref, oa_ref, vt_ref, sqt_ref, svt_ref):
    xn = _rms(h_ref[...], g_ref[...]).astype(BF)
    tm = xn.shape[0]
    n_col = w_ref.shape[1]
    for c in range(0, n_col, BRANCH_W):
        w = min(BRANCH_W, n_col - c)
        proj_ref[:, c:c + w] = _dot(xn, w_ref[:, c:c + w]).astype(BF)
    f_ref[...] = _dot(xn, wf_ref[...])
    ones = jnp.concatenate([ones_ref[...]] * (tm // LANES), axis=1)
    fm = (_dot_nt(wt_ref[...], xn) + ones).astype(BF)
    vt_ref[...] = fm[:VT_ALL]
    sqt_ref[...] = fm[VT_ALL:VT_ALL + BRANCH_W]
    svt_ref[...] = fm[VT_ALL + BRANCH_W:]

    u = _gelu(_dot(xn, wa_ref[:, :BRANCH_W]))
    v = _gelu(_dot(xn, wa_ref[:, BRANCH_W:]))
    vn = _rms(v, sg_ref[...]).astype(BF)
    row = lax.broadcasted_iota(jnp.int32, (CHUNK, CHUNK), 0)
    col = lax.broadcasted_iota(jnp.int32, (CHUNK, CHUNK), 1)
    gw = BRANCH_W // SGU_GROUPS
    for g in range(SGU_GROUPS):
        wg = jnp.where(col <= row, sw_ref[g], 0.0).astype(BF)
        bg = sbt_ref[:, g:g + 1]
        for c in range(tm // CHUNK):
            rs = slice(c * CHUNK, (c + 1) * CHUNK)
            cs = slice(g * gw, (g + 1) * gw)
            mixed = _dot(wg, vn[rs, cs]) + bg
            oa_ref[rs, cs] = (u[rs, cs] * mixed).astype(BF)


def _inproj(h, g, w, wa, wf, wt, ones, sgu_g, sgu_w, sgu_bt, tm=1024):
    t = h.shape[0]
    row = lambda i: (i, 0)
    col = lambda i: (0, i)
    return pl.pallas_call(
        _inproj_kernel,
        grid=(t // tm,),
        in_specs=[pl.BlockSpec((tm, D_MODEL), row),
                  _const_spec((1, D_MODEL)),
                  _const_spec((D_MODEL, N_PROJ)),
                  _const_spec((D_MODEL, 2 * BRANCH_W)),
                  _const_spec((D_MODEL, LANES)),
                  _const_spec((FM_ROWS, D_MODEL)),
                  _const_spec((FM_ROWS, LANES)),
                  _const_spec((1, BRANCH_W)),
                  _const_spec((SGU_GROUPS, CHUNK, CHUNK)),
                  _const_spec((CHUNK, SGU_GROUPS))],
        out_specs=[pl.BlockSpec((tm, N_PROJ), row),
                   pl.BlockSpec((tm, LANES), row),
                   pl.BlockSpec((tm, BRANCH_W), row),
                   pl.BlockSpec((VT_ALL, tm), col),
                   pl.BlockSpec((BRANCH_W, tm), col),
                   pl.BlockSpec((SWA_VT, tm), col)],
        out_shape=[jax.ShapeDtypeStruct((t, N_PROJ), BF), jax.ShapeDtypeStruct((t, LANES), F32),
                   jax.ShapeDtypeStruct((t, BRANCH_W), BF), jax.ShapeDtypeStruct((VT_ALL, t), BF),
                   jax.ShapeDtypeStruct((BRANCH_W, t), BF), jax.ShapeDtypeStruct((SWA_VT, t), BF)],
        compiler_params=_params("parallel", vmem=BIG_VMEM_LIMIT),
        name="inproj",
    )(h, g, w, wa, wf, wt, ones, sgu_g, sgu_w, sgu_bt)


def _rglru_kernel(x_ref, y_ref, cw_ref, cb_ref, wax_ref, ba_ref, bx_ref, lam_ref, o_ref, xs_ref, hc_ref):
    ts = x_ref.shape[0]
    hw = BRANCH_W // LRU_HEADS

    @pl.when(pl.program_id(1) == 0)
    def _():
        xs_ref[0:8, :] = jnp.zeros((8, BRANCH_W), F32)
        hc_ref[...] = jnp.zeros_like(hc_ref)

    x = x_ref[...].astype(F32)
    xs_ref[8:8 + ts, :] = x
    cw = cw_ref[...]
    xc = cb_ref[...] + cw[CONV_W - 1:CONV_W] * x
    for k in range(1, CONV_W):
        xc = xc + cw[CONV_W - 1 - k:CONV_W - k] * xs_ref[8 - k:8 - k + ts, :]
    xs_ref[0:8, :] = x[ts - 8:ts, :]

    xcb = xc.astype(BF)
    r_parts, i_parts = [], []
    for hd in range(LRU_HEADS):
        z = _dot(xcb[:, hd * hw:(hd + 1) * hw], wax_ref[hd])
        r_parts.append(z[:, :hw])
        i_parts.append(z[:, hw:])
    r = _sigmoid(jnp.concatenate(r_parts, axis=1) + ba_ref[...])
    gi = _sigmoid(jnp.concatenate(i_parts, axis=1) + bx_ref[...])
    nl = -lam_ref[...]
    softplus = jnp.maximum(nl, 0.0) + jnp.log1p(jnp.exp(-jnp.abs(nl)))
    log_a = (-LRU_C * r) * softplus
    a = jnp.exp(log_a)
    th = jnp.tanh(log_a)
    e2 = -2.0 * th / (1.0 - th)
    b = (xc * gi) * (e2 * lax.rsqrt(jnp.maximum(e2, 1e-30)))

    sub = lax.broadcasted_iota(jnp.int32, a.shape, 0) & 7
    for d in (1, 2, 4):
        keep = sub >= d
        a_sh = jnp.where(keep, pltpu.roll(a, d, 0), 1.0)
        b_sh = jnp.where(keep, pltpu.roll(b, d, 0), 0.0)
        b = b + a * b_sh
        a = a * a_sh
    carry = hc_ref[...]
    groups = []
    for g in range(ts // 8):
        hg = a[8 * g:8 * g + 8] * carry + b[8 * g:8 * g + 8]
        groups.append(hg)
        carry = hg[7:8]
    h = jnp.concatenate(groups, axis=0)
    hc_ref[...] = carry
    o_ref[...] = (h * _gelu(y_ref[...].astype(F32))).astype(BF)


def _rglru(proj, bsz, cw, cb, wax, ba, bx, lam, ts=512):
    t = proj.shape[0]
    ns = t // bsz // ts
    return pl.pallas_call(
        _rglru_kernel,
        grid=(bsz, ns),
        in_specs=[pl.BlockSpec((ts, BRANCH_W), lambda b, j: (b * ns + j, COL_BX)),
                  pl.BlockSpec((ts, BRANCH_W), lambda b, j: (b * ns + j, COL_BY)),
                  _const_spec((CONV_W, BRANCH_W)),
                  _const_spec((1, BRANCH_W)),
                  _const_spec((LRU_HEADS, BRANCH_W // LRU_HEADS, 2 * BRANCH_W // LRU_HEADS)),
                  _const_spec((1, BRANCH_W)),
                  _const_spec((1, BRANCH_W)),
                  _const_spec((1, BRANCH_W))],
        out_specs=pl.BlockSpec((ts, BRANCH_W), lambda b, j: (b * ns + j, 0)),
        out_shape=jax.ShapeDtypeStruct((t, BRANCH_W), BF),
        scratch_shapes=[pltpu.VMEM((ts + 8, BRANCH_W), F32), pltpu.VMEM((1, BRANCH_W), F32)],
        compiler_params=_params("parallel", "arbitrary"),
        name="rglru",
    )(proj, proj, cw, cb, wax, ba, bx, lam)


def _swa_tables(sinks):
    t_idx = np.arange(CHUNK)[None, :] + CHUNK
    dist = (t_idx - np.arange(2 * CHUNK)[:, None]).astype(np.float32)
    in_win = (dist >= 0) & (dist < CHUNK)
    slopes = 2.0 ** (-(8.0 / SWA_HEADS) * np.arange(1, SWA_HEADS + 1, dtype=np.float32))
    bias = np.where(in_win[None], -slopes[:, None, None] * dist[None] * LOG2E, NEG)
    grp = SWA_HEADS // SWA_KV
    bias = bias.reshape(SWA_KV, grp, 2 * CHUNK, CHUNK).transpose(0, 2, 1, 3).reshape(SWA_KV, 2 * CHUNK, grp * CHUNK)
    sink = jnp.repeat(sinks.astype(F32) * LOG2E, CHUNK).reshape(SWA_KV, 1, grp * CHUNK)
    return jnp.asarray(bias, F32), sink


def _swa_kernel(qt_ref, kv_ref, kvp_ref, vt_ref, vtp_ref, bias_ref, sink_ref, o_ref):
    tq = kv_ref.shape[0]
    grp = SWA_HEADS // SWA_KV
    first_key = jnp.where(pl.program_id(1) == 0, CHUNK, 0)
    k_all = jnp.concatenate([kvp_ref[...], kv_ref[...]], axis=0)
    vt_all = jnp.concatenate([vtp_ref[...], vt_ref[...]], axis=1)
    qt = (qt_ref[...].astype(F32) * (HEAD_DIM ** -0.5 * LOG2E)).astype(BF)
    key_row = lax.broadcasted_iota(jnp.int32, (2 * CHUNK, grp * CHUNK), 0)
    kw = 2 * SWA_KV * HEAD_DIM
    for qb in range(tq // CHUNK):
        cols = slice(qb * CHUNK, (qb + 1) * CHUNK)
        kb = k_all[qb * CHUNK:(qb + 2) * CHUNK]
        vtb = vt_all[:, qb * CHUNK:(qb + 2) * CHUNK]
        outs = []
        for kh in range(SWA_KV):
            qg = jnp.concatenate([qt[(kh * grp + g) * HEAD_DIM:(kh * grp + g + 1) * HEAD_DIM, cols]
                                  for g in range(grp)], axis=1)
            parts = []
            if kh > 0:
                parts.append(jnp.zeros((kh * HEAD_DIM, grp * CHUNK), BF))
            parts.append(qg)
            parts.append(jnp.zeros((kw - (kh + 1) * HEAD_DIM, grp * CHUNK), BF))
            st = _dot(kb, jnp.concatenate(parts, axis=0)) + bias_ref[kh]
            if qb == 0:
                st = jnp.where(key_row >= first_key, st, NEG)
            sink = sink_ref[kh]
            m = jnp.maximum(jnp.max(st, axis=0, keepdims=True), sink)
            pt = jnp.exp2(st - m).astype(BF)
            acc = _dot(vtb[kh * VT_ROWS:(kh + 1) * VT_ROWS], pt)
            ot = acc[:HEAD_DIM] / (acc[HEAD_DIM:HEAD_DIM + 1] + jnp.exp2(sink - m))
            outs.extend(ot[:, g * CHUNK:(g + 1) * CHUNK] for g in range(grp))
        o_ref[cols, :] = jnp.concatenate(outs, axis=0).T.astype(BF)


def _swa(proj, sqt, svt, bsz, sinks, tq=1024):
    t = proj.shape[0]
    seq = t // bsz
    nq = seq // tq
    per = tq // CHUNK
    kv_blk = N_WIDE * BRANCH_W // (2 * SWA_KV * HEAD_DIM)
    prev = lambda b, i: b * (seq // CHUNK) + jnp.maximum(i * per - 1, 0)
    bias, sink = _swa_tables(sinks)
    grp = SWA_HEADS // SWA_KV
    return pl.pallas_call(
        _swa_kernel,
        grid=(bsz, nq),
        in_specs=[pl.BlockSpec((BRANCH_W, tq), lambda b, i: (0, b * nq + i)),
                  pl.BlockSpec((tq, 2 * SWA_KV * HEAD_DIM), lambda b, i: (b * nq + i, kv_blk)),
                  pl.BlockSpec((CHUNK, 2 * SWA_KV * HEAD_DIM), lambda b, i: (prev(b, i), kv_blk)),
                  pl.BlockSpec((SWA_VT, tq), lambda b, i: (0, b * nq + i)),
                  pl.BlockSpec((SWA_VT, CHUNK), lambda b, i: (0, prev(b, i))),
                  _const_spec((SWA_KV, 2 * CHUNK, grp * CHUNK)),
                  _const_spec((SWA_KV, 1, grp * CHUNK))],
        out_specs=pl.BlockSpec((tq, BRANCH_W), lambda b, i: (b * nq + i, 0)),
        out_shape=jax.ShapeDtypeStruct((t, BRANCH_W), BF),
        compiler_params=_params("parallel", "parallel"),
        name="swa",
    )(sqt, proj, proj, svt, svt, bias, sink)


def _aug_tables():
    eq = np.zeros((3, LANES, FOX_PAIRS * LANES), np.float32)
    ek = np.zeros((3, LANES, FOX_PAIRS * LANES), np.float32)
    oq = np.zeros((1, FOX_PAIRS * LANES), np.float32)
    ok = np.zeros((1, FOX_PAIRS * LANES), np.float32)
    for h in range(FOX_HEADS):
        base = (h // 2) * LANES + 6 * (h % 2)
        for s in range(3):
            eq[s, h, base + s] = 1.0
            ek[s, h, base + 3 + s] = -1.0
            ok[0, base + s] = 1.0
            oq[0, base + 3 + s] = 1.0
    return eq, ek, oq, ok


def _fox_prep_kernel(q_ref, k_ref, f_ref, bf_ref, eq_ref, ek_ref, oq_ref, ok_ref, qp_ref, kp_ref, cum_ref):
    ts = q_ref.shape[0]

    @pl.when(pl.program_id(1) == 0)
    def _():
        cum_ref[...] = jnp.zeros_like(cum_ref)

    z = f_ref[...] + bf_ref[...]
    c = jnp.minimum(z, 0.0) - jnp.log1p(jnp.exp(-jnp.abs(z)))
    d = 1
    while d < ts:
        c = c + _shift_rows(c, d, 0.0)
        d *= 2
    c = c + cum_ref[...]
    cum_ref[...] = c[ts - 1:ts, :]
    c = c * LOG2E
    c1 = c.astype(BF)
    r1 = c - c1.astype(F32)
    c2 = r1.astype(BF)
    c3 = (r1 - c2.astype(F32)).astype(BF)
    augq = _dot(c1, eq_ref[0]) + _dot(c2, eq_ref[1]) + _dot(c3, eq_ref[2]) + oq_ref[...]
    augk = _dot(c1, ek_ref[0]) + _dot(c2, ek_ref[1]) + _dot(c3, ek_ref[2]) + ok_ref[...]
    qt = (q_ref[...].astype(F32) * (HEAD_DIM ** -0.5 * LOG2E)).T.astype(BF)
    augqt = augq.T.astype(BF)
    k = k_ref[...]
    for p in range(FOX_PAIRS):
        qp_ref[2 * p * LANES:(2 * p + 1) * LANES, :] = qt[p * LANES:(p + 1) * LANES]
        qp_ref[(2 * p + 1) * LANES:(2 * p + 2) * LANES, :] = augqt[p * LANES:(p + 1) * LANES]
        kp_ref[:, 2 * p * LANES:(2 * p + 1) * LANES] = k[:, p * LANES:(p + 1) * LANES]
        kp_ref[:, (2 * p + 1) * LANES:(2 * p + 2) * LANES] = augk[:, p * LANES:(p + 1) * LANES].astype(BF)


def _fox_prep(proj, f, bsz, bf_pad, ts=1024):
    t = proj.shape[0]
    ns = t // bsz // ts
    eq, ek, oq, ok = _aug_tables()
    wide = 2 * FOX_PAIRS * LANES
    return pl.pallas_call(
        _fox_prep_kernel,
        grid=(bsz, ns),
        in_specs=[pl.BlockSpec((ts, BRANCH_W), lambda b, j: (b * ns + j, COL_DQ)),
                  pl.BlockSpec((ts, BRANCH_W), lambda b, j: (b * ns + j, COL_DK)),
                  pl.BlockSpec((ts, LANES), lambda b, j: (b * ns + j, 0)),
                  _const_spec((1, LANES)),
                  _const_spec(eq.shape), _const_spec(ek.shape), _const_spec(oq.shape), _const_spec(ok.shape)],
        out_specs=[pl.BlockSpec((wide, ts), lambda b, j: (0, b * ns + j)),
                   pl.BlockSpec((ts, wide), lambda b, j: (b * ns + j, 0))],
        out_shape=[jax.ShapeDtypeStruct((wide, t), BF), jax.ShapeDtypeStruct((t, wide), BF)],
        scratch_shapes=[pltpu.VMEM((1, LANES), F32)],
        compiler_params=_params("parallel", "arbitrary"),
        name="fox_prep",
    )(proj, proj, f, bf_pad, jnp.asarray(eq, BF), jnp.asarray(ek, BF), jnp.asarray(oq), jnp.asarray(ok))


def _fox_kernel(q_ref, k_ref, vt_ref, o_ref, st_ref, pt_ref, al_ref, m_ref, acc_ref):
    tq = q_ref.shape[1]
    tk = tq // 2
    n_heads = m_ref.shape[0]
    qi = pl.program_id(2)
    row = lax.broadcasted_iota(jnp.int32, (2 * LANES, tq), 0)
    in0 = (row < HEAD_DIM) | ((row >= LANES) & (row < LANES + 6))
    in1 = ((row >= HEAD_DIM) & (row < LANES)) | ((row >= LANES + 6) & (row < LANES + 12))
    qh = []
    for p in range(n_heads // 2):
        q = q_ref[2 * p * LANES:(2 * p + 2) * LANES, :]
        zero = jnp.zeros_like(q)
        qh += [jnp.where(in0, q, zero), jnp.where(in1, q, zero)]
    for h in range(n_heads):
        m_ref[h] = jnp.full(m_ref.shape[1:], NEG, F32)
        acc_ref[h] = jnp.zeros(acc_ref.shape[1:], F32)
        pt_ref[1, h] = jnp.zeros(pt_ref.shape[2:], BF)
        al_ref[1, h] = jnp.ones(al_ref.shape[2:], F32)

    def scores(t, slot, lo=0):
        rows = pl.ds(pl.multiple_of(t * tk, tk), tk)
        for p in range(n_heads // 2):
            kp = k_ref[rows, 2 * p * LANES:(2 * p + 2) * LANES]
            for h in (2 * p, 2 * p + 1):
                st_ref[slot, h, :, lo:] = _dot(kp, qh[h][:, lo:])

    def numerators(slot, key_offset, lo=0):
        for h in range(n_heads):
            st = st_ref[slot, h, :, lo:]
            if key_offset is not None:
                key = lax.broadcasted_iota(jnp.int32, st.shape, 0) + key_offset
                qry = lax.broadcasted_iota(jnp.int32, st.shape, 1) + lo
                st = jnp.where(key <= qry, st, NEG)
            m_old = m_ref[h, :, lo:]
            m_new = jnp.maximum(m_old, jnp.max(st, axis=0, keepdims=True))
            al_ref[slot, h, :, lo:] = jnp.exp2(m_old - m_new)
            pt_ref[slot, h, :, lo:] = jnp.exp2(st - m_new).astype(BF)
            m_ref[h, :, lo:] = m_new

    def accumulate(t, slot, lo=0):
        cols = pl.ds(pl.multiple_of(t * tk, tk), tk)
        for h in range(n_heads):
            vt = vt_ref[h * VT_ROWS:(h + 1) * VT_ROWS, cols]
            acc_ref[h, :, lo:] = al_ref[slot, h, :, lo:] * acc_ref[h, :, lo:] + _dot(vt, pt_ref[slot, h, :, lo:])

    def tile_pair(j, diagonal):
        lo = tk if diagonal else 0
        scores(2 * j + 1, 1, lo)
        accumulate(jnp.maximum(2 * j - 1, 0), 1)
        numerators(0, 0 if diagonal else None)
        if not diagonal:
            scores(2 * j + 2, 0)
        accumulate(2 * j, 0)
        numerators(1, tk if diagonal else None, lo)

    def body(jj, carry):
        for u in range(FOX_UNROLL):
            tile_pair(FOX_UNROLL * jj + u, False)
        return carry

    def remainder(j, carry):
        tile_pair(j, False)
        return carry

    scores(0, 0)
    lax.fori_loop(0, qi // FOX_UNROLL, body, 0)
    lax.fori_loop((qi // FOX_UNROLL) * FOX_UNROLL, qi, remainder, 0)
    tile_pair(qi, True)
    accumulate(2 * qi + 1, 1, tk)
    outs = []
    for h in range(n_heads):
        acc = acc_ref[h]
        outs.append(acc[:HEAD_DIM] / acc[HEAD_DIM:HEAD_DIM + 1])
    o_ref[...] = jnp.concatenate(outs, axis=0).T.astype(BF)


def _fox(qp, kp, vt, bsz, tq=512):
    t = kp.shape[0]
    seq = t // bsz
    nq = seq // tq
    np_ = FOX_STEP_PAIRS
    nh = 2 * np_
    return pl.pallas_call(
        _fox_kernel,
        grid=(bsz, FOX_PAIRS // np_, nq),
        in_specs=[pl.BlockSpec((np_ * 2 * LANES, tq), lambda b, p, i: (p, b * nq + i)),
                  pl.BlockSpec((seq, np_ * 2 * LANES), lambda b, p, i: (b, p)),
                  pl.BlockSpec((nh * VT_ROWS, seq), lambda b, p, i: (p, b))],
        out_specs=pl.BlockSpec((tq, np_ * LANES), lambda b, p, i: (b * nq + i, p)),
        out_shape=jax.ShapeDtypeStruct((t, BRANCH_W), BF),
        scratch_shapes=[pltpu.VMEM((2, nh, tq // 2, tq), F32),
                        pltpu.VMEM((2, nh, tq // 2, tq), BF),
                        pltpu.VMEM((2, nh, 1, tq), F32),
                        pltpu.VMEM((nh, 1, tq), F32),
                        pltpu.VMEM((nh, VT_ROWS, tq), F32)],
        compiler_params=_params("parallel", "parallel", "arbitrary"),
        name="fox",
    )(qp, kp, vt)


def _merge_kernel(h_ref, g_ref, oa_ref, ob_ref, oc_ref, od_ref, wg_ref, bg_ref, wb_ref, wo_ref, out_ref):
    h = h_ref[...]
    xn = _rms(h, g_ref[...]).astype(BF)
    merged = None
    for br, o_ref in enumerate((oa_ref, ob_ref, oc_ref, od_ref)):
        gate = _sigmoid(_dot(xn, wg_ref[br]) + bg_ref[br])
        term = gate * _dot(o_ref[...], wb_ref[br])
        merged = term if merged is None else merged + term
    out_ref[...] = h + _dot(merged.astype(BF), wo_ref[...])


def _merge(h, g, oa, ob, oc, od, wg, bg, wb, wo, tm=1024):
    t = h.shape[0]
    row = lambda i: (i, 0)
    return pl.pallas_call(
        _merge_kernel,
        grid=(t // tm,),
        in_specs=[pl.BlockSpec((tm, D_MODEL), row),
                  _const_spec((1, D_MODEL)),
                  pl.BlockSpec((tm, BRANCH_W), row), pl.BlockSpec((tm, BRANCH_W), row),
                  pl.BlockSpec((tm, BRANCH_W), row), pl.BlockSpec((tm, BRANCH_W), row),
                  _const_spec((4, D_MODEL, D_MODEL)),
                  _const_spec((4, 1, D_MODEL)),
                  _const_spec((4, BRANCH_W, D_MODEL)),
                  _const_spec((D_MODEL, D_MODEL))],
        out_specs=pl.BlockSpec((tm, D_MODEL), row),
        out_shape=jax.ShapeDtypeStruct((t, D_MODEL), F32),
        compiler_params=_params("parallel", vmem=BIG_VMEM_LIMIT),
        name="merge",
    )(h, g, oa, ob, oc, od, wg, bg, wb, wo)


def _memkv_kernel(mem_ref, g_ref, w_ref, kv_ref):
    mn = _rms(mem_ref[...], g_ref[...]).astype(BF)
    kv_ref[...] = _dot(mn, w_ref[...]).astype(BF)


def _memkv(mem2, g, w, m_len):
    n = mem2.shape[0]
    width = 2 * X_HEADS * X_HEAD_DIM
    return pl.pallas_call(
        _memkv_kernel,
        grid=(n // m_len,),
        in_specs=[pl.BlockSpec((m_len, D_MODEL), lambda b: (b, 0)),
                  _const_spec((1, D_MODEL)),
                  _const_spec((D_MODEL, width))],
        out_specs=pl.BlockSpec((m_len, width), lambda b: (b, 0)),
        out_shape=jax.ShapeDtypeStruct((n, width), BF),
        compiler_params=_params("parallel"),
        name="memkv",
    )(mem2, g, w)


def _cross_kernel(h_ref, g_ref, wq_ref, kv_ref, wo_ref, out_ref):
    h = h_ref[...]
    hn = _rms(h, g_ref[...]).astype(BF)
    q = _dot(hn, wq_ref[...]).astype(BF)
    kv = kv_ref[...]
    width = X_HEADS * X_HEAD_DIM
    outs = []
    for hd in range(X_HEADS):
        cs = slice(hd * X_HEAD_DIM, (hd + 1) * X_HEAD_DIM)
        s = _dot_nt(q[:, cs], kv[:, cs]) * (X_HEAD_DIM ** -0.5)
        m = jnp.max(s, axis=-1, keepdims=True)
        p = jnp.exp(s - m)
        denom = jnp.sum(p, axis=-1, keepdims=True)
        v = kv[:, width + hd * X_HEAD_DIM:width + (hd + 1) * X_HEAD_DIM]
        outs.append((_dot(p.astype(BF), v) / denom).astype(BF))
    o = jnp.concatenate(outs, axis=1)
    out_ref[...] = h + _dot(o, wo_ref[...])


def _cross(h, g, wq, kv, wo, bsz, m_len, tm=1024):
    t = h.shape[0]
    per = t // bsz // tm
    width = X_HEADS * X_HEAD_DIM
    return pl.pallas_call(
        _cross_kernel,
        grid=(bsz, per),
        in_specs=[pl.BlockSpec((tm, D_MODEL), lambda b, i: (b * per + i, 0)),
                  _const_spec((1, D_MODEL)),
                  _const_spec((D_MODEL, width)),
                  pl.BlockSpec((m_len, 2 * width), lambda b, i: (b, 0)),
                  _const_spec((width, D_MODEL))],
        out_specs=pl.BlockSpec((tm, D_MODEL), lambda b, i: (b * per + i, 0)),
        out_shape=jax.ShapeDtypeStruct((t, D_MODEL), F32),
        compiler_params=_params("parallel", "parallel"),
        name="cross",
    )(h, g, wq, kv, wo)


def _swiglu(xb, w13_ref, w2_ref):
    out = None
    for lo, hi in FF_SPLITS:
        gate = _dot(xb, w13_ref[:, lo:hi])
        up = _dot(xb, w13_ref[:, D_FF + lo:D_FF + hi])
        act = (gate * _sigmoid(gate) * up).astype(BF)
        part = _dot(act, w2_ref[lo:hi, :])
        out = part if out is None else out + part
    return out


def _ffn_kernel(h_ref, g_ref, w13_ref, w2_ref, out_ref):
    h = h_ref[...]
    out_ref[...] = h + _swiglu(_rms(h, g_ref[...]).astype(BF), w13_ref, w2_ref)


def _ffn(h, g, w13, w2, tm=1024):
    t = h.shape[0]
    return pl.pallas_call(
        _ffn_kernel,
        grid=(t // tm,),
        in_specs=[pl.BlockSpec((tm, D_MODEL), lambda i: (i, 0)),
                  _const_spec((1, D_MODEL)),
                  _const_spec((D_MODEL, 2 * D_FF)),
                  _const_spec((D_FF, D_MODEL))],
        out_specs=pl.BlockSpec((tm, D_MODEL), lambda i: (i, 0)),
        out_shape=jax.ShapeDtypeStruct((t, D_MODEL), F32),
        compiler_params=_params("parallel", vmem=BIG_VMEM_LIMIT),
        name="ffn",
    )(h, g, w13, w2)


def _split_slabs(ref):
    rows = ref.shape[0] // ROW_SLABS
    return jnp.concatenate([ref[pl.ds(c, rows, stride=ROW_SLABS), :] for c in range(ROW_SLABS)], axis=1)


def _store_slabs(ref, x):
    rows = ref.shape[0] // ROW_SLABS
    for c in range(ROW_SLABS):
        ref[pl.ds(c, rows, stride=ROW_SLABS), :] = x[:, c * LANES:(c + 1) * LANES]


def _slab_spec(tm, index_map):
    return pl.BlockSpec((tm * ROW_SLABS, LANES), index_map)


def _router_kernel(h_ref, g_ref, whi_ref, wlo_ref, br_ref, hn_ref, idx_ref, wts_ref):
    hn = _rms(h_ref[...], g_ref[...])
    _store_slabs(hn_ref, hn)
    hi = hn.astype(BF)
    lo = (hn - hi.astype(F32)).astype(BF)
    logits = _dot(hi, whi_ref[...]) + (_dot(lo, whi_ref[...]) + _dot(hi, wlo_ref[...])) + br_ref[...]
    lane = lax.broadcasted_iota(jnp.int32, logits.shape, 1)
    logits = jnp.where(lane < N_EXPERTS, logits, NEG)
    v1 = jnp.max(logits, axis=-1, keepdims=True)
    i1 = jnp.min(jnp.where(logits == v1, lane, LANES), axis=-1, keepdims=True)
    rest = jnp.where(lane == i1, NEG, logits)
    v2 = jnp.max(rest, axis=-1, keepdims=True)
    i2 = jnp.min(jnp.where(rest == v2, lane, LANES), axis=-1, keepdims=True)
    e2 = jnp.exp(v2 - v1)
    w1 = 1.0 / (1.0 + e2)
    w2 = e2 / (1.0 + e2)
    idx_ref[...] = jnp.where(lane == 0, i1, jnp.where(lane == 1, i2, 0))
    wts_ref[...] = jnp.where(lane == 0, w1, jnp.where(lane == 1, w2, 0.0))


def _router(h, g, whi, wlo, br, chunk, n_chunks, tm=1024):
    t = h.shape[0] // n_chunks
    first = chunk * (t // tm)
    return pl.pallas_call(
        _router_kernel,
        grid=(t // tm,),
        in_specs=[pl.BlockSpec((tm, D_MODEL), lambda i: (first + i, 0)),
                  _const_spec((1, D_MODEL)),
                  _const_spec((D_MODEL, LANES)),
                  _const_spec((D_MODEL, LANES)),
                  _const_spec((1, LANES))],
        out_specs=[_slab_spec(tm, lambda i: (i, 0)),
                   pl.BlockSpec((tm, LANES), lambda i: (i, 0)),
                   pl.BlockSpec((tm, LANES), lambda i: (i, 0))],
        out_shape=[jax.ShapeDtypeStruct((t * ROW_SLABS, LANES), F32),
                   jax.ShapeDtypeStruct((t, LANES), jnp.int32),
                   jax.ShapeDtypeStruct((t, LANES), F32)],
        compiler_params=_params("parallel"),
        name="router",
    )(h, g, whi, wlo, br)


def _route_plan(idx, tm):
    t = idx.shape[0]
    n_pairs = 2 * t
    n_rows = n_pairs + N_EXPERTS * tm
    e_flat = jnp.concatenate([idx[:, 0], idx[:, 1]])
    onehot = (e_flat[:, None] == jnp.arange(N_EXPERTS, dtype=jnp.int32)[None, :]).astype(jnp.int32)
    csum = jnp.cumsum(onehot, axis=0)
    rank = jnp.sum(onehot * csum, axis=1) - 1
    counts = csum[-1]
    padded = ((counts + tm - 1) // tm) * tm
    ends = jnp.cumsum(padded)
    starts = ends - padded
    pos = starts[e_flat] + rank
    order = jnp.argsort(e_flat, stable=True).astype(jnp.int32)
    first = jnp.cumsum(counts) - counts
    r = jnp.minimum(jnp.arange(n_rows, dtype=jnp.int32), ends[-1] - 1)
    e_r = jnp.minimum(jnp.searchsorted(ends, r, side="right").astype(jnp.int32), N_EXPERTS - 1)
    local = r - starts[e_r]
    src_pair = order[jnp.clip(first[e_r] + local, 0, n_pairs - 1)]
    src_tok = jnp.where(local < counts[e_r], src_pair % t, 0).astype(jnp.int32)
    tile_expert = e_r[::tm]
    n_valid = (ends[-1] // tm).astype(jnp.int32).reshape(1)
    return pos.astype(jnp.int32), src_tok, tile_expert, n_valid


def _gather_rows(table, idx):
    n = idx.shape[0]
    info = plsc.get_sparse_core_info()
    n_workers = info.num_cores * info.num_subcores
    per_worker = n // n_workers
    assert per_worker * n_workers == n and per_worker % SC_CHUNK == 0, (n, n_workers)
    mesh = plsc.VectorSubcoreMesh(core_axis_name="c", subcore_axis_name="s")

    @functools.partial(
        pl.kernel, mesh=mesh,
        out_type=jax.ShapeDtypeStruct((n,) + table.shape[1:], table.dtype),
        scratch_types=[pltpu.VMEM((SC_CHUNK,), jnp.int32),
                       pltpu.VMEM((SC_CHUNK,) + table.shape[1:], table.dtype),
                       pltpu.SemaphoreType.DMA],
    )
    def gather(table_hbm, idx_hbm, out_hbm, idx_v, rows_v, sem):
        worker = lax.axis_index("s") * info.num_cores + lax.axis_index("c")
        base = worker * per_worker

        @pl.loop(0, per_worker // SC_CHUNK)
        def _(i):
            off = pl.multiple_of(base + i * SC_CHUNK, SC_CHUNK)
            pltpu.sync_copy(idx_hbm.at[pl.ds(off, SC_CHUNK)], idx_v)
            pltpu.async_copy(table_hbm.at[idx_v], rows_v, sem).wait()
            pltpu.sync_copy(rows_v, out_hbm.at[pl.ds(off, SC_CHUNK)])

    return gather(table, idx)


def _expert_ffn_kernel(te_ref, nv_ref, x_ref, w13_ref, w2_ref, y_ref):
    valid = pl.program_id(0) < nv_ref[0]

    @pl.when(valid)
    def _():
        _store_slabs(y_ref, _swiglu(_split_slabs(x_ref).astype(BF), w13_ref.at[0], w2_ref.at[0]))

    @pl.when(jnp.logical_not(valid))
    def _():
        y_ref[...] = jnp.zeros_like(y_ref)


def _expert_ffn(xs, tile_expert, n_valid, w13, w2, tm):
    n_rows = xs.shape[0] // ROW_SLABS
    last = lambda nv: jnp.maximum(nv[0] - 1, 0)
    grid_spec = pltpu.PrefetchScalarGridSpec(
        num_scalar_prefetch=2,
        grid=(n_rows // tm,),
        in_specs=[_slab_spec(tm, lambda i, te, nv: (jnp.minimum(i, last(nv)), 0)),
                  pl.BlockSpec((1, D_MODEL, 2 * D_FF), lambda i, te, nv: (te[i], 0, 0)),
                  pl.BlockSpec((1, D_FF, D_MODEL), lambda i, te, nv: (te[i], 0, 0))],
        out_specs=_slab_spec(tm, lambda i, te, nv: (i, 0)),
    )
    return pl.pallas_call(
        _expert_ffn_kernel,
        grid_spec=grid_spec,
        out_shape=jax.ShapeDtypeStruct((n_rows * ROW_SLABS, LANES), F32),
        compiler_params=_params("arbitrary", vmem=BIG_VMEM_LIMIT),
        name="expert_ffn",
    )(tile_expert, n_valid, xs, w13, w2)


def _combine_kernel(h_ref, y0_ref, y1_ref, wts_ref, gf_ref, *rest):
    out_ref = rest[-1]
    wts = wts_ref[...]
    tot = h_ref[...] + wts[:, 0:1] * _split_slabs(y0_ref) + wts[:, 1:2] * _split_slabs(y1_ref)
    out_ref[...] = _rms(tot, gf_ref[...])


def _combine(h, yg, wts, g_final, chunk, n_chunks, out_so_far, tm=1024):
    t = h.shape[0]
    nt = t // n_chunks // tm
    first = chunk * nt
    in_specs = [pl.BlockSpec((tm, D_MODEL), lambda i: (first + i, 0)),
                _slab_spec(tm, lambda i: (i, 0)),
                _slab_spec(tm, lambda i: (nt + i, 0)),
                pl.BlockSpec((tm, LANES), lambda i: (i, 0)),
                _const_spec((1, D_MODEL))]
    args = [h, yg, yg, wts, g_final]
    aliases = {}
    if out_so_far is not None:
        in_specs.append(pl.BlockSpec(memory_space=pl.ANY))
        args.append(out_so_far)
        aliases = {len(args) - 1: 0}
    return pl.pallas_call(
        _combine_kernel,
        grid=(nt,),
        in_specs=in_specs,
        out_specs=pl.BlockSpec((tm, D_MODEL), lambda i: (first + i, 0)),
        out_shape=jax.ShapeDtypeStruct((t, D_MODEL), F32),
        input_output_aliases=aliases,
        compiler_params=_params("parallel"),
        name="combine",
    )(*args)


def _moe(h, g, router_w, router_b, w13, w2, g_final, tm=512):
    wr = jnp.pad(router_w, ((0, 0), (0, LANES - N_EXPERTS)))
    whi = wr.astype(BF)
    wlo = (wr - whi.astype(F32)).astype(BF)
    br = jnp.pad(router_b, (0, LANES - N_EXPERTS)).reshape(1, LANES)
    as_rows = lambda a: a.reshape(-1, ROW_SLABS, LANES)
    as_slabs = lambda a: a.reshape(-1, LANES)
    chunks = range(MOE_CHUNKS)
    routed = [_router(h, g, whi, wlo, br, c, MOE_CHUNKS) for c in chunks]
    plans = [_route_plan(idx, tm) for _, idx, _ in routed]
    xs = [as_slabs(_gather_rows(as_rows(routed[c][0]), plans[c][1])) for c in chunks]
    ys = [_expert_ffn(xs[c], plans[c][2], plans[c][3], w13, w2, tm) for c in chunks]
    yg = [as_slabs(_gather_rows(as_rows(ys[c]), plans[c][0])) for c in chunks]
    out = None
    for c in chunks:
        out = _combine(h, yg[c], routed[c][2], g_final, c, MOE_CHUNKS, out)
    return out


def _feature_major(w, heads):
    wt = w.T.reshape(heads, HEAD_DIM, D_MODEL)
    return jnp.pad(wt, ((0, 0), (0, VT_ROWS - HEAD_DIM), (0, 0))).reshape(heads * VT_ROWS, D_MODEL)


def _pack_w_in(w_in):
    cuts = np.cumsum((512, 512, 512, 512, 512, 128, 128, 512, 512, 512, 8))[:-1].tolist()
    a_u, a_v, b_x, b_y, c_q, c_k, c_v, d_q, d_k, d_v, d_f = jnp.split(w_in, cuts, axis=-1)
    w = jnp.concatenate([b_x, b_y, d_q, d_k, c_k, c_v], axis=-1).astype(BF)
    wa = jnp.concatenate([a_u, a_v], axis=-1).astype(BF)
    wf = jnp.pad(d_f, ((0, 0), (0, LANES - FOX_HEADS))).astype(BF)
    wt = jnp.concatenate([_feature_major(d_v, FOX_HEADS), c_q.T, _feature_major(c_v, SWA_KV)], axis=0).astype(BF)
    return w, wa, wf, wt


def _fm_ones():
    ones = np.zeros((FM_ROWS, LANES), np.float32)
    for base, heads in ((0, FOX_HEADS), (VT_ALL + BRANCH_W, SWA_KV)):
        for h in range(heads):
            ones[base + h * VT_ROWS + HEAD_DIM, :] = 1.0
    return jnp.asarray(ones)


def _row(v):
    return v.reshape(1, -1)


def _hybrid_mixer(h, bsz, norm_mix, w_in, sgu_g, sgu_w, sgu_b, conv_w, conv_b, rg_wa, rg_ba, rg_wx, rg_bx,
                  rg_lambda, swa_sinks, fox_bf, w_branch, w_gate, b_gate, w_out):
    w, wa, wf, wt = _pack_w_in(w_in)
    proj, f, o_a, vt, sqt, svt = _inproj(h, _row(norm_mix), w, wa, wf, wt, _fm_ones(), _row(sgu_g), sgu_w, sgu_b.T)
    wax = jnp.concatenate([rg_wa, rg_wx], axis=-1).astype(BF)
    o_b = _rglru(proj, bsz, conv_w, _row(conv_b), wax, _row(rg_ba), _row(rg_bx), _row(rg_lambda))
    o_c = _swa(proj, sqt, svt, bsz, swa_sinks)
    bf_pad = jnp.pad(fox_bf, (0, LANES - FOX_HEADS)).reshape(1, LANES)
    qp, kp = _fox_prep(proj, f, bsz, bf_pad)
    o_d = _fox(qp, kp, vt, bsz)
    return _merge(h, _row(norm_mix), o_a, o_b, o_c, o_d, w_gate.astype(BF), b_gate[:, None, :],
                  w_branch.astype(BF), w_out.astype(BF))


def kernel(x, mem, norm_mix, w_in, sgu_g, sgu_w, sgu_b, conv_w, conv_b, rg_wa, rg_ba, rg_wx, rg_bx, rg_lambda, swa_sinks, fox_bf, w_branch, w_gate, b_gate, w_out, norm_cross, norm_mem, wq_c, wkv_c, wo_c, norm_ffn, dense_w13, dense_w2, router_w, router_b, moe_w13, moe_w2, norm_final):
    bsz, seq, d = x.shape
    m_len = mem.shape[1]
    depth = norm_mix.shape[0]
    assert depth == 2, "the final RMSNorm is fused into the routed layer, which must be the last one"
    h = x.reshape(bsz * seq, d)
    mem2 = mem.reshape(bsz * m_len, d)
    for l in range(depth):
        h = _hybrid_mixer(h, bsz, norm_mix[l], w_in[l], sgu_g[l], sgu_w[l], sgu_b[l], conv_w[l], conv_b[l],
                          rg_wa[l], rg_ba[l], rg_wx[l], rg_bx[l], rg_lambda[l], swa_sinks[l], fox_bf[l],
                          w_branch[l], w_gate[l], b_gate[l], w_out[l])
        kv = _memkv(mem2, _row(norm_mem[l]), wkv_c[l].astype(BF), m_len)
        h = _cross(h, _row(norm_cross[l]), wq_c[l].astype(BF), kv, wo_c[l].astype(BF), bsz, m_len)
        if l % 2 == 0:
            h = _ffn(h, _row(norm_ffn[l]), dense_w13[l // 2].astype(BF), dense_w2[l // 2].astype(BF))
        else:
            h = _moe(h, _row(norm_ffn[l]), router_w[l // 2], router_b[l // 2], moe_w13[l // 2].astype(BF),
                     moe_w2[l // 2].astype(BF), _row(norm_final))
    return h.reshape(bsz, seq, d)
```

```python
import functools
import math

import numpy as np
import jax
import jax.numpy as jnp
from jax import lax
from jax.experimental import pallas as pl
from jax.experimental.pallas import tpu as pltpu
from jax.experimental.pallas import tpu_sc as plsc

F32 = jnp.float32
BF = jnp.bfloat16

D_MODEL = 1024
BRANCH_W = 512
HEAD_DIM = 64
CHUNK = 128
SGU_GROUPS = 4
LRU_HEADS = 4
LRU_C = 8.0
CONV_W = 4
SWA_HEADS = 8
SWA_KV = 2
FOX_HEADS = 8
FOX_PAIRS = FOX_HEADS // 2
X_HEADS = 4
X_HEAD_DIM = 128
D_FF = 2816
N_EXPERTS = 8
EPS = 1e-6
LOG2E = math.log2(math.e)
NEG = -1e30
LANES = 128
ROW_SLABS = D_MODEL // LANES
SC_CHUNK = 64
MOE_CHUNKS = 2
VMEM_LIMIT = 56 * 1024 * 1024
BIG_VMEM_LIMIT = 62 * 1024 * 1024

COL_BX, COL_BY, COL_DQ, COL_DK = range(4)
N_WIDE = 4
N_PROJ = N_WIDE * BRANCH_W + 2 * SWA_KV * HEAD_DIM
VT_ROWS = 80
VT_ALL = FOX_HEADS * VT_ROWS
SWA_VT = SWA_KV * VT_ROWS
FM_ROWS = VT_ALL + BRANCH_W + SWA_VT
FOX_UNROLL = 4
FOX_STEP_PAIRS = 1
FF_SPLITS = ((0, 1536), (1536, D_FF))


def _rms(x, g):
    return x * lax.rsqrt(jnp.mean(x * x, axis=-1, keepdims=True) + EPS) * g


def _sigmoid(x):
    return 1.0 / (1.0 + jnp.exp(-x))


def _gelu(x):
    return 0.5 * x * (1.0 + jnp.tanh(math.sqrt(2.0 / math.pi) * (x + 0.044715 * (x * x * x))))


def _dot(a, b):
    return jnp.dot(a, b, preferred_element_type=F32)


def _dot_nt(a, b):
    return lax.dot_general(a, b, (((1,), (1,)), ((), ())), preferred_element_type=F32)


def _shift_rows(x, d, fill):
    row = lax.broadcasted_iota(jnp.int32, x.shape, 0)
    return jnp.where(row >= d, pltpu.roll(x, d, 0), fill)


def _params(*sem, vmem=VMEM_LIMIT):
    return pltpu.CompilerParams(dimension_semantics=sem, vmem_limit_bytes=vmem)


def _const_spec(shape):
    nd = len(shape)
    return pl.BlockSpec(shape, lambda *_: (0,) * nd, pipeline_mode=pl.Buffered(1))


def _inproj_kernel(h_ref, g_ref, w_ref, wa_ref, wf_ref, wt_ref, ones_ref, sg_ref, sw_ref, sbt_ref,
                   proj_ref, f_ref, oa_ref, vt_ref, sqt_ref, svt_ref):
    xn = _rms(h_ref[...], g_ref[...]).astype(BF)
    tm = xn.shape[0]
    n_col = w_ref.shape[1]
    for c in range(0, n_col, BRANCH_W):
        w = min(BRANCH_W, n_col - c)
        proj_ref[:, c:c + w] = _dot(xn, w_ref[:, c:c + w]).astype(BF)
    f_ref[...] = _dot(xn, wf_ref[...])
    ones = jnp.concatenate([ones_ref[...]] * (tm // LANES), axis=1)
    fm = (_dot_nt(wt_ref[...], xn) + ones).astype(BF)
    vt_ref[...] = fm[:VT_ALL]
    sqt_ref[...] = fm[VT_ALL:VT_ALL + BRANCH_W]
    svt_ref[...] = fm[VT_ALL + BRANCH_W:]

    u = _gelu(_dot(xn, wa_ref[:, :BRANCH_W]))
    v = _gelu(_dot(xn, wa_ref[:, BRANCH_W:]))
    vn = _rms(v, sg_ref[...]).astype(BF)
    row = lax.broadcasted_iota(jnp.int32, (CHUNK, CHUNK), 0)
    col = lax.broadcasted_iota(jnp.int32, (CHUNK, CHUNK), 1)
    gw = BRANCH_W // SGU_GROUPS
    for g in range(SGU_GROUPS):
        wg = jnp.where(col <= row, sw_ref[g], 0.0).astype(BF)
        bg = sbt_ref[:, g:g + 1]
        for c in range(tm // CHUNK):
            rs = slice(c * CHUNK, (c + 1) * CHUNK)
            cs = slice(g * gw, (g + 1) * gw)
            mixed = _dot(wg, vn[rs, cs]) + bg
            oa_ref[rs, cs] = (u[rs, cs] * mixed).astype(BF)


def _inproj(h, g, w, wa, wf, wt, ones, sgu_g, sgu_w, sgu_bt, tm=1024):
    t = h.shape[0]
    row = lambda i: (i, 0)
    col = lambda i: (0, i)
    return pl.pallas_call(
        _inproj_kernel,
        grid=(t // tm,),
        in_specs=[pl.BlockSpec((tm, D_MODEL), row),
                  _const_spec((1, D_MODEL)),
                  _const_spec((D_MODEL, N_PROJ)),
                  _const_spec((D_MODEL, 2 * BRANCH_W)),
                  _const_spec((D_MODEL, LANES)),
                  _const_spec((FM_ROWS, D_MODEL)),
                  _const_spec((FM_ROWS, LANES)),
                  _const_spec((1, BRANCH_W)),
                  _const_spec((SGU_GROUPS, CHUNK, CHUNK)),
                  _const_spec((CHUNK, SGU_GROUPS))],
        out_specs=[pl.BlockSpec((tm, N_PROJ), row),
                   pl.BlockSpec((tm, LANES), row),
                   pl.BlockSpec((tm, BRANCH_W), row),
                   pl.BlockSpec((VT_ALL, tm), col),
                   pl.BlockSpec((BRANCH_W, tm), col),
                   pl.BlockSpec((SWA_VT, tm), col)],
        out_shape=[jax.ShapeDtypeStruct((t, N_PROJ), BF), jax.ShapeDtypeStruct((t, LANES), F32),
                   jax.ShapeDtypeStruct((t, BRANCH_W), BF), jax.ShapeDtypeStruct((VT_ALL, t), BF),
                   jax.ShapeDtypeStruct((BRANCH_W, t), BF), jax.ShapeDtypeStruct((SWA_VT, t), BF)],
        compiler_params=_params("parallel", vmem=BIG_VMEM_LIMIT),
        name="inproj",
    )(h, g, w, wa, wf, wt, ones, sgu_g, sgu_w, sgu_bt)


def _rglru_kernel(x_ref, y_ref, cw_ref, cb_ref, wax_ref, ba_ref, bx_ref, lam_ref, o_ref, xs_ref, hc_ref):
    ts = x_ref.shape[0]
    hw = BRANCH_W // LRU_HEADS

    @pl.when(pl.program_id(1) == 0)
    def _():
        xs_ref[0:8, :] = jnp.zeros((8, BRANCH_W), F32)
        hc_ref[...] = jnp.zeros_like(hc_ref)

    x = x_ref[...].astype(F32)
    xs_ref[8:8 + ts, :] = x
    cw = cw_ref[...]
    xc = cb_ref[...] + cw[CONV_W - 1:CONV_W] * x
    for k in range(1, CONV_W):
        xc = xc + cw[CONV_W - 1 - k:CONV_W - k] * xs_ref[8 - k:8 - k + ts, :]
    xs_ref[0:8, :] = x[ts - 8:ts, :]

    xcb = xc.astype(BF)
    r_parts, i_parts = [], []
    for hd in range(LRU_HEADS):
        z = _dot(xcb[:, hd * hw:(hd + 1) * hw], wax_ref[hd])
        r_parts.append(z[:, :hw])
        i_parts.append(z[:, hw:])
    r = _sigmoid(jnp.concatenate(r_parts, axis=1) + ba_ref[...])
    gi = _sigmoid(jnp.concatenate(i_parts, axis=1) + bx_ref[...])
    nl = -lam_ref[...]
    softplus = jnp.maximum(nl, 0.0) + jnp.log1p(jnp.exp(-jnp.abs(nl)))
    log_a = (-LRU_C * r) * softplus
    a = jnp.exp(log_a)
    th = jnp.tanh(log_a)
    e2 = -2.0 * th / (1.0 - th)
    b = (xc * gi) * (e2 * lax.rsqrt(jnp.maximum(e2, 1e-30)))

    sub = lax.broadcasted_iota(jnp.int32, a.shape, 0) & 7
    for d in (1, 2, 4):
        keep = sub >= d
        a_sh = jnp.where(keep, pltpu.roll(a, d, 0), 1.0)
        b_sh = jnp.where(keep, pltpu.roll(b, d, 0), 0.0)
        b = b + a * b_sh
        a = a * a_sh
    carry = hc_ref[...]
    groups = []
    for g in range(ts // 8):
        hg = a[8 * g:8 * g + 8] * carry + b[8 * g:8 * g + 8]
        groups.append(hg)
        carry = hg[7:8]
    h = jnp.concatenate(groups, axis=0)
    hc_ref[...] = carry
    o_ref[...] = (h * _gelu(y_ref[...].astype(F32))).astype(BF)


def _rglru(proj, bsz, cw, cb, wax, ba, bx, lam, ts=512):
    t = proj.shape[0]
    ns = t // bsz // ts
    return pl.pallas_call(
        _rglru_kernel,
        grid=(bsz, ns),
        in_specs=[pl.BlockSpec((ts, BRANCH_W), lambda b, j: (b * ns + j, COL_BX)),
                  pl.BlockSpec((ts, BRANCH_W), lambda b, j: (b * ns + j, COL_BY)),
                  _const_spec((CONV_W, BRANCH_W)),
                  _const_spec((1, BRANCH_W)),
                  _const_spec((LRU_HEADS, BRANCH_W // LRU_HEADS, 2 * BRANCH_W // LRU_HEADS)),
                  _const_spec((1, BRANCH_W)),
                  _const_spec((1, BRANCH_W)),
                  _const_spec((1, BRANCH_W))],
        out_specs=pl.BlockSpec((ts, BRANCH_W), lambda b, j: (b * ns + j, 0)),
        out_shape=jax.ShapeDtypeStruct((t, BRANCH_W), BF),
        scratch_shapes=[pltpu.VMEM((ts + 8, BRANCH_W), F32), pltpu.VMEM((1, BRANCH_W), F32)],
        compiler_params=_params("parallel", "arbitrary"),
        name="rglru",
    )(proj, proj, cw, cb, wax, ba, bx, lam)


def _swa_tables(sinks):
    t_idx = np.arange(CHUNK)[None, :] + CHUNK
    dist = (t_idx - np.arange(2 * CHUNK)[:, None]).astype(np.float32)
    in_win = (dist >= 0) & (dist < CHUNK)
    slopes = 2.0 ** (-(8.0 / SWA_HEADS) * np.arange(1, SWA_HEADS + 1, dtype=np.float32))
    bias = np.where(in_win[None], -slopes[:, None, None] * dist[None] * LOG2E, NEG)
    grp = SWA_HEADS // SWA_KV
    bias = bias.reshape(SWA_KV, grp, 2 * CHUNK, CHUNK).transpose(0, 2, 1, 3).reshape(SWA_KV, 2 * CHUNK, grp * CHUNK)
    sink = jnp.repeat(sinks.astype(F32) * LOG2E, CHUNK).reshape(SWA_KV, 1, grp * CHUNK)
    return jnp.asarray(bias, F32), sink


def _swa_kernel(qt_ref, kv_ref, kvp_ref, vt_ref, vtp_ref, bias_ref, sink_ref, o_ref):
    tq = kv_ref.shape[0]
    grp = SWA_HEADS // SWA_KV
    first_key = jnp.where(pl.program_id(1) == 0, CHUNK, 0)
    k_all = jnp.concatenate([kvp_ref[...], kv_ref[...]], axis=0)
    vt_all = jnp.concatenate([vtp_ref[...], vt_ref[...]], axis=1)
    qt = (qt_ref[...].astype(F32) * (HEAD_DIM ** -0.5 * LOG2E)).astype(BF)
    key_row = lax.broadcasted_iota(jnp.int32, (2 * CHUNK, grp * CHUNK), 0)
    kw = 2 * SWA_KV * HEAD_DIM
    for qb in range(tq // CHUNK):
        cols = slice(qb * CHUNK, (qb + 1) * CHUNK)
        kb = k_all[qb * CHUNK:(qb + 2) * CHUNK]
        vtb = vt_all[:, qb * CHUNK:(qb + 2) * CHUNK]
        outs = []
        for kh in range(SWA_KV):
            qg = jnp.concatenate([qt[(kh * grp + g) * HEAD_DIM:(kh * grp + g + 1) * HEAD_DIM, cols]
                                  for g in range(grp)], axis=1)
            parts = []
            if kh > 0:
                parts.append(jnp.zeros((kh * HEAD_DIM, grp * CHUNK), BF))
            parts.append(qg)
            parts.append(jnp.zeros((kw - (kh + 1) * HEAD_DIM, grp * CHUNK), BF))
            st = _dot(kb, jnp.concatenate(parts, axis=0)) + bias_ref[kh]
            if qb == 0:
                st = jnp.where(key_row >= first_key, st, NEG)
            sink = sink_ref[kh]
            m = jnp.maximum(jnp.max(st, axis=0, keepdims=True), sink)
            pt = jnp.exp2(st - m).astype(BF)
            acc = _dot(vtb[kh * VT_ROWS:(kh + 1) * VT_ROWS], pt)
            ot = acc[:HEAD_DIM] / (acc[HEAD_DIM:HEAD_DIM + 1] + jnp.exp2(sink - m))
            outs.extend(ot[:, g * CHUNK:(g + 1) * CHUNK] for g in range(grp))
        o_ref[cols, :] = jnp.concatenate(outs, axis=0).T.astype(BF)


def _swa(proj, sqt, svt, bsz, sinks, tq=1024):
    t = proj.shape[0]
    seq = t // bsz
    nq = seq // tq
    per = tq // CHUNK
    kv_blk = N_WIDE * BRANCH_W // (2 * SWA_KV * HEAD_DIM)
    prev = lambda b, i: b * (seq // CHUNK) + jnp.maximum(i * per - 1, 0)
    bias, sink = _swa_tables(sinks)
    grp = SWA_HEADS // SWA_KV
    return pl.pallas_call(
        _swa_kernel,
        grid=(bsz, nq),
        in_specs=[pl.BlockSpec((BRANCH_W, tq), lambda b, i: (0, b * nq + i)),
                  pl.BlockSpec((tq, 2 * SWA_KV * HEAD_DIM), lambda b, i: (b * nq + i, kv_blk)),
                  pl.BlockSpec((CHUNK, 2 * SWA_KV * HEAD_DIM), lambda b, i: (prev(b, i), kv_blk)),
                  pl.BlockSpec((SWA_VT, tq), lambda b, i: (0, b * nq + i)),
                  pl.BlockSpec((SWA_VT, CHUNK), lambda b, i: (0, prev(b, i))),
                  _const_spec((SWA_KV, 2 * CHUNK, grp * CHUNK)),
                  _const_spec((SWA_KV, 1, grp * CHUNK))],
        out_specs=pl.BlockSpec((tq, BRANCH_W), lambda b, i: (b * nq + i, 0)),
        out_shape=jax.ShapeDtypeStruct((t, BRANCH_W), BF),
        compiler_params=_params("parallel", "parallel"),
        name="swa",
    )(sqt, proj, proj, svt, svt, bias, sink)


def _aug_tables():
    eq = np.zeros((3, LANES, FOX_PAIRS * LANES), np.float32)
    ek = np.zeros((3, LANES, FOX_PAIRS * LANES), np.float32)
    oq = np.zeros((1, FOX_PAIRS * LANES), np.float32)
    ok = np.zeros((1, FOX_PAIRS * LANES), np.float32)
    for h in range(FOX_HEADS):
        base = (h // 2) * LANES + 6 * (h % 2)
        for s in range(3):
            eq[s, h, base + s] = 1.0
            ek[s, h, base + 3 + s] = -1.0
            ok[0, base + s] = 1.0
            oq[0, base + 3 + s] = 1.0
    return eq, ek, oq, ok


def _fox_prep_kernel(q_ref, k_ref, f_ref, bf_ref, eq_ref, ek_ref, oq_ref, ok_ref, qp_ref, kp_ref, cum_ref):
    ts = q_ref.shape[0]

    @pl.when(pl.program_id(1) == 0)
    def _():
        cum_ref[...] = jnp.zeros_like(cum_ref)

    z = f_ref[...] + bf_ref[...]
    c = jnp.minimum(z, 0.0) - jnp.log1p(jnp.exp(-jnp.abs(z)))
    d = 1
    while d < ts:
        c = c + _shift_rows(c, d, 0.0)
        d *= 2
    c = c + cum_ref[...]
    cum_ref[...] = c[ts - 1:ts, :]
    c = c * LOG2E
    c1 = c.astype(BF)
    r1 = c - c1.astype(F32)
    c2 = r1.astype(BF)
    c3 = (r1 - c2.astype(F32)).astype(BF)
    augq = _dot(c1, eq_ref[0]) + _dot(c2, eq_ref[1]) + _dot(c3, eq_ref[2]) + oq_ref[...]
    augk = _dot(c1, ek_ref[0]) + _dot(c2, ek_ref[1]) + _dot(c3, ek_ref[2]) + ok_ref[...]
    qt = (q_ref[...].astype(F32) * (HEAD_DIM ** -0.5 * LOG2E)).T.astype(BF)
    augqt = augq.T.astype(BF)
    k = k_ref[...]
    for p in range(FOX_PAIRS):
        qp_ref[2 * p * LANES:(2 * p + 1) * LANES, :] = qt[p * LANES:(p + 1) * LANES]
        qp_ref[(2 * p + 1) * LANES:(2 * p + 2) * LANES, :] = augqt[p * LANES:(p + 1) * LANES]
        kp_ref[:, 2 * p * LANES:(2 * p + 1) * LANES] = k[:, p * LANES:(p + 1) * LANES]
        kp_ref[:, (2 * p + 1) * LANES:(2 * p + 2) * LANES] = augk[:, p * LANES:(p + 1) * LANES].astype(BF)


def _fox_prep(proj, f, bsz, bf_pad, ts=1024):
    t = proj.shape[0]
    ns = t // bsz // ts
    eq, ek, oq, ok = _aug_tables()
    wide = 2 * FOX_PAIRS * LANES
    return pl.pallas_call(
        _fox_prep_kernel,
        grid=(bsz, ns),
        in_specs=[pl.BlockSpec((ts, BRANCH_W), lambda b, j: (b * ns + j, COL_DQ)),
                  pl.BlockSpec((ts, BRANCH_W), lambda b, j: (b * ns + j, COL_DK)),
                  pl.BlockSpec((ts, LANES), lambda b, j: (b * ns + j, 0)),
                  _const_spec((1, LANES)),
                  _const_spec(eq.shape), _const_spec(ek.shape), _const_spec(oq.shape), _const_spec(ok.shape)],
        out_specs=[pl.BlockSpec((wide, ts), lambda b, j: (0, b * ns + j)),
                   pl.BlockSpec((ts, wide), lambda b, j: (b * ns + j, 0))],
        out_shape=[jax.ShapeDtypeStruct((wide, t), BF), jax.ShapeDtypeStruct((t, wide), BF)],
        scratch_shapes=[pltpu.VMEM((1, LANES), F32)],
        compiler_params=_params("parallel", "arbitrary"),
        name="fox_prep",
    )(proj, proj, f, bf_pad, jnp.asarray(eq, BF), jnp.asarray(ek, BF), jnp.asarray(oq), jnp.asarray(ok))


def _fox_kernel(q_ref, k_ref, vt_ref, o_ref, st_ref, pt_ref, al_ref, m_ref, acc_ref):
    tq = q_ref.shape[1]
    tk = tq // 2
    n_heads = m_ref.shape[0]
    qi = pl.program_id(2)
    row = lax.broadcasted_iota(jnp.int32, (2 * LANES, tq), 0)
    in0 = (row < HEAD_DIM) | ((row >= LANES) & (row < LANES + 6))
    in1 = ((row >= HEAD_DIM) & (row < LANES)) | ((row >= LANES + 6) & (row < LANES + 12))
    qh = []
    for p in range(n_heads // 2):
        q = q_ref[2 * p * LANES:(2 * p + 2) * LANES, :]
        zero = jnp.zeros_like(q)
        qh += [jnp.where(in0, q, zero), jnp.where(in1, q, zero)]
    for h in range(n_heads):
        m_ref[h] = jnp.full(m_ref.shape[1:], NEG, F32)
        acc_ref[h] = jnp.zeros(acc_ref.shape[1:], F32)
        pt_ref[1, h] = jnp.zeros(pt_ref.shape[2:], BF)
        al_ref[1, h] = jnp.ones(al_ref.shape[2:], F32)

    def scores(t, slot, lo=0):
        rows = pl.ds(pl.multiple_of(t * tk, tk), tk)
        for p in range(n_heads // 2):
            kp = k_ref[rows, 2 * p * LANES:(2 * p + 2) * LANES]
            for h in (2 * p, 2 * p + 1):
                st_ref[slot, h, :, lo:] = _dot(kp, qh[h][:, lo:])

    def numerators(slot, key_offset, lo=0):
        for h in range(n_heads):
            st = st_ref[slot, h, :, lo:]
            if key_offset is not None:
                key = lax.broadcasted_iota(jnp.int32, st.shape, 0) + key_offset
                qry = lax.broadcasted_iota(jnp.int32, st.shape, 1) + lo
                st = jnp.where(key <= qry, st, NEG)
            m_old = m_ref[h, :, lo:]
            m_new = jnp.maximum(m_old, jnp.max(st, axis=0, keepdims=True))
            al_ref[slot, h, :, lo:] = jnp.exp2(m_old - m_new)
            pt_ref[slot, h, :, lo:] = jnp.exp2(st - m_new).astype(BF)
            m_ref[h, :, lo:] = m_new

    def accumulate(t, slot, lo=0):
        cols = pl.ds(pl.multiple_of(t * tk, tk), tk)
        for h in range(n_heads):
            vt = vt_ref[h * VT_ROWS:(h + 1) * VT_ROWS, cols]
            acc_ref[h, :, lo:] = al_ref[slot, h, :, lo:] * acc_ref[h, :, lo:] + _dot(vt, pt_ref[slot, h, :, lo:])

    def tile_pair(j, diagonal):
        lo = tk if diagonal else 0
        scores(2 * j + 1, 1, lo)
        accumulate(jnp.maximum(2 * j - 1, 0), 1)
        numerators(0, 0 if diagonal else None)
        if not diagonal:
            scores(2 * j + 2, 0)
        accumulate(2 * j, 0)
        numerators(1, tk if diagonal else None, lo)

    def body(jj, carry):
        for u in range(FOX_UNROLL):
            tile_pair(FOX_UNROLL * jj + u, False)
        return carry

    scores(0, 0)
    lax.fori_loop(0, qi // FOX_UNROLL, body, 0)
    assert FOX_UNROLL == 4, "the leftover pairs below are handled as one block of two and one single"
    rest = qi % FOX_UNROLL
    first_rest = (qi // FOX_UNROLL) * FOX_UNROLL

    @pl.when(rest >= 2)
    def _():
        tile_pair(first_rest, False)
        tile_pair(first_rest + 1, False)

    @pl.when(rest % 2 == 1)
    def _():
        tile_pair(qi - 1, False)

    tile_pair(qi, True)
    accumulate(2 * qi + 1, 1, tk)
    outs = []
    for h in range(n_heads):
        acc = acc_ref[h]
        outs.append(acc[:HEAD_DIM] / acc[HEAD_DIM:HEAD_DIM + 1])
    o_ref[...] = jnp.concatenate(outs, axis=0).T.astype(BF)


def _fox(qp, kp, vt, bsz, tq=512):
    t = kp.shape[0]
    seq = t // bsz
    nq = seq // tq
    np_ = FOX_STEP_PAIRS
    nh = 2 * np_
    return pl.pallas_call(
        _fox_kernel,
        grid=(bsz, FOX_PAIRS // np_, nq),
        in_specs=[pl.BlockSpec((np_ * 2 * LANES, tq), lambda b, p, i: (p, b * nq + i)),
                  pl.BlockSpec((seq, np_ * 2 * LANES), lambda b, p, i: (b, p)),
                  pl.BlockSpec((nh * VT_ROWS, seq), lambda b, p, i: (p, b))],
        out_specs=pl.BlockSpec((tq, np_ * LANES), lambda b, p, i: (b * nq + i, p)),
        out_shape=jax.ShapeDtypeStruct((t, BRANCH_W), BF),
        scratch_shapes=[pltpu.VMEM((2, nh, tq // 2, tq), F32),
                        pltpu.VMEM((2, nh, tq // 2, tq), BF),
                        pltpu.VMEM((2, nh, 1, tq), F32),
                        pltpu.VMEM((nh, 1, tq), F32),
                        pltpu.VMEM((nh, VT_ROWS, tq), F32)],
        compiler_params=_params("parallel", "parallel", "arbitrary"),
        name="fox",
    )(qp, kp, vt)


def _merge_kernel(h_ref, g_ref, oa_ref, ob_ref, oc_ref, od_ref, wg_ref, bg_ref, wb_ref, wo_ref, out_ref):
    h = h_ref[...]
    xn = _rms(h, g_ref[...]).astype(BF)
    merged = None
    for br, o_ref in enumerate((oa_ref, ob_ref, oc_ref, od_ref)):
        gate = _sigmoid(_dot(xn, wg_ref[br]) + bg_ref[br])
        term = gate * _dot(o_ref[...], wb_ref[br])
        merged = term if merged is None else merged + term
    out_ref[...] = h + _dot(merged.astype(BF), wo_ref[...])


def _merge(h, g, oa, ob, oc, od, wg, bg, wb, wo, tm=1024):
    t = h.shape[0]
    row = lambda i: (i, 0)
    return pl.pallas_call(
        _merge_kernel,
        grid=(t // tm,),
        in_specs=[pl.BlockSpec((tm, D_MODEL), row),
                  _const_spec((1, D_MODEL)),
                  pl.BlockSpec((tm, BRANCH_W), row), pl.BlockSpec((tm, BRANCH_W), row),
                  pl.BlockSpec((tm, BRANCH_W), row), pl.BlockSpec((tm, BRANCH_W), row),
                  _const_spec((4, D_MODEL, D_MODEL)),
                  _const_spec((4, 1, D_MODEL)),
                  _const_spec((4, BRANCH_W, D_MODEL)),
                  _const_spec((D_MODEL, D_MODEL))],
        out_specs=pl.BlockSpec((tm, D_MODEL), row),
        out_shape=jax.ShapeDtypeStruct((t, D_MODEL), F32),
        compiler_params=_params("parallel", vmem=BIG_VMEM_LIMIT),
        name="merge",
    )(h, g, oa, ob, oc, od, wg, bg, wb, wo)


def _memkv_kernel(mem_ref, g_ref, w_ref, kv_ref):
    mn = _rms(mem_ref[...], g_ref[...]).astype(BF)
    kv_ref[...] = _dot(mn, w_ref[...]).astype(BF)


def _memkv(mem2, g, w, m_len):
    n = mem2.shape[0]
    width = 2 * X_HEADS * X_HEAD_DIM
    return pl.pallas_call(
        _memkv_kernel,
        grid=(n // m_len,),
        in_specs=[pl.BlockSpec((m_len, D_MODEL), lambda b: (b, 0)),
                  _const_spec((1, D_MODEL)),
                  _const_spec((D_MODEL, width))],
        out_specs=pl.BlockSpec((m_len, width), lambda b: (b, 0)),
        out_shape=jax.ShapeDtypeStruct((n, width), BF),
        compiler_params=_params("parallel"),
        name="memkv",
    )(mem2, g, w)


def _cross_kernel(h_ref, g_ref, wq_ref, kv_ref, wo_ref, out_ref):
    h = h_ref[...]
    hn = _rms(h, g_ref[...]).astype(BF)
    q = _dot(hn, wq_ref[...]).astype(BF)
    kv = kv_ref[...]
    width = X_HEADS * X_HEAD_DIM
    outs = []
    for hd in range(X_HEADS):
        cs = slice(hd * X_HEAD_DIM, (hd + 1) * X_HEAD_DIM)
        s = _dot_nt(q[:, cs], kv[:, cs]) * (X_HEAD_DIM ** -0.5)
        m = jnp.max(s, axis=-1, keepdims=True)
        p = jnp.exp(s - m)
        denom = jnp.sum(p, axis=-1, keepdims=True)
        v = kv[:, width + hd * X_HEAD_DIM:width + (hd + 1) * X_HEAD_DIM]
        outs.append((_dot(p.astype(BF), v) / denom).astype(BF))
    o = jnp.concatenate(outs, axis=1)
    out_ref[...] = h + _dot(o, wo_ref[...])


def _cross(h, g, wq, kv, wo, bsz, m_len, tm=1024):
    t = h.shape[0]
    per = t // bsz // tm
    width = X_HEADS * X_HEAD_DIM
    return pl.pallas_call(
        _cross_kernel,
        grid=(bsz, per),
        in_specs=[pl.BlockSpec((tm, D_MODEL), lambda b, i: (b * per + i, 0)),
                  _const_spec((1, D_MODEL)),
                  _const_spec((D_MODEL, width)),
                  pl.BlockSpec((m_len, 2 * width), lambda b, i: (b, 0)),
                  _const_spec((width, D_MODEL))],
        out_specs=pl.BlockSpec((tm, D_MODEL), lambda b, i: (b * per + i, 0)),
        out_shape=jax.ShapeDtypeStruct((t, D_MODEL), F32),
        compiler_params=_params("parallel", "parallel"),
        name="cross",
    )(h, g, wq, kv, wo)


def _swiglu(xb, w13_ref, w2_ref):
    out = None
    for lo, hi in FF_SPLITS:
        gate = _dot(xb, w13_ref[:, lo:hi])
        up = _dot(xb, w13_ref[:, D_FF + lo:D_FF + hi])
        act = (gate * _sigmoid(gate) * up).astype(BF)
        part = _dot(act, w2_ref[lo:hi, :])
        out = part if out is None else out + part
    return out


def _ffn_kernel(h_ref, g_ref, w13_ref, w2_ref, out_ref):
    h = h_ref[...]
    out_ref[...] = h + _swiglu(_rms(h, g_ref[...]).astype(BF), w13_ref, w2_ref)


def _ffn(h, g, w13, w2, tm=1024):
    t = h.shape[0]
    return pl.pallas_call(
        _ffn_kernel,
        grid=(t // tm,),
        in_specs=[pl.BlockSpec((tm, D_MODEL), lambda i: (i, 0)),
                  _const_spec((1, D_MODEL)),
                  _const_spec((D_MODEL, 2 * D_FF)),
                  _const_spec((D_FF, D_MODEL))],
        out_specs=pl.BlockSpec((tm, D_MODEL), lambda i: (i, 0)),
        out_shape=jax.ShapeDtypeStruct((t, D_MODEL), F32),
        compiler_params=_params("parallel", vmem=BIG_VMEM_LIMIT),
        name="ffn",
    )(h, g, w13, w2)


def _split_slabs(ref):
    rows = ref.shape[0] // ROW_SLABS
    return jnp.concatenate([ref[pl.ds(c, rows, stride=ROW_SLABS), :] for c in range(ROW_SLABS)], axis=1)


def _store_slabs(ref, x):
    rows = ref.shape[0] // ROW_SLABS
    for c in range(ROW_SLABS):
        ref[pl.ds(c, rows, stride=ROW_SLABS), :] = x[:, c * LANES:(c + 1) * LANES]


def _slab_spec(tm, index_map):
    return pl.BlockSpec((tm * ROW_SLABS, LANES), index_map)


def _router_kernel(h_ref, g_ref, whi_ref, wlo_ref, br_ref, hn_ref, idx_ref, wts_ref):
    hn = _rms(h_ref[...], g_ref[...])
    _store_slabs(hn_ref, hn)
    hi = hn.astype(BF)
    lo = (hn - hi.astype(F32)).astype(BF)
    logits = _dot(hi, whi_ref[...]) + (_dot(lo, whi_ref[...]) + _dot(hi, wlo_ref[...])) + br_ref[...]
    lane = lax.broadcasted_iota(jnp.int32, logits.shape, 1)
    logits = jnp.where(lane < N_EXPERTS, logits, NEG)
    v1 = jnp.max(logits, axis=-1, keepdims=True)
    i1 = jnp.min(jnp.where(logits == v1, lane, LANES), axis=-1, keepdims=True)
    rest = jnp.where(lane == i1, NEG, logits)
    v2 = jnp.max(rest, axis=-1, keepdims=True)
    i2 = jnp.min(jnp.where(rest == v2, lane, LANES), axis=-1, keepdims=True)
    e2 = jnp.exp(v2 - v1)
    w1 = 1.0 / (1.0 + e2)
    w2 = e2 / (1.0 + e2)
    idx_ref[...] = jnp.where(lane == 0, i1, jnp.where(lane == 1, i2, 0))
    wts_ref[...] = jnp.where(lane == 0, w1, jnp.where(lane == 1, w2, 0.0))


def _router(h, g, whi, wlo, br, chunk, n_chunks, tm=1024):
    t = h.shape[0] // n_chunks
    first = chunk * (t // tm)
    return pl.pallas_call(
        _router_kernel,
        grid=(t // tm,),
        in_specs=[pl.BlockSpec((tm, D_MODEL), lambda i: (first + i, 0)),
                  _const_spec((1, D_MODEL)),
                  _const_spec((D_MODEL, LANES)),
                  _const_spec((D_MODEL, LANES)),
                  _const_spec((1, LANES))],
        out_specs=[_slab_spec(tm, lambda i: (i, 0)),
                   pl.BlockSpec((tm, LANES), lambda i: (i, 0)),
                   pl.BlockSpec((tm, LANES), lambda i: (i, 0))],
        out_shape=[jax.ShapeDtypeStruct((t * ROW_SLABS, LANES), F32),
                   jax.ShapeDtypeStruct((t, LANES), jnp.int32),
                   jax.ShapeDtypeStruct((t, LANES), F32)],
        compiler_params=_params("parallel"),
        name="router",
    )(h, g, whi, wlo, br)


def _route_plan(idx, tm):
    t = idx.shape[0]
    n_pairs = 2 * t
    n_rows = n_pairs + N_EXPERTS * tm
    e_flat = jnp.concatenate([idx[:, 0], idx[:, 1]])
    onehot = (e_flat[:, None] == jnp.arange(N_EXPERTS, dtype=jnp.int32)[None, :]).astype(jnp.int32)
    csum = jnp.cumsum(onehot, axis=0)
    rank = jnp.sum(onehot * csum, axis=1) - 1
    counts = csum[-1]
    padded = ((counts + tm - 1) // tm) * tm
    ends = jnp.cumsum(padded)
    starts = ends - padded
    pos = starts[e_flat] + rank
    order = jnp.argsort(e_flat, stable=True).astype(jnp.int32)
    first = jnp.cumsum(counts) - counts
    r = jnp.minimum(jnp.arange(n_rows, dtype=jnp.int32), ends[-1] - 1)
    e_r = jnp.minimum(jnp.searchsorted(ends, r, side="right").astype(jnp.int32), N_EXPERTS - 1)
    local = r - starts[e_r]
    src_pair = order[jnp.clip(first[e_r] + local, 0, n_pairs - 1)]
    src_tok = jnp.where(local < counts[e_r], src_pair % t, 0).astype(jnp.int32)
    tile_expert = e_r[::tm]
    n_valid = (ends[-1] // tm).astype(jnp.int32).reshape(1)
    return pos.astype(jnp.int32), src_tok, tile_expert, n_valid


def _gather_rows(table, idx):
    n = idx.shape[0]
    info = plsc.get_sparse_core_info()
    n_workers = info.num_cores * info.num_subcores
    per_worker = n // n_workers
    assert per_worker * n_workers == n and per_worker % SC_CHUNK == 0, (n, n_workers)
    mesh = plsc.VectorSubcoreMesh(core_axis_name="c", subcore_axis_name="s")

    @functools.partial(
        pl.kernel, mesh=mesh,
        out_type=jax.ShapeDtypeStruct((n,) + table.shape[1:], table.dtype),
        scratch_types=[pltpu.VMEM((SC_CHUNK,), jnp.int32),
                       pltpu.VMEM((SC_CHUNK,) + table.shape[1:], table.dtype),
                       pltpu.SemaphoreType.DMA],
    )
    def gather(table_hbm, idx_hbm, out_hbm, idx_v, rows_v, sem):
        worker = lax.axis_index("s") * info.num_cores + lax.axis_index("c")
        base = worker * per_worker

        @pl.loop(0, per_worker // SC_CHUNK)
        def _(i):
            off = pl.multiple_of(base + i * SC_CHUNK, SC_CHUNK)
            pltpu.sync_copy(idx_hbm.at[pl.ds(off, SC_CHUNK)], idx_v)
            pltpu.async_copy(table_hbm.at[idx_v], rows_v, sem).wait()
            pltpu.sync_copy(rows_v, out_hbm.at[pl.ds(off, SC_CHUNK)])

    return gather(table, idx)


def _expert_ffn_kernel(te_ref, nv_ref, x_ref, w13_ref, w2_ref, y_ref):
    valid = pl.program_id(0) < nv_ref[0]

    @pl.when(valid)
    def _():
        _store_slabs(y_ref, _swiglu(_split_slabs(x_ref).astype(BF), w13_ref.at[0], w2_ref.at[0]))

    @pl.when(jnp.logical_not(valid))
    def _():
        y_ref[...] = jnp.zeros_like(y_ref)


def _expert_ffn(xs, tile_expert, n_valid, w13, w2, tm):
    n_rows = xs.shape[0] // ROW_SLABS
    last = lambda nv: jnp.maximum(nv[0] - 1, 0)
    grid_spec = pltpu.PrefetchScalarGridSpec(
        num_scalar_prefetch=2,
        grid=(n_rows // tm,),
        in_specs=[_slab_spec(tm, lambda i, te, nv: (jnp.minimum(i, last(nv)), 0)),
                  pl.BlockSpec((1, D_MODEL, 2 * D_FF), lambda i, te, nv: (te[i], 0, 0)),
                  pl.BlockSpec((1, D_FF, D_MODEL), lambda i, te, nv: (te[i], 0, 0))],
        out_specs=_slab_spec(tm, lambda i, te, nv: (i, 0)),
    )
    return pl.pallas_call(
        _expert_ffn_kernel,
        grid_spec=grid_spec,
        out_shape=jax.ShapeDtypeStruct((n_rows * ROW_SLABS, LANES), F32),
        compiler_params=_params("arbitrary", vmem=BIG_VMEM_LIMIT),
        name="expert_ffn",
    )(tile_expert, n_valid, xs, w13, w2)


def _combine_kernel(h_ref, y0_ref, y1_ref, wts_ref, gf_ref, *rest):
    out_ref = rest[-1]
    wts = wts_ref[...]
    tot = h_ref[...] + wts[:, 0:1] * _split_slabs(y0_ref) + wts[:, 1:2] * _split_slabs(y1_ref)
    out_ref[...] = _rms(tot, gf_ref[...])


def _combine(h, yg, wts, g_final, chunk, n_chunks, out_so_far, tm=1024):
    t = h.shape[0]
    nt = t // n_chunks // tm
    first = chunk * nt
    in_specs = [pl.BlockSpec((tm, D_MODEL), lambda i: (first + i, 0)),
                _slab_spec(tm, lambda i: (i, 0)),
                _slab_spec(tm, lambda i: (nt + i, 0)),
                pl.BlockSpec((tm, LANES), lambda i: (i, 0)),
                _const_spec((1, D_MODEL))]
    args = [h, yg, yg, wts, g_final]
    aliases = {}
    if out_so_far is not None:
        in_specs.append(pl.BlockSpec(memory_space=pl.ANY))
        args.append(out_so_far)
        aliases = {len(args) - 1: 0}
    return pl.pallas_call(
        _combine_kernel,
        grid=(nt,),
        in_specs=in_specs,
        out_specs=pl.BlockSpec((tm, D_MODEL), lambda i: (first + i, 0)),
        out_shape=jax.ShapeDtypeStruct((t, D_MODEL), F32),
        input_output_aliases=aliases,
        compiler_params=_params("parallel"),
        name="combine",
    )(*args)


def _moe(h, g, router_w, router_b, w13, w2, g_final, tm=512):
    wr = jnp.pad(router_w, ((0, 0), (0, LANES - N_EXPERTS)))
    whi = wr.astype(BF)
    wlo = (wr - whi.astype(F32)).astype(BF)
    br = jnp.pad(router_b, (0, LANES - N_EXPERTS)).reshape(1, LANES)
    as_rows = lambda a: a.reshape(-1, ROW_SLABS, LANES)
    as_slabs = lambda a: a.reshape(-1, LANES)
    chunks = range(MOE_CHUNKS)
    routed = [_router(h, g, whi, wlo, br, c, MOE_CHUNKS) for c in chunks]
    plans = [_route_plan(idx, tm) for _, idx, _ in routed]
    xs = [as_slabs(_gather_rows(as_rows(routed[c][0]), plans[c][1])) for c in chunks]
    ys = [_expert_ffn(xs[c], plans[c][2], plans[c][3], w13, w2, tm) for c in chunks]
    yg = [as_slabs(_gather_rows(as_rows(ys[c]), plans[c][0])) for c in chunks]
    out = None
    for c in chunks:
        out = _combine(h, yg[c], routed[c][2], g_final, c, MOE_CHUNKS, out)
    return out


def _feature_major(w, heads):
    wt = w.T.reshape(heads, HEAD_DIM, D_MODEL)
    return jnp.pad(wt, ((0, 0), (0, VT_ROWS - HEAD_DIM), (0, 0))).reshape(heads * VT_ROWS, D_MODEL)


def _pack_w_in(w_in):
    cuts = np.cumsum((512, 512, 512, 512, 512, 128, 128, 512, 512, 512, 8))[:-1].tolist()
    a_u, a_v, b_x, b_y, c_q, c_k, c_v, d_q, d_k, d_v, d_f = jnp.split(w_in, cuts, axis=-1)
    w = jnp.concatenate([b_x, b_y, d_q, d_k, c_k, c_v], axis=-1).astype(BF)
    wa = jnp.concatenate([a_u, a_v], axis=-1).astype(BF)
    wf = jnp.pad(d_f, ((0, 0), (0, LANES - FOX_HEADS))).astype(BF)
    wt = jnp.concatenate([_feature_major(d_v, FOX_HEADS), c_q.T, _feature_major(c_v, SWA_KV)], axis=0).astype(BF)
    return w, wa, wf, wt


def _fm_ones():
    ones = np.zeros((FM_ROWS, LANES), np.float32)
    for base, heads in ((0, FOX_HEADS), (VT_ALL + BRANCH_W, SWA_KV)):
        for h in range(heads):
            ones[base + h * VT_ROWS + HEAD_DIM, :] = 1.0
    return jnp.asarray(ones)


def _row(v):
    return v.reshape(1, -1)


def _hybrid_mixer(h, bsz, norm_mix, w_in, sgu_g, sgu_w, sgu_b, conv_w, conv_b, rg_wa, rg_ba, rg_wx, rg_bx,
                  rg_lambda, swa_sinks, fox_bf, w_branch, w_gate, b_gate, w_out):
    w, wa, wf, wt = _pack_w_in(w_in)
    proj, f, o_a, vt, sqt, svt = _inproj(h, _row(norm_mix), w, wa, wf, wt, _fm_ones(), _row(sgu_g), sgu_w, sgu_b.T)
    wax = jnp.concatenate([rg_wa, rg_wx], axis=-1).astype(BF)
    o_b = _rglru(proj, bsz, conv_w, _row(conv_b), wax, _row(rg_ba), _row(rg_bx), _row(rg_lambda))
    o_c = _swa(proj, sqt, svt, bsz, swa_sinks)
    bf_pad = jnp.pad(fox_bf, (0, LANES - FOX_HEADS)).reshape(1, LANES)
    qp, kp = _fox_prep(proj, f, bsz, bf_pad)
    o_d = _fox(qp, kp, vt, bsz)
    return _merge(h, _row(norm_mix), o_a, o_b, o_c, o_d, w_gate.astype(BF), b_gate[:, None, :],
                  w_branch.astype(BF), w_out.astype(BF))


def kernel(x, mem, norm_mix, w_in, sgu_g, sgu_w, sgu_b, conv_w, conv_b, rg_wa, rg_ba, rg_wx, rg_bx, rg_lambda, swa_sinks, fox_bf, w_branch, w_gate, b_gate, w_out, norm_cross, norm_mem, wq_c, wkv_c, wo_c, norm_ffn, dense_w13, dense_w2, router_w, router_b, moe_w13, moe_w2, norm_final):
    bsz, seq, d = x.shape
    m_len = mem.shape[1]
    depth = norm_mix.shape[0]
    assert depth == 2, "the final RMSNorm is fused into the routed layer, which must be the last one"
    h = x.reshape(bsz * seq, d)
    mem2 = mem.reshape(bsz * m_len, d)
    for l in range(depth):
        h = _hybrid_mixer(h, bsz, norm_mix[l], w_in[l], sgu_g[l], sgu_w[l], sgu_b[l], conv_w[l], conv_b[l],
                          rg_wa[l], rg_ba[l], rg_wx[l], rg_bx[l], rg_lambda[l], swa_sinks[l], fox_bf[l],
                          w_branch[l], w_gate[l], b_gate[l], w_out[l])
        kv = _memkv(mem2, _row(norm_mem[l]), wkv_c[l].astype(BF), m_len)
        h = _cross(h, _row(norm_cross[l]), wq_c[l].astype(BF), kv, wo_c[l].astype(BF), bsz, m_len)
        if l % 2 == 0:
            h = _ffn(h, _row(norm_ffn[l]), dense_w13[l // 2].astype(BF), dense_w2[l // 2].astype(BF))
        else:
            h = _moe(h, _row(norm_ffn[l]), router_w[l // 2], router_b[l // 2], moe_w13[l // 2].astype(BF),
                     moe_w2[l // 2].astype(BF), _row(norm_final))
    return h.reshape(bsz, seq, d)
```

```python
import functools
import math

import numpy as np
import jax
import jax.numpy as jnp
from jax import lax
from jax.experimental import pallas as pl
from jax.experimental.pallas import tpu as pltpu
from jax.experimental.pallas import tpu_sc as plsc

F32 = jnp.float32
BF = jnp.bfloat16

D_MODEL = 1024
BRANCH_W = 512
HEAD_DIM = 64
CHUNK = 128
SGU_GROUPS = 4
LRU_HEADS = 4
LRU_C = 8.0
CONV_W = 4
SWA_HEADS = 8
SWA_KV = 2
FOX_HEADS = 8
FOX_PAIRS = FOX_HEADS // 2
X_HEADS = 4
X_HEAD_DIM = 128
D_FF = 2816
N_EXPERTS = 8
EPS = 1e-6
LOG2E = math.log2(math.e)
NEG = -1e30
LANES = 128
ROW_SLABS = D_MODEL // LANES
SC_CHUNK = 64
MOE_CHUNKS = 2
VMEM_LIMIT = 56 * 1024 * 1024
BIG_VMEM_LIMIT = 62 * 1024 * 1024

COL_BX, COL_BY, COL_DQ, COL_DK = range(4)
N_WIDE = 4
N_PROJ = N_WIDE * BRANCH_W + 2 * SWA_KV * HEAD_DIM
VT_ROWS = 80
VT_ALL = FOX_HEADS * VT_ROWS
SWA_VT = SWA_KV * VT_ROWS
FM_ROWS = VT_ALL + BRANCH_W + SWA_VT
FOX_UNROLL = 4
FOX_STEP_PAIRS = 1
FF_SPLITS = ((0, 1536), (1536, D_FF))


def _rms(x, g):
    return x * lax.rsqrt(jnp.mean(x * x, axis=-1, keepdims=True) + EPS) * g


def _sigmoid(x):
    return 1.0 / (1.0 + jnp.exp(-x))


def _gelu(x):
    return 0.5 * x * (1.0 + jnp.tanh(math.sqrt(2.0 / math.pi) * (x + 0.044715 * (x * x * x))))


def _dot(a, b):
    return jnp.dot(a, b, preferred_element_type=F32)


def _dot_nt(a, b):
    return lax.dot_general(a, b, (((1,), (1,)), ((), ())), preferred_element_type=F32)


def _shift_rows(x, d, fill):
    row = lax.broadcasted_iota(jnp.int32, x.shape, 0)
    return jnp.where(row >= d, pltpu.roll(x, d, 0), fill)


def _params(*sem, vmem=VMEM_LIMIT):
    return pltpu.CompilerParams(dimension_semantics=sem, vmem_limit_bytes=vmem)


def _const_spec(shape):
    nd = len(shape)
    return pl.BlockSpec(shape, lambda *_: (0,) * nd, pipeline_mode=pl.Buffered(1))


def _inproj_kernel(h_ref, g_ref, w_ref, wa_ref, wf_ref, wt_ref, ones_ref, sg_ref, sw_ref, sbt_ref,
                   proj_ref, f_ref, oa_ref, vt_ref, sqt_ref, svt_ref):
    xn = _rms(h_ref[...], g_ref[...]).astype(BF)
    tm = xn.shape[0]
    n_col = w_ref.shape[1]
    for c in range(0, n_col, BRANCH_W):
        w = min(BRANCH_W, n_col - c)
        proj_ref[:, c:c + w] = _dot(xn, w_ref[:, c:c + w]).astype(BF)
    f_ref[...] = _dot(xn, wf_ref[...])
    ones = jnp.concatenate([ones_ref[...]] * (tm // LANES), axis=1)
    fm = (_dot_nt(wt_ref[...], xn) + ones).astype(BF)
    vt_ref[...] = fm[:VT_ALL]
    sqt_ref[...] = fm[VT_ALL:VT_ALL + BRANCH_W]
    svt_ref[...] = fm[VT_ALL + BRANCH_W:]

    u = _gelu(_dot(xn, wa_ref[:, :BRANCH_W]))
    v = _gelu(_dot(xn, wa_ref[:, BRANCH_W:]))
    vn = _rms(v, sg_ref[...]).astype(BF)
    row = lax.broadcasted_iota(jnp.int32, (CHUNK, CHUNK), 0)
    col = lax.broadcasted_iota(jnp.int32, (CHUNK, CHUNK), 1)
    gw = BRANCH_W // SGU_GROUPS
    for g in range(SGU_GROUPS):
        wg = jnp.where(col <= row, sw_ref[g], 0.0).astype(BF)
        bg = sbt_ref[:, g:g + 1]
        for c in range(tm // CHUNK):
            rs = slice(c * CHUNK, (c + 1) * CHUNK)
            cs = slice(g * gw, (g + 1) * gw)
            mixed = _dot(wg, vn[rs, cs]) + bg
            oa_ref[rs, cs] = (u[rs, cs] * mixed).astype(BF)


def _inproj(h, g, w, wa, wf, wt, ones, sgu_g, sgu_w, sgu_bt, tm=1024):
    t = h.shape[0]
    row = lambda i: (i, 0)
    col = lambda i: (0, i)
    return pl.pallas_call(
        _inproj_kernel,
        grid=(t // tm,),
        in_specs=[pl.BlockSpec((tm, D_MODEL), row),
                  _const_spec((1, D_MODEL)),
                  _const_spec((D_MODEL, N_PROJ)),
                  _const_spec((D_MODEL, 2 * BRANCH_W)),
                  _const_spec((D_MODEL, LANES)),
                  _const_spec((FM_ROWS, D_MODEL)),
                  _const_spec((FM_ROWS, LANES)),
                  _const_spec((1, BRANCH_W)),
                  _const_spec((SGU_GROUPS, CHUNK, CHUNK)),
                  _const_spec((CHUNK, SGU_GROUPS))],
        out_specs=[pl.BlockSpec((tm, N_PROJ), row),
                   pl.BlockSpec((tm, LANES), row),
                   pl.BlockSpec((tm, BRANCH_W), row),
                   pl.BlockSpec((VT_ALL, tm), col),
                   pl.BlockSpec((BRANCH_W, tm), col),
                   pl.BlockSpec((SWA_VT, tm), col)],
        out_shape=[jax.ShapeDtypeStruct((t, N_PROJ), BF), jax.ShapeDtypeStruct((t, LANES), F32),
                   jax.ShapeDtypeStruct((t, BRANCH_W), BF), jax.ShapeDtypeStruct((VT_ALL, t), BF),
                   jax.ShapeDtypeStruct((BRANCH_W, t), BF), jax.ShapeDtypeStruct((SWA_VT, t), BF)],
        compiler_params=_params("parallel", vmem=BIG_VMEM_LIMIT),
        name="inproj",
    )(h, g, w, wa, wf, wt, ones, sgu_g, sgu_w, sgu_bt)


def _rglru_kernel(x_ref, y_ref, cw_ref, cb_ref, wax_ref, ba_ref, bx_ref, lam_ref, o_ref, xs_ref, hc_ref):
    ts = x_ref.shape[0]
    hw = BRANCH_W // LRU_HEADS

    @pl.when(pl.program_id(1) == 0)
    def _():
        xs_ref[0:8, :] = jnp.zeros((8, BRANCH_W), F32)
        hc_ref[...] = jnp.zeros_like(hc_ref)

    x = x_ref[...].astype(F32)
    xs_ref[8:8 + ts, :] = x
    cw = cw_ref[...]
    xc = cb_ref[...] + cw[CONV_W - 1:CONV_W] * x
    for k in range(1, CONV_W):
        xc = xc + cw[CONV_W - 1 - k:CONV_W - k] * xs_ref[8 - k:8 - k + ts, :]
    xs_ref[0:8, :] = x[ts - 8:ts, :]

    xcb = xc.astype(BF)
    r_parts, i_parts = [], []
    for hd in range(LRU_HEADS):
        z = _dot(xcb[:, hd * hw:(hd + 1) * hw], wax_ref[hd])
        r_parts.append(z[:, :hw])
        i_parts.append(z[:, hw:])
    r = _sigmoid(jnp.concatenate(r_parts, axis=1) + ba_ref[...])
    gi = _sigmoid(jnp.concatenate(i_parts, axis=1) + bx_ref[...])
    nl = -lam_ref[...]
    softplus = jnp.maximum(nl, 0.0) + jnp.log1p(jnp.exp(-jnp.abs(nl)))
    log_a = (-LRU_C * r) * softplus
    a = jnp.exp(log_a)
    th = jnp.tanh(log_a)
    e2 = -2.0 * th / (1.0 - th)
    b = (xc * gi) * (e2 * lax.rsqrt(jnp.maximum(e2, 1e-30)))

    sub = lax.broadcasted_iota(jnp.int32, a.shape, 0) & 7
    for d in (1, 2, 4):
        keep = sub >= d
        a_sh = jnp.where(keep, pltpu.roll(a, d, 0), 1.0)
        b_sh = jnp.where(keep, pltpu.roll(b, d, 0), 0.0)
        b = b + a * b_sh
        a = a * a_sh
    carry = hc_ref[...]
    groups = []
    for g in range(ts // 8):
        hg = a[8 * g:8 * g + 8] * carry + b[8 * g:8 * g + 8]
        groups.append(hg)
        carry = hg[7:8]
    h = jnp.concatenate(groups, axis=0)
    hc_ref[...] = carry
    o_ref[...] = (h * _gelu(y_ref[...].astype(F32))).astype(BF)


def _rglru(proj, bsz, cw, cb, wax, ba, bx, lam, ts=512):
    t = proj.shape[0]
    ns = t // bsz // ts
    return pl.pallas_call(
        _rglru_kernel,
        grid=(bsz, ns),
        in_specs=[pl.BlockSpec((ts, BRANCH_W), lambda b, j: (b * ns + j, COL_BX)),
                  pl.BlockSpec((ts, BRANCH_W), lambda b, j: (b * ns + j, COL_BY)),
                  _const_spec((CONV_W, BRANCH_W)),
                  _const_spec((1, BRANCH_W)),
                  _const_spec((LRU_HEADS, BRANCH_W // LRU_HEADS, 2 * BRANCH_W // LRU_HEADS)),
                  _const_spec((1, BRANCH_W)),
                  _const_spec((1, BRANCH_W)),
                  _const_spec((1, BRANCH_W))],
        out_specs=pl.BlockSpec((ts, BRANCH_W), lambda b, j: (b * ns + j, 0)),
        out_shape=jax.ShapeDtypeStruct((t, BRANCH_W), BF),
        scratch_shapes=[pltpu.VMEM((ts + 8, BRANCH_W), F32), pltpu.VMEM((1, BRANCH_W), F32)],
        compiler_params=_params("parallel", "arbitrary"),
        name="rglru",
    )(proj, proj, cw, cb, wax, ba, bx, lam)


def _swa_tables(sinks):
    t_idx = np.arange(CHUNK)[None, :] + CHUNK
    dist = (t_idx - np.arange(2 * CHUNK)[:, None]).astype(np.float32)
    in_win = (dist >= 0) & (dist < CHUNK)
    slopes = 2.0 ** (-(8.0 / SWA_HEADS) * np.arange(1, SWA_HEADS + 1, dtype=np.float32))
    bias = np.where(in_win[None], -slopes[:, None, None] * dist[None] * LOG2E, NEG)
    grp = SWA_HEADS // SWA_KV
    bias = bias.reshape(SWA_KV, grp, 2 * CHUNK, CHUNK).transpose(0, 2, 1, 3).reshape(SWA_KV, 2 * CHUNK, grp * CHUNK)
    sink = jnp.repeat(sinks.astype(F32) * LOG2E, CHUNK).reshape(SWA_KV, 1, grp * CHUNK)
    return jnp.asarray(bias, F32), sink


def _swa_kernel(qt_ref, kv_ref, kvp_ref, vt_ref, vtp_ref, bias_ref, sink_ref, o_ref):
    tq = kv_ref.shape[0]
    grp = SWA_HEADS // SWA_KV
    first_key = jnp.where(pl.program_id(1) == 0, CHUNK, 0)
    k_all = jnp.concatenate([kvp_ref[...], kv_ref[...]], axis=0)
    vt_all = jnp.concatenate([vtp_ref[...], vt_ref[...]], axis=1)
    qt = (qt_ref[...].astype(F32) * (HEAD_DIM ** -0.5 * LOG2E)).astype(BF)
    key_row = lax.broadcasted_iota(jnp.int32, (2 * CHUNK, grp * CHUNK), 0)
    kw = 2 * SWA_KV * HEAD_DIM
    for qb in range(tq // CHUNK):
        cols = slice(qb * CHUNK, (qb + 1) * CHUNK)
        kb = k_all[qb * CHUNK:(qb + 2) * CHUNK]
        vtb = vt_all[:, qb * CHUNK:(qb + 2) * CHUNK]
        outs = []
        for kh in range(SWA_KV):
            qg = jnp.concatenate([qt[(kh * grp + g) * HEAD_DIM:(kh * grp + g + 1) * HEAD_DIM, cols]
                                  for g in range(grp)], axis=1)
            parts = []
            if kh > 0:
                parts.append(jnp.zeros((kh * HEAD_DIM, grp * CHUNK), BF))
            parts.append(qg)
            parts.append(jnp.zeros((kw - (kh + 1) * HEAD_DIM, grp * CHUNK), BF))
            st = _dot(kb, jnp.concatenate(parts, axis=0)) + bias_ref[kh]
            if qb == 0:
                st = jnp.where(key_row >= first_key, st, NEG)
            sink = sink_ref[kh]
            m = jnp.maximum(jnp.max(st, axis=0, keepdims=True), sink)
            pt = jnp.exp2(st - m).astype(BF)
            acc = _dot(vtb[kh * VT_ROWS:(kh + 1) * VT_ROWS], pt)
            ot = acc[:HEAD_DIM] / (acc[HEAD_DIM:HEAD_DIM + 1] + jnp.exp2(sink - m))
            outs.extend(ot[:, g * CHUNK:(g + 1) * CHUNK] for g in range(grp))
        o_ref[cols, :] = jnp.concatenate(outs, axis=0).T.astype(BF)


def _swa(proj, sqt, svt, bsz, sinks, tq=1024):
    t = proj.shape[0]
    seq = t // bsz
    nq = seq // tq
    per = tq // CHUNK
    kv_blk = N_WIDE * BRANCH_W // (2 * SWA_KV * HEAD_DIM)
    prev = lambda b, i: b * (seq // CHUNK) + jnp.maximum(i * per - 1, 0)
    bias, sink = _swa_tables(sinks)
    grp = SWA_HEADS // SWA_KV
    return pl.pallas_call(
        _swa_kernel,
        grid=(bsz, nq),
        in_specs=[pl.BlockSpec((BRANCH_W, tq), lambda b, i: (0, b * nq + i)),
                  pl.BlockSpec((tq, 2 * SWA_KV * HEAD_DIM), lambda b, i: (b * nq + i, kv_blk)),
                  pl.BlockSpec((CHUNK, 2 * SWA_KV * HEAD_DIM), lambda b, i: (prev(b, i), kv_blk)),
                  pl.BlockSpec((SWA_VT, tq), lambda b, i: (0, b * nq + i)),
                  pl.BlockSpec((SWA_VT, CHUNK), lambda b, i: (0, prev(b, i))),
                  _const_spec((SWA_KV, 2 * CHUNK, grp * CHUNK)),
                  _const_spec((SWA_KV, 1, grp * CHUNK))],
        out_specs=pl.BlockSpec((tq, BRANCH_W), lambda b, i: (b * nq + i, 0)),
        out_shape=jax.ShapeDtypeStruct((t, BRANCH_W), BF),
        compiler_params=_params("parallel", "parallel"),
        name="swa",
    )(sqt, proj, proj, svt, svt, bias, sink)


def _aug_tables():
    eq = np.zeros((3, LANES, FOX_PAIRS * LANES), np.float32)
    ek = np.zeros((3, LANES, FOX_PAIRS * LANES), np.float32)
    oq = np.zeros((1, FOX_PAIRS * LANES), np.float32)
    ok = np.zeros((1, FOX_PAIRS * LANES), np.float32)
    for h in range(FOX_HEADS):
        base = (h // 2) * LANES + 6 * (h % 2)
        for s in range(3):
            eq[s, h, base + s] = 1.0
            ek[s, h, base + 3 + s] = -1.0
            ok[0, base + s] = 1.0
            oq[0, base + 3 + s] = 1.0
    return eq, ek, oq, ok


def _fox_prep_kernel(q_ref, k_ref, f_ref, bf_ref, eq_ref, ek_ref, oq_ref, ok_ref, qp_ref, kp_ref, cum_ref):
    ts = q_ref.shape[0]

    @pl.when(pl.program_id(1) == 0)
    def _():
        cum_ref[...] = jnp.zeros_like(cum_ref)

    z = f_ref[...] + bf_ref[...]
    c = jnp.minimum(z, 0.0) - jnp.log1p(jnp.exp(-jnp.abs(z)))
    d = 1
    while d < ts:
        c = c + _shift_rows(c, d, 0.0)
        d *= 2
    c = c + cum_ref[...]
    cum_ref[...] = c[ts - 1:ts, :]
    c = c * LOG2E
    c1 = c.astype(BF)
    r1 = c - c1.astype(F32)
    c2 = r1.astype(BF)
    c3 = (r1 - c2.astype(F32)).astype(BF)
    augq = _dot(c1, eq_ref[0]) + _dot(c2, eq_ref[1]) + _dot(c3, eq_ref[2]) + oq_ref[...]
    augk = _dot(c1, ek_ref[0]) + _dot(c2, ek_ref[1]) + _dot(c3, ek_ref[2]) + ok_ref[...]
    qt = (q_ref[...].astype(F32) * (HEAD_DIM ** -0.5 * LOG2E)).T.astype(BF)
    augqt = augq.T.astype(BF)
    k = k_ref[...]
    for p in range(FOX_PAIRS):
        qp_ref[2 * p * LANES:(2 * p + 1) * LANES, :] = qt[p * LANES:(p + 1) * LANES]
        qp_ref[(2 * p + 1) * LANES:(2 * p + 2) * LANES, :] = augqt[p * LANES:(p + 1) * LANES]
        kp_ref[:, 2 * p * LANES:(2 * p + 1) * LANES] = k[:, p * LANES:(p + 1) * LANES]
        kp_ref[:, (2 * p + 1) * LANES:(2 * p + 2) * LANES] = augk[:, p * LANES:(p + 1) * LANES].astype(BF)


def _fox_prep(proj, f, bsz, bf_pad, ts=1024):
    t = proj.shape[0]
    ns = t // bsz // ts
    eq, ek, oq, ok = _aug_tables()
    wide = 2 * FOX_PAIRS * LANES
    return pl.pallas_call(
        _fox_prep_kernel,
        grid=(bsz, ns),
        in_specs=[pl.BlockSpec((ts, BRANCH_W), lambda b, j: (b * ns + j, COL_DQ)),
                  pl.BlockSpec((ts, BRANCH_W), lambda b, j: (b * ns + j, COL_DK)),
                  pl.BlockSpec((ts, LANES), lambda b, j: (b * ns + j, 0)),
                  _const_spec((1, LANES)),
                  _const_spec(eq.shape), _const_spec(ek.shape), _const_spec(oq.shape), _const_spec(ok.shape)],
        out_specs=[pl.BlockSpec((wide, ts), lambda b, j: (0, b * ns + j)),
                   pl.BlockSpec((ts, wide), lambda b, j: (b * ns + j, 0))],
        out_shape=[jax.ShapeDtypeStruct((wide, t), BF), jax.ShapeDtypeStruct((t, wide), BF)],
        scratch_shapes=[pltpu.VMEM((1, LANES), F32)],
        compiler_params=_params("parallel", "arbitrary"),
        name="fox_prep",
    )(proj, proj, f, bf_pad, jnp.asarray(eq, BF), jnp.asarray(ek, BF), jnp.asarray(oq), jnp.asarray(ok))


def _fox_kernel(q_ref, k_ref, vt_ref, o_ref, st_ref, pt_ref, al_ref, m_ref, acc_ref):
    tq = q_ref.shape[1]
    tk = tq // 2
    n_heads = m_ref.shape[0]
    qi = pl.program_id(2)
    row = lax.broadcasted_iota(jnp.int32, (2 * LANES, tq), 0)
    in0 = (row < HEAD_DIM) | ((row >= LANES) & (row < LANES + 6))
    in1 = ((row >= HEAD_DIM) & (row < LANES)) | ((row >= LANES + 6) & (row < LANES + 12))
    qh = []
    for p in range(n_heads // 2):
        q = q_ref[2 * p * LANES:(2 * p + 2) * LANES, :]
        zero = jnp.zeros_like(q)
        qh += [jnp.where(in0, q, zero), jnp.where(in1, q, zero)]
    for h in range(n_heads):
        m_ref[h] = jnp.full(m_ref.shape[1:], NEG, F32)
        acc_ref[h] = jnp.zeros(acc_ref.shape[1:], F32)
        pt_ref[1, h] = jnp.zeros(pt_ref.shape[2:], BF)
        al_ref[1, h] = jnp.ones(al_ref.shape[2:], F32)

    def scores(t, slot, lo=0):
        rows = pl.ds(pl.multiple_of(t * tk, tk), tk)
        for p in range(n_heads // 2):
            kp = k_ref[rows, 2 * p * LANES:(2 * p + 2) * LANES]
            for h in (2 * p, 2 * p + 1):
                st_ref[slot, h, :, lo:] = _dot(kp, qh[h][:, lo:])

    def numerators(slot, key_offset, lo=0):
        for h in range(n_heads):
            st = st_ref[slot, h, :, lo:]
            if key_offset is not None:
                key = lax.broadcasted_iota(jnp.int32, st.shape, 0) + key_offset
                qry = lax.broadcasted_iota(jnp.int32, st.shape, 1) + lo
                st = jnp.where(key <= qry, st, NEG)
            m_old = m_ref[h, :, lo:]
            m_new = jnp.maximum(m_old, jnp.max(st, axis=0, keepdims=True))
            al_ref[slot, h, :, lo:] = jnp.exp2(m_old - m_new)
            pt_ref[slot, h, :, lo:] = jnp.exp2(st - m_new).astype(BF)
            m_ref[h, :, lo:] = m_new

    def accumulate(t, slot, lo=0):
        cols = pl.ds(pl.multiple_of(t * tk, tk), tk)
        for h in range(n_heads):
            vt = vt_ref[h * VT_ROWS:(h + 1) * VT_ROWS, cols]
            acc_ref[h, :, lo:] = al_ref[slot, h, :, lo:] * acc_ref[h, :, lo:] + _dot(vt, pt_ref[slot, h, :, lo:])

    def tile_pair(j, diagonal):
        lo = tk if diagonal else 0
        accumulate(jnp.maximum(2 * j - 1, 0), 1)
        scores(2 * j + 1, 1, lo)
        numerators(0, 0 if diagonal else None)
        accumulate(2 * j, 0)
        if not diagonal:
            scores(2 * j + 2, 0)
        numerators(1, tk if diagonal else None, lo)

    def body(jj, carry):
        for u in range(FOX_UNROLL):
            tile_pair(FOX_UNROLL * jj + u, False)
        return carry

    scores(0, 0)
    lax.fori_loop(0, qi // FOX_UNROLL, body, 0)
    assert FOX_UNROLL == 4, "the leftover pairs below are handled as one block of two and one single"
    rest = qi % FOX_UNROLL
    first_rest = (qi // FOX_UNROLL) * FOX_UNROLL

    @pl.when(rest >= 2)
    def _():
        tile_pair(first_rest, False)
        tile_pair(first_rest + 1, False)

    @pl.when(rest % 2 == 1)
    def _():
        tile_pair(qi - 1, False)

    tile_pair(qi, True)
    accumulate(2 * qi + 1, 1, tk)
    outs = []
    for h in range(n_heads):
        acc = acc_ref[h]
        outs.append(acc[:HEAD_DIM] / acc[HEAD_DIM:HEAD_DIM + 1])
    o_ref[...] = jnp.concatenate(outs, axis=0).T.astype(BF)


def _fox(qp, kp, vt, bsz, tq=512):
    t = kp.shape[0]
    seq = t // bsz
    nq = seq // tq
    np_ = FOX_STEP_PAIRS
    nh = 2 * np_
    return pl.pallas_call(
        _fox_kernel,
        grid=(bsz, FOX_PAIRS // np_, nq),
        in_specs=[pl.BlockSpec((np_ * 2 * LANES, tq), lambda b, p, i: (p, b * nq + i)),
                  pl.BlockSpec((seq, np_ * 2 * LANES), lambda b, p, i: (b, p)),
                  pl.BlockSpec((nh * VT_ROWS, seq), lambda b, p, i: (p, b))],
        out_specs=pl.BlockSpec((tq, np_ * LANES), lambda b, p, i: (b * nq + i, p)),
        out_shape=jax.ShapeDtypeStruct((t, BRANCH_W), BF),
        scratch_shapes=[pltpu.VMEM((2, nh, tq // 2, tq), F32),
                        pltpu.VMEM((2, nh, tq // 2, tq), BF),
                        pltpu.VMEM((2, nh, 1, tq), F32),
                        pltpu.VMEM((nh, 1, tq), F32),
                        pltpu.VMEM((nh, VT_ROWS, tq), F32)],
        compiler_params=_params("parallel", "parallel", "arbitrary"),
        name="fox",
    )(qp, kp, vt)


def _merge_kernel(h_ref, g_ref, oa_ref, ob_ref, oc_ref, od_ref, wg_ref, bg_ref, wb_ref, wo_ref, out_ref):
    h = h_ref[...]
    xn = _rms(h, g_ref[...]).astype(BF)
    merged = None
    for br, o_ref in enumerate((oa_ref, ob_ref, oc_ref, od_ref)):
        gate = _sigmoid(_dot(xn, wg_ref[br]) + bg_ref[br])
        term = gate * _dot(o_ref[...], wb_ref[br])
        merged = term if merged is None else merged + term
    out_ref[...] = h + _dot(merged.astype(BF), wo_ref[...])


def _merge(h, g, oa, ob, oc, od, wg, bg, wb, wo, tm=1024):
    t = h.shape[0]
    row = lambda i: (i, 0)
    return pl.pallas_call(
        _merge_kernel,
        grid=(t // tm,),
        in_specs=[pl.BlockSpec((tm, D_MODEL), row),
                  _const_spec((1, D_MODEL)),
                  pl.BlockSpec((tm, BRANCH_W), row), pl.BlockSpec((tm, BRANCH_W), row),
                  pl.BlockSpec((tm, BRANCH_W), row), pl.BlockSpec((tm, BRANCH_W), row),
                  _const_spec((4, D_MODEL, D_MODEL)),
                  _const_spec((4, 1, D_MODEL)),
                  _const_spec((4, BRANCH_W, D_MODEL)),
                  _const_spec((D_MODEL, D_MODEL))],
        out_specs=pl.BlockSpec((tm, D_MODEL), row),
        out_shape=jax.ShapeDtypeStruct((t, D_MODEL), F32),
        compiler_params=_params("parallel", vmem=BIG_VMEM_LIMIT),
        name="merge",
    )(h, g, oa, ob, oc, od, wg, bg, wb, wo)


def _memkv_kernel(mem_ref, g_ref, w_ref, kv_ref):
    mn = _rms(mem_ref[...], g_ref[...]).astype(BF)
    kv_ref[...] = _dot(mn, w_ref[...]).astype(BF)


def _memkv(mem2, g, w, m_len):
    n = mem2.shape[0]
    width = 2 * X_HEADS * X_HEAD_DIM
    return pl.pallas_call(
        _memkv_kernel,
        grid=(n // m_len,),
        in_specs=[pl.BlockSpec((m_len, D_MODEL), lambda b: (b, 0)),
                  _const_spec((1, D_MODEL)),
                  _const_spec((D_MODEL, width))],
        out_specs=pl.BlockSpec((m_len, width), lambda b: (b, 0)),
        out_shape=jax.ShapeDtypeStruct((n, width), BF),
        compiler_params=_params("parallel"),
        name="memkv",
    )(mem2, g, w)


def _cross_kernel(h_ref, g_ref, wq_ref, kv_ref, wo_ref, out_ref):
    h = h_ref[...]
    hn = _rms(h, g_ref[...]).astype(BF)
    q = _dot(hn, wq_ref[...]).astype(BF)
    kv = kv_ref[...]
    width = X_HEADS * X_HEAD_DIM
    outs = []
    for hd in range(X_HEADS):
        cs = slice(hd * X_HEAD_DIM, (hd + 1) * X_HEAD_DIM)
        s = _dot_nt(q[:, cs], kv[:, cs]) * (X_HEAD_DIM ** -0.5)
        m = jnp.max(s, axis=-1, keepdims=True)
        p = jnp.exp(s - m)
        denom = jnp.sum(p, axis=-1, keepdims=True)
        v = kv[:, width + hd * X_HEAD_DIM:width + (hd + 1) * X_HEAD_DIM]
        outs.append((_dot(p.astype(BF), v) / denom).astype(BF))
    o = jnp.concatenate(outs, axis=1)
    out_ref[...] = h + _dot(o, wo_ref[...])


def _cross(h, g, wq, kv, wo, bsz, m_len, tm=1024):
    t = h.shape[0]
    per = t // bsz // tm
    width = X_HEADS * X_HEAD_DIM
    return pl.pallas_call(
        _cross_kernel,
        grid=(bsz, per),
        in_specs=[pl.BlockSpec((tm, D_MODEL), lambda b, i: (b * per + i, 0)),
                  _const_spec((1, D_MODEL)),
                  _const_spec((D_MODEL, width)),
                  pl.BlockSpec((m_len, 2 * width), lambda b, i: (b, 0)),
                  _const_spec((width, D_MODEL))],
        out_specs=pl.BlockSpec((tm, D_MODEL), lambda b, i: (b * per + i, 0)),
        out_shape=jax.ShapeDtypeStruct((t, D_MODEL), F32),
        compiler_params=_params("parallel", "parallel"),
        name="cross",
    )(h, g, wq, kv, wo)


def _swiglu(xb, w13_ref, w2_ref):
    out = None
    for lo, hi in FF_SPLITS:
        gate = _dot(xb, w13_ref[:, lo:hi])
        up = _dot(xb, w13_ref[:, D_FF + lo:D_FF + hi])
        act = (gate * _sigmoid(gate) * up).astype(BF)
        part = _dot(act, w2_ref[lo:hi, :])
        out = part if out is None else out + part
    return out


def _ffn_kernel(h_ref, g_ref, w13_ref, w2_ref, out_ref):
    h = h_ref[...]
    out_ref[...] = h + _swiglu(_rms(h, g_ref[...]).astype(BF), w13_ref, w2_ref)


def _ffn(h, g, w13, w2, tm=1024):
    t = h.shape[0]
    return pl.pallas_call(
        _ffn_kernel,
        grid=(t // tm,),
        in_specs=[pl.BlockSpec((tm, D_MODEL), lambda i: (i, 0)),
                  _const_spec((1, D_MODEL)),
                  _const_spec((D_MODEL, 2 * D_FF)),
                  _const_spec((D_FF, D_MODEL))],
        out_specs=pl.BlockSpec((tm, D_MODEL), lambda i: (i, 0)),
        out_shape=jax.ShapeDtypeStruct((t, D_MODEL), F32),
        compiler_params=_params("parallel", vmem=BIG_VMEM_LIMIT),
        name="ffn",
    )(h, g, w13, w2)


def _split_slabs(ref):
    rows = ref.shape[0] // ROW_SLABS
    return jnp.concatenate([ref[pl.ds(c, rows, stride=ROW_SLABS), :] for c in range(ROW_SLABS)], axis=1)


def _store_slabs(ref, x):
    rows = ref.shape[0] // ROW_SLABS
    for c in range(ROW_SLABS):
        ref[pl.ds(c, rows, stride=ROW_SLABS), :] = x[:, c * LANES:(c + 1) * LANES]


def _slab_spec(tm, index_map):
    return pl.BlockSpec((tm * ROW_SLABS, LANES), index_map)


def _router_kernel(h_ref, g_ref, whi_ref, wlo_ref, br_ref, hn_ref, idx_ref, wts_ref):
    hn = _rms(h_ref[...], g_ref[...])
    _store_slabs(hn_ref, hn)
    hi = hn.astype(BF)
    lo = (hn - hi.astype(F32)).astype(BF)
    logits = _dot(hi, whi_ref[...]) + (_dot(lo, whi_ref[...]) + _dot(hi, wlo_ref[...])) + br_ref[...]
    lane = lax.broadcasted_iota(jnp.int32, logits.shape, 1)
    logits = jnp.where(lane < N_EXPERTS, logits, NEG)
    v1 = jnp.max(logits, axis=-1, keepdims=True)
    i1 = jnp.min(jnp.where(logits == v1, lane, LANES), axis=-1, keepdims=True)
    rest = jnp.where(lane == i1, NEG, logits)
    v2 = jnp.max(rest, axis=-1, keepdims=True)
    i2 = jnp.min(jnp.where(rest == v2, lane, LANES), axis=-1, keepdims=True)
    e2 = jnp.exp(v2 - v1)
    w1 = 1.0 / (1.0 + e2)
    w2 = e2 / (1.0 + e2)
    idx_ref[...] = jnp.where(lane == 0, i1, jnp.where(lane == 1, i2, 0))
    wts_ref[...] = jnp.where(lane == 0, w1, jnp.where(lane == 1, w2, 0.0))


def _router(h, g, whi, wlo, br, chunk, n_chunks, tm=1024):
    t = h.shape[0] // n_chunks
    first = chunk * (t // tm)
    return pl.pallas_call(
        _router_kernel,
        grid=(t // tm,),
        in_specs=[pl.BlockSpec((tm, D_MODEL), lambda i: (first + i, 0)),
                  _const_spec((1, D_MODEL)),
                  _const_spec((D_MODEL, LANES)),
                  _const_spec((D_MODEL, LANES)),
                  _const_spec((1, LANES))],
        out_specs=[_slab_spec(tm, lambda i: (i, 0)),
                   pl.BlockSpec((tm, LANES), lambda i: (i, 0)),
                   pl.BlockSpec((tm, LANES), lambda i: (i, 0))],
        out_shape=[jax.ShapeDtypeStruct((t * ROW_SLABS, LANES), F32),
                   jax.ShapeDtypeStruct((t, LANES), jnp.int32),
                   jax.ShapeDtypeStruct((t, LANES), F32)],
        compiler_params=_params("parallel"),
        name="router",
    )(h, g, whi, wlo, br)


def _route_plan(idx, tm):
    t = idx.shape[0]
    n_pairs = 2 * t
    n_rows = n_pairs + N_EXPERTS * tm
    e_flat = jnp.concatenate([idx[:, 0], idx[:, 1]])
    onehot = (e_flat[:, None] == jnp.arange(N_EXPERTS, dtype=jnp.int32)[None, :]).astype(jnp.int32)
    csum = jnp.cumsum(onehot, axis=0)
    rank = jnp.sum(onehot * csum, axis=1) - 1
    counts = csum[-1]
    padded = ((counts + tm - 1) // tm) * tm
    ends = jnp.cumsum(padded)
    starts = ends - padded
    pos = starts[e_flat] + rank
    order = jnp.argsort(e_flat, stable=True).astype(jnp.int32)
    first = jnp.cumsum(counts) - counts
    r = jnp.minimum(jnp.arange(n_rows, dtype=jnp.int32), ends[-1] - 1)
    e_r = jnp.minimum(jnp.searchsorted(ends, r, side="right").astype(jnp.int32), N_EXPERTS - 1)
    local = r - starts[e_r]
    src_pair = order[jnp.clip(first[e_r] + local, 0, n_pairs - 1)]
    src_tok = jnp.where(local < counts[e_r], src_pair % t, 0).astype(jnp.int32)
    tile_expert = e_r[::tm]
    n_valid = (ends[-1] // tm).astype(jnp.int32).reshape(1)
    return pos.astype(jnp.int32), src_tok, tile_expert, n_valid


def _gather_rows(table, idx):
    n = idx.shape[0]
    info = plsc.get_sparse_core_info()
    n_workers = info.num_cores * info.num_subcores
    per_worker = n // n_workers
    assert per_worker * n_workers == n and per_worker % SC_CHUNK == 0, (n, n_workers)
    mesh = plsc.VectorSubcoreMesh(core_axis_name="c", subcore_axis_name="s")

    @functools.partial(
        pl.kernel, mesh=mesh,
        out_type=jax.ShapeDtypeStruct((n,) + table.shape[1:], table.dtype),
        scratch_types=[pltpu.VMEM((SC_CHUNK,), jnp.int32),
                       pltpu.VMEM((SC_CHUNK,) + table.shape[1:], table.dtype),
                       pltpu.SemaphoreType.DMA],
    )
    def gather(table_hbm, idx_hbm, out_hbm, idx_v, rows_v, sem):
        worker = lax.axis_index("s") * info.num_cores + lax.axis_index("c")
        base = worker * per_worker

        @pl.loop(0, per_worker // SC_CHUNK)
        def _(i):
            off = pl.multiple_of(base + i * SC_CHUNK, SC_CHUNK)
            pltpu.sync_copy(idx_hbm.at[pl.ds(off, SC_CHUNK)], idx_v)
            pltpu.async_copy(table_hbm.at[idx_v], rows_v, sem).wait()
            pltpu.sync_copy(rows_v, out_hbm.at[pl.ds(off, SC_CHUNK)])

    return gather(table, idx)


def _expert_ffn_kernel(te_ref, nv_ref, x_ref, w13_ref, w2_ref, y_ref):
    valid = pl.program_id(0) < nv_ref[0]

    @pl.when(valid)
    def _():
        _store_slabs(y_ref, _swiglu(_split_slabs(x_ref).astype(BF), w13_ref.at[0], w2_ref.at[0]))

    @pl.when(jnp.logical_not(valid))
    def _():
        y_ref[...] = jnp.zeros_like(y_ref)


def _expert_ffn(xs, tile_expert, n_valid, w13, w2, tm):
    n_rows = xs.shape[0] // ROW_SLABS
    last = lambda nv: jnp.maximum(nv[0] - 1, 0)
    grid_spec = pltpu.PrefetchScalarGridSpec(
        num_scalar_prefetch=2,
        grid=(n_rows // tm,),
        in_specs=[_slab_spec(tm, lambda i, te, nv: (jnp.minimum(i, last(nv)), 0)),
                  pl.BlockSpec((1, D_MODEL, 2 * D_FF), lambda i, te, nv: (te[i], 0, 0)),
                  pl.BlockSpec((1, D_FF, D_MODEL), lambda i, te, nv: (te[i], 0, 0))],
        out_specs=_slab_spec(tm, lambda i, te, nv: (i, 0)),
    )
    return pl.pallas_call(
        _expert_ffn_kernel,
        grid_spec=grid_spec,
        out_shape=jax.ShapeDtypeStruct((n_rows * ROW_SLABS, LANES), F32),
        compiler_params=_params("arbitrary", vmem=BIG_VMEM_LIMIT),
        name="expert_ffn",
    )(tile_expert, n_valid, xs, w13, w2)


def _combine_kernel(h_ref, y0_ref, y1_ref, wts_ref, gf_ref, *rest):
    out_ref = rest[-1]
    wts = wts_ref[...]
    tot = h_ref[...] + wts[:, 0:1] * _split_slabs(y0_ref) + wts[:, 1:2] * _split_slabs(y1_ref)
    out_ref[...] = _rms(tot, gf_ref[...])


def _combine(h, yg, wts, g_final, chunk, n_chunks, out_so_far, tm=1024):
    t = h.shape[0]
    nt = t // n_chunks // tm
    first = chunk * nt
    in_specs = [pl.BlockSpec((tm, D_MODEL), lambda i: (first + i, 0)),
                _slab_spec(tm, lambda i: (i, 0)),
                _slab_spec(tm, lambda i: (nt + i, 0)),
                pl.BlockSpec((tm, LANES), lambda i: (i, 0)),
                _const_spec((1, D_MODEL))]
    args = [h, yg, yg, wts, g_final]
    aliases = {}
    if out_so_far is not None:
        in_specs.append(pl.BlockSpec(memory_space=pl.ANY))
        args.append(out_so_far)
        aliases = {len(args) - 1: 0}
    return pl.pallas_call(
        _combine_kernel,
        grid=(nt,),
        in_specs=in_specs,
        out_specs=pl.BlockSpec((tm, D_MODEL), lambda i: (first + i, 0)),
        out_shape=jax.ShapeDtypeStruct((t, D_MODEL), F32),
        input_output_aliases=aliases,
        compiler_params=_params("parallel"),
        name="combine",
    )(*args)


def _moe(h, g, router_w, router_b, w13, w2, g_final, tm=512):
    wr = jnp.pad(router_w, ((0, 0), (0, LANES - N_EXPERTS)))
    whi = wr.astype(BF)
    wlo = (wr - whi.astype(F32)).astype(BF)
    br = jnp.pad(router_b, (0, LANES - N_EXPERTS)).reshape(1, LANES)
    as_rows = lambda a: a.reshape(-1, ROW_SLABS, LANES)
    as_slabs = lambda a: a.reshape(-1, LANES)
    chunks = range(MOE_CHUNKS)
    routed = [_router(h, g, whi, wlo, br, c, MOE_CHUNKS) for c in chunks]
    plans = [_route_plan(idx, tm) for _, idx, _ in routed]
    xs = [as_slabs(_gather_rows(as_rows(routed[c][0]), plans[c][1])) for c in chunks]
    ys = [_expert_ffn(xs[c], plans[c][2], plans[c][3], w13, w2, tm) for c in chunks]
    yg = [as_slabs(_gather_rows(as_rows(ys[c]), plans[c][0])) for c in chunks]
    out = None
    for c in chunks:
        out = _combine(h, yg[c], routed[c][2], g_final, c, MOE_CHUNKS, out)
    return out


def _feature_major(w, heads):
    wt = w.T.reshape(heads, HEAD_DIM, D_MODEL)
    return jnp.pad(wt, ((0, 0), (0, VT_ROWS - HEAD_DIM), (0, 0))).reshape(heads * VT_ROWS, D_MODEL)


def _pack_w_in(w_in):
    cuts = np.cumsum((512, 512, 512, 512, 512, 128, 128, 512, 512, 512, 8))[:-1].tolist()
    a_u, a_v, b_x, b_y, c_q, c_k, c_v, d_q, d_k, d_v, d_f = jnp.split(w_in, cuts, axis=-1)
    w = jnp.concatenate([b_x, b_y, d_q, d_k, c_k, c_v], axis=-1).astype(BF)
    wa = jnp.concatenate([a_u, a_v], axis=-1).astype(BF)
    wf = jnp.pad(d_f, ((0, 0), (0, LANES - FOX_HEADS))).astype(BF)
    wt = jnp.concatenate([_feature_major(d_v, FOX_HEADS), c_q.T, _feature_major(c_v, SWA_KV)], axis=0).astype(BF)
    return w, wa, wf, wt


def _fm_ones():
    ones = np.zeros((FM_ROWS, LANES), np.float32)
    for base, heads in ((0, FOX_HEADS), (VT_ALL + BRANCH_W, SWA_KV)):
        for h in range(heads):
            ones[base + h * VT_ROWS + HEAD_DIM, :] = 1.0
    return jnp.asarray(ones)


def _row(v):
    return v.reshape(1, -1)


def _hybrid_mixer(h, bsz, norm_mix, w_in, sgu_g, sgu_w, sgu_b, conv_w, conv_b, rg_wa, rg_ba, rg_wx, rg_bx,
                  rg_lambda, swa_sinks, fox_bf, w_branch, w_gate, b_gate, w_out):
    w, wa, wf, wt = _pack_w_in(w_in)
    proj, f, o_a, vt, sqt, svt = _inproj(h, _row(norm_mix), w, wa, wf, wt, _fm_ones(), _row(sgu_g), sgu_w, sgu_b.T)
    wax = jnp.concatenate([rg_wa, rg_wx], axis=-1).astype(BF)
    o_b = _rglru(proj, bsz, conv_w, _row(conv_b), wax, _row(rg_ba), _row(rg_bx), _row(rg_lambda))
    o_c = _swa(proj, sqt, svt, bsz, swa_sinks)
    bf_pad = jnp.pad(fox_bf, (0, LANES - FOX_HEADS)).reshape(1, LANES)
    qp, kp = _fox_prep(proj, f, bsz, bf_pad)
    o_d = _fox(qp, kp, vt, bsz)
    return _merge(h, _row(norm_mix), o_a, o_b, o_c, o_d, w_gate.astype(BF), b_gate[:, None, :],
                  w_branch.astype(BF), w_out.astype(BF))


def kernel(x, mem, norm_mix, w_in, sgu_g, sgu_w, sgu_b, conv_w, conv_b, rg_wa, rg_ba, rg_wx, rg_bx, rg_lambda, swa_sinks, fox_bf, w_branch, w_gate, b_gate, w_out, norm_cross, norm_mem, wq_c, wkv_c, wo_c, norm_ffn, dense_w13, dense_w2, router_w, router_b, moe_w13, moe_w2, norm_final):
    bsz, seq, d = x.shape
    m_len = mem.shape[1]
    depth = norm_mix.shape[0]
    assert depth == 2, "the final RMSNorm is fused into the routed layer, which must be the last one"
    h = x.reshape(bsz * seq, d)
    mem2 = mem.reshape(bsz * m_len, d)
    for l in range(depth):
        h = _hybrid_mixer(h, bsz, norm_mix[l], w_in[l], sgu_g[l], sgu_w[l], sgu_b[l], conv_w[l], conv_b[l],
                          rg_wa[l], rg_ba[l], rg_wx[l], rg_bx[l], rg_lambda[l], swa_sinks[l], fox_bf[l],
                          w_branch[l], w_gate[l], b_gate[l], w_out[l])
        kv = _memkv(mem2, _row(norm_mem[l]), wkv_c[l].astype(BF), m_len)
        h = _cross(h, _row(norm_cross[l]), wq_c[l].astype(BF), kv, wo_c[l].astype(BF), bsz, m_len)
        if l % 2 == 0:
            h = _ffn(h, _row(norm_ffn[l]), dense_w13[l // 2].astype(BF), dense_w2[l // 2].astype(BF))
        else:
            h = _moe(h, _row(norm_ffn[l]), router_w[l // 2], router_b[l // 2], moe_w13[l // 2].astype(BF),
                     moe_w2[l // 2].astype(BF), _row(norm_final))
    return h.reshape(bsz, seq, d)
```
